```python
import math
import jax
import jax.numpy as jnp
from jax import lax
import numpy as np

D_MODEL = 1024
BATCH = 2
SEQ = 8192
DEPTH = 2

GRID_W = 64
CTX_LEN = 256
CHUNK = 128
Q_BLOCK = 128
N_DIR = 2

M_HEADS = 4
M_QK_DIM = D_MODEL // 8
M_V_DIM = D_MODEL // 4
M_WIDTH = M_HEADS * M_V_DIM

S_HEADS = 16
S_HEAD_DIM = D_MODEL // 16
S_GROUPS = 2
S_HEADS_PER_GROUP = S_HEADS // S_GROUPS
S_STATE = 128
S_WIDTH = S_HEADS * S_HEAD_DIM
S_CONV_CH = S_WIDTH + 2 * S_GROUPS * S_STATE
CONV_K = 4

AB_WIDTHS = (M_HEADS * M_QK_DIM, M_HEADS * M_QK_DIM, M_WIDTH, M_WIDTH, N_DIR * M_HEADS, N_DIR * M_HEADS, S_WIDTH, S_CONV_CH, N_DIR * S_HEADS)
AB_IN = sum(AB_WIDTHS)
AB_SPLITS = tuple(int(v) for v in np.cumsum(AB_WIDTHS)[:-1])
AB_MIX = M_WIDTH + S_WIDTH

A_HEADS = 8
A_KV_HEADS = 2
A_HEAD_DIM = D_MODEL // A_HEADS
A_Q_W = A_HEADS * A_HEAD_DIM
A_KV_W = A_KV_HEADS * A_HEAD_DIM
AT_IN = A_Q_W + 2 * A_KV_W
ROPE_THETA = 10000.0

N_EXPERTS = 16
N_EXPERT_GROUPS = 4
EXPERTS_PER_GROUP = N_EXPERTS // N_EXPERT_GROUPS
TOP_K = 2
D_FF = D_MODEL
MOE_BLOCK = 256

DEEPNORM_ALPHA = (2 * DEPTH) ** 0.25
DEEPNORM_BETA = (8 * DEPTH) ** -0.25
N_EVEN = (DEPTH + 1) // 2
N_ODD = DEPTH // 2
LN_EPS = 1e-5
RMS_EPS = 1e-6
F32 = jnp.float32

kernel_name = 'hybrid_mlstm_ssd_gqa_moe_diffusion_block'


def layer_norm(x, g, b):
    xf = x.astype(F32)
    mu = xf.mean(-1, keepdims=True)
    var = jnp.mean(jnp.square(xf - mu), -1, keepdims=True)
    return ((xf - mu) * lax.rsqrt(var + LN_EPS) * g + b).astype(x.dtype)


def rms_norm(x, g):
    xf = x.astype(F32)
    return (xf * lax.rsqrt(jnp.mean(xf * xf, -1, keepdims=True) + RMS_EPS) * g).astype(x.dtype)


def head_layer_norm(h, g):
    mu = h.mean(-1, keepdims=True)
    var = jnp.mean(jnp.square(h - mu), -1, keepdims=True)
    y = (h - mu) * lax.rsqrt(var + LN_EPS)
    return y.reshape(h.shape[0], h.shape[1], -1) * g


def ada_params(cvec, w, b):
    m = jax.nn.silu(cvec) @ w + b
    return jnp.split(m, 6, axis=-1)


def modulate(h, shift, scale):
    return h * (1 + scale) + shift


def axial_rope_tables(n_tokens):
    rows = n_tokens // GRID_W
    row = jnp.repeat(jnp.arange(rows), GRID_W).astype(F32)
    col = jnp.tile(jnp.arange(GRID_W), rows).astype(F32)
    n_freq = A_HEAD_DIM // 4
    inv = ROPE_THETA ** (-jnp.arange(n_freq, dtype=F32) / n_freq)
    ang = jnp.concatenate([row[:, None] * inv, col[:, None] * inv], -1)
    return jnp.cos(ang), jnp.sin(ang)


def apply_rope(x, cos, sin):
    half = x.shape[-1] // 2
    x1, x2 = x[..., :half].astype(F32), x[..., half:].astype(F32)
    c = cos[None, :, None, :]
    s = sin[None, :, None, :]
    return jnp.concatenate([x1 * c - x2 * s, x2 * c + x1 * s], -1).astype(x.dtype)


def dwconv_centred(x, w, b):
    k = w.shape[0]
    y = lax.conv_general_dilated(x, w[:, None, :].astype(x.dtype), window_strides=(1,),
                                 padding=[(k // 2, k - 1 - k // 2)],
                                 dimension_numbers=('NWC', 'WIO', 'NWC'),
                                 feature_group_count=x.shape[-1])
    return y + b


def mlstm_chunkwise(q, k, v, ig, lf, state):
    bsz, t_len, nh, dk = q.shape
    dv = v.shape[-1]
    nc, L = t_len // CHUNK, CHUNK
    q = q.astype(F32).reshape(bsz, nc, L, nh, dk)
    k = k.astype(F32).reshape(bsz, nc, L, nh, dk)
    v = v.astype(F32).reshape(bsz, nc, L, nh, dv)
    ig = ig.reshape(bsz, nc, L, nh)
    lf = lf.reshape(bsz, nc, L, nh)
    b = jnp.cumsum(lf, axis=2)
    b_end = b[:, :, -1]
    g = b_end[:, :, None] - b + ig
    g_max = g.max(axis=2)
    w = jnp.exp(g - g_max[:, :, None])
    d_c = jnp.einsum('bcshv,bcshd->bchvd', v * w[..., None], k)
    d_n = jnp.einsum('bcsh,bcshd->bchd', w, k)

    def step(carry, inp):
        c_st, n_st, m_st = carry
        dc, dn, be, gm = inp
        m_new = jnp.maximum(be + m_st, gm)
        a = jnp.exp(be + m_st - m_new)
        s = jnp.exp(gm - m_new)
        c_new = a[..., None, None] * c_st + s[..., None, None] * dc
        n_new = a[..., None] * n_st + s[..., None] * dn
        return (c_new, n_new, m_new), (c_st, n_st, m_st)

    final, (cs, ns, ms) = lax.scan(step, state, (jnp.moveaxis(d_c, 1, 0), jnp.moveaxis(d_n, 1, 0),
                                                 jnp.moveaxis(b_end, 1, 0), jnp.moveaxis(g_max, 1, 0)))
    cs, ns, ms = jnp.moveaxis(cs, 0, 1), jnp.moveaxis(ns, 0, 1), jnp.moveaxis(ms, 0, 1)
    lower = jnp.tril(jnp.ones((L, L), bool))
    dmat = jnp.where(lower[:, :, None], b[:, :, :, None, :] - b[:, :, None, :, :] + ig[:, :, None, :, :], -jnp.inf)
    inter = b + ms[:, :, None, :]
    m_t = jnp.maximum(inter, dmat.max(axis=3))
    sc = jnp.einsum('bcthd,bcshd->bctsh', q, k) * jnp.exp(dmat - m_t[:, :, :, None, :])
    a_in = jnp.exp(inter - m_t)
    num = jnp.einsum('bctsh,bcshv->bcthv', sc, v) + a_in[..., None] * jnp.einsum('bcthd,bchvd->bcthv', q, cs)
    den = sc.sum(axis=3) + a_in * jnp.einsum('bcthd,bchd->bcth', q, ns)
    h = num / jnp.maximum(jnp.abs(den), jnp.exp(-m_t))[..., None]
    return h.reshape(bsz, t_len, nh, dv), final


def ssd_chunked(xs, a, bm, cm, h0):
    bsz, t_len, ng, nj, p = xs.shape
    n = bm.shape[-1]
    nc, L = t_len // CHUNK, CHUNK
    xs = xs.astype(F32).reshape(bsz, nc, L, ng, nj, p)
    a = a.astype(F32).reshape(bsz, nc, L, ng, nj)
    bm = bm.astype(F32).reshape(bsz, nc, L, ng, n)
    cm = cm.astype(F32).reshape(bsz, nc, L, ng, n)
    a_cs = jnp.cumsum(a, axis=2)
    lower = jnp.tril(jnp.ones((L, L), bool))
    seg = a_cs[:, :, :, None] - a_cs[:, :, None]
    decay = jnp.exp(jnp.where(lower[:, :, None, None], seg, -jnp.inf))
    cb = jnp.einsum('bctgn,bcsgn->bctsg', cm, bm)
    y = jnp.einsum('bctsgj,bcsgjp->bctgjp', cb[..., None] * decay, xs)
    a_end = a_cs[:, :, -1]
    xw = xs * jnp.exp(a_end[:, :, None] - a_cs)[..., None]
    d_states = jnp.einsum('bcsgn,bcsgjp->bcgjpn', bm, xw)

    def step(h, inp):
        ds, ae = inp
        return jnp.exp(ae)[..., None, None] * h + ds, h

    h_final, hs = lax.scan(step, h0, (jnp.moveaxis(d_states, 1, 0), jnp.moveaxis(a_end, 1, 0)))
    hs = jnp.moveaxis(hs, 0, 1)
    y = y + jnp.einsum('bctgn,bcgjpn->bctgjp', cm, hs) * jnp.exp(a_cs)[..., None]
    return y.reshape(bsz, t_len, ng, nj, p), h_final


def _flip_time(t):
    return jnp.flip(t, axis=1)


def _identity(t):
    return t


def bidirectional_prefix_scan(run, ctx_dirs, lat_dirs, zero_state):
    outs_c, outs_l = [], []
    for d in range(N_DIR):
        f = _flip_time if d == 1 else _identity
        yc, st = run(*[f(t) for t in ctx_dirs[d]], zero_state)
        yl, _ = run(*[f(t) for t in lat_dirs[d]], st)
        outs_c.append(f(yc))
        outs_l.append(f(yl))
    return outs_c[0] + outs_c[1], outs_l[0] + outs_l[1]


def mixer_mlstm_ssd(hl, hc, w_in, w_out, ig_b, fg_b, m_norm_g, conv_w, conv_b, dt_b, a_log, d_skip, s_norm_g):
    a_neg = (-jnp.exp(a_log.astype(F32))).reshape(N_DIR, S_GROUPS, S_HEADS_PER_GROUP)

    def prepare(h):
        bsz, t_len, _ = h.shape
        q, k, v, o, ig, fg, z, xbc, dt = jnp.split(h @ w_in, AB_SPLITS, axis=-1)
        q = q.reshape(bsz, t_len, M_HEADS, M_QK_DIM) * (M_QK_DIM ** -0.5)
        k = k.reshape(bsz, t_len, M_HEADS, M_QK_DIM)
        v = v.reshape(bsz, t_len, M_HEADS, M_V_DIM)
        ig = ig.reshape(bsz, t_len, N_DIR, M_HEADS).astype(F32) + ig_b
        lf = jax.nn.log_sigmoid(fg.reshape(bsz, t_len, N_DIR, M_HEADS).astype(F32) + fg_b)
        xbc = jax.nn.silu(dwconv_centred(xbc, conv_w, conv_b))
        xs, bm, cm = jnp.split(xbc, [S_WIDTH, S_WIDTH + S_GROUPS * S_STATE], axis=-1)
        xs = xs.reshape(bsz, t_len, S_GROUPS, S_HEADS_PER_GROUP, S_HEAD_DIM).astype(F32)
        bm = bm.reshape(bsz, t_len, S_GROUPS, S_STATE)
        cm = cm.reshape(bsz, t_len, S_GROUPS, S_STATE)
        dt = jax.nn.softplus(dt.reshape(bsz, t_len, N_DIR, S_HEADS).astype(F32) + dt_b)
        dt = dt.reshape(bsz, t_len, N_DIR, S_GROUPS, S_HEADS_PER_GROUP)
        m_dirs = [(q, k, v, ig[:, :, d], lf[:, :, d]) for d in range(N_DIR)]
        s_dirs = [(xs * dt[:, :, d][..., None], dt[:, :, d] * a_neg[d], bm, cm) for d in range(N_DIR)]
        return m_dirs, s_dirs, o, z, xs

    mc, sc, oc, zc, xsc = prepare(hc)
    ml, sl, ol, zl, xsl = prepare(hl)
    bsz = hl.shape[0]
    m_zero = (jnp.zeros((bsz, M_HEADS, M_V_DIM, M_QK_DIM), F32), jnp.zeros((bsz, M_HEADS, M_QK_DIM), F32),
              jnp.zeros((bsz, M_HEADS), F32))
    s_zero = jnp.zeros((bsz, S_GROUPS, S_HEADS_PER_GROUP, S_HEAD_DIM, S_STATE), F32)
    hm_c, hm_l = bidirectional_prefix_scan(mlstm_chunkwise, mc, ml, m_zero)
    hs_c, hs_l = bidirectional_prefix_scan(ssd_chunked, sc, sl, s_zero)

    def finish(hm, hs, o, z, xs):
        bsz_, t_len = o.shape[:2]
        ym = head_layer_norm(hm, m_norm_g) * jax.nn.sigmoid(o.astype(F32))
        ys = (hs + d_skip.astype(F32).reshape(S_GROUPS, S_HEADS_PER_GROUP, 1) * xs).reshape(bsz_, t_len, S_WIDTH)
        ys = rms_norm(ys * jax.nn.silu(z.astype(F32)), s_norm_g)
        return jnp.concatenate([ym, ys], -1).astype(o.dtype) @ w_out

    return finish(hm_l, hs_l, ol, zl, xsl), finish(hm_c, hs_c, oc, zc, xsc)


def blocked_attention(q, k, v):
    bsz, t_len, _, dh = q.shape
    nb = t_len // Q_BLOCK
    qb = jnp.moveaxis(q.reshape(bsz, nb, Q_BLOCK, A_KV_HEADS, A_HEADS // A_KV_HEADS, dh), 1, 0)
    scale = dh ** -0.5

    def one(qblk):
        s = jnp.einsum('bqkgd,bskd->bkgqs', qblk, k).astype(F32) * scale
        p = jax.nn.softmax(s, axis=-1).astype(v.dtype)
        return jnp.einsum('bkgqs,bskd->bqkgd', p, v)

    o = lax.map(one, qb)
    return jnp.moveaxis(o, 0, 1).reshape(bsz, t_len, A_Q_W)


def mixer_gqa(hl, hc, w_in, w_out, q_g, k_g, cos, sin, need_ctx):
    def q_proj(h):
        bsz, t_len, _ = h.shape
        return rms_norm((h @ w_in[:, :A_Q_W]).reshape(bsz, t_len, A_HEADS, A_HEAD_DIM), q_g)

    def kv_proj(h):
        bsz, t_len, _ = h.shape
        k, v = jnp.split(h @ w_in[:, A_Q_W:], [A_KV_W], axis=-1)
        k = rms_norm(k.reshape(bsz, t_len, A_KV_HEADS, A_HEAD_DIM), k_g)
        return k, v.reshape(bsz, t_len, A_KV_HEADS, A_HEAD_DIM)

    kl, vl = kv_proj(hl)
    kl = apply_rope(kl, cos, sin)
    ql = apply_rope(q_proj(hl), cos, sin)
    kc, vc = kv_proj(hc)
    k_all = jnp.concatenate([kc, kl], axis=1)
    v_all = jnp.concatenate([vc, vl], axis=1)
    yl = blocked_attention(ql, k_all, v_all) @ w_out
    yc = blocked_attention(q_proj(hc), kc, vc) @ w_out if need_ctx else None
    return yl, yc


def moe_ffn(h, router_w, router_b, w_gate, w_up, w_down):
    n_tok, d = h.shape
    probs = jax.nn.softmax((h @ router_w).astype(F32), axis=-1)
    sel = (probs + router_b).reshape(n_tok, N_EXPERT_GROUPS, EXPERTS_PER_GROUP)
    g_score = lax.top_k(sel, TOP_K)[0].sum(-1)
    g_idx = jnp.argmax(g_score, axis=-1)
    in_group = jnp.take_along_axis(sel, g_idx[:, None, None], axis=1)[:, 0]
    _, local = lax.top_k(in_group, TOP_K)
    e_idx = g_idx[:, None] * EXPERTS_PER_GROUP + local
    gate = jnp.take_along_axis(probs, e_idx, axis=1)
    gate = gate / gate.sum(-1, keepdims=True)
    n_asg = n_tok * TOP_K
    flat_e = e_idx.reshape(-1)
    flat_tok = jnp.repeat(jnp.arange(n_tok), TOP_K)
    order = jnp.argsort(flat_e)
    se, stok, sgate = flat_e[order], flat_tok[order], gate.reshape(-1)[order]
    counts = jnp.bincount(flat_e, length=N_EXPERTS)
    padded = (counts + MOE_BLOCK - 1) // MOE_BLOCK * MOE_BLOCK
    start = jnp.cumsum(counts) - counts
    pend = jnp.cumsum(padded)
    pstart = pend - padded
    dest = pstart[se] + jnp.arange(n_asg) - start[se]
    n_blocks = -(-n_asg // MOE_BLOCK) + N_EXPERTS
    slot_tok = jnp.zeros((n_blocks * MOE_BLOCK,), jnp.int32).at[dest].set(stok)
    blk_exp = jnp.minimum(jnp.searchsorted(pend, jnp.arange(n_blocks) * MOE_BLOCK, side='right'), N_EXPERTS - 1)

    def expert_block(args):
        toks, e = args
        xb = h[toks]
        return (jax.nn.silu(xb @ w_gate[e]) * (xb @ w_up[e])) @ w_down[e]

    y_slots = lax.map(expert_block, (slot_tok.reshape(n_blocks, MOE_BLOCK), blk_exp)).reshape(-1, d)
    contrib = y_slots[dest] * sgate[:, None].astype(h.dtype)
    return jnp.zeros_like(h).at[stok].add(contrib)


def setup_inputs(seed: int = 0) -> dict:
    key = jax.random.key(seed)
    keys = iter(jax.random.split(key, 40))
    D = D_MODEL

    def nrm(shape, scale):
        return jax.random.normal(next(keys), shape, F32) * scale

    def uni(shape, lo, hi):
        return jax.random.uniform(next(keys), shape, F32, lo, hi)

    dt0 = jnp.exp(uni((N_EVEN, N_DIR, S_HEADS), math.log(1e-3), math.log(1e-1)))
    return {
        'x': nrm((BATCH, SEQ, D), 1.0),
        'c': nrm((BATCH, D), 1.0),
        'ctx': nrm((BATCH, CTX_LEN, D), 1.0),
        'c_ctx': nrm((D,), 1.0),
        'ada_w': nrm((DEPTH, D, 6 * D), 0.5 * D ** -0.5),
        'ada_b': nrm((DEPTH, 6 * D), 0.02),
        'ln_g': 1.0 + nrm((DEPTH, 2, D), 0.02),
        'ln_b': nrm((DEPTH, 2, D), 0.02),
        'ab_w_in': nrm((N_EVEN, D, AB_IN), D ** -0.5),
        'ab_w_out': nrm((N_EVEN, AB_MIX, D), AB_MIX ** -0.5 * DEEPNORM_BETA),
        'ml_ig_b': nrm((N_EVEN, N_DIR, M_HEADS), 0.1),
        'ml_fg_b': uni((N_EVEN, N_DIR, M_HEADS), 3.0, 6.0),
        'ml_norm_g': 1.0 + nrm((N_EVEN, M_WIDTH), 0.02),
        'ssm_conv_w': nrm((N_EVEN, CONV_K, S_CONV_CH), CONV_K ** -0.5),
        'ssm_conv_b': nrm((N_EVEN, S_CONV_CH), 0.02),
        'ssm_dt_b': dt0 + jnp.log(-jnp.expm1(-dt0)),
        'ssm_a_log': jnp.log(uni((N_EVEN, N_DIR, S_HEADS), 1.0, 16.0)),
        'ssm_d': 1.0 + nrm((N_EVEN, S_HEADS), 0.1),
        'ssm_norm_g': 1.0 + nrm((N_EVEN, S_WIDTH), 0.02),
        'at_w_in': nrm((N_ODD, D, AT_IN), D ** -0.5),
        'at_w_out': nrm((N_ODD, A_Q_W, D), A_Q_W ** -0.5 * DEEPNORM_BETA),
        'at_q_g': 1.0 + nrm((N_ODD, A_HEAD_DIM), 0.02),
        'at_k_g': 1.0 + nrm((N_ODD, A_HEAD_DIM), 0.02),
        'router_w': nrm((D, N_EXPERTS), D ** -0.5),
        'router_b': nrm((N_EXPERTS,), 0.01),
        'moe_w_gate': nrm((DEPTH, N_EXPERTS, D, D_FF), D ** -0.5),
        'moe_w_up': nrm((DEPTH, N_EXPERTS, D, D_FF), D ** -0.5),
        'moe_w_down': nrm((DEPTH, N_EXPERTS, D_FF, D), D_FF ** -0.5 * DEEPNORM_BETA),
    }


def reference(x, c, ctx, c_ctx, ada_w, ada_b, ln_g, ln_b, ab_w_in, ab_w_out, ml_ig_b, ml_fg_b, ml_norm_g,
              ssm_conv_w, ssm_conv_b, ssm_dt_b, ssm_a_log, ssm_d, ssm_norm_g, at_w_in, at_w_out, at_q_g, at_k_g,
              router_w, router_b, moe_w_gate, moe_w_up, moe_w_down):
    bsz, n_lat, d = x.shape
    cos, sin = axial_rope_tables(n_lat)
    xl, xc = x, ctx
    for i in range(DEPTH):
        last = i == DEPTH - 1
        j = i // 2
        sh1l, sc1l, gt1l, sh2l, sc2l, gt2l = ada_params(c[:, None, :], ada_w[i], ada_b[i])
        sh1c, sc1c, gt1c, sh2c, sc2c, gt2c = ada_params(c_ctx, ada_w[i], ada_b[i])
        hl = modulate(xl, sh1l, sc1l)
        hc = modulate(xc, sh1c, sc1c)
        if i % 2 == 0:
            yl, yc = mixer_mlstm_ssd(hl, hc, ab_w_in[j], ab_w_out[j], ml_ig_b[j], ml_fg_b[j], ml_norm_g[j],
                                     ssm_conv_w[j], ssm_conv_b[j], ssm_dt_b[j], ssm_a_log[j], ssm_d[j], ssm_norm_g[j])
        else:
            yl, yc = mixer_gqa(hl, hc, at_w_in[j], at_w_out[j], at_q_g[j], at_k_g[j], cos, sin, not last)
        xl = layer_norm(DEEPNORM_ALPHA * xl + gt1l * yl, ln_g[i, 0], ln_b[i, 0])
        hl = modulate(xl, sh2l, sc2l)
        if last:
            yl = moe_ffn(hl.reshape(-1, d), router_w, router_b, moe_w_gate[i], moe_w_up[i], moe_w_down[i]).reshape(xl.shape)
        else:
            xc = layer_norm(DEEPNORM_ALPHA * xc + gt1c * yc, ln_g[i, 0], ln_b[i, 0])
            hc = modulate(xc, sh2c, sc2c)
            y_tok = moe_ffn(jnp.concatenate([hl.reshape(-1, d), hc.reshape(-1, d)], axis=0),
                            router_w, router_b, moe_w_gate[i], moe_w_up[i], moe_w_down[i])
            yl = y_tok[:bsz * n_lat].reshape(xl.shape)
            yc = y_tok[bsz * n_lat:].reshape(xc.shape)
            xc = layer_norm(DEEPNORM_ALPHA * xc + gt2c * yc, ln_g[i, 1], ln_b[i, 1])
        xl = layer_norm(DEEPNORM_ALPHA * xl + gt2l * yl, ln_g[i, 1], ln_b[i, 1])
    return xl
```

```python
import functools
import math

import jax
import jax.numpy as jnp
from jax import lax
from jax.experimental import pallas as pl
from jax.experimental.pallas import tpu as pltpu

F32 = jnp.float32
MXU_DTYPE = jnp.bfloat16
HIGHEST = lax.Precision.HIGHEST

D_MODEL = 1024
DEPTH = 2
GRID_W = 64
CHUNK = 128
M_HEADS = 4
M_QK_DIM = D_MODEL // 8
M_V_DIM = D_MODEL // 4
M_WIDTH = M_HEADS * M_V_DIM
S_HEADS = 16
S_HEAD_DIM = D_MODEL // 16
S_GROUPS = 2
S_HEADS_PER_GROUP = S_HEADS // S_GROUPS
S_STATE = 128
S_WIDTH = S_HEADS * S_HEAD_DIM
S_CONV_CH = S_WIDTH + 2 * S_GROUPS * S_STATE
CONV_K = 4
N_DIR = 2
A_HEADS = 8
A_KV_HEADS = 2
A_HEAD_DIM = D_MODEL // A_HEADS
A_Q_W = A_HEADS * A_HEAD_DIM
A_KV_W = A_KV_HEADS * A_HEAD_DIM
ROPE_THETA = 10000.0
N_EXPERTS = 16
N_EXPERT_GROUPS = 4
EXPERTS_PER_GROUP = N_EXPERTS // N_EXPERT_GROUPS
TOP_K = 2
DEEPNORM_ALPHA = (2 * DEPTH) ** 0.25
LN_EPS = 1e-5
RMS_EPS = 1e-6

LANES = 128
SUBLANES = 8
TM = 256
MOE_BLK = 256
GATHER_ROWS = 256
ATT_TQ = 256
ATT_TK = 768
VMEM_LIMIT = 56 * 1024 * 1024
GATE_LANES = 128
IG_OFF, FG_OFF, DT_OFF = 0, N_DIR * M_HEADS, 2 * N_DIR * M_HEADS


def _cparams(sem):
    return pltpu.CompilerParams(dimension_semantics=sem, vmem_limit_bytes=VMEM_LIMIT)


def _silu(x):
    return x / (1.0 + jnp.exp(-x))


def _sigmoid(x):
    return 1.0 / (1.0 + jnp.exp(-x))


def _softplus(x):
    return jnp.maximum(x, 0.0) + jnp.log(1.0 + jnp.exp(-jnp.abs(x)))


def _log_sigmoid(x):
    return jnp.minimum(x, 0.0) - jnp.log(1.0 + jnp.exp(-jnp.abs(x)))


def _layer_norm_rows(x, g, b):
    mu = jnp.mean(x, axis=-1, keepdims=True)
    xc = x - mu
    var = jnp.mean(xc * xc, axis=-1, keepdims=True)
    return xc * lax.rsqrt(var + LN_EPS) * g + b


def _mm(a, b):
    return jnp.dot(a.astype(MXU_DTYPE), b.astype(MXU_DTYPE), preferred_element_type=F32)


def _mm_nt(a, b):
    return lax.dot_general(a.astype(MXU_DTYPE), b.astype(MXU_DTYPE), (((1,), (1,)), ((), ())),
                           preferred_element_type=F32)


def _mm_tn(a, b):
    return lax.dot_general(a.astype(MXU_DTYPE), b.astype(MXU_DTYPE), (((0,), (0,)), ((), ())),
                           preferred_element_type=F32)


def _ada_kernel(c_ref, w_ref, b_ref, o_ref):
    s = _silu(c_ref[...])
    o_ref[0] = jnp.dot(s, w_ref[0], precision=HIGHEST, preferred_element_type=F32) + b_ref[0]


def ada_call(crows, ada_w, ada_b):
    depth, d, n6 = ada_w.shape
    tn = 1536
    return pl.pallas_call(
        _ada_kernel,
        grid=(depth, n6 // tn),
        in_specs=[pl.BlockSpec((SUBLANES, d), lambda i, j: (0, 0)),
                  pl.BlockSpec((1, d, tn), lambda i, j: (i, 0, j)),
                  pl.BlockSpec((1, 1, tn), lambda i, j: (i, 0, j))],
        out_specs=pl.BlockSpec((1, SUBLANES, tn), lambda i, j: (i, 0, j)),
        out_shape=jax.ShapeDtypeStruct((depth, SUBLANES, n6), F32),
        compiler_params=_cparams(("arbitrary", "arbitrary")),
        name="ada",
    )(crows, ada_w, ada_b.reshape(depth, 1, n6))


def _inproj0_kernel(x_ref, mod_ref, wb_ref, ws_ref, q_ref, k_ref, v_ref, o_ref, z_ref, xbc_ref, g_ref):
    x = x_ref[0]
    mod = mod_ref[0, 0]
    h = x * (1.0 + mod[1:2]) + mod[0:1]
    hb = h.astype(MXU_DTYPE)
    qk = M_HEADS * M_QK_DIM
    c0 = 0
    q_ref[0] = (jnp.dot(hb, wb_ref[:, c0:c0 + qk], preferred_element_type=F32)
                * (M_QK_DIM ** -0.5)).astype(q_ref.dtype)
    c0 += qk
    k_ref[0] = jnp.dot(hb, wb_ref[:, c0:c0 + qk], preferred_element_type=F32).astype(k_ref.dtype)
    c0 += qk
    v_ref[0] = jnp.dot(hb, wb_ref[:, c0:c0 + M_WIDTH], preferred_element_type=F32).astype(v_ref.dtype)
    c0 += M_WIDTH
    o_ref[0] = jnp.dot(hb, wb_ref[:, c0:c0 + M_WIDTH], preferred_element_type=F32)
    c0 += M_WIDTH
    z_ref[0] = jnp.dot(hb, wb_ref[:, c0:c0 + S_WIDTH], preferred_element_type=F32)
    c0 += S_WIDTH
    xbc_ref[0] = jnp.dot(hb, wb_ref[:, c0:c0 + S_CONV_CH], preferred_element_type=F32)
    g_ref[0] = jnp.dot(h, ws_ref[...], precision=HIGHEST, preferred_element_type=F32)


def inproj0_call(xa, modsel, w_big, w_small, ctx_tiles):
    bsz, ta, d = xa.shape
    nt = ta // TM
    qk = M_HEADS * M_QK_DIM
    widths = (qk, qk, M_WIDTH, M_WIDTH, S_WIDTH, S_CONV_CH, GATE_LANES)
    dtypes = (MXU_DTYPE, MXU_DTYPE, MXU_DTYPE, F32, F32, F32, F32)
    tok = lambda w: pl.BlockSpec((1, TM, w), lambda b, j: (b, j, 0))
    return pl.pallas_call(
        _inproj0_kernel,
        grid=(bsz, nt),
        in_specs=[tok(d),
                  pl.BlockSpec((1, 1, 6, d), lambda b, j: (b, (j >= ctx_tiles).astype(jnp.int32), 0, 0)),
                  pl.BlockSpec(w_big.shape, lambda b, j: (0, 0)),
                  pl.BlockSpec(w_small.shape, lambda b, j: (0, 0))],
        out_specs=[tok(w) for w in widths],
        out_shape=[jax.ShapeDtypeStruct((bsz, ta, w), dt) for w, dt in zip(widths, dtypes)],
        compiler_params=_cparams(("arbitrary", "arbitrary")),
        name="inproj0",
    )(xa, modsel, w_big, w_small)


def _conv_kernel(cur_ref, prev_ref, next_ref, w_ref, b_ref, o_ref, *, ctx_tiles, n_tiles):
    j = pl.program_id(1)
    has_prev = jnp.logical_and(j != 0, j != ctx_tiles)
    has_next = jnp.logical_and(j != ctx_tiles - 1, j != n_tiles - 1)
    prev = jnp.where(has_prev, prev_ref[0], 0.0)
    nxt = jnp.where(has_next, next_ref[0], 0.0)
    ext = jnp.concatenate([prev, cur_ref[0], nxt], axis=0)
    n = TM + 2 * SUBLANES
    w = w_ref[...]
    lo, hi = SUBLANES, SUBLANES + TM
    acc = ext[lo:hi] * w[2:3]
    acc = acc + pltpu.roll(ext, 2, 0)[lo:hi] * w[0:1]
    acc = acc + pltpu.roll(ext, 1, 0)[lo:hi] * w[1:2]
    acc = acc + pltpu.roll(ext, n - 1, 0)[lo:hi] * w[3:4]
    o_ref[0] = _silu(acc + b_ref[...])


def conv_call(xbc, conv_w, conv_b, ctx_tiles):
    bsz, ta, ch = xbc.shape
    nt = ta // TM
    r = TM // SUBLANES
    last = ta // SUBLANES - 1
    return pl.pallas_call(
        functools.partial(_conv_kernel, ctx_tiles=ctx_tiles, n_tiles=nt),
        grid=(bsz, nt),
        in_specs=[pl.BlockSpec((1, TM, ch), lambda b, j: (b, j, 0)),
                  pl.BlockSpec((1, SUBLANES, ch), lambda b, j: (b, jnp.maximum(j * r - 1, 0), 0)),
                  pl.BlockSpec((1, SUBLANES, ch), lambda b, j: (b, jnp.minimum((j + 1) * r, last), 0)),
                  pl.BlockSpec((CONV_K, ch), lambda b, j: (0, 0)),
                  pl.BlockSpec((1, ch), lambda b, j: (0, 0))],
        out_specs=pl.BlockSpec((1, TM, ch), lambda b, j: (b, j, 0)),
        out_shape=jax.ShapeDtypeStruct((bsz, ta, ch), F32),
        compiler_params=_cparams(("arbitrary", "arbitrary")),
        name="conv",
    )(xbc, xbc, xbc, conv_w, conv_b.reshape(1, ch))


def _chunk_order(i, rev, ctx_chunks, n_chunks):
    if not rev:
        return i
    return jnp.where(i < ctx_chunks, ctx_chunks - 1 - i, n_chunks - 1 - (i - ctx_chunks))


def _scan_masks(rev):
    r = lax.broadcasted_iota(jnp.int32, (CHUNK, CHUNK), 0)
    c = lax.broadcasted_iota(jnp.int32, (CHUNK, CHUNK), 1)
    mask = (c >= r) if rev else (c <= r)
    mask_t = (r >= c) if rev else (r <= c)
    return mask, mask.astype(F32), mask_t.astype(F32)


def _mlstm_kernel(*refs, rev, d, add_prev):
    if add_prev:
        q_ref, k_ref, v_ref, gc_ref, gr_ref, brow_ref, bcol_ref, prev_ref, o_ref, ct_s, n_s, m_s = refs
    else:
        q_ref, k_ref, v_ref, gc_ref, gr_ref, brow_ref, bcol_ref, o_ref, ct_s, n_s, m_s = refs
        prev_ref = None

    @pl.when(pl.program_id(1) == 0)
    def _():
        ct_s[...] = jnp.zeros_like(ct_s)
        n_s[...] = jnp.zeros_like(n_s)
        m_s[...] = jnp.zeros_like(m_s)

    mask, mask_f, mask_tf = _scan_masks(rev)
    end = 0 if rev else CHUNK - 1
    gcol = gc_ref[0] + brow_ref[...]
    grow = gr_ref[0] + bcol_ref[...]
    nh2 = N_DIR * M_HEADS
    lf_col = _log_sigmoid(gcol)
    lf_row = _log_sigmoid(grow[FG_OFF:FG_OFF + nh2])
    b_col_all = jnp.dot(mask_f, lf_col, precision=HIGHEST, preferred_element_type=F32)
    b_row_all = jnp.dot(lf_row, mask_tf, precision=HIGHEST, preferred_element_type=F32)

    for j in range(M_HEADS):
        ci = d * M_HEADS + j
        bcol = b_col_all[:, FG_OFF + ci:FG_OFF + ci + 1]
        brow = b_row_all[ci:ci + 1, :]
        igcol = gcol[:, IG_OFF + ci:IG_OFF + ci + 1]
        igrow = grow[IG_OFF + ci:IG_OFF + ci + 1, :]
        b_end = bcol[end:end + 1, :]
        m_prev = m_s[j][:, 0:1]
        n_prev = n_s[j]
        ct_prev = ct_s[j]
        q = q_ref[0, :, j * M_QK_DIM:(j + 1) * M_QK_DIM]
        k = k_ref[0, :, j * M_QK_DIM:(j + 1) * M_QK_DIM]
        v = v_ref[0, :, j * M_V_DIM:(j + 1) * M_V_DIM]
        qf = q.astype(F32)
        kf = k.astype(F32)
        vf = v.astype(F32)

        dmat = jnp.where(mask, bcol - brow + igrow, -jnp.inf)
        inter = bcol + m_prev
        m_t = jnp.maximum(inter, jnp.max(dmat, axis=1, keepdims=True))
        sc = _mm_nt(q, k) * jnp.exp(dmat - m_t)
        a_in = jnp.exp(inter - m_t)
        num = _mm(sc, v) + a_in * _mm(q, ct_prev)
        den = jnp.sum(sc, axis=1, keepdims=True) + a_in * jnp.sum(qf * n_prev, axis=1, keepdims=True)
        h = num / jnp.maximum(jnp.abs(den), jnp.exp(-m_t))
        sl = slice(j * M_V_DIM, (j + 1) * M_V_DIM)
        if add_prev:
            h = h + prev_ref[0, :, sl]
        o_ref[0, :, sl] = h

        g_col = b_end - bcol + igcol
        g_row = b_end - brow + igrow
        g_max = jnp.max(g_row, axis=1, keepdims=True)
        w_col = jnp.exp(g_col - g_max)
        d_ct = _mm_tn(k, vf * w_col)
        d_n = jnp.sum(kf * w_col, axis=0, keepdims=True)
        m_new = jnp.maximum(b_end + m_prev, g_max)
        a = jnp.exp(b_end + m_prev - m_new)
        s = jnp.exp(g_max - m_new)
        ct_s[j] = a * ct_prev + s * d_ct
        n_s[j] = a * n_prev + s * d_n
        m_s[j] = jnp.broadcast_to(m_new, (1, LANES))


def mlstm_call(q, k, v, gcol, grow, bias_row, bias_col, prev, *, rev, d, ctx_chunks):
    bsz, ta, _ = q.shape
    nc = ta // CHUNK
    order = lambda i: _chunk_order(i, rev, ctx_chunks, nc)
    tok = lambda w: pl.BlockSpec((1, CHUNK, w), lambda b, i: (b, order(i), 0))
    in_specs = [tok(q.shape[-1]), tok(k.shape[-1]), tok(v.shape[-1]), tok(GATE_LANES),
                pl.BlockSpec((1, grow.shape[1], CHUNK), lambda b, i: (b, 0, order(i))),
                pl.BlockSpec(bias_row.shape, lambda b, i: (0, 0)),
                pl.BlockSpec(bias_col.shape, lambda b, i: (0, 0))]
    args = [q, k, v, gcol, grow, bias_row, bias_col]
    if prev is not None:
        in_specs.append(tok(M_WIDTH))
        args.append(prev)
    return pl.pallas_call(
        functools.partial(_mlstm_kernel, rev=rev, d=d, add_prev=prev is not None),
        grid=(bsz, nc),
        in_specs=in_specs,
        out_specs=tok(M_WIDTH),
        out_shape=jax.ShapeDtypeStruct((bsz, ta, M_WIDTH), F32),
        scratch_shapes=[pltpu.VMEM((M_HEADS, M_QK_DIM, M_V_DIM), F32),
                        pltpu.VMEM((M_HEADS, 1, M_QK_DIM), F32),
                        pltpu.VMEM((M_HEADS, 1, LANES), F32)],
        compiler_params=_cparams(("arbitrary", "arbitrary")),
        name="mlstm_rev" if rev else "mlstm_fwd",
    )(*args)


def _ssd_kernel(*refs, rev, d, add_prev):
    if add_prev:
        x_ref, gc_ref, gr_ref, dtb_row_ref, dtb_col_ref, an_row_ref, an_col_ref, prev_ref, o_ref, ht_s = refs
    else:
        x_ref, gc_ref, gr_ref, dtb_row_ref, dtb_col_ref, an_row_ref, an_col_ref, o_ref, ht_s = refs
        prev_ref = None

    @pl.when(pl.program_id(1) == 0)
    def _():
        ht_s[...] = jnp.zeros_like(ht_s)

    mask, mask_f, mask_tf = _scan_masks(rev)
    end = 0 if rev else CHUNK - 1
    lane = lax.broadcasted_iota(jnp.int32, (CHUNK, LANES), 1)
    first_half = lane < S_HEAD_DIM
    dt_col = _softplus(gc_ref[0] + dtb_row_ref[...])
    dt_row = _softplus(gr_ref[0] + dtb_col_ref[...])
    acs_col = jnp.dot(mask_f, dt_col * an_row_ref[...], precision=HIGHEST, preferred_element_type=F32)
    acs_row = jnp.dot(dt_row * an_col_ref[...], mask_tf, precision=HIGHEST, preferred_element_type=F32)
    a_end_row = acs_col[end:end + 1, :]
    e_cs = jnp.exp(acs_col)
    e_rem = jnp.exp(a_end_row - acs_col)
    e_end = jnp.exp(a_end_row)
    gw = S_HEADS_PER_GROUP * S_HEAD_DIM
    pairs = S_HEADS_PER_GROUP // 2

    def pick(arr, la):
        return jnp.where(first_half[:arr.shape[0]], arr[:, la:la + 1], arr[:, la + 1:la + 2])

    for g in range(S_GROUPS):
        bm = x_ref[0, :, S_WIDTH + g * S_STATE:S_WIDTH + (g + 1) * S_STATE]
        cm = x_ref[0, :, S_WIDTH + (S_GROUPS + g) * S_STATE:S_WIDTH + (S_GROUPS + g + 1) * S_STATE]
        cb = _mm_nt(cm, bm)
        ht_prev = ht_s[g]
        y_inter = _mm(cm, ht_prev)
        xw_parts = []
        decay_parts = []
        for p in range(pairs):
            h0 = g * S_HEADS_PER_GROUP + 2 * p
            la = DT_OFF + d * S_HEADS + h0
            ra = d * S_HEADS + h0
            lhs = []
            for u in range(2):
                seg = acs_col[:, la + u:la + u + 1] - acs_row[ra + u:ra + u + 1, :]
                dec = jnp.exp(jnp.where(mask, seg, -jnp.inf))
                lhs.append((cb * dec).astype(MXU_DTYPE))
            xs = x_ref[0, :, h0 * S_HEAD_DIM:(h0 + 2) * S_HEAD_DIM]
            xsd = xs * pick(dt_col, la)
            rhs = jnp.concatenate([jnp.where(first_half, xsd, 0.0), jnp.where(first_half, 0.0, xsd)],
                                  axis=0).astype(MXU_DTYPE)
            y = jnp.dot(jnp.concatenate(lhs, axis=1), rhs, preferred_element_type=F32)
            y = y + y_inter[:, p * LANES:(p + 1) * LANES] * pick(e_cs, la)
            sl = slice(h0 * S_HEAD_DIM, (h0 + 2) * S_HEAD_DIM)
            if add_prev:
                y = y + prev_ref[0, :, sl]
            o_ref[0, :, sl] = y
            xw_parts.append(xsd * pick(e_rem, la))
            decay_parts.append(pick(e_end, la))
        xw = jnp.concatenate(xw_parts, axis=1)
        decay = jnp.concatenate(decay_parts, axis=1)
        ht_s[g] = decay * ht_prev + _mm_tn(bm, xw)


def ssd_call(xbc_act, gcol, dtrow, dtb_row, dtb_col, an_row, an_col, prev, *, rev, d, ctx_chunks):
    bsz, ta, ch = xbc_act.shape
    nc = ta // CHUNK
    order = lambda i: _chunk_order(i, rev, ctx_chunks, nc)
    tok = lambda w: pl.BlockSpec((1, CHUNK, w), lambda b, i: (b, order(i), 0))
    const = lambda a: pl.BlockSpec(a.shape, lambda b, i: (0, 0))
    in_specs = [tok(ch), tok(GATE_LANES),
                pl.BlockSpec((1, dtrow.shape[1], CHUNK), lambda b, i: (b, 0, order(i))),
                const(dtb_row), const(dtb_col), const(an_row), const(an_col)]
    args = [xbc_act, gcol, dtrow, dtb_row, dtb_col, an_row, an_col]
    if prev is not None:
        in_specs.append(tok(S_WIDTH))
        args.append(prev)
    return pl.pallas_call(
        functools.partial(_ssd_kernel, rev=rev, d=d, add_prev=prev is not None),
        grid=(bsz, nc),
        in_specs=in_specs,
        out_specs=tok(S_WIDTH),
        out_shape=jax.ShapeDtypeStruct((bsz, ta, S_WIDTH), F32),
        scratch_shapes=[pltpu.VMEM((S_GROUPS, S_STATE, S_HEADS_PER_GROUP * S_HEAD_DIM), F32)],
        compiler_params=_cparams(("arbitrary", "arbitrary")),
        name="ssd_rev" if rev else "ssd_fwd",
    )(*args)


def _outproj0_kernel(hm_ref, o_ref, hs_ref, xs_ref, z_ref, x_ref, mod_ref, mg_ref, dsk_ref, sg_ref, w_ref,
                     lng_ref, lnb_ref, x1_ref, h2_ref):
    mod = mod_ref[0, 0]
    parts = []
    for j in range(M_HEADS):
        sl = slice(j * M_V_DIM, (j + 1) * M_V_DIM)
        hj = hm_ref[0, :, sl]
        mu = jnp.mean(hj, axis=-1, keepdims=True)
        hc = hj - mu
        var = jnp.mean(hc * hc, axis=-1, keepdims=True)
        parts.append(hc * lax.rsqrt(var + LN_EPS) * mg_ref[:, sl] * _sigmoid(o_ref[0, :, sl]))
    ym = jnp.concatenate(parts, axis=1)
    ys = (hs_ref[0] + dsk_ref[...] * xs_ref[0]) * _silu(z_ref[0])
    ys = ys * lax.rsqrt(jnp.mean(ys * ys, axis=-1, keepdims=True) + RMS_EPS) * sg_ref[...]
    y = (jnp.dot(ym.astype(MXU_DTYPE), w_ref[:M_WIDTH, :], preferred_element_type=F32)
         + jnp.dot(ys.astype(MXU_DTYPE), w_ref[M_WIDTH:, :], preferred_element_type=F32))
    x1 = _layer_norm_rows(DEEPNORM_ALPHA * x_ref[0] + mod[2:3] * y, lng_ref[...], lnb_ref[...])
    x1_ref[0] = x1
    h2_ref[0] = x1 * (1.0 + mod[4:5]) + mod[3:4]


def outproj0_call(hm, o, hs, xbc_act, z, xa, modsel, m_norm_g, dskip, s_norm_g, w_out, ln_g, ln_b, ctx_tiles):
    bsz, ta, d = xa.shape
    nt = ta // TM
    tok = lambda w: pl.BlockSpec((1, TM, w), lambda b, j: (b, j, 0))
    const = lambda a: pl.BlockSpec(a.shape, lambda b, j: (0, 0))
    return pl.pallas_call(
        _outproj0_kernel,
        grid=(bsz, nt),
        in_specs=[tok(M_WIDTH), tok(M_WIDTH), tok(S_WIDTH), tok(S_WIDTH), tok(S_WIDTH), tok(d),
                  pl.BlockSpec((1, 1, 6, d), lambda b, j: (b, (j >= ctx_tiles).astype(jnp.int32), 0, 0)),
                  const(m_norm_g), const(dskip), const(s_norm_g), const(w_out), const(ln_g), const(ln_b)],
        out_specs=[tok(d), tok(d)],
        out_shape=[jax.ShapeDtypeStruct((bsz, ta, d), F32)] * 2,
        compiler_params=_cparams(("arbitrary", "arbitrary")),
        name="outproj0",
    )(hm, o, hs, xbc_act, z, xa, modsel, m_norm_g, dskip, s_norm_g, w_out, ln_g, ln_b)


def _top2(vals, probs):
    v1, i1, p1 = vals[0], jnp.zeros_like(vals[0], dtype=jnp.int32), probs[0]
    for i in range(1, len(vals)):
        better = vals[i] > v1
        v1 = jnp.where(better, vals[i], v1)
        i1 = jnp.where(better, i, i1)
        p1 = jnp.where(better, probs[i], p1)
    v2 = jnp.full_like(vals[0], -jnp.inf)
    i2 = jnp.zeros_like(i1)
    p2 = jnp.zeros_like(p1)
    for i in range(len(vals)):
        better = jnp.logical_and(i1 != i, vals[i] > v2)
        v2 = jnp.where(better, vals[i], v2)
        i2 = jnp.where(better, i, i2)
        p2 = jnp.where(better, probs[i], p2)
    return v1, i1, p1, v2, i2, p2


def _router_kernel(h_ref, wt_ref, b_ref, e_ref, g_ref):
    logits = lax.dot_general(wt_ref[...], h_ref[...], (((1,), (1,)), ((), ())), precision=HIGHEST,
                             preferred_element_type=F32)
    mx = jnp.max(logits, axis=0, keepdims=True)
    ex = jnp.exp(logits - mx)
    probs = ex / jnp.sum(ex, axis=0, keepdims=True)
    sel = probs + b_ref[...]
    best = None
    for g in range(N_EXPERT_GROUPS):
        rows = range(g * EXPERTS_PER_GROUP, (g + 1) * EXPERTS_PER_GROUP)
        v1, i1, p1, v2, i2, p2 = _top2([sel[r:r + 1] for r in rows], [probs[r:r + 1] for r in rows])
        cand = (v1 + v2, i1 + g * EXPERTS_PER_GROUP, p1, i2 + g * EXPERTS_PER_GROUP, p2)
        if best is None:
            best = cand
        else:
            better = cand[0] > best[0]
            best = tuple(jnp.where(better, c, o) for c, o in zip(cand, best))
    _, e1, p1, e2, p2 = best
    tot = p1 + p2
    e_ref[...] = jnp.concatenate([e1, e2], axis=0)
    g_ref[...] = jnp.concatenate([p1 / tot, p2 / tot], axis=0)


def router_call(h2, router_wt, router_b):
    n, d = h2.shape
    return pl.pallas_call(
        _router_kernel,
        grid=(n // TM,),
        in_specs=[pl.BlockSpec((TM, d), lambda i: (i, 0)),
                  pl.BlockSpec(router_wt.shape, lambda i: (0, 0)),
                  pl.BlockSpec(router_b.shape, lambda i: (0, 0))],
        out_specs=[pl.BlockSpec((TOP_K, TM), lambda i: (0, i))] * 2,
        out_shape=[jax.ShapeDtypeStruct((TOP_K, n), jnp.int32), jax.ShapeDtypeStruct((TOP_K, n), F32)],
        compiler_params=_cparams(("arbitrary",)),
        name="router",
    )(h2, router_wt, router_b)


def _gather_kernel(idx_ref, src_ref, o_ref, sem):
    base = pl.program_id(0) * GATHER_ROWS

    def row_copy(r, src_row):
        return pltpu.make_async_copy(src_ref.at[pl.ds(src_row, 1)], o_ref.at[pl.ds(r, 1)], sem)

    def issue(r, carry):
        row_copy(r, idx_ref[base + r]).start()
        return carry

    def drain(r, carry):
        row_copy(r, 0).wait()
        return carry

    lax.fori_loop(0, GATHER_ROWS, issue, 0, unroll=8)
    lax.fori_loop(0, GATHER_ROWS, drain, 0, unroll=8)


def gather_rows(src, idx):
    m = idx.shape[0]
    d = src.shape[1]
    return pl.pallas_call(
        _gather_kernel,
        grid_spec=pltpu.PrefetchScalarGridSpec(
            num_scalar_prefetch=1,
            grid=(m // GATHER_ROWS,),
            in_specs=[pl.BlockSpec(memory_space=pl.ANY)],
            out_specs=pl.BlockSpec((GATHER_ROWS, d), lambda i, idx: (i, 0)),
            scratch_shapes=[pltpu.SemaphoreType.DMA(())]),
        out_shape=jax.ShapeDtypeStruct((m, d), src.dtype),
        compiler_params=_cparams(("arbitrary",)),
        name="gather_rows",
    )(idx, src)


def _experts_kernel(be_ref, nu_ref, x_ref, wg_ref, wu_ref, wd_ref, o_ref, wg_s, wu_s, wd_s):
    i = pl.program_id(0)
    e = be_ref[i]
    e_before = be_ref[jnp.maximum(i - 1, 0)]

    @pl.when(jnp.logical_or(i == 0, e != e_before))
    def _():
        wg_s[...] = wg_ref[0].astype(wg_s.dtype)
        wu_s[...] = wu_ref[0].astype(wu_s.dtype)
        wd_s[...] = wd_ref[0].astype(wd_s.dtype)

    @pl.when(i < nu_ref[0])
    def _():
        xb = x_ref[...].astype(MXU_DTYPE)
        gt = jnp.dot(xb, wg_s[...], preferred_element_type=F32)
        up = jnp.dot(xb, wu_s[...], preferred_element_type=F32)
        o_ref[...] = jnp.dot((_silu(gt) * up).astype(MXU_DTYPE), wd_s[...], preferred_element_type=F32)

    @pl.when(i >= nu_ref[0])
    def _():
        o_ref[...] = jnp.zeros_like(o_ref)


def experts_call(blk_exp, n_used, xs, w_gate, w_up, w_down):
    n_slots, d = xs.shape
    f = w_gate.shape[-1]
    wspec = lambda a: pl.BlockSpec((1,) + a.shape[1:], lambda i, be, nu: (be[i], 0, 0))
    return pl.pallas_call(
        _experts_kernel,
        grid_spec=pltpu.PrefetchScalarGridSpec(
            num_scalar_prefetch=2,
            grid=(n_slots // MOE_BLK,),
            in_specs=[pl.BlockSpec((MOE_BLK, d), lambda i, be, nu: (i, 0)),
                      wspec(w_gate), wspec(w_up), wspec(w_down)],
            out_specs=pl.BlockSpec((MOE_BLK, d), lambda i, be, nu: (i, 0)),
            scratch_shapes=[pltpu.VMEM((d, f), MXU_DTYPE), pltpu.VMEM((d, f), MXU_DTYPE),
                            pltpu.VMEM((f, d), MXU_DTYPE)]),
        out_shape=jax.ShapeDtypeStruct((n_slots, d), F32),
        compiler_params=_cparams(("arbitrary",)),
        name="experts",
    )(blk_exp, n_used, xs, w_gate, w_up, w_down)


def _combine_kernel(y0_ref, y1_ref, gate_ref, x_ref, mod_ref, lng_ref, lnb_ref, o_ref):
    mod = mod_ref[0, 0]
    gate = gate_ref[...]
    y = gate[:, 0:1] * y0_ref[...] + gate[:, 1:2] * y1_ref[...]
    o_ref[0] = _layer_norm_rows(DEEPNORM_ALPHA * x_ref[0] + mod[5:6] * y, lng_ref[...], lnb_ref[...])


def combine_call(yg, gate_cols, x1, modsel, ln_g, ln_b, ctx_tiles):
    bsz, tt, d = x1.shape
    nt = tt // TM
    n_tiles = bsz * nt
    return pl.pallas_call(
        _combine_kernel,
        grid=(bsz, nt),
        in_specs=[pl.BlockSpec((TM, d), lambda b, j: (b * nt + j, 0)),
                  pl.BlockSpec((TM, d), lambda b, j: (n_tiles + b * nt + j, 0)),
                  pl.BlockSpec((TM, TOP_K), lambda b, j: (b * nt + j, 0)),
                  pl.BlockSpec((1, TM, d), lambda b, j: (b, j, 0)),
                  pl.BlockSpec((1, 1, 6, d), lambda b, j: (b, (j >= ctx_tiles).astype(jnp.int32), 0, 0)),
                  pl.BlockSpec(ln_g.shape, lambda b, j: (0, 0)),
                  pl.BlockSpec(ln_b.shape, lambda b, j: (0, 0))],
        out_specs=pl.BlockSpec((1, TM, d), lambda b, j: (b, j, 0)),
        out_shape=jax.ShapeDtypeStruct((bsz, tt, d), F32),
        compiler_params=_cparams(("arbitrary", "arbitrary")),
        name="combine",
    )(yg, yg, gate_cols, x1, modsel, ln_g, ln_b)


def moe_block(x1, h2, modsel, ln_g, ln_b, router_wt, router_b, w_gate, w_up, w_down, ctx_tiles):
    bsz, tt, d = x1.shape
    n = bsz * tt
    hflat = h2.reshape(n, d)
    e_idx, gates = router_call(hflat, router_wt, router_b)

    n_asg = TOP_K * n
    flat_e = e_idx.reshape(n_asg)
    onehot = (flat_e[:, None] == jnp.arange(N_EXPERTS, dtype=jnp.int32)[None, :]).astype(jnp.int32)
    csum = jnp.cumsum(onehot, axis=0)
    rank = jnp.sum(onehot * csum, axis=1) - 1
    counts = csum[-1]
    padded = (counts + MOE_BLK - 1) // MOE_BLK * MOE_BLK
    pend = jnp.cumsum(padded)
    pstart = pend - padded
    dest = (pstart[flat_e] + rank).astype(jnp.int32)
    n_blocks = -(-n_asg // MOE_BLK) + N_EXPERTS
    n_slots = n_blocks * MOE_BLK
    tok = jnp.arange(n_asg, dtype=jnp.int32) % n
    slot_tok = jnp.zeros((n_slots,), jnp.int32).at[dest].set(tok)
    blk_start = jnp.arange(n_blocks, dtype=jnp.int32) * MOE_BLK
    blk_exp = jnp.minimum(jnp.searchsorted(pend, blk_start, side='right'), N_EXPERTS - 1).astype(jnp.int32)
    n_used = (pend[-1:] // MOE_BLK).astype(jnp.int32)

    xs = gather_rows(hflat, slot_tok)
    y_slots = experts_call(blk_exp, n_used, xs, w_gate, w_up, w_down)
    yg = gather_rows(y_slots, dest)
    return combine_call(yg, gates.T, x1, modsel, ln_g, ln_b, ctx_tiles)


def _inproj1_kernel(x_ref, mod_ref, w_ref, qg_ref, kg_ref, cos_ref, sin_ref, q_ref, k_ref, v_ref):
    mod = mod_ref[0, 0]
    hb = (x_ref[0] * (1.0 + mod[1:2]) + mod[0:1]).astype(MXU_DTYPE)
    cos = cos_ref[...]
    sin = sin_ref[...]

    def norm_rope(t, g):
        t = t * lax.rsqrt(jnp.mean(t * t, axis=-1, keepdims=True) + RMS_EPS) * g
        return t * cos + pltpu.roll(t, A_HEAD_DIM // 2, 1) * sin

    for j in range(A_HEADS):
        sl = slice(j * A_HEAD_DIM, (j + 1) * A_HEAD_DIM)
        t = jnp.dot(hb, w_ref[:, sl], preferred_element_type=F32)
        q_ref[0, :, sl] = (norm_rope(t, qg_ref[...]) * (A_HEAD_DIM ** -0.5)).astype(q_ref.dtype)
    for j in range(A_KV_HEADS):
        sl = slice(j * A_HEAD_DIM, (j + 1) * A_HEAD_DIM)
        t = jnp.dot(hb, w_ref[:, A_Q_W + j * A_HEAD_DIM:A_Q_W + (j + 1) * A_HEAD_DIM], preferred_element_type=F32)
        k_ref[0, :, sl] = norm_rope(t, kg_ref[...]).astype(k_ref.dtype)
    v_ref[0] = jnp.dot(hb, w_ref[:, A_Q_W + A_KV_W:], preferred_element_type=F32).astype(v_ref.dtype)


def inproj1_call(xa, modsel, w_in, q_g, k_g, cos2, sin2, ctx_tiles):
    bsz, ta, d = xa.shape
    nt = ta // TM
    tok = lambda w: pl.BlockSpec((1, TM, w), lambda b, j: (b, j, 0))
    const = lambda a: pl.BlockSpec(a.shape, lambda b, j: (0, 0))
    widths = (A_Q_W, A_KV_W, A_KV_W)
    return pl.pallas_call(
        _inproj1_kernel,
        grid=(bsz, nt),
        in_specs=[tok(d),
                  pl.BlockSpec((1, 1, 6, d), lambda b, j: (b, (j >= ctx_tiles).astype(jnp.int32), 0, 0)),
                  const(w_in), const(q_g), const(k_g),
                  pl.BlockSpec((TM, A_HEAD_DIM), lambda b, j: (j, 0)),
                  pl.BlockSpec((TM, A_HEAD_DIM), lambda b, j: (j, 0))],
        out_specs=[tok(w) for w in widths],
        out_shape=[jax.ShapeDtypeStruct((bsz, ta, w), MXU_DTYPE) for w in widths],
        compiler_params=_cparams(("arbitrary", "arbitrary")),
        name="inproj1",
    )(xa, modsel, w_in, q_g, k_g, cos2, sin2)


def _attn_kernel(q_ref, k_ref, v_ref, o_ref):
    rep = A_HEADS // A_KV_HEADS
    q = jnp.concatenate([q_ref[0, :, r * A_HEAD_DIM:(r + 1) * A_HEAD_DIM] for r in range(rep)], axis=0)
    rows = q.shape[0]
    n_kv = k_ref.shape[1] // ATT_TK

    def step(j, carry):
        m, l, acc = carry
        off = pl.multiple_of(j * ATT_TK, ATT_TK)
        kb = k_ref[0, pl.ds(off, ATT_TK), :]
        vb = v_ref[0, pl.ds(off, ATT_TK), :]
        s = lax.dot_general(q, kb, (((1,), (1,)), ((), ())), preferred_element_type=F32)
        m_new = jnp.maximum(m, jnp.max(s, axis=1, keepdims=True))
        p = jnp.exp(s - m_new)
        alpha = jnp.exp(m - m_new)
        l = alpha * l + jnp.sum(p, axis=1, keepdims=True)
        acc = alpha * acc + jnp.dot(p.astype(vb.dtype), vb, preferred_element_type=F32)
        return m_new, l, acc

    init = (jnp.full((rows, 1), -jnp.inf, F32), jnp.zeros((rows, 1), F32), jnp.zeros((rows, A_HEAD_DIM), F32))
    _, l, acc = lax.fori_loop(0, n_kv, step, init)
    out = acc / l
    for r in range(rep):
        o_ref[0, :, r * A_HEAD_DIM:(r + 1) * A_HEAD_DIM] = out[r * ATT_TQ:(r + 1) * ATT_TQ].astype(o_ref.dtype)


def attn_call(q, k, v, n_ctx):
    bsz, ta, _ = q.shape
    t_lat = ta - n_ctx
    gw = (A_HEADS // A_KV_HEADS) * A_HEAD_DIM
    q_off = n_ctx // ATT_TQ
    return pl.pallas_call(
        _attn_kernel,
        grid=(bsz, A_KV_HEADS, t_lat // ATT_TQ),
        in_specs=[pl.BlockSpec((1, ATT_TQ, gw), lambda b, g, i: (b, i + q_off, g)),
                  pl.BlockSpec((1, ta, A_HEAD_DIM), lambda b, g, i: (b, 0, g)),
                  pl.BlockSpec((1, ta, A_HEAD_DIM), lambda b, g, i: (b, 0, g))],
        out_specs=pl.BlockSpec((1, ATT_TQ, gw), lambda b, g, i: (b, i, g)),
        out_shape=jax.ShapeDtypeStruct((bsz, t_lat, A_Q_W), MXU_DTYPE),
        compiler_params=_cparams(("arbitrary", "arbitrary", "arbitrary")),
        name="attention",
    )(q, k, v)


def _outproj1_kernel(a_ref, x_ref, mod_ref, w_ref, lng_ref, lnb_ref, x1_ref, h2_ref):
    mod = mod_ref[0, 0]
    y = jnp.dot(a_ref[0], w_ref[...], preferred_element_type=F32)
    x1 = _layer_norm_rows(DEEPNORM_ALPHA * x_ref[0] + mod[2:3] * y, lng_ref[...], lnb_ref[...])
    x1_ref[0] = x1
    h2_ref[0] = x1 * (1.0 + mod[4:5]) + mod[3:4]


def outproj1_call(att, xa, modsel, w_out, ln_g, ln_b, ctx_tiles):
    bsz, t_lat, _ = att.shape
    d = xa.shape[-1]
    const = lambda a: pl.BlockSpec(a.shape, lambda b, j: (0, 0))
    return pl.pallas_call(
        _outproj1_kernel,
        grid=(bsz, t_lat // TM),
        in_specs=[pl.BlockSpec((1, TM, att.shape[-1]), lambda b, j: (b, j, 0)),
                  pl.BlockSpec((1, TM, d), lambda b, j: (b, j + ctx_tiles, 0)),
                  pl.BlockSpec((1, 1, 6, d), lambda b, j: (b, 1, 0, 0)),
                  const(w_out), const(ln_g), const(ln_b)],
        out_specs=[pl.BlockSpec((1, TM, d), lambda b, j: (b, j, 0))] * 2,
        out_shape=[jax.ShapeDtypeStruct((bsz, t_lat, d), F32)] * 2,
        compiler_params=_cparams(("arbitrary", "arbitrary")),
        name="outproj1",
    )(att, xa, modsel, w_out, ln_g, ln_b)


def _rope_tables(n_ctx, n_lat):
    rows = n_lat // GRID_W
    row = jnp.repeat(jnp.arange(rows), GRID_W).astype(F32)
    col = jnp.tile(jnp.arange(GRID_W), rows).astype(F32)
    n_freq = A_HEAD_DIM // 4
    inv = ROPE_THETA ** (-jnp.arange(n_freq, dtype=F32) / n_freq)
    ang = jnp.concatenate([row[:, None] * inv, col[:, None] * inv], -1)
    cos, sin = jnp.cos(ang), jnp.sin(ang)
    cos2 = jnp.concatenate([cos, cos], -1)
    sin2 = jnp.concatenate([-sin, sin], -1)
    cos2 = jnp.concatenate([jnp.ones((n_ctx, A_HEAD_DIM), F32), cos2], 0)
    sin2 = jnp.concatenate([jnp.zeros((n_ctx, A_HEAD_DIM), F32), sin2], 0)
    return cos2, sin2


def _lane_row(parts, width=GATE_LANES):
    row = jnp.concatenate([p.reshape(-1).astype(F32) for p in parts])
    return jnp.pad(row, (0, width - row.shape[0])).reshape(1, width)


def kernel(x, c, ctx, c_ctx, ada_w, ada_b, ln_g, ln_b, ab_w_in, ab_w_out, ml_ig_b, ml_fg_b, ml_norm_g,
           ssm_conv_w, ssm_conv_b, ssm_dt_b, ssm_a_log, ssm_d, ssm_norm_g, at_w_in, at_w_out, at_q_g, at_k_g,
           router_w, router_b, moe_w_gate, moe_w_up, moe_w_down):
    bsz, n_lat, d = x.shape
    n_ctx = ctx.shape[1]
    assert d == D_MODEL and bsz + 1 <= SUBLANES
    assert n_ctx % TM == 0 and n_lat % TM == 0 and n_lat % GRID_W == 0
    assert (n_ctx + n_lat) % ATT_TK == 0 and n_ctx % ATT_TQ == 0 and n_lat % ATT_TQ == 0
    ctx_tiles = n_ctx // TM
    ctx_chunks = n_ctx // CHUNK
    xa = jnp.concatenate([ctx, x], axis=1)

    crows = jnp.zeros((SUBLANES, d), F32).at[:bsz].set(c).at[bsz].set(c_ctx)
    mods = ada_call(crows, ada_w, ada_b)

    def mod_table(i):
        lat = mods[i, :bsz].reshape(bsz, 1, 6, d)
        cx = jnp.broadcast_to(mods[i, bsz].reshape(1, 1, 6, d), (bsz, 1, 6, d))
        return jnp.concatenate([cx, lat], axis=1)

    router_wt = router_w.T
    router_bc = router_b.reshape(N_EXPERTS, 1)

    modsel = mod_table(0)
    w_in = ab_w_in[0]
    s_q, s_k, s_v, s_o, s_ig, s_fg, s_z, s_xbc = (int(v) for v in
        (0, 512, 1024, 2048, 3072, 3072 + 8, 3072 + 16, 3072 + 16 + 1024))
    s_dt = s_xbc + S_CONV_CH
    w_big = jnp.concatenate([w_in[:, :s_ig], w_in[:, s_z:s_dt]], axis=1).astype(MXU_DTYPE)
    w_small = jnp.concatenate([w_in[:, s_ig:s_z], w_in[:, s_dt:]], axis=1)
    w_small = jnp.pad(w_small, ((0, 0), (0, GATE_LANES - w_small.shape[1])))
    q, k, v, o, z, xbc, gates = inproj0_call(xa, modsel, w_big, w_small, ctx_tiles)
    xbc_act = conv_call(xbc, ssm_conv_w[0], ssm_conv_b[0], ctx_tiles)

    grow = jnp.swapaxes(gates[:, :, :DT_OFF], 1, 2)
    dtrow = jnp.swapaxes(gates[:, :, DT_OFF:DT_OFF + N_DIR * S_HEADS], 1, 2)
    gate_b_row = _lane_row([ml_ig_b[0], ml_fg_b[0], ssm_dt_b[0]])
    gate_b_col = jnp.concatenate([ml_ig_b[0].reshape(-1), ml_fg_b[0].reshape(-1)]).reshape(-1, 1)
    a_neg = -jnp.exp(ssm_a_log[0].astype(F32)).reshape(-1)
    an_row = _lane_row([jnp.zeros((DT_OFF,), F32), a_neg])
    an_col = a_neg.reshape(-1, 1)
    dtb_col = ssm_dt_b[0].reshape(-1, 1)

    hm = None
    hs = None
    for dd in range(N_DIR):
        hm = mlstm_call(q, k, v, gates, grow, gate_b_row, gate_b_col, hm, rev=dd == 1, d=dd, ctx_chunks=ctx_chunks)
        hs = ssd_call(xbc_act, gates, dtrow, gate_b_row, dtb_col, an_row, an_col, hs, rev=dd == 1, d=dd,
                      ctx_chunks=ctx_chunks)

    dskip = jnp.repeat(ssm_d[0].astype(F32), S_HEAD_DIM).reshape(1, S_WIDTH)
    x1, h2 = outproj0_call(hm, o, hs, xbc_act, z, xa, modsel, ml_norm_g[0].reshape(1, -1), dskip,
                           ssm_norm_g[0].reshape(1, -1), ab_w_out[0].astype(MXU_DTYPE),
                           ln_g[0, 0].reshape(1, d), ln_b[0, 0].reshape(1, d), ctx_tiles)
    xa = moe_block(x1, h2, modsel, ln_g[0, 1].reshape(1, d), ln_b[0, 1].reshape(1, d), router_wt, router_bc,
                   moe_w_gate[0], moe_w_up[0], moe_w_down[0], ctx_tiles)

    modsel = mod_table(1)
    cos2, sin2 = _rope_tables(n_ctx, n_lat)
    qa, ka, va = inproj1_call(xa, modsel, at_w_in[0].astype(MXU_DTYPE), at_q_g[0].reshape(1, -1),
                              at_k_g[0].reshape(1, -1), cos2, sin2, ctx_tiles)
    att = attn_call(qa, ka, va, n_ctx)
    x1, h2 = outproj1_call(att, xa, modsel, at_w_out[0].astype(MXU_DTYPE), ln_g[1, 0].reshape(1, d),
                           ln_b[1, 0].reshape(1, d), ctx_tiles)
    return moe_block(x1, h2, modsel, ln_g[1, 1].reshape(1, d), ln_b[1, 1].reshape(1, d), router_wt, router_bc,
                     moe_w_gate[1], moe_w_up[1], moe_w_down[1], 0)
```

```python
import functools
import math

import jax
import jax.numpy as jnp
from jax import lax
from jax.experimental import pallas as pl
from jax.experimental.pallas import tpu as pltpu

F32 = jnp.float32
MXU_DTYPE = jnp.bfloat16
HIGHEST = lax.Precision.HIGHEST

D_MODEL = 1024
DEPTH = 2
GRID_W = 64
CHUNK = 128
M_HEADS = 4
M_QK_DIM = D_MODEL // 8
M_V_DIM = D_MODEL // 4
M_WIDTH = M_HEADS * M_V_DIM
S_HEADS = 16
S_HEAD_DIM = D_MODEL // 16
S_GROUPS = 2
S_HEADS_PER_GROUP = S_HEADS // S_GROUPS
S_STATE = 128
S_WIDTH = S_HEADS * S_HEAD_DIM
S_CONV_CH = S_WIDTH + 2 * S_GROUPS * S_STATE
CONV_K = 4
N_DIR = 2
A_HEADS = 8
A_KV_HEADS = 2
A_HEAD_DIM = D_MODEL // A_HEADS
A_Q_W = A_HEADS * A_HEAD_DIM
A_KV_W = A_KV_HEADS * A_HEAD_DIM
ROPE_THETA = 10000.0
N_EXPERTS = 16
N_EXPERT_GROUPS = 4
EXPERTS_PER_GROUP = N_EXPERTS // N_EXPERT_GROUPS
TOP_K = 2
DEEPNORM_ALPHA = (2 * DEPTH) ** 0.25
LN_EPS = 1e-5
RMS_EPS = 1e-6

LANES = 128
SUBLANES = 8
TM = 256
MOE_BLK = 256
DISPATCH_ROWS = 512
ROW_CHUNKS = D_MODEL // LANES
assert ROW_CHUNKS == SUBLANES
ATT_TQ = 256
ATT_TK = 768
VMEM_LIMIT = 56 * 1024 * 1024
GATE_LANES = 128
IG_OFF, FG_OFF, DT_OFF = 0, N_DIR * M_HEADS, 2 * N_DIR * M_HEADS
ATT_Q_SCALE = A_HEAD_DIM ** -0.5 * math.log2(math.e)


def _cparams(sem):
    return pltpu.CompilerParams(dimension_semantics=sem, vmem_limit_bytes=VMEM_LIMIT)


def _silu(x):
    return x / (1.0 + jnp.exp(-x))


def _sigmoid(x):
    return 1.0 / (1.0 + jnp.exp(-x))


def _softplus(x):
    return jnp.maximum(x, 0.0) + jnp.log(1.0 + jnp.exp(-jnp.abs(x)))


def _log_sigmoid(x):
    return jnp.minimum(x, 0.0) - jnp.log(1.0 + jnp.exp(-jnp.abs(x)))


def _layer_norm_rows(x, g, b):
    mu = jnp.mean(x, axis=-1, keepdims=True)
    xc = x - mu
    var = jnp.mean(xc * xc, axis=-1, keepdims=True)
    return xc * lax.rsqrt(var + LN_EPS) * g + b


def _mm(a, b):
    return jnp.dot(a.astype(MXU_DTYPE), b.astype(MXU_DTYPE), preferred_element_type=F32)


def _mm_nt(a, b):
    return lax.dot_general(a.astype(MXU_DTYPE), b.astype(MXU_DTYPE), (((1,), (1,)), ((), ())),
                           preferred_element_type=F32)


def _mm_tn(a, b):
    return lax.dot_general(a.astype(MXU_DTYPE), b.astype(MXU_DTYPE), (((0,), (0,)), ((), ())),
                           preferred_element_type=F32)


def _ada_kernel(c_ref, w_ref, b_ref, o_ref):
    s = _silu(c_ref[...])
    o_ref[0] = jnp.dot(s, w_ref[0], precision=HIGHEST, preferred_element_type=F32) + b_ref[0]


def ada_call(crows, ada_w, ada_b):
    depth, d, n6 = ada_w.shape
    tn = 1536
    return pl.pallas_call(
        _ada_kernel,
        grid=(depth, n6 // tn),
        in_specs=[pl.BlockSpec((SUBLANES, d), lambda i, j: (0, 0)),
                  pl.BlockSpec((1, d, tn), lambda i, j: (i, 0, j)),
                  pl.BlockSpec((1, 1, tn), lambda i, j: (i, 0, j))],
        out_specs=pl.BlockSpec((1, SUBLANES, tn), lambda i, j: (i, 0, j)),
        out_shape=jax.ShapeDtypeStruct((depth, SUBLANES, n6), F32),
        compiler_params=_cparams(("arbitrary", "arbitrary")),
        name="ada",
    )(crows, ada_w, ada_b.reshape(depth, 1, n6))


def _inproj0_kernel(x_ref, mod_ref, wb_ref, ws_ref, q_ref, k_ref, v_ref, o_ref, z_ref, xbc_ref, g_ref):
    x = x_ref[0]
    mod = mod_ref[0, 0]
    h = x * (1.0 + mod[1:2]) + mod[0:1]
    hb = h.astype(MXU_DTYPE)
    qk = M_HEADS * M_QK_DIM
    c0 = 0
    q_ref[0] = (jnp.dot(hb, wb_ref[:, c0:c0 + qk], preferred_element_type=F32)
                * (M_QK_DIM ** -0.5)).astype(q_ref.dtype)
    c0 += qk
    k_ref[0] = jnp.dot(hb, wb_ref[:, c0:c0 + qk], preferred_element_type=F32).astype(k_ref.dtype)
    c0 += qk
    v_ref[0] = jnp.dot(hb, wb_ref[:, c0:c0 + M_WIDTH], preferred_element_type=F32).astype(v_ref.dtype)
    c0 += M_WIDTH
    o_ref[0] = jnp.dot(hb, wb_ref[:, c0:c0 + M_WIDTH], preferred_element_type=F32)
    c0 += M_WIDTH
    z_ref[0] = jnp.dot(hb, wb_ref[:, c0:c0 + S_WIDTH], preferred_element_type=F32)
    c0 += S_WIDTH
    xbc_ref[0] = jnp.dot(hb, wb_ref[:, c0:c0 + S_CONV_CH], preferred_element_type=F32)
    g_ref[0] = jnp.dot(h, ws_ref[...], precision=HIGHEST, preferred_element_type=F32)


def inproj0_call(xa, modsel, w_big, w_small, ctx_tiles):
    bsz, ta, d = xa.shape
    nt = ta // TM
    qk = M_HEADS * M_QK_DIM
    widths = (qk, qk, M_WIDTH, M_WIDTH, S_WIDTH, S_CONV_CH, GATE_LANES)
    dtypes = (MXU_DTYPE, MXU_DTYPE, MXU_DTYPE, F32, F32, F32, F32)
    tok = lambda w: pl.BlockSpec((1, TM, w), lambda b, j: (b, j, 0))
    return pl.pallas_call(
        _inproj0_kernel,
        grid=(bsz, nt),
        in_specs=[tok(d),
                  pl.BlockSpec((1, 1, 6, d), lambda b, j: (b, (j >= ctx_tiles).astype(jnp.int32), 0, 0)),
                  pl.BlockSpec(w_big.shape, lambda b, j: (0, 0)),
                  pl.BlockSpec(w_small.shape, lambda b, j: (0, 0))],
        out_specs=[tok(w) for w in widths],
        out_shape=[jax.ShapeDtypeStruct((bsz, ta, w), dt) for w, dt in zip(widths, dtypes)],
        compiler_params=_cparams(("arbitrary", "arbitrary")),
        name="inproj0",
    )(xa, modsel, w_big, w_small)


def _conv_kernel(cur_ref, prev_ref, next_ref, w_ref, b_ref, o_ref, *, ctx_tiles, n_tiles):
    j = pl.program_id(1)
    has_prev = jnp.logical_and(j != 0, j != ctx_tiles)
    has_next = jnp.logical_and(j != ctx_tiles - 1, j != n_tiles - 1)
    prev = jnp.where(has_prev, prev_ref[0], 0.0)
    nxt = jnp.where(has_next, next_ref[0], 0.0)
    ext = jnp.concatenate([prev, cur_ref[0], nxt], axis=0)
    n = TM + 2 * SUBLANES
    w = w_ref[...]
    lo, hi = SUBLANES, SUBLANES + TM
    acc = ext[lo:hi] * w[2:3]
    acc = acc + pltpu.roll(ext, 2, 0)[lo:hi] * w[0:1]
    acc = acc + pltpu.roll(ext, 1, 0)[lo:hi] * w[1:2]
    acc = acc + pltpu.roll(ext, n - 1, 0)[lo:hi] * w[3:4]
    o_ref[0] = _silu(acc + b_ref[...])


def conv_call(xbc, conv_w, conv_b, ctx_tiles):
    bsz, ta, ch = xbc.shape
    nt = ta // TM
    r = TM // SUBLANES
    last = ta // SUBLANES - 1
    return pl.pallas_call(
        functools.partial(_conv_kernel, ctx_tiles=ctx_tiles, n_tiles=nt),
        grid=(bsz, nt),
        in_specs=[pl.BlockSpec((1, TM, ch), lambda b, j: (b, j, 0)),
                  pl.BlockSpec((1, SUBLANES, ch), lambda b, j: (b, jnp.maximum(j * r - 1, 0), 0)),
                  pl.BlockSpec((1, SUBLANES, ch), lambda b, j: (b, jnp.minimum((j + 1) * r, last), 0)),
                  pl.BlockSpec((CONV_K, ch), lambda b, j: (0, 0)),
                  pl.BlockSpec((1, ch), lambda b, j: (0, 0))],
        out_specs=pl.BlockSpec((1, TM, ch), lambda b, j: (b, j, 0)),
        out_shape=jax.ShapeDtypeStruct((bsz, ta, ch), F32),
        compiler_params=_cparams(("arbitrary", "arbitrary")),
        name="conv",
    )(xbc, xbc, xbc, conv_w, conv_b.reshape(1, ch))


def _chunk_order(i, rev, ctx_chunks, n_chunks):
    if not rev:
        return i
    return jnp.where(i < ctx_chunks, ctx_chunks - 1 - i, n_chunks - 1 - (i - ctx_chunks))


def _scan_masks(rev):
    r = lax.broadcasted_iota(jnp.int32, (CHUNK, CHUNK), 0)
    c = lax.broadcasted_iota(jnp.int32, (CHUNK, CHUNK), 1)
    mask = (c >= r) if rev else (c <= r)
    mask_t = (r >= c) if rev else (r <= c)
    return mask, mask.astype(F32), mask_t.astype(F32)


def _mlstm_kernel(*refs, rev, d, add_prev):
    if add_prev:
        q_ref, k_ref, v_ref, gc_ref, gr_ref, brow_ref, bcol_ref, prev_ref, o_ref, ct_s, n_s, m_s = refs
    else:
        q_ref, k_ref, v_ref, gc_ref, gr_ref, brow_ref, bcol_ref, o_ref, ct_s, n_s, m_s = refs
        prev_ref = None

    @pl.when(pl.program_id(1) == 0)
    def _():
        ct_s[...] = jnp.zeros_like(ct_s)
        n_s[...] = jnp.zeros_like(n_s)
        m_s[...] = jnp.zeros_like(m_s)

    mask, mask_f, mask_tf = _scan_masks(rev)
    end = 0 if rev else CHUNK - 1
    gcol = gc_ref[0] + brow_ref[...]
    grow = gr_ref[0] + bcol_ref[...]
    nh2 = N_DIR * M_HEADS
    lf_col = _log_sigmoid(gcol)
    lf_row = _log_sigmoid(grow[FG_OFF:FG_OFF + nh2])
    b_col_all = jnp.dot(mask_f, lf_col, precision=HIGHEST, preferred_element_type=F32)
    b_row_all = jnp.dot(lf_row, mask_tf, precision=HIGHEST, preferred_element_type=F32)

    for j in range(M_HEADS):
        ci = d * M_HEADS + j
        bcol = b_col_all[:, FG_OFF + ci:FG_OFF + ci + 1]
        brow = b_row_all[ci:ci + 1, :]
        igcol = gcol[:, IG_OFF + ci:IG_OFF + ci + 1]
        igrow = grow[IG_OFF + ci:IG_OFF + ci + 1, :]
        b_end = bcol[end:end + 1, :]
        m_prev = m_s[j][:, 0:1]
        n_prev = n_s[j]
        ct_prev = ct_s[j]
        q = q_ref[0, :, j * M_QK_DIM:(j + 1) * M_QK_DIM]
        k = k_ref[0, :, j * M_QK_DIM:(j + 1) * M_QK_DIM]
        v = v_ref[0, :, j * M_V_DIM:(j + 1) * M_V_DIM]
        qf = q.astype(F32)
        kf = k.astype(F32)
        vf = v.astype(F32)

        dmat = jnp.where(mask, bcol - brow + igrow, -jnp.inf)
        inter = bcol + m_prev
        m_t = jnp.maximum(inter, jnp.max(dmat, axis=1, keepdims=True))
        sc = _mm_nt(q, k) * jnp.exp(dmat - m_t)
        a_in = jnp.exp(inter - m_t)
        num = _mm(sc, v) + a_in * _mm(q, ct_prev)
        den = jnp.sum(sc, axis=1, keepdims=True) + a_in * jnp.sum(qf * n_prev, axis=1, keepdims=True)
        h = num / jnp.maximum(jnp.abs(den), jnp.exp(-m_t))
        sl = slice(j * M_V_DIM, (j + 1) * M_V_DIM)
        if add_prev:
            h = h + prev_ref[0, :, sl]
        o_ref[0, :, sl] = h

        g_col = b_end - bcol + igcol
        g_row = b_end - brow + igrow
        g_max = jnp.max(g_row, axis=1, keepdims=True)
        w_col = jnp.exp(g_col - g_max)
        d_ct = _mm_tn(k, vf * w_col)
        d_n = jnp.sum(kf * w_col, axis=0, keepdims=True)
        m_new = jnp.maximum(b_end + m_prev, g_max)
        a = jnp.exp(b_end + m_prev - m_new)
        s = jnp.exp(g_max - m_new)
        ct_s[j] = a * ct_prev + s * d_ct
        n_s[j] = a * n_prev + s * d_n
        m_s[j] = jnp.broadcast_to(m_new, (1, LANES))


def mlstm_call(q, k, v, gcol, grow, bias_row, bias_col, prev, *, rev, d, ctx_chunks):
    bsz, ta, _ = q.shape
    nc = ta // CHUNK
    order = lambda i: _chunk_order(i, rev, ctx_chunks, nc)
    tok = lambda w: pl.BlockSpec((1, CHUNK, w), lambda b, i: (b, order(i), 0))
    in_specs = [tok(q.shape[-1]), tok(k.shape[-1]), tok(v.shape[-1]), tok(GATE_LANES),
                pl.BlockSpec((1, grow.shape[1], CHUNK), lambda b, i: (b, 0, order(i))),
                pl.BlockSpec(bias_row.shape, lambda b, i: (0, 0)),
                pl.BlockSpec(bias_col.shape, lambda b, i: (0, 0))]
    args = [q, k, v, gcol, grow, bias_row, bias_col]
    if prev is not None:
        in_specs.append(tok(M_WIDTH))
        args.append(prev)
    return pl.pallas_call(
        functools.partial(_mlstm_kernel, rev=rev, d=d, add_prev=prev is not None),
        grid=(bsz, nc),
        in_specs=in_specs,
        out_specs=tok(M_WIDTH),
        out_shape=jax.ShapeDtypeStruct((bsz, ta, M_WIDTH), F32),
        scratch_shapes=[pltpu.VMEM((M_HEADS, M_QK_DIM, M_V_DIM), F32),
                        pltpu.VMEM((M_HEADS, 1, M_QK_DIM), F32),
                        pltpu.VMEM((M_HEADS, 1, LANES), F32)],
        compiler_params=_cparams(("arbitrary", "arbitrary")),
        name="mlstm_rev" if rev else "mlstm_fwd",
    )(*args)


def _ssd_kernel(*refs, rev, d, add_prev):
    if add_prev:
        x_ref, gc_ref, gr_ref, dtb_row_ref, dtb_col_ref, an_row_ref, an_col_ref, prev_ref, o_ref, ht_s = refs
    else:
        x_ref, gc_ref, gr_ref, dtb_row_ref, dtb_col_ref, an_row_ref, an_col_ref, o_ref, ht_s = refs
        prev_ref = None

    @pl.when(pl.program_id(1) == 0)
    def _():
        ht_s[...] = jnp.zeros_like(ht_s)

    mask, mask_f, mask_tf = _scan_masks(rev)
    end = 0 if rev else CHUNK - 1
    lane = lax.broadcasted_iota(jnp.int32, (CHUNK, LANES), 1)
    first_half = lane < S_HEAD_DIM
    dt_col = _softplus(gc_ref[0] + dtb_row_ref[...])
    dt_row = _softplus(gr_ref[0] + dtb_col_ref[...])
    acs_col = jnp.dot(mask_f, dt_col * an_row_ref[...], precision=HIGHEST, preferred_element_type=F32)
    acs_row = jnp.dot(dt_row * an_col_ref[...], mask_tf, precision=HIGHEST, preferred_element_type=F32)
    a_end_row = acs_col[end:end + 1, :]
    e_cs = jnp.exp(acs_col)
    e_rem = jnp.exp(a_end_row - acs_col)
    e_end = jnp.exp(a_end_row)
    gw = S_HEADS_PER_GROUP * S_HEAD_DIM
    pairs = S_HEADS_PER_GROUP // 2

    def pick(arr, la):
        return jnp.where(first_half[:arr.shape[0]], arr[:, la:la + 1], arr[:, la + 1:la + 2])

    for g in range(S_GROUPS):
        bm = x_ref[0, :, S_WIDTH + g * S_STATE:S_WIDTH + (g + 1) * S_STATE]
        cm = x_ref[0, :, S_WIDTH + (S_GROUPS + g) * S_STATE:S_WIDTH + (S_GROUPS + g + 1) * S_STATE]
        cb = _mm_nt(cm, bm)
        ht_prev = ht_s[g]
        y_inter = _mm(cm, ht_prev)
        xw_parts = []
        decay_parts = []
        for p in range(pairs):
            h0 = g * S_HEADS_PER_GROUP + 2 * p
            la = DT_OFF + d * S_HEADS + h0
            ra = d * S_HEADS + h0
            lhs = []
            for u in range(2):
                seg = acs_col[:, la + u:la + u + 1] - acs_row[ra + u:ra + u + 1, :]
                dec = jnp.exp(jnp.where(mask, seg, -jnp.inf))
                lhs.append((cb * dec).astype(MXU_DTYPE))
            xs = x_ref[0, :, h0 * S_HEAD_DIM:(h0 + 2) * S_HEAD_DIM]
            xsd = xs * pick(dt_col, la)
            rhs = jnp.concatenate([jnp.where(first_half, xsd, 0.0), jnp.where(first_half, 0.0, xsd)],
                                  axis=0).astype(MXU_DTYPE)
            y = jnp.dot(jnp.concatenate(lhs, axis=1), rhs, preferred_element_type=F32)
            y = y + y_inter[:, p * LANES:(p + 1) * LANES] * pick(e_cs, la)
            sl = slice(h0 * S_HEAD_DIM, (h0 + 2) * S_HEAD_DIM)
            if add_prev:
                y = y + prev_ref[0, :, sl]
            o_ref[0, :, sl] = y
            xw_parts.append(xsd * pick(e_rem, la))
            decay_parts.append(pick(e_end, la))
        xw = jnp.concatenate(xw_parts, axis=1)
        decay = jnp.concatenate(decay_parts, axis=1)
        ht_s[g] = decay * ht_prev + _mm_tn(bm, xw)


def ssd_call(xbc_act, gcol, dtrow, dtb_row, dtb_col, an_row, an_col, prev, *, rev, d, ctx_chunks):
    bsz, ta, ch = xbc_act.shape
    nc = ta // CHUNK
    order = lambda i: _chunk_order(i, rev, ctx_chunks, nc)
    tok = lambda w: pl.BlockSpec((1, CHUNK, w), lambda b, i: (b, order(i), 0))
    const = lambda a: pl.BlockSpec(a.shape, lambda b, i: (0, 0))
    in_specs = [tok(ch), tok(GATE_LANES),
                pl.BlockSpec((1, dtrow.shape[1], CHUNK), lambda b, i: (b, 0, order(i))),
                const(dtb_row), const(dtb_col), const(an_row), const(an_col)]
    args = [xbc_act, gcol, dtrow, dtb_row, dtb_col, an_row, an_col]
    if prev is not None:
        in_specs.append(tok(S_WIDTH))
        args.append(prev)
    return pl.pallas_call(
        functools.partial(_ssd_kernel, rev=rev, d=d, add_prev=prev is not None),
        grid=(bsz, nc),
        in_specs=in_specs,
        out_specs=tok(S_WIDTH),
        out_shape=jax.ShapeDtypeStruct((bsz, ta, S_WIDTH), F32),
        scratch_shapes=[pltpu.VMEM((S_GROUPS, S_STATE, S_HEADS_PER_GROUP * S_HEAD_DIM), F32)],
        compiler_params=_cparams(("arbitrary", "arbitrary")),
        name="ssd_rev" if rev else "ssd_fwd",
    )(*args)


def _store_row_tiles(ref, val):
    for s in range(ROW_CHUNKS):
        ref[pl.ds(s, val.shape[0], stride=ROW_CHUNKS), :] = val[:, s * LANES:(s + 1) * LANES]


def _load_row_tiles(ref, rows):
    return jnp.concatenate([ref[pl.ds(s, rows, stride=ROW_CHUNKS), :] for s in range(ROW_CHUNKS)], axis=1)


def _finish_sublayer(x, y, mod, lng_ref, lnb_ref, rw_ref, rb_ref, x1_ref, h2_ref, e_ref, g_ref):
    x1 = _layer_norm_rows(DEEPNORM_ALPHA * x + mod[2:3] * y, lng_ref[...], lnb_ref[...])
    x1_ref[0] = x1
    h2 = x1 * (1.0 + mod[4:5]) + mod[3:4]
    _store_row_tiles(h2_ref, h2)
    e, g = _route(h2, rw_ref, rb_ref)
    e_ref[...] = e
    g_ref[...] = g


def _outproj0_kernel(hm_ref, o_ref, hs_ref, xs_ref, z_ref, x_ref, mod_ref, mg_ref, dsk_ref, sg_ref, w_ref,
                     lng_ref, lnb_ref, rw_ref, rb_ref, x1_ref, h2_ref, e_ref, g_ref):
    mod = mod_ref[0, 0]
    parts = []
    for j in range(M_HEADS):
        sl = slice(j * M_V_DIM, (j + 1) * M_V_DIM)
        hj = hm_ref[0, :, sl]
        mu = jnp.mean(hj, axis=-1, keepdims=True)
        hc = hj - mu
        var = jnp.mean(hc * hc, axis=-1, keepdims=True)
        parts.append(hc * lax.rsqrt(var + LN_EPS) * mg_ref[:, sl] * _sigmoid(o_ref[0, :, sl]))
    ym = jnp.concatenate(parts, axis=1)
    ys = (hs_ref[0] + dsk_ref[...] * xs_ref[0]) * _silu(z_ref[0])
    ys = ys * lax.rsqrt(jnp.mean(ys * ys, axis=-1, keepdims=True) + RMS_EPS) * sg_ref[...]
    y = (jnp.dot(ym.astype(MXU_DTYPE), w_ref[:M_WIDTH, :], preferred_element_type=F32)
         + jnp.dot(ys.astype(MXU_DTYPE), w_ref[M_WIDTH:, :], preferred_element_type=F32))
    _finish_sublayer(x_ref[0], y, mod, lng_ref, lnb_ref, rw_ref, rb_ref, x1_ref, h2_ref, e_ref, g_ref)


def _sublayer_out(bsz, tt, d):
    nt = tt // TM
    n = bsz * tt
    specs = [pl.BlockSpec((1, TM, d), lambda b, j: (b, j, 0)),
             pl.BlockSpec((TM * ROW_CHUNKS, LANES), lambda b, j: (b * nt + j, 0)),
             pl.BlockSpec((TOP_K, TM), lambda b, j: (0, b * nt + j)),
             pl.BlockSpec((TOP_K, TM), lambda b, j: (0, b * nt + j))]
    shapes = [jax.ShapeDtypeStruct((bsz, tt, d), F32), jax.ShapeDtypeStruct((n * ROW_CHUNKS, LANES), F32),
              jax.ShapeDtypeStruct((TOP_K, n), jnp.int32), jax.ShapeDtypeStruct((TOP_K, n), F32)]
    return specs, shapes


def outproj0_call(hm, o, hs, xbc_act, z, xa, modsel, m_norm_g, dskip, s_norm_g, w_out, ln_g, ln_b, router_wt,
                  router_b, ctx_tiles):
    bsz, ta, d = xa.shape
    nt = ta // TM
    tok = lambda w: pl.BlockSpec((1, TM, w), lambda b, j: (b, j, 0))
    const = lambda a: pl.BlockSpec(a.shape, lambda b, j: (0, 0))
    out_specs, out_shape = _sublayer_out(bsz, ta, d)
    return pl.pallas_call(
        _outproj0_kernel,
        grid=(bsz, nt),
        in_specs=[tok(M_WIDTH), tok(M_WIDTH), tok(S_WIDTH), tok(S_WIDTH), tok(S_WIDTH), tok(d),
                  pl.BlockSpec((1, 1, 6, d), lambda b, j: (b, (j >= ctx_tiles).astype(jnp.int32), 0, 0)),
                  const(m_norm_g), const(dskip), const(s_norm_g), const(w_out), const(ln_g), const(ln_b),
                  const(router_wt), const(router_b)],
        out_specs=out_specs,
        out_shape=out_shape,
        compiler_params=_cparams(("arbitrary", "arbitrary")),
        name="outproj0",
    )(hm, o, hs, xbc_act, z, xa, modsel, m_norm_g, dskip, s_norm_g, w_out, ln_g, ln_b, router_wt, router_b)


def _top2(vals, probs):
    v1, i1, p1 = vals[0], jnp.zeros_like(vals[0], dtype=jnp.int32), probs[0]
    for i in range(1, len(vals)):
        better = vals[i] > v1
        v1 = jnp.where(better, vals[i], v1)
        i1 = jnp.where(better, i, i1)
        p1 = jnp.where(better, probs[i], p1)
    v2 = jnp.full_like(vals[0], -jnp.inf)
    i2 = jnp.zeros_like(i1)
    p2 = jnp.zeros_like(p1)
    for i in range(len(vals)):
        better = jnp.logical_and(i1 != i, vals[i] > v2)
        v2 = jnp.where(better, vals[i], v2)
        i2 = jnp.where(better, i, i2)
        p2 = jnp.where(better, probs[i], p2)
    return v1, i1, p1, v2, i2, p2


def _route(h, wt_ref, b_ref):
    logits = lax.dot_general(wt_ref[...], h, (((1,), (1,)), ((), ())), precision=HIGHEST,
                             preferred_element_type=F32)
    mx = jnp.max(logits, axis=0, keepdims=True)
    ex = jnp.exp(logits - mx)
    probs = ex / jnp.sum(ex, axis=0, keepdims=True)
    sel = probs + b_ref[...]
    best = None
    for g in range(N_EXPERT_GROUPS):
        rows = range(g * EXPERTS_PER_GROUP, (g + 1) * EXPERTS_PER_GROUP)
        v1, i1, p1, v2, i2, p2 = _top2([sel[r:r + 1] for r in rows], [probs[r:r + 1] for r in rows])
        cand = (v1 + v2, i1 + g * EXPERTS_PER_GROUP, p1, i2 + g * EXPERTS_PER_GROUP, p2)
        if best is None:
            best = cand
        else:
            better = cand[0] > best[0]
            best = tuple(jnp.where(better, c, o) for c, o in zip(cand, best))
    _, e1, p1, e2, p2 = best
    tot = p1 + p2
    return jnp.concatenate([e1, e2], axis=0), jnp.concatenate([p1 / tot, p2 / tot], axis=0)


def _row_tile(ref, r):
    return ref.at[pl.ds(pl.multiple_of(r * ROW_CHUNKS, ROW_CHUNKS), ROW_CHUNKS)]


def _dispatch_kernel(dest_ref, zblk_ref, h_ref, xs_ref, zero_s, sem, zsem, *, n_tok):
    base = pl.program_id(0) * DISPATCH_ROWS
    blk_rows = MOE_BLK * ROW_CHUNKS

    @pl.when(pl.program_id(0) == 0)
    def _():
        zero_s[...] = jnp.zeros_like(zero_s)

        def zero_copy(t):
            start = pl.multiple_of(zblk_ref[t] * blk_rows, blk_rows)
            return pltpu.make_async_copy(zero_s, xs_ref.at[pl.ds(start, blk_rows)], zsem)

        for t in range(2 * N_EXPERTS):
            pl.when(zblk_ref[t] >= 0)(lambda t=t: zero_copy(t).start())
        for t in range(2 * N_EXPERTS):
            pl.when(zblk_ref[t] >= 0)(lambda t=t: zero_copy(t).wait())

    def slot_copy(r, slot):
        return pltpu.make_async_copy(_row_tile(h_ref, r), _row_tile(xs_ref, slot), sem)

    def issue(r, carry):
        for c in range(TOP_K):
            slot_copy(r, dest_ref[c * n_tok + base + r]).start()
        return carry

    def drain(r, carry):
        for c in range(TOP_K):
            slot_copy(r, 0).wait()
        return carry

    lax.fori_loop(0, DISPATCH_ROWS, issue, 0, unroll=8)
    lax.fori_loop(0, DISPATCH_ROWS, drain, 0, unroll=8)


def dispatch_call(dest, zero_blocks, h2t, n_slots):
    n_tok = h2t.shape[0] // ROW_CHUNKS
    return pl.pallas_call(
        functools.partial(_dispatch_kernel, n_tok=n_tok),
        grid_spec=pltpu.PrefetchScalarGridSpec(
            num_scalar_prefetch=2,
            grid=(n_tok // DISPATCH_ROWS,),
            in_specs=[pl.BlockSpec((DISPATCH_ROWS * ROW_CHUNKS, LANES), lambda i, dest, zb: (i, 0))],
            out_specs=pl.BlockSpec(memory_space=pl.ANY),
            scratch_shapes=[pltpu.VMEM((MOE_BLK * ROW_CHUNKS, LANES), h2t.dtype),
                            pltpu.SemaphoreType.DMA(()), pltpu.SemaphoreType.DMA(())]),
        out_shape=jax.ShapeDtypeStruct((n_slots * ROW_CHUNKS, LANES), h2t.dtype),
        compiler_params=_cparams(("arbitrary",)),
        name="dispatch",
    )(dest, zero_blocks, h2t)


def _experts_kernel(be_ref, cnt_ref, x_ref, wg_ref, wu_ref, wd_ref, o_ref, wg_s, wu_s, wd_s):
    i = pl.program_id(0)
    e = be_ref[i]
    e_before = be_ref[jnp.maximum(i - 1, 0)]
    cnt = cnt_ref[i]

    @pl.when(jnp.logical_or(i == 0, e != e_before))
    def _():
        wg_s[...] = wg_ref[0, 0].astype(wg_s.dtype)
        wu_s[...] = wu_ref[0, 0].astype(wu_s.dtype)
        wd_s[...] = wd_ref[0, 0].astype(wd_s.dtype)

    @pl.when(cnt > 0)
    def _():
        xb = _load_row_tiles(x_ref, MOE_BLK).astype(MXU_DTYPE)
        gt = jnp.dot(xb, wg_s[...], preferred_element_type=F32)
        up = jnp.dot(xb, wu_s[...], preferred_element_type=F32)
        y = jnp.dot((_silu(gt) * up).astype(MXU_DTYPE), wd_s[...], preferred_element_type=F32)
        _store_row_tiles(o_ref, y)

    @pl.when(cnt == 0)
    def _():
        o_ref[...] = jnp.zeros_like(o_ref)


def experts_call(blk_exp, blk_cnt, xs, w_gate, w_up, w_down, layer):
    n_slots = xs.shape[0] // ROW_CHUNKS
    d, f = w_gate.shape[-2:]
    wspec = lambda a: pl.BlockSpec((1, 1) + a.shape[2:], lambda i, be, cnt: (layer, be[i], 0, 0))
    blk = pl.BlockSpec((MOE_BLK * ROW_CHUNKS, LANES), lambda i, be, cnt: (i, 0))
    return pl.pallas_call(
        _experts_kernel,
        grid_spec=pltpu.PrefetchScalarGridSpec(
            num_scalar_prefetch=2,
            grid=(n_slots // MOE_BLK,),
            in_specs=[blk, wspec(w_gate), wspec(w_up), wspec(w_down)],
            out_specs=blk,
            scratch_shapes=[pltpu.VMEM((d, f), MXU_DTYPE), pltpu.VMEM((d, f), MXU_DTYPE),
                            pltpu.VMEM((f, d), MXU_DTYPE)]),
        out_shape=jax.ShapeDtypeStruct(xs.shape, F32),
        compiler_params=_cparams(("arbitrary",)),
        name="experts",
    )(blk_exp, blk_cnt, xs, w_gate, w_up, w_down)


def _combine_kernel(dest_ref, y_ref, gate_ref, x_ref, mod_ref, lng_ref, lnb_ref, o_ref, ybuf, sem, *, n_tok, nt):
    base = (pl.program_id(0) * nt + pl.program_id(1)) * TM

    def row_copy(r, c, slot):
        return pltpu.make_async_copy(_row_tile(y_ref, slot), _row_tile(ybuf.at[c], r), sem)

    def issue(r, carry):
        for c in range(TOP_K):
            row_copy(r, c, dest_ref[c * n_tok + base + r]).start()
        return carry

    def drain(r, carry):
        for c in range(TOP_K):
            row_copy(r, c, 0).wait()
        return carry

    lax.fori_loop(0, TM, issue, 0, unroll=8)
    lax.fori_loop(0, TM, drain, 0, unroll=8)
    mod = mod_ref[0, 0]
    gate = gate_ref[...]
    y = gate[:, 0:1] * _load_row_tiles(ybuf.at[0], TM) + gate[:, 1:2] * _load_row_tiles(ybuf.at[1], TM)
    o_ref[0] = _layer_norm_rows(DEEPNORM_ALPHA * x_ref[0] + mod[5:6] * y, lng_ref[...], lnb_ref[...])


def combine_call(dest, y_slots, gate_cols, x1, modsel, ln_g, ln_b, ctx_tiles):
    bsz, tt, d = x1.shape
    nt = tt // TM
    return pl.pallas_call(
        functools.partial(_combine_kernel, n_tok=bsz * tt, nt=nt),
        grid_spec=pltpu.PrefetchScalarGridSpec(
            num_scalar_prefetch=1,
            grid=(bsz, nt),
            in_specs=[pl.BlockSpec(memory_space=pl.ANY),
                      pl.BlockSpec((TM, TOP_K), lambda b, j, dest: (b * nt + j, 0)),
                      pl.BlockSpec((1, TM, d), lambda b, j, dest: (b, j, 0)),
                      pl.BlockSpec((1, 1, 6, d),
                                   lambda b, j, dest: (b, (j >= ctx_tiles).astype(jnp.int32), 0, 0)),
                      pl.BlockSpec(ln_g.shape, lambda b, j, dest: (0, 0)),
                      pl.BlockSpec(ln_b.shape, lambda b, j, dest: (0, 0))],
            out_specs=pl.BlockSpec((1, TM, d), lambda b, j, dest: (b, j, 0)),
            scratch_shapes=[pltpu.VMEM((TOP_K, TM * ROW_CHUNKS, LANES), F32), pltpu.SemaphoreType.DMA(())]),
        out_shape=jax.ShapeDtypeStruct((bsz, tt, d), F32),
        compiler_params=_cparams(("arbitrary", "arbitrary")),
        name="combine",
    )(dest, y_slots, gate_cols, x1, modsel, ln_g, ln_b)


def moe_block(x1, h2t, e_idx, gates, modsel, ln_g, ln_b, w_gate, w_up, w_down, layer, ctx_tiles):
    bsz, tt, d = x1.shape
    n = bsz * tt

    n_asg = TOP_K * n
    flat_e = e_idx.reshape(n_asg)
    onehot = (flat_e[:, None] == jnp.arange(N_EXPERTS, dtype=jnp.int32)[None, :]).astype(jnp.int32)
    csum = jnp.cumsum(onehot, axis=0)
    rank = jnp.sum(onehot * csum, axis=1) - 1
    counts = csum[-1]
    padded = (counts + MOE_BLK - 1) // MOE_BLK * MOE_BLK
    pend = jnp.cumsum(padded)
    pstart = pend - padded
    dest = (pstart[flat_e] + rank).astype(jnp.int32)
    n_blocks = -(-n_asg // MOE_BLK) + N_EXPERTS
    blk_start = jnp.arange(n_blocks, dtype=jnp.int32) * MOE_BLK
    blk_exp = jnp.sum((pend[None, :] <= blk_start[:, None]).astype(jnp.int32), axis=1)
    blk_exp = jnp.minimum(blk_exp, N_EXPERTS - 1)
    blk_cnt = jnp.clip(counts[blk_exp] - (blk_start - pstart[blk_exp]), 0, MOE_BLK).astype(jnp.int32)

    part = jnp.where(counts % MOE_BLK != 0, pend // MOE_BLK - 1, -1)
    tail = pend[-1] // MOE_BLK + jnp.arange(N_EXPERTS, dtype=jnp.int32)
    tail = jnp.where(tail < n_blocks, tail, -1)
    zero_blocks = jnp.concatenate([part, tail]).astype(jnp.int32)

    xs = dispatch_call(dest, zero_blocks, h2t, n_blocks * MOE_BLK)
    y_slots = experts_call(blk_exp, blk_cnt, xs, w_gate, w_up, w_down, layer)
    return combine_call(dest, y_slots, gates.T, x1, modsel, ln_g, ln_b, ctx_tiles)


def _inproj1_kernel(x_ref, mod_ref, w_ref, qg_ref, kg_ref, cos_ref, sin_ref, q_ref, k_ref, v_ref):
    mod = mod_ref[0, 0]
    hb = (x_ref[0] * (1.0 + mod[1:2]) + mod[0:1]).astype(MXU_DTYPE)
    cos = cos_ref[...]
    sin = sin_ref[...]

    def norm_rope(t, g):
        t = t * lax.rsqrt(jnp.mean(t * t, axis=-1, keepdims=True) + RMS_EPS) * g
        return t * cos + pltpu.roll(t, A_HEAD_DIM // 2, 1) * sin

    for j in range(A_HEADS):
        sl = slice(j * A_HEAD_DIM, (j + 1) * A_HEAD_DIM)
        t = jnp.dot(hb, w_ref[:, sl], preferred_element_type=F32)
        q_ref[0, :, sl] = (norm_rope(t, qg_ref[...]) * ATT_Q_SCALE).astype(q_ref.dtype)
    for j in range(A_KV_HEADS):
        sl = slice(j * A_HEAD_DIM, (j + 1) * A_HEAD_DIM)
        t = jnp.dot(hb, w_ref[:, A_Q_W + j * A_HEAD_DIM:A_Q_W + (j + 1) * A_HEAD_DIM], preferred_element_type=F32)
        k_ref[0, :, sl] = norm_rope(t, kg_ref[...]).astype(k_ref.dtype)
    v_ref[0] = jnp.dot(hb, w_ref[:, A_Q_W + A_KV_W:], preferred_element_type=F32).astype(v_ref.dtype)


def inproj1_call(xa, modsel, w_in, q_g, k_g, cos2, sin2, ctx_tiles):
    bsz, ta, d = xa.shape
    nt = ta // TM
    tok = lambda w: pl.BlockSpec((1, TM, w), lambda b, j: (b, j, 0))
    const = lambda a: pl.BlockSpec(a.shape, lambda b, j: (0, 0))
    widths = (A_Q_W, A_KV_W, A_KV_W)
    return pl.pallas_call(
        _inproj1_kernel,
        grid=(bsz, nt),
        in_specs=[tok(d),
                  pl.BlockSpec((1, 1, 6, d), lambda b, j: (b, (j >= ctx_tiles).astype(jnp.int32), 0, 0)),
                  const(w_in), const(q_g), const(k_g),
                  pl.BlockSpec((TM, A_HEAD_DIM), lambda b, j: (j, 0)),
                  pl.BlockSpec((TM, A_HEAD_DIM), lambda b, j: (j, 0))],
        out_specs=[tok(w) for w in widths],
        out_shape=[jax.ShapeDtypeStruct((bsz, ta, w), MXU_DTYPE) for w in widths],
        compiler_params=_cparams(("arbitrary", "arbitrary")),
        name="inproj1",
    )(xa, modsel, w_in, q_g, k_g, cos2, sin2)


def _attn_kernel(q_ref, k_ref, vt_ref, o_ref, s_buf):
    rep = A_HEADS // A_KV_HEADS
    qts = [q_ref[0, :, r * A_HEAD_DIM:(r + 1) * A_HEAD_DIM].astype(F32).T.astype(MXU_DTYPE) for r in range(rep)]
    n_kv = vt_ref.shape[2]

    def scores(j, slot):
        kb = k_ref[0, pl.ds(pl.multiple_of(j * ATT_TK, ATT_TK), ATT_TK), :]
        for r in range(rep):
            s_buf[slot, r] = jnp.dot(kb, qts[r], preferred_element_type=F32)

    def consume(j, slot, stats):
        vt = vt_ref[0, 0, j]
        new = []
        for r in range(rep):
            m, l, acc = stats[r]
            s = s_buf[slot, r]
            m_new = jnp.maximum(m, jnp.max(s, axis=0, keepdims=True))
            p = jnp.exp2(s - m_new)
            alpha = jnp.exp2(m - m_new)
            l = alpha * l + jnp.sum(p, axis=0, keepdims=True)
            acc = alpha * acc + jnp.dot(vt, p.astype(vt.dtype), preferred_element_type=F32)
            new.append((m_new, l, acc))
        return tuple(new)

    def pair(i, stats):
        j = 2 * i
        scores(j + 1, 1)
        stats = consume(j, 0, stats)
        scores(j + 2, 0)
        return consume(j + 1, 1, stats)

    stats = tuple((jnp.full((1, ATT_TQ), -jnp.inf, F32), jnp.zeros((1, ATT_TQ), F32),
                   jnp.zeros((A_HEAD_DIM, ATT_TQ), F32)) for _ in range(rep))
    scores(0, 0)
    n_pairs = (n_kv - 1) // 2
    stats = lax.fori_loop(0, n_pairs, pair, stats)
    if n_kv % 2 == 0:
        scores(n_kv - 1, 1)
        stats = consume(n_kv - 2, 0, stats)
        final = consume(n_kv - 1, 1, stats)
    else:
        final = consume(n_kv - 1, 0, stats)
    for r in range(rep):
        _, l, acc = final[r]
        o_ref[0, :, r * A_HEAD_DIM:(r + 1) * A_HEAD_DIM] = (acc / l).T.astype(o_ref.dtype)


def attn_call(q, k, v, n_ctx):
    bsz, ta, _ = q.shape
    t_lat = ta - n_ctx
    gw = (A_HEADS // A_KV_HEADS) * A_HEAD_DIM
    q_off = n_ctx // ATT_TQ
    n_kv = ta // ATT_TK
    vt = v.reshape(bsz, n_kv, ATT_TK, A_KV_HEADS, A_HEAD_DIM).transpose(0, 3, 1, 4, 2)
    return pl.pallas_call(
        _attn_kernel,
        grid=(bsz, A_KV_HEADS, t_lat // ATT_TQ),
        in_specs=[pl.BlockSpec((1, ATT_TQ, gw), lambda b, g, i: (b, i + q_off, g)),
                  pl.BlockSpec((1, ta, A_HEAD_DIM), lambda b, g, i: (b, 0, g)),
                  pl.BlockSpec((1, 1, n_kv, A_HEAD_DIM, ATT_TK), lambda b, g, i: (b, g, 0, 0, 0))],
        out_specs=pl.BlockSpec((1, ATT_TQ, gw), lambda b, g, i: (b, i, g)),
        out_shape=jax.ShapeDtypeStruct((bsz, t_lat, A_Q_W), MXU_DTYPE),
        scratch_shapes=[pltpu.VMEM((2, A_HEADS // A_KV_HEADS, ATT_TK, ATT_TQ), F32)],
        compiler_params=_cparams(("arbitrary", "arbitrary", "arbitrary")),
        name="attention",
    )(q, k, vt)


def _outproj1_kernel(a_ref, x_ref, mod_ref, w_ref, lng_ref, lnb_ref, rw_ref, rb_ref, x1_ref, h2_ref, e_ref, g_ref):
    y = jnp.dot(a_ref[0], w_ref[...], preferred_element_type=F32)
    _finish_sublayer(x_ref[0], y, mod_ref[0, 0], lng_ref, lnb_ref, rw_ref, rb_ref, x1_ref, h2_ref, e_ref, g_ref)


def outproj1_call(att, xa, modsel, w_out, ln_g, ln_b, router_wt, router_b, ctx_tiles):
    bsz, t_lat, _ = att.shape
    d = xa.shape[-1]
    const = lambda a: pl.BlockSpec(a.shape, lambda b, j: (0, 0))
    out_specs, out_shape = _sublayer_out(bsz, t_lat, d)
    return pl.pallas_call(
        _outproj1_kernel,
        grid=(bsz, t_lat // TM),
        in_specs=[pl.BlockSpec((1, TM, att.shape[-1]), lambda b, j: (b, j, 0)),
                  pl.BlockSpec((1, TM, d), lambda b, j: (b, j + ctx_tiles, 0)),
                  pl.BlockSpec((1, 1, 6, d), lambda b, j: (b, 1, 0, 0)),
                  const(w_out), const(ln_g), const(ln_b), const(router_wt), const(router_b)],
        out_specs=out_specs,
        out_shape=out_shape,
        compiler_params=_cparams(("arbitrary", "arbitrary")),
        name="outproj1",
    )(att, xa, modsel, w_out, ln_g, ln_b, router_wt, router_b)


def _rope_tables(n_ctx, n_lat):
    rows = n_lat // GRID_W
    row = jnp.repeat(jnp.arange(rows), GRID_W).astype(F32)
    col = jnp.tile(jnp.arange(GRID_W), rows).astype(F32)
    n_freq = A_HEAD_DIM // 4
    inv = ROPE_THETA ** (-jnp.arange(n_freq, dtype=F32) / n_freq)
    ang = jnp.concatenate([row[:, None] * inv, col[:, None] * inv], -1)
    cos, sin = jnp.cos(ang), jnp.sin(ang)
    cos2 = jnp.concatenate([cos, cos], -1)
    sin2 = jnp.concatenate([-sin, sin], -1)
    cos2 = jnp.concatenate([jnp.ones((n_ctx, A_HEAD_DIM), F32), cos2], 0)
    sin2 = jnp.concatenate([jnp.zeros((n_ctx, A_HEAD_DIM), F32), sin2], 0)
    return cos2, sin2


def _lane_row(parts, width=GATE_LANES):
    row = jnp.concatenate([p.reshape(-1).astype(F32) for p in parts])
    return jnp.pad(row, (0, width - row.shape[0])).reshape(1, width)


def kernel(x, c, ctx, c_ctx, ada_w, ada_b, ln_g, ln_b, ab_w_in, ab_w_out, ml_ig_b, ml_fg_b, ml_norm_g,
           ssm_conv_w, ssm_conv_b, ssm_dt_b, ssm_a_log, ssm_d, ssm_norm_g, at_w_in, at_w_out, at_q_g, at_k_g,
           router_w, router_b, moe_w_gate, moe_w_up, moe_w_down):
    bsz, n_lat, d = x.shape
    n_ctx = ctx.shape[1]
    assert d == D_MODEL and bsz + 1 <= SUBLANES
    assert n_ctx % TM == 0 and n_lat % TM == 0 and n_lat % GRID_W == 0
    assert (n_ctx + n_lat) % ATT_TK == 0 and n_ctx % ATT_TQ == 0 and n_lat % ATT_TQ == 0
    assert (bsz * (n_ctx + n_lat)) % DISPATCH_ROWS == 0 and (bsz * n_lat) % DISPATCH_ROWS == 0
    ctx_tiles = n_ctx // TM
    ctx_chunks = n_ctx // CHUNK
    xa = jnp.concatenate([ctx, x], axis=1)

    crows = jnp.zeros((SUBLANES, d), F32).at[:bsz].set(c).at[bsz].set(c_ctx)
    mods = ada_call(crows, ada_w, ada_b)

    def mod_table(i):
        lat = mods[i, :bsz].reshape(bsz, 1, 6, d)
        cx = jnp.broadcast_to(mods[i, bsz].reshape(1, 1, 6, d), (bsz, 1, 6, d))
        return jnp.concatenate([cx, lat], axis=1)

    router_wt = router_w.T
    router_bc = router_b.reshape(N_EXPERTS, 1)

    modsel = mod_table(0)
    w_in = ab_w_in[0]
    s_q, s_k, s_v, s_o, s_ig, s_fg, s_z, s_xbc = (int(v) for v in
        (0, 512, 1024, 2048, 3072, 3072 + 8, 3072 + 16, 3072 + 16 + 1024))
    s_dt = s_xbc + S_CONV_CH
    w_big = jnp.concatenate([w_in[:, :s_ig], w_in[:, s_z:s_dt]], axis=1).astype(MXU_DTYPE)
    w_small = jnp.concatenate([w_in[:, s_ig:s_z], w_in[:, s_dt:]], axis=1)
    w_small = jnp.pad(w_small, ((0, 0), (0, GATE_LANES - w_small.shape[1])))
    q, k, v, o, z, xbc, gates = inproj0_call(xa, modsel, w_big, w_small, ctx_tiles)
    xbc_act = conv_call(xbc, ssm_conv_w[0], ssm_conv_b[0], ctx_tiles)

    grow = jnp.swapaxes(gates[:, :, :DT_OFF], 1, 2)
    dtrow = jnp.swapaxes(gates[:, :, DT_OFF:DT_OFF + N_DIR * S_HEADS], 1, 2)
    gate_b_row = _lane_row([ml_ig_b[0], ml_fg_b[0], ssm_dt_b[0]])
    gate_b_col = jnp.concatenate([ml_ig_b[0].reshape(-1), ml_fg_b[0].reshape(-1)]).reshape(-1, 1)
    a_neg = -jnp.exp(ssm_a_log[0].astype(F32)).reshape(-1)
    an_row = _lane_row([jnp.zeros((DT_OFF,), F32), a_neg])
    an_col = a_neg.reshape(-1, 1)
    dtb_col = ssm_dt_b[0].reshape(-1, 1)

    hm = None
    hs = None
    for dd in range(N_DIR):
        hm = mlstm_call(q, k, v, gates, grow, gate_b_row, gate_b_col, hm, rev=dd == 1, d=dd, ctx_chunks=ctx_chunks)
        hs = ssd_call(xbc_act, gates, dtrow, gate_b_row, dtb_col, an_row, an_col, hs, rev=dd == 1, d=dd,
                      ctx_chunks=ctx_chunks)

    dskip = jnp.repeat(ssm_d[0].astype(F32), S_HEAD_DIM).reshape(1, S_WIDTH)
    x1, h2t, e_idx, gates = outproj0_call(hm, o, hs, xbc_act, z, xa, modsel, ml_norm_g[0].reshape(1, -1), dskip,
                                          ssm_norm_g[0].reshape(1, -1), ab_w_out[0].astype(MXU_DTYPE),
                                          ln_g[0, 0].reshape(1, d), ln_b[0, 0].reshape(1, d), router_wt, router_bc,
                                          ctx_tiles)
    xa = moe_block(x1, h2t, e_idx, gates, modsel, ln_g[0, 1].reshape(1, d), ln_b[0, 1].reshape(1, d),
                   moe_w_gate, moe_w_up, moe_w_down, 0, ctx_tiles)

    modsel = mod_table(1)
    cos2, sin2 = _rope_tables(n_ctx, n_lat)
    qa, ka, va = inproj1_call(xa, modsel, at_w_in[0].astype(MXU_DTYPE), at_q_g[0].reshape(1, -1),
                              at_k_g[0].reshape(1, -1), cos2, sin2, ctx_tiles)
    att = attn_call(qa, ka, va, n_ctx)
    x1, h2t, e_idx, gates = outproj1_call(att, xa, modsel, at_w_out[0].astype(MXU_DTYPE), ln_g[1, 0].reshape(1, d),
                                          ln_b[1, 0].reshape(1, d), router_wt, router_bc, ctx_tiles)
    return moe_block(x1, h2t, e_idx, gates, modsel, ln_g[1, 1].reshape(1, d), ln_b[1, 1].reshape(1, d),
                     moe_w_gate, moe_w_up, moe_w_down, 1, 0)
```

```python
import functools
import math

import jax
import jax.numpy as jnp
from jax import lax
from jax.experimental import pallas as pl
from jax.experimental.pallas import tpu as pltpu

F32 = jnp.float32
MXU_DTYPE = jnp.bfloat16
HIGHEST = lax.Precision.HIGHEST

D_MODEL = 1024
DEPTH = 2
GRID_W = 64
CHUNK = 128
M_HEADS = 4
M_QK_DIM = D_MODEL // 8
M_V_DIM = D_MODEL // 4
M_WIDTH = M_HEADS * M_V_DIM
S_HEADS = 16
S_HEAD_DIM = D_MODEL // 16
S_GROUPS = 2
S_HEADS_PER_GROUP = S_HEADS // S_GROUPS
S_STATE = 128
S_WIDTH = S_HEADS * S_HEAD_DIM
S_CONV_CH = S_WIDTH + 2 * S_GROUPS * S_STATE
CONV_K = 4
N_DIR = 2
A_HEADS = 8
A_KV_HEADS = 2
A_HEAD_DIM = D_MODEL // A_HEADS
A_Q_W = A_HEADS * A_HEAD_DIM
A_KV_W = A_KV_HEADS * A_HEAD_DIM
ROPE_THETA = 10000.0
N_EXPERTS = 16
N_EXPERT_GROUPS = 4
EXPERTS_PER_GROUP = N_EXPERTS // N_EXPERT_GROUPS
TOP_K = 2
DEEPNORM_ALPHA = (2 * DEPTH) ** 0.25
LN_EPS = 1e-5
RMS_EPS = 1e-6

LANES = 128
SUBLANES = 8
TM = 256
MOE_BLK = 256
DISPATCH_ROWS = 512
ROW_CHUNKS = D_MODEL // LANES
assert ROW_CHUNKS == SUBLANES
ATT_TQ = 256
ATT_TK = 1408
VMEM_LIMIT = 56 * 1024 * 1024
GATE_LANES = 128
IG_OFF, FG_OFF, DT_OFF = 0, N_DIR * M_HEADS, 2 * N_DIR * M_HEADS
ATT_Q_SCALE = A_HEAD_DIM ** -0.5 * math.log2(math.e)


def _cparams(sem):
    return pltpu.CompilerParams(dimension_semantics=sem, vmem_limit_bytes=VMEM_LIMIT)


def _silu(x):
    return x / (1.0 + jnp.exp(-x))


def _sigmoid(x):
    return 1.0 / (1.0 + jnp.exp(-x))


def _softplus(x):
    return jnp.maximum(x, 0.0) + jnp.log(1.0 + jnp.exp(-jnp.abs(x)))


def _log_sigmoid(x):
    return jnp.minimum(x, 0.0) - jnp.log(1.0 + jnp.exp(-jnp.abs(x)))


def _layer_norm_rows(x, g, b):
    mu = jnp.mean(x, axis=-1, keepdims=True)
    xc = x - mu
    var = jnp.mean(xc * xc, axis=-1, keepdims=True)
    return xc * lax.rsqrt(var + LN_EPS) * g + b


def _mm(a, b):
    return jnp.dot(a.astype(MXU_DTYPE), b.astype(MXU_DTYPE), preferred_element_type=F32)


def _mm_nt(a, b):
    return lax.dot_general(a.astype(MXU_DTYPE), b.astype(MXU_DTYPE), (((1,), (1,)), ((), ())),
                           preferred_element_type=F32)


def _mm_tn(a, b):
    return lax.dot_general(a.astype(MXU_DTYPE), b.astype(MXU_DTYPE), (((0,), (0,)), ((), ())),
                           preferred_element_type=F32)


def _ada_kernel(c_ref, w_ref, b_ref, o_ref):
    s = _silu(c_ref[...])
    o_ref[0] = jnp.dot(s, w_ref[0], precision=HIGHEST, preferred_element_type=F32) + b_ref[0]


def ada_call(crows, ada_w, ada_b):
    depth, d, n6 = ada_w.shape
    tn = 1536
    return pl.pallas_call(
        _ada_kernel,
        grid=(depth, n6 // tn),
        in_specs=[pl.BlockSpec((SUBLANES, d), lambda i, j: (0, 0)),
                  pl.BlockSpec((1, d, tn), lambda i, j: (i, 0, j)),
                  pl.BlockSpec((1, 1, tn), lambda i, j: (i, 0, j))],
        out_specs=pl.BlockSpec((1, SUBLANES, tn), lambda i, j: (i, 0, j)),
        out_shape=jax.ShapeDtypeStruct((depth, SUBLANES, n6), F32),
        compiler_params=_cparams(("arbitrary", "arbitrary")),
        name="ada",
    )(crows, ada_w, ada_b.reshape(depth, 1, n6))


def _inproj0_kernel(x_ref, mod_ref, wb_ref, ws_ref, q_ref, k_ref, v_ref, o_ref, z_ref, xbc_ref, g_ref):
    x = x_ref[0]
    mod = mod_ref[0, 0]
    h = x * (1.0 + mod[1:2]) + mod[0:1]
    hb = h.astype(MXU_DTYPE)
    qk = M_HEADS * M_QK_DIM
    c0 = 0
    q_ref[0] = (jnp.dot(hb, wb_ref[:, c0:c0 + qk], preferred_element_type=F32)
                * (M_QK_DIM ** -0.5)).astype(q_ref.dtype)
    c0 += qk
    k_ref[0] = jnp.dot(hb, wb_ref[:, c0:c0 + qk], preferred_element_type=F32).astype(k_ref.dtype)
    c0 += qk
    v_ref[0] = jnp.dot(hb, wb_ref[:, c0:c0 + M_WIDTH], preferred_element_type=F32).astype(v_ref.dtype)
    c0 += M_WIDTH
    o_ref[0] = jnp.dot(hb, wb_ref[:, c0:c0 + M_WIDTH], preferred_element_type=F32)
    c0 += M_WIDTH
    z_ref[0] = jnp.dot(hb, wb_ref[:, c0:c0 + S_WIDTH], preferred_element_type=F32)
    c0 += S_WIDTH
    xbc_ref[0] = jnp.dot(hb, wb_ref[:, c0:c0 + S_CONV_CH], preferred_element_type=F32)
    g_ref[0] = jnp.dot(h, ws_ref[...], precision=HIGHEST, preferred_element_type=F32)


def inproj0_call(xa, modsel, w_big, w_small, ctx_tiles):
    bsz, ta, d = xa.shape
    nt = ta // TM
    qk = M_HEADS * M_QK_DIM
    widths = (qk, qk, M_WIDTH, M_WIDTH, S_WIDTH, S_CONV_CH, GATE_LANES)
    dtypes = (MXU_DTYPE, MXU_DTYPE, MXU_DTYPE, F32, F32, F32, F32)
    tok = lambda w: pl.BlockSpec((1, TM, w), lambda b, j: (b, j, 0))
    return pl.pallas_call(
        _inproj0_kernel,
        grid=(bsz, nt),
        in_specs=[tok(d),
                  pl.BlockSpec((1, 1, 6, d), lambda b, j: (b, (j >= ctx_tiles).astype(jnp.int32), 0, 0)),
                  pl.BlockSpec(w_big.shape, lambda b, j: (0, 0)),
                  pl.BlockSpec(w_small.shape, lambda b, j: (0, 0))],
        out_specs=[tok(w) for w in widths],
        out_shape=[jax.ShapeDtypeStruct((bsz, ta, w), dt) for w, dt in zip(widths, dtypes)],
        compiler_params=_cparams(("arbitrary", "arbitrary")),
        name="inproj0",
    )(xa, modsel, w_big, w_small)


def _conv_kernel(cur_ref, prev_ref, next_ref, w_ref, b_ref, o_ref, *, ctx_tiles, n_tiles):
    j = pl.program_id(1)
    has_prev = jnp.logical_and(j != 0, j != ctx_tiles)
    has_next = jnp.logical_and(j != ctx_tiles - 1, j != n_tiles - 1)
    prev = jnp.where(has_prev, prev_ref[0], 0.0)
    nxt = jnp.where(has_next, next_ref[0], 0.0)
    ext = jnp.concatenate([prev, cur_ref[0], nxt], axis=0)
    n = TM + 2 * SUBLANES
    w = w_ref[...]
    lo, hi = SUBLANES, SUBLANES + TM
    acc = ext[lo:hi] * w[2:3]
    acc = acc + pltpu.roll(ext, 2, 0)[lo:hi] * w[0:1]
    acc = acc + pltpu.roll(ext, 1, 0)[lo:hi] * w[1:2]
    acc = acc + pltpu.roll(ext, n - 1, 0)[lo:hi] * w[3:4]
    o_ref[0] = _silu(acc + b_ref[...])


def conv_call(xbc, conv_w, conv_b, ctx_tiles):
    bsz, ta, ch = xbc.shape
    nt = ta // TM
    r = TM // SUBLANES
    last = ta // SUBLANES - 1
    return pl.pallas_call(
        functools.partial(_conv_kernel, ctx_tiles=ctx_tiles, n_tiles=nt),
        grid=(bsz, nt),
        in_specs=[pl.BlockSpec((1, TM, ch), lambda b, j: (b, j, 0)),
                  pl.BlockSpec((1, SUBLANES, ch), lambda b, j: (b, jnp.maximum(j * r - 1, 0), 0)),
                  pl.BlockSpec((1, SUBLANES, ch), lambda b, j: (b, jnp.minimum((j + 1) * r, last), 0)),
                  pl.BlockSpec((CONV_K, ch), lambda b, j: (0, 0)),
                  pl.BlockSpec((1, ch), lambda b, j: (0, 0))],
        out_specs=pl.BlockSpec((1, TM, ch), lambda b, j: (b, j, 0)),
        out_shape=jax.ShapeDtypeStruct((bsz, ta, ch), F32),
        compiler_params=_cparams(("arbitrary", "arbitrary")),
        name="conv",
    )(xbc, xbc, xbc, conv_w, conv_b.reshape(1, ch))


def _chunk_order(i, rev, ctx_chunks, n_chunks):
    if not rev:
        return i
    return jnp.where(i < ctx_chunks, ctx_chunks - 1 - i, n_chunks - 1 - (i - ctx_chunks))


def _scan_masks(rev):
    r = lax.broadcasted_iota(jnp.int32, (CHUNK, CHUNK), 0)
    c = lax.broadcasted_iota(jnp.int32, (CHUNK, CHUNK), 1)
    mask = (c >= r) if rev else (c <= r)
    mask_t = (r >= c) if rev else (r <= c)
    return mask, mask.astype(F32), mask_t.astype(F32)


def _mlstm_kernel(*refs, rev, d, add_prev):
    if add_prev:
        q_ref, k_ref, v_ref, gc_ref, gr_ref, brow_ref, bcol_ref, prev_ref, o_ref, ct_s, n_s, m_s = refs
    else:
        q_ref, k_ref, v_ref, gc_ref, gr_ref, brow_ref, bcol_ref, o_ref, ct_s, n_s, m_s = refs
        prev_ref = None

    @pl.when(pl.program_id(1) == 0)
    def _():
        ct_s[...] = jnp.zeros_like(ct_s)
        n_s[...] = jnp.zeros_like(n_s)
        m_s[...] = jnp.zeros_like(m_s)

    mask, mask_f, mask_tf = _scan_masks(rev)
    end = 0 if rev else CHUNK - 1
    gcol = gc_ref[0] + brow_ref[...]
    grow = gr_ref[0] + bcol_ref[...]
    nh2 = N_DIR * M_HEADS
    lf_col = _log_sigmoid(gcol)
    lf_row = _log_sigmoid(grow[FG_OFF:FG_OFF + nh2])
    b_col_all = jnp.dot(mask_f, lf_col, precision=HIGHEST, preferred_element_type=F32)
    b_row_all = jnp.dot(lf_row, mask_tf, precision=HIGHEST, preferred_element_type=F32)

    for j in range(M_HEADS):
        ci = d * M_HEADS + j
        bcol = b_col_all[:, FG_OFF + ci:FG_OFF + ci + 1]
        brow = b_row_all[ci:ci + 1, :]
        igcol = gcol[:, IG_OFF + ci:IG_OFF + ci + 1]
        igrow = grow[IG_OFF + ci:IG_OFF + ci + 1, :]
        b_end = bcol[end:end + 1, :]
        m_prev = m_s[j][:, 0:1]
        n_prev = n_s[j]
        ct_prev = ct_s[j]
        q = q_ref[0, :, j * M_QK_DIM:(j + 1) * M_QK_DIM]
        k = k_ref[0, :, j * M_QK_DIM:(j + 1) * M_QK_DIM]
        v = v_ref[0, :, j * M_V_DIM:(j + 1) * M_V_DIM]
        qf = q.astype(F32)
        kf = k.astype(F32)
        vf = v.astype(F32)

        dmat = jnp.where(mask, bcol - brow + igrow, -jnp.inf)
        inter = bcol + m_prev
        m_t = jnp.maximum(inter, jnp.max(dmat, axis=1, keepdims=True))
        sc = _mm_nt(q, k) * jnp.exp(dmat - m_t)
        a_in = jnp.exp(inter - m_t)
        num = _mm(sc, v) + a_in * _mm(q, ct_prev)
        den = jnp.sum(sc, axis=1, keepdims=True) + a_in * jnp.sum(qf * n_prev, axis=1, keepdims=True)
        h = num / jnp.maximum(jnp.abs(den), jnp.exp(-m_t))
        sl = slice(j * M_V_DIM, (j + 1) * M_V_DIM)
        if add_prev:
            h = h + prev_ref[0, :, sl]
        o_ref[0, :, sl] = h

        g_col = b_end - bcol + igcol
        g_row = b_end - brow + igrow
        g_max = jnp.max(g_row, axis=1, keepdims=True)
        w_col = jnp.exp(g_col - g_max)
        d_ct = _mm_tn(k, vf * w_col)
        d_n = jnp.sum(kf * w_col, axis=0, keepdims=True)
        m_new = jnp.maximum(b_end + m_prev, g_max)
        a = jnp.exp(b_end + m_prev - m_new)
        s = jnp.exp(g_max - m_new)
        ct_s[j] = a * ct_prev + s * d_ct
        n_s[j] = a * n_prev + s * d_n
        m_s[j] = jnp.broadcast_to(m_new, (1, LANES))


def mlstm_call(q, k, v, gcol, grow, bias_row, bias_col, prev, *, rev, d, ctx_chunks):
    bsz, ta, _ = q.shape
    nc = ta // CHUNK
    order = lambda i: _chunk_order(i, rev, ctx_chunks, nc)
    tok = lambda w: pl.BlockSpec((1, CHUNK, w), lambda b, i: (b, order(i), 0))
    in_specs = [tok(q.shape[-1]), tok(k.shape[-1]), tok(v.shape[-1]), tok(GATE_LANES),
                pl.BlockSpec((1, grow.shape[1], CHUNK), lambda b, i: (b, 0, order(i))),
                pl.BlockSpec(bias_row.shape, lambda b, i: (0, 0)),
                pl.BlockSpec(bias_col.shape, lambda b, i: (0, 0))]
    args = [q, k, v, gcol, grow, bias_row, bias_col]
    if prev is not None:
        in_specs.append(tok(M_WIDTH))
        args.append(prev)
    return pl.pallas_call(
        functools.partial(_mlstm_kernel, rev=rev, d=d, add_prev=prev is not None),
        grid=(bsz, nc),
        in_specs=in_specs,
        out_specs=tok(M_WIDTH),
        out_shape=jax.ShapeDtypeStruct((bsz, ta, M_WIDTH), F32),
        scratch_shapes=[pltpu.VMEM((M_HEADS, M_QK_DIM, M_V_DIM), F32),
                        pltpu.VMEM((M_HEADS, 1, M_QK_DIM), F32),
                        pltpu.VMEM((M_HEADS, 1, LANES), F32)],
        compiler_params=_cparams(("arbitrary", "arbitrary")),
        name="mlstm_rev" if rev else "mlstm_fwd",
    )(*args)


def _ssd_kernel(*refs, rev, d, add_prev):
    if add_prev:
        x_ref, gc_ref, gr_ref, dtb_row_ref, dtb_col_ref, an_row_ref, an_col_ref, prev_ref, o_ref, ht_s = refs
    else:
        x_ref, gc_ref, gr_ref, dtb_row_ref, dtb_col_ref, an_row_ref, an_col_ref, o_ref, ht_s = refs
        prev_ref = None

    @pl.when(pl.program_id(1) == 0)
    def _():
        ht_s[...] = jnp.zeros_like(ht_s)

    mask, mask_f, mask_tf = _scan_masks(rev)
    end = 0 if rev else CHUNK - 1
    lane = lax.broadcasted_iota(jnp.int32, (CHUNK, LANES), 1)
    first_half = lane < S_HEAD_DIM
    dt_col = _softplus(gc_ref[0] + dtb_row_ref[...])
    dt_row = _softplus(gr_ref[0] + dtb_col_ref[...])
    acs_col = jnp.dot(mask_f, dt_col * an_row_ref[...], precision=HIGHEST, preferred_element_type=F32)
    acs_row = jnp.dot(dt_row * an_col_ref[...], mask_tf, precision=HIGHEST, preferred_element_type=F32)
    a_end_row = acs_col[end:end + 1, :]
    e_cs = jnp.exp(acs_col)
    e_rem = jnp.exp(a_end_row - acs_col)
    e_end = jnp.exp(a_end_row)
    gw = S_HEADS_PER_GROUP * S_HEAD_DIM
    pairs = S_HEADS_PER_GROUP // 2

    def pick(arr, la):
        return jnp.where(first_half[:arr.shape[0]], arr[:, la:la + 1], arr[:, la + 1:la + 2])

    for g in range(S_GROUPS):
        bm = x_ref[0, :, S_WIDTH + g * S_STATE:S_WIDTH + (g + 1) * S_STATE]
        cm = x_ref[0, :, S_WIDTH + (S_GROUPS + g) * S_STATE:S_WIDTH + (S_GROUPS + g + 1) * S_STATE]
        cb = _mm_nt(cm, bm)
        ht_prev = ht_s[g]
        y_inter = _mm(cm, ht_prev)
        xw_parts = []
        decay_parts = []
        for p in range(pairs):
            h0 = g * S_HEADS_PER_GROUP + 2 * p
            la = DT_OFF + d * S_HEADS + h0
            ra = d * S_HEADS + h0
            lhs = []
            for u in range(2):
                seg = acs_col[:, la + u:la + u + 1] - acs_row[ra + u:ra + u + 1, :]
                dec = jnp.exp(jnp.where(mask, seg, -jnp.inf))
                lhs.append((cb * dec).astype(MXU_DTYPE))
            xs = x_ref[0, :, h0 * S_HEAD_DIM:(h0 + 2) * S_HEAD_DIM]
            xsd = xs * pick(dt_col, la)
            rhs = jnp.concatenate([jnp.where(first_half, xsd, 0.0), jnp.where(first_half, 0.0, xsd)],
                                  axis=0).astype(MXU_DTYPE)
            y = jnp.dot(jnp.concatenate(lhs, axis=1), rhs, preferred_element_type=F32)
            y = y + y_inter[:, p * LANES:(p + 1) * LANES] * pick(e_cs, la)
            sl = slice(h0 * S_HEAD_DIM, (h0 + 2) * S_HEAD_DIM)
            if add_prev:
                y = y + prev_ref[0, :, sl]
            o_ref[0, :, sl] = y
            xw_parts.append(xsd * pick(e_rem, la))
            decay_parts.append(pick(e_end, la))
        xw = jnp.concatenate(xw_parts, axis=1)
        decay = jnp.concatenate(decay_parts, axis=1)
        ht_s[g] = decay * ht_prev + _mm_tn(bm, xw)


def ssd_call(xbc_act, gcol, dtrow, dtb_row, dtb_col, an_row, an_col, prev, *, rev, d, ctx_chunks):
    bsz, ta, ch = xbc_act.shape
    nc = ta // CHUNK
    order = lambda i: _chunk_order(i, rev, ctx_chunks, nc)
    tok = lambda w: pl.BlockSpec((1, CHUNK, w), lambda b, i: (b, order(i), 0))
    const = lambda a: pl.BlockSpec(a.shape, lambda b, i: (0, 0))
    in_specs = [tok(ch), tok(GATE_LANES),
                pl.BlockSpec((1, dtrow.shape[1], CHUNK), lambda b, i: (b, 0, order(i))),
                const(dtb_row), const(dtb_col), const(an_row), const(an_col)]
    args = [xbc_act, gcol, dtrow, dtb_row, dtb_col, an_row, an_col]
    if prev is not None:
        in_specs.append(tok(S_WIDTH))
        args.append(prev)
    return pl.pallas_call(
        functools.partial(_ssd_kernel, rev=rev, d=d, add_prev=prev is not None),
        grid=(bsz, nc),
        in_specs=in_specs,
        out_specs=tok(S_WIDTH),
        out_shape=jax.ShapeDtypeStruct((bsz, ta, S_WIDTH), F32),
        scratch_shapes=[pltpu.VMEM((S_GROUPS, S_STATE, S_HEADS_PER_GROUP * S_HEAD_DIM), F32)],
        compiler_params=_cparams(("arbitrary", "arbitrary")),
        name="ssd_rev" if rev else "ssd_fwd",
    )(*args)


def _store_row_tiles(ref, val):
    for s in range(ROW_CHUNKS):
        ref[pl.ds(s, val.shape[0], stride=ROW_CHUNKS), :] = val[:, s * LANES:(s + 1) * LANES]


def _load_row_tiles(ref, rows):
    return jnp.concatenate([ref[pl.ds(s, rows, stride=ROW_CHUNKS), :] for s in range(ROW_CHUNKS)], axis=1)


def _finish_sublayer(x, y, mod, lng_ref, lnb_ref, rw_ref, rb_ref, x1_ref, h2_ref, e_ref, g_ref):
    x1 = _layer_norm_rows(DEEPNORM_ALPHA * x + mod[2:3] * y, lng_ref[...], lnb_ref[...])
    x1_ref[0] = x1
    h2 = x1 * (1.0 + mod[4:5]) + mod[3:4]
    _store_row_tiles(h2_ref, h2)
    e, g = _route(h2, rw_ref, rb_ref)
    e_ref[...] = e
    g_ref[...] = g


def _outproj0_kernel(hm_ref, o_ref, hs_ref, xs_ref, z_ref, x_ref, mod_ref, mg_ref, dsk_ref, sg_ref, w_ref,
                     lng_ref, lnb_ref, rw_ref, rb_ref, x1_ref, h2_ref, e_ref, g_ref):
    mod = mod_ref[0, 0]
    parts = []
    for j in range(M_HEADS):
        sl = slice(j * M_V_DIM, (j + 1) * M_V_DIM)
        hj = hm_ref[0, :, sl]
        mu = jnp.mean(hj, axis=-1, keepdims=True)
        hc = hj - mu
        var = jnp.mean(hc * hc, axis=-1, keepdims=True)
        parts.append(hc * lax.rsqrt(var + LN_EPS) * mg_ref[:, sl] * _sigmoid(o_ref[0, :, sl]))
    ym = jnp.concatenate(parts, axis=1)
    ys = (hs_ref[0] + dsk_ref[...] * xs_ref[0]) * _silu(z_ref[0])
    ys = ys * lax.rsqrt(jnp.mean(ys * ys, axis=-1, keepdims=True) + RMS_EPS) * sg_ref[...]
    y = (jnp.dot(ym.astype(MXU_DTYPE), w_ref[:M_WIDTH, :], preferred_element_type=F32)
         + jnp.dot(ys.astype(MXU_DTYPE), w_ref[M_WIDTH:, :], preferred_element_type=F32))
    _finish_sublayer(x_ref[0], y, mod, lng_ref, lnb_ref, rw_ref, rb_ref, x1_ref, h2_ref, e_ref, g_ref)


def _sublayer_out(bsz, tt, d):
    nt = tt // TM
    n = bsz * tt
    specs = [pl.BlockSpec((1, TM, d), lambda b, j: (b, j, 0)),
             pl.BlockSpec((TM * ROW_CHUNKS, LANES), lambda b, j: (b * nt + j, 0)),
             pl.BlockSpec((TOP_K, TM), lambda b, j: (0, b * nt + j)),
             pl.BlockSpec((TOP_K, TM), lambda b, j: (0, b * nt + j))]
    shapes = [jax.ShapeDtypeStruct((bsz, tt, d), F32), jax.ShapeDtypeStruct((n * ROW_CHUNKS, LANES), F32),
              jax.ShapeDtypeStruct((TOP_K, n), jnp.int32), jax.ShapeDtypeStruct((TOP_K, n), F32)]
    return specs, shapes


def outproj0_call(hm, o, hs, xbc_act, z, xa, modsel, m_norm_g, dskip, s_norm_g, w_out, ln_g, ln_b, router_wt,
                  router_b, ctx_tiles):
    bsz, ta, d = xa.shape
    nt = ta // TM
    tok = lambda w: pl.BlockSpec((1, TM, w), lambda b, j: (b, j, 0))
    const = lambda a: pl.BlockSpec(a.shape, lambda b, j: (0, 0))
    out_specs, out_shape = _sublayer_out(bsz, ta, d)
    return pl.pallas_call(
        _outproj0_kernel,
        grid=(bsz, nt),
        in_specs=[tok(M_WIDTH), tok(M_WIDTH), tok(S_WIDTH), tok(S_WIDTH), tok(S_WIDTH), tok(d),
                  pl.BlockSpec((1, 1, 6, d), lambda b, j: (b, (j >= ctx_tiles).astype(jnp.int32), 0, 0)),
                  const(m_norm_g), const(dskip), const(s_norm_g), const(w_out), const(ln_g), const(ln_b),
                  const(router_wt), const(router_b)],
        out_specs=out_specs,
        out_shape=out_shape,
        compiler_params=_cparams(("arbitrary", "arbitrary")),
        name="outproj0",
    )(hm, o, hs, xbc_act, z, xa, modsel, m_norm_g, dskip, s_norm_g, w_out, ln_g, ln_b, router_wt, router_b)


def _top2(vals, probs):
    v1, i1, p1 = vals[0], jnp.zeros_like(vals[0], dtype=jnp.int32), probs[0]
    for i in range(1, len(vals)):
        better = vals[i] > v1
        v1 = jnp.where(better, vals[i], v1)
        i1 = jnp.where(better, i, i1)
        p1 = jnp.where(better, probs[i], p1)
    v2 = jnp.full_like(vals[0], -jnp.inf)
    i2 = jnp.zeros_like(i1)
    p2 = jnp.zeros_like(p1)
    for i in range(len(vals)):
        better = jnp.logical_and(i1 != i, vals[i] > v2)
        v2 = jnp.where(better, vals[i], v2)
        i2 = jnp.where(better, i, i2)
        p2 = jnp.where(better, probs[i], p2)
    return v1, i1, p1, v2, i2, p2


def _route(h, wt_ref, b_ref):
    logits = lax.dot_general(wt_ref[...], h, (((1,), (1,)), ((), ())), precision=HIGHEST,
                             preferred_element_type=F32)
    mx = jnp.max(logits, axis=0, keepdims=True)
    ex = jnp.exp(logits - mx)
    probs = ex / jnp.sum(ex, axis=0, keepdims=True)
    sel = probs + b_ref[...]
    best = None
    for g in range(N_EXPERT_GROUPS):
        rows = range(g * EXPERTS_PER_GROUP, (g + 1) * EXPERTS_PER_GROUP)
        v1, i1, p1, v2, i2, p2 = _top2([sel[r:r + 1] for r in rows], [probs[r:r + 1] for r in rows])
        cand = (v1 + v2, i1 + g * EXPERTS_PER_GROUP, p1, i2 + g * EXPERTS_PER_GROUP, p2)
        if best is None:
            best = cand
        else:
            better = cand[0] > best[0]
            best = tuple(jnp.where(better, c, o) for c, o in zip(cand, best))
    _, e1, p1, e2, p2 = best
    tot = p1 + p2
    return jnp.concatenate([e1, e2], axis=0), jnp.concatenate([p1 / tot, p2 / tot], axis=0)


def _row_tile(ref, r):
    return ref.at[pl.ds(pl.multiple_of(r * ROW_CHUNKS, ROW_CHUNKS), ROW_CHUNKS)]


def _dispatch_kernel(dest_ref, zblk_ref, h_ref, xs_ref, zero_s, sem, zsem, *, n_tok):
    base = pl.program_id(0) * DISPATCH_ROWS
    blk_rows = MOE_BLK * ROW_CHUNKS

    @pl.when(pl.program_id(0) == 0)
    def _():
        zero_s[...] = jnp.zeros_like(zero_s)

        def zero_copy(t):
            start = pl.multiple_of(zblk_ref[t] * blk_rows, blk_rows)
            return pltpu.make_async_copy(zero_s, xs_ref.at[pl.ds(start, blk_rows)], zsem)

        for t in range(2 * N_EXPERTS):
            pl.when(zblk_ref[t] >= 0)(lambda t=t: zero_copy(t).start())
        for t in range(2 * N_EXPERTS):
            pl.when(zblk_ref[t] >= 0)(lambda t=t: zero_copy(t).wait())

    def slot_copy(r, slot):
        return pltpu.make_async_copy(_row_tile(h_ref, r), _row_tile(xs_ref, slot), sem)

    def issue(r, carry):
        for c in range(TOP_K):
            slot_copy(r, dest_ref[c * n_tok + base + r]).start(priority=c)
        return carry

    def drain(r, carry):
        for c in range(TOP_K):
            slot_copy(r, 0).wait()
        return carry

    lax.fori_loop(0, DISPATCH_ROWS, issue, 0, unroll=8)
    lax.fori_loop(0, DISPATCH_ROWS, drain, 0, unroll=8)


def dispatch_call(dest, zero_blocks, h2t, n_slots):
    n_tok = h2t.shape[0] // ROW_CHUNKS
    return pl.pallas_call(
        functools.partial(_dispatch_kernel, n_tok=n_tok),
        grid_spec=pltpu.PrefetchScalarGridSpec(
            num_scalar_prefetch=2,
            grid=(n_tok // DISPATCH_ROWS,),
            in_specs=[pl.BlockSpec((DISPATCH_ROWS * ROW_CHUNKS, LANES), lambda i, dest, zb: (i, 0))],
            out_specs=pl.BlockSpec(memory_space=pl.ANY),
            scratch_shapes=[pltpu.VMEM((MOE_BLK * ROW_CHUNKS, LANES), h2t.dtype),
                            pltpu.SemaphoreType.DMA(()), pltpu.SemaphoreType.DMA(())]),
        out_shape=jax.ShapeDtypeStruct((n_slots * ROW_CHUNKS, LANES), h2t.dtype),
        compiler_params=_cparams(("arbitrary",)),
        name="dispatch",
    )(dest, zero_blocks, h2t)


def _experts_kernel(be_ref, cnt_ref, x_ref, wg_ref, wu_ref, wd_ref, o_ref, wg_s, wu_s, wd_s):
    i = pl.program_id(0)
    e = be_ref[i]
    e_before = be_ref[jnp.maximum(i - 1, 0)]
    cnt = cnt_ref[i]

    @pl.when(jnp.logical_or(i == 0, e != e_before))
    def _():
        wg_s[...] = wg_ref[0, 0].astype(wg_s.dtype)
        wu_s[...] = wu_ref[0, 0].astype(wu_s.dtype)
        wd_s[...] = wd_ref[0, 0].astype(wd_s.dtype)

    @pl.when(cnt > 0)
    def _():
        xb = _load_row_tiles(x_ref, MOE_BLK).astype(MXU_DTYPE)
        gt = jnp.dot(xb, wg_s[...], preferred_element_type=F32)
        up = jnp.dot(xb, wu_s[...], preferred_element_type=F32)
        y = jnp.dot((_silu(gt) * up).astype(MXU_DTYPE), wd_s[...], preferred_element_type=F32)
        _store_row_tiles(o_ref, y)

    @pl.when(cnt == 0)
    def _():
        o_ref[...] = jnp.zeros_like(o_ref)


def experts_call(blk_exp, blk_cnt, xs, w_gate, w_up, w_down, layer):
    n_slots = xs.shape[0] // ROW_CHUNKS
    d, f = w_gate.shape[-2:]
    wspec = lambda a: pl.BlockSpec((1, 1) + a.shape[2:], lambda i, be, cnt: (layer, be[i], 0, 0))
    blk = pl.BlockSpec((MOE_BLK * ROW_CHUNKS, LANES), lambda i, be, cnt: (i, 0))
    return pl.pallas_call(
        _experts_kernel,
        grid_spec=pltpu.PrefetchScalarGridSpec(
            num_scalar_prefetch=2,
            grid=(n_slots // MOE_BLK,),
            in_specs=[blk, wspec(w_gate), wspec(w_up), wspec(w_down)],
            out_specs=blk,
            scratch_shapes=[pltpu.VMEM((d, f), MXU_DTYPE), pltpu.VMEM((d, f), MXU_DTYPE),
                            pltpu.VMEM((f, d), MXU_DTYPE)]),
        out_shape=jax.ShapeDtypeStruct(xs.shape, F32),
        compiler_params=_cparams(("arbitrary",)),
        name="experts",
    )(blk_exp, blk_cnt, xs, w_gate, w_up, w_down)


def _combine_kernel(dest_ref, y_ref, gate_ref, x_ref, mod_ref, lng_ref, lnb_ref, o_ref, ybuf, sems, *, n_tok, nt,
                    n_steps):
    step = pl.program_id(0) * nt + pl.program_id(1)
    slot = step % 2

    def row_copy(buf, r, c, src_slot):
        return pltpu.make_async_copy(_row_tile(y_ref, src_slot), _row_tile(ybuf.at[buf, c], r), sems.at[buf])

    def issue(tile, buf):
        base = tile * TM

        def body(r, carry):
            for c in range(TOP_K):
                row_copy(buf, r, c, dest_ref[c * n_tok + base + r]).start(priority=c)
            return carry

        lax.fori_loop(0, TM, body, 0, unroll=8)

    pl.when(step == 0)(lambda: issue(0, 0))
    pl.when(step + 1 < n_steps)(lambda: issue(step + 1, 1 - slot))

    def drain(r, carry):
        for c in range(TOP_K):
            row_copy(slot, r, c, 0).wait()
        return carry

    lax.fori_loop(0, TM, drain, 0, unroll=8)
    mod = mod_ref[0, 0]
    gate = gate_ref[...]
    y = (gate[:, 0:1] * _load_row_tiles(ybuf.at[slot, 0], TM)
         + gate[:, 1:2] * _load_row_tiles(ybuf.at[slot, 1], TM))
    o_ref[0] = _layer_norm_rows(DEEPNORM_ALPHA * x_ref[0] + mod[5:6] * y, lng_ref[...], lnb_ref[...])


def combine_call(dest, y_slots, gate_cols, x1, modsel, ln_g, ln_b, ctx_tiles):
    bsz, tt, d = x1.shape
    nt = tt // TM
    return pl.pallas_call(
        functools.partial(_combine_kernel, n_tok=bsz * tt, nt=nt, n_steps=bsz * nt),
        grid_spec=pltpu.PrefetchScalarGridSpec(
            num_scalar_prefetch=1,
            grid=(bsz, nt),
            in_specs=[pl.BlockSpec(memory_space=pl.ANY),
                      pl.BlockSpec((TM, TOP_K), lambda b, j, dest: (b * nt + j, 0)),
                      pl.BlockSpec((1, TM, d), lambda b, j, dest: (b, j, 0)),
                      pl.BlockSpec((1, 1, 6, d),
                                   lambda b, j, dest: (b, (j >= ctx_tiles).astype(jnp.int32), 0, 0)),
                      pl.BlockSpec(ln_g.shape, lambda b, j, dest: (0, 0)),
                      pl.BlockSpec(ln_b.shape, lambda b, j, dest: (0, 0))],
            out_specs=pl.BlockSpec((1, TM, d), lambda b, j, dest: (b, j, 0)),
            scratch_shapes=[pltpu.VMEM((2, TOP_K, TM * ROW_CHUNKS, LANES), F32), pltpu.SemaphoreType.DMA((2,))]),
        out_shape=jax.ShapeDtypeStruct((bsz, tt, d), F32),
        compiler_params=_cparams(("arbitrary", "arbitrary")),
        name="combine",
    )(dest, y_slots, gate_cols, x1, modsel, ln_g, ln_b)


def moe_block(x1, h2t, e_idx, gates, modsel, ln_g, ln_b, w_gate, w_up, w_down, layer, ctx_tiles):
    bsz, tt, d = x1.shape
    n = bsz * tt

    n_asg = TOP_K * n
    flat_e = e_idx.reshape(n_asg)
    onehot = (flat_e[:, None] == jnp.arange(N_EXPERTS, dtype=jnp.int32)[None, :]).astype(jnp.int32)
    csum = jnp.cumsum(onehot, axis=0)
    rank = jnp.sum(onehot * csum, axis=1) - 1
    counts = csum[-1]
    padded = (counts + MOE_BLK - 1) // MOE_BLK * MOE_BLK
    pend = jnp.cumsum(padded)
    pstart = pend - padded
    dest = (pstart[flat_e] + rank).astype(jnp.int32)
    n_blocks = -(-n_asg // MOE_BLK) + N_EXPERTS
    blk_start = jnp.arange(n_blocks, dtype=jnp.int32) * MOE_BLK
    blk_exp = jnp.sum((pend[None, :] <= blk_start[:, None]).astype(jnp.int32), axis=1)
    blk_exp = jnp.minimum(blk_exp, N_EXPERTS - 1)
    blk_cnt = jnp.clip(counts[blk_exp] - (blk_start - pstart[blk_exp]), 0, MOE_BLK).astype(jnp.int32)

    part = jnp.where(counts % MOE_BLK != 0, pend // MOE_BLK - 1, -1)
    tail = pend[-1] // MOE_BLK + jnp.arange(N_EXPERTS, dtype=jnp.int32)
    tail = jnp.where(tail < n_blocks, tail, -1)
    zero_blocks = jnp.concatenate([part, tail]).astype(jnp.int32)

    xs = dispatch_call(dest, zero_blocks, h2t, n_blocks * MOE_BLK)
    y_slots = experts_call(blk_exp, blk_cnt, xs, w_gate, w_up, w_down, layer)
    return combine_call(dest, y_slots, gates.T, x1, modsel, ln_g, ln_b, ctx_tiles)


def _inproj1_kernel(x_ref, mod_ref, w_ref, qg_ref, kg_ref, cos_ref, sin_ref, q_ref, k_ref, v_ref):
    mod = mod_ref[0, 0]
    hb = (x_ref[0] * (1.0 + mod[1:2]) + mod[0:1]).astype(MXU_DTYPE)
    cos = cos_ref[...]
    sin = sin_ref[...]

    def norm_rope(t, g):
        t = t * lax.rsqrt(jnp.mean(t * t, axis=-1, keepdims=True) + RMS_EPS) * g
        return t * cos + pltpu.roll(t, A_HEAD_DIM // 2, 1) * sin

    for j in range(A_HEADS):
        sl = slice(j * A_HEAD_DIM, (j + 1) * A_HEAD_DIM)
        t = jnp.dot(hb, w_ref[:, sl], preferred_element_type=F32)
        q_ref[0, :, sl] = (norm_rope(t, qg_ref[...]) * ATT_Q_SCALE).astype(q_ref.dtype)
    for j in range(A_KV_HEADS):
        sl = slice(j * A_HEAD_DIM, (j + 1) * A_HEAD_DIM)
        t = jnp.dot(hb, w_ref[:, A_Q_W + j * A_HEAD_DIM:A_Q_W + (j + 1) * A_HEAD_DIM], preferred_element_type=F32)
        k_ref[0, :, sl] = norm_rope(t, kg_ref[...]).astype(k_ref.dtype)
    v_ref[0] = jnp.dot(hb, w_ref[:, A_Q_W + A_KV_W:], preferred_element_type=F32).astype(v_ref.dtype)


def inproj1_call(xa, modsel, w_in, q_g, k_g, cos2, sin2, ctx_tiles):
    bsz, ta, d = xa.shape
    nt = ta // TM
    tok = lambda w: pl.BlockSpec((1, TM, w), lambda b, j: (b, j, 0))
    const = lambda a: pl.BlockSpec(a.shape, lambda b, j: (0, 0))
    widths = (A_Q_W, A_KV_W, A_KV_W)
    return pl.pallas_call(
        _inproj1_kernel,
        grid=(bsz, nt),
        in_specs=[tok(d),
                  pl.BlockSpec((1, 1, 6, d), lambda b, j: (b, (j >= ctx_tiles).astype(jnp.int32), 0, 0)),
                  const(w_in), const(q_g), const(k_g),
                  pl.BlockSpec((TM, A_HEAD_DIM), lambda b, j: (j, 0)),
                  pl.BlockSpec((TM, A_HEAD_DIM), lambda b, j: (j, 0))],
        out_specs=[tok(w) for w in widths],
        out_shape=[jax.ShapeDtypeStruct((bsz, ta, w), MXU_DTYPE) for w in widths],
        compiler_params=_cparams(("arbitrary", "arbitrary")),
        name="inproj1",
    )(xa, modsel, w_in, q_g, k_g, cos2, sin2)


def _attn_kernel(q_ref, qn_ref, k_ref, vt_ref, o_ref, s_buf):
    rep = A_HEADS // A_KV_HEADS
    n_kv = vt_ref.shape[2]
    assert n_kv % 2 == 0

    def transposed(ref):
        return [ref[0, :, r * A_HEAD_DIM:(r + 1) * A_HEAD_DIM].astype(F32).T.astype(MXU_DTYPE) for r in range(rep)]

    qts = transposed(q_ref)

    def scores(j, slot, qt=qts):
        kb = k_ref[0, pl.ds(pl.multiple_of(j * ATT_TK, ATT_TK), ATT_TK), :]
        for r in range(rep):
            s_buf[slot, r] = jnp.dot(kb, qt[r], preferred_element_type=F32)

    def consume(j, slot, stats):
        vt = vt_ref[0, 0, j]
        new = []
        for r in range(rep):
            m, l, acc = stats[r]
            s = s_buf[slot, r]
            m_new = jnp.maximum(m, jnp.max(s, axis=0, keepdims=True))
            p = jnp.exp2(s - m_new)
            alpha = jnp.exp2(m - m_new)
            l = alpha * l + jnp.sum(p, axis=0, keepdims=True)
            acc = alpha * acc + jnp.dot(vt, p.astype(vt.dtype), preferred_element_type=F32)
            new.append((m_new, l, acc))
        return tuple(new)

    def pair(i, stats):
        j = 2 * i
        scores(j + 1, 1)
        stats = consume(j, 0, stats)
        scores(j + 2, 0)
        return consume(j + 1, 1, stats)

    stats = tuple((jnp.full((1, ATT_TQ), -jnp.inf, F32), jnp.zeros((1, ATT_TQ), F32),
                   jnp.zeros((A_HEAD_DIM, ATT_TQ), F32)) for _ in range(rep))
    pl.when(pl.program_id(2) == 0)(lambda: scores(0, 0))
    stats = lax.fori_loop(0, n_kv // 2 - 1, pair, stats)
    scores(n_kv - 1, 1)
    stats = consume(n_kv - 2, 0, stats)
    scores(0, 0, transposed(qn_ref))
    final = consume(n_kv - 1, 1, stats)
    for r in range(rep):
        _, l, acc = final[r]
        o_ref[0, :, r * A_HEAD_DIM:(r + 1) * A_HEAD_DIM] = (acc / l).T.astype(o_ref.dtype)


def attn_call(q, k, v, n_ctx):
    bsz, ta, _ = q.shape
    t_lat = ta - n_ctx
    gw = (A_HEADS // A_KV_HEADS) * A_HEAD_DIM
    q_off = n_ctx // ATT_TQ
    n_kv = ta // ATT_TK
    vt = v.reshape(bsz, n_kv, ATT_TK, A_KV_HEADS, A_HEAD_DIM).transpose(0, 3, 1, 4, 2)
    nq = t_lat // ATT_TQ
    return pl.pallas_call(
        _attn_kernel,
        grid=(bsz, A_KV_HEADS, nq),
        in_specs=[pl.BlockSpec((1, ATT_TQ, gw), lambda b, g, i: (b, i + q_off, g)),
                  pl.BlockSpec((1, ATT_TQ, gw), lambda b, g, i: (b, jnp.minimum(i + 1, nq - 1) + q_off, g)),
                  pl.BlockSpec((1, ta, A_HEAD_DIM), lambda b, g, i: (b, 0, g)),
                  pl.BlockSpec((1, 1, n_kv, A_HEAD_DIM, ATT_TK), lambda b, g, i: (b, g, 0, 0, 0))],
        out_specs=pl.BlockSpec((1, ATT_TQ, gw), lambda b, g, i: (b, i, g)),
        out_shape=jax.ShapeDtypeStruct((bsz, t_lat, A_Q_W), MXU_DTYPE),
        scratch_shapes=[pltpu.VMEM((2, A_HEADS // A_KV_HEADS, ATT_TK, ATT_TQ), F32)],
        compiler_params=_cparams(("arbitrary", "arbitrary", "arbitrary")),
        name="attention",
    )(q, q, k, vt)


def _outproj1_kernel(a_ref, x_ref, mod_ref, w_ref, lng_ref, lnb_ref, rw_ref, rb_ref, x1_ref, h2_ref, e_ref, g_ref):
    y = jnp.dot(a_ref[0], w_ref[...], preferred_element_type=F32)
    _finish_sublayer(x_ref[0], y, mod_ref[0, 0], lng_ref, lnb_ref, rw_ref, rb_ref, x1_ref, h2_ref, e_ref, g_ref)


def outproj1_call(att, xa, modsel, w_out, ln_g, ln_b, router_wt, router_b, ctx_tiles):
    bsz, t_lat, _ = att.shape
    d = xa.shape[-1]
    const = lambda a: pl.BlockSpec(a.shape, lambda b, j: (0, 0))
    out_specs, out_shape = _sublayer_out(bsz, t_lat, d)
    return pl.pallas_call(
        _outproj1_kernel,
        grid=(bsz, t_lat // TM),
        in_specs=[pl.BlockSpec((1, TM, att.shape[-1]), lambda b, j: (b, j, 0)),
                  pl.BlockSpec((1, TM, d), lambda b, j: (b, j + ctx_tiles, 0)),
                  pl.BlockSpec((1, 1, 6, d), lambda b, j: (b, 1, 0, 0)),
                  const(w_out), const(ln_g), const(ln_b), const(router_wt), const(router_b)],
        out_specs=out_specs,
        out_shape=out_shape,
        compiler_params=_cparams(("arbitrary", "arbitrary")),
        name="outproj1",
    )(att, xa, modsel, w_out, ln_g, ln_b, router_wt, router_b)


def _rope_tables(n_ctx, n_lat):
    rows = n_lat // GRID_W
    row = jnp.repeat(jnp.arange(rows), GRID_W).astype(F32)
    col = jnp.tile(jnp.arange(GRID_W), rows).astype(F32)
    n_freq = A_HEAD_DIM // 4
    inv = ROPE_THETA ** (-jnp.arange(n_freq, dtype=F32) / n_freq)
    ang = jnp.concatenate([row[:, None] * inv, col[:, None] * inv], -1)
    cos, sin = jnp.cos(ang), jnp.sin(ang)
    cos2 = jnp.concatenate([cos, cos], -1)
    sin2 = jnp.concatenate([-sin, sin], -1)
    cos2 = jnp.concatenate([jnp.ones((n_ctx, A_HEAD_DIM), F32), cos2], 0)
    sin2 = jnp.concatenate([jnp.zeros((n_ctx, A_HEAD_DIM), F32), sin2], 0)
    return cos2, sin2


def _lane_row(parts, width=GATE_LANES):
    row = jnp.concatenate([p.reshape(-1).astype(F32) for p in parts])
    return jnp.pad(row, (0, width - row.shape[0])).reshape(1, width)


def kernel(x, c, ctx, c_ctx, ada_w, ada_b, ln_g, ln_b, ab_w_in, ab_w_out, ml_ig_b, ml_fg_b, ml_norm_g,
           ssm_conv_w, ssm_conv_b, ssm_dt_b, ssm_a_log, ssm_d, ssm_norm_g, at_w_in, at_w_out, at_q_g, at_k_g,
           router_w, router_b, moe_w_gate, moe_w_up, moe_w_down):
    bsz, n_lat, d = x.shape
    n_ctx = ctx.shape[1]
    assert d == D_MODEL and bsz + 1 <= SUBLANES
    assert n_ctx % TM == 0 and n_lat % TM == 0 and n_lat % GRID_W == 0
    assert (n_ctx + n_lat) % ATT_TK == 0 and n_ctx % ATT_TQ == 0 and n_lat % ATT_TQ == 0
    assert (bsz * (n_ctx + n_lat)) % DISPATCH_ROWS == 0 and (bsz * n_lat) % DISPATCH_ROWS == 0
    ctx_tiles = n_ctx // TM
    ctx_chunks = n_ctx // CHUNK
    xa = jnp.concatenate([ctx, x], axis=1)

    crows = jnp.zeros((SUBLANES, d), F32).at[:bsz].set(c).at[bsz].set(c_ctx)
    mods = ada_call(crows, ada_w, ada_b)

    def mod_table(i):
        lat = mods[i, :bsz].reshape(bsz, 1, 6, d)
        cx = jnp.broadcast_to(mods[i, bsz].reshape(1, 1, 6, d), (bsz, 1, 6, d))
        return jnp.concatenate([cx, lat], axis=1)

    router_wt = router_w.T
    router_bc = router_b.reshape(N_EXPERTS, 1)

    modsel = mod_table(0)
    w_in = ab_w_in[0]
    s_q, s_k, s_v, s_o, s_ig, s_fg, s_z, s_xbc = (int(v) for v in
        (0, 512, 1024, 2048, 3072, 3072 + 8, 3072 + 16, 3072 + 16 + 1024))
    s_dt = s_xbc + S_CONV_CH
    w_big = jnp.concatenate([w_in[:, :s_ig], w_in[:, s_z:s_dt]], axis=1).astype(MXU_DTYPE)
    w_small = jnp.concatenate([w_in[:, s_ig:s_z], w_in[:, s_dt:]], axis=1)
    w_small = jnp.pad(w_small, ((0, 0), (0, GATE_LANES - w_small.shape[1])))
    q, k, v, o, z, xbc, gates = inproj0_call(xa, modsel, w_big, w_small, ctx_tiles)
    xbc_act = conv_call(xbc, ssm_conv_w[0], ssm_conv_b[0], ctx_tiles)

    grow = jnp.swapaxes(gates[:, :, :DT_OFF], 1, 2)
    dtrow = jnp.swapaxes(gates[:, :, DT_OFF:DT_OFF + N_DIR * S_HEADS], 1, 2)
    gate_b_row = _lane_row([ml_ig_b[0], ml_fg_b[0], ssm_dt_b[0]])
    gate_b_col = jnp.concatenate([ml_ig_b[0].reshape(-1), ml_fg_b[0].reshape(-1)]).reshape(-1, 1)
    a_neg = -jnp.exp(ssm_a_log[0].astype(F32)).reshape(-1)
    an_row = _lane_row([jnp.zeros((DT_OFF,), F32), a_neg])
    an_col = a_neg.reshape(-1, 1)
    dtb_col = ssm_dt_b[0].reshape(-1, 1)

    hm = None
    hs = None
    for dd in range(N_DIR):
        hm = mlstm_call(q, k, v, gates, grow, gate_b_row, gate_b_col, hm, rev=dd == 1, d=dd, ctx_chunks=ctx_chunks)
        hs = ssd_call(xbc_act, gates, dtrow, gate_b_row, dtb_col, an_row, an_col, hs, rev=dd == 1, d=dd,
                      ctx_chunks=ctx_chunks)

    dskip = jnp.repeat(ssm_d[0].astype(F32), S_HEAD_DIM).reshape(1, S_WIDTH)
    x1, h2t, e_idx, gates = outproj0_call(hm, o, hs, xbc_act, z, xa, modsel, ml_norm_g[0].reshape(1, -1), dskip,
                                          ssm_norm_g[0].reshape(1, -1), ab_w_out[0].astype(MXU_DTYPE),
                                          ln_g[0, 0].reshape(1, d), ln_b[0, 0].reshape(1, d), router_wt, router_bc,
                                          ctx_tiles)
    xa = moe_block(x1, h2t, e_idx, gates, modsel, ln_g[0, 1].reshape(1, d), ln_b[0, 1].reshape(1, d),
                   moe_w_gate, moe_w_up, moe_w_down, 0, ctx_tiles)

    modsel = mod_table(1)
    cos2, sin2 = _rope_tables(n_ctx, n_lat)
    qa, ka, va = inproj1_call(xa, modsel, at_w_in[0].astype(MXU_DTYPE), at_q_g[0].reshape(1, -1),
                              at_k_g[0].reshape(1, -1), cos2, sin2, ctx_tiles)
    att = attn_call(qa, ka, va, n_ctx)
    x1, h2t, e_idx, gates = outproj1_call(att, xa, modsel, at_w_out[0].astype(MXU_DTYPE), ln_g[1, 0].reshape(1, d),
                                          ln_b[1, 0].reshape(1, d), router_wt, router_bc, ctx_tiles)
    return moe_block(x1, h2t, e_idx, gates, modsel, ln_g[1, 1].reshape(1, d), ln_b[1, 1].reshape(1, d),
                     moe_w_gate, moe_w_up, moe_w_down, 1, 0)
```

```python
import functools
import math

import jax
import jax.numpy as jnp
from jax import lax
from jax.experimental import pallas as pl
from jax.experimental.pallas import tpu as pltpu

F32 = jnp.float32
MXU_DTYPE = jnp.bfloat16
HIGHEST = lax.Precision.HIGHEST

D_MODEL = 1024
DEPTH = 2
GRID_W = 64
CHUNK = 128
M_HEADS = 4
M_QK_DIM = D_MODEL // 8
M_V_DIM = D_MODEL // 4
M_WIDTH = M_HEADS * M_V_DIM
S_HEADS = 16
S_HEAD_DIM = D_MODEL // 16
S_GROUPS = 2
S_HEADS_PER_GROUP = S_HEADS // S_GROUPS
S_STATE = 128
S_WIDTH = S_HEADS * S_HEAD_DIM
S_CONV_CH = S_WIDTH + 2 * S_GROUPS * S_STATE
CONV_K = 4
N_DIR = 2
A_HEADS = 8
A_KV_HEADS = 2
A_HEAD_DIM = D_MODEL // A_HEADS
A_Q_W = A_HEADS * A_HEAD_DIM
A_KV_W = A_KV_HEADS * A_HEAD_DIM
ROPE_THETA = 10000.0
N_EXPERTS = 16
N_EXPERT_GROUPS = 4
EXPERTS_PER_GROUP = N_EXPERTS // N_EXPERT_GROUPS
TOP_K = 2
DEEPNORM_ALPHA = (2 * DEPTH) ** 0.25
LN_EPS = 1e-5
RMS_EPS = 1e-6

LANES = 128
SUBLANES = 8
TM = 256
MOE_BLK = 512
DISPATCH_ROWS = 512
ROW_CHUNKS = D_MODEL // LANES
assert ROW_CHUNKS == SUBLANES
ATT_TQ = 256
ATT_TK = 1408
VMEM_LIMIT = 56 * 1024 * 1024
GATE_LANES = 128
IG_OFF, FG_OFF, DT_OFF = 0, N_DIR * M_HEADS, 2 * N_DIR * M_HEADS
ATT_Q_SCALE = A_HEAD_DIM ** -0.5 * math.log2(math.e)


def _cparams(sem):
    return pltpu.CompilerParams(dimension_semantics=sem, vmem_limit_bytes=VMEM_LIMIT)


def _silu(x):
    return x / (1.0 + jnp.exp(-x))


def _sigmoid(x):
    return 1.0 / (1.0 + jnp.exp(-x))


def _softplus(x):
    return jnp.maximum(x, 0.0) + jnp.log(1.0 + jnp.exp(-jnp.abs(x)))


def _log_sigmoid(x):
    return jnp.minimum(x, 0.0) - jnp.log(1.0 + jnp.exp(-jnp.abs(x)))


def _layer_norm_rows(x, g, b):
    mu = jnp.mean(x, axis=-1, keepdims=True)
    xc = x - mu
    var = jnp.mean(xc * xc, axis=-1, keepdims=True)
    return xc * lax.rsqrt(var + LN_EPS) * g + b


def _mm(a, b):
    return jnp.dot(a.astype(MXU_DTYPE), b.astype(MXU_DTYPE), preferred_element_type=F32)


def _mm_nt(a, b):
    return lax.dot_general(a.astype(MXU_DTYPE), b.astype(MXU_DTYPE), (((1,), (1,)), ((), ())),
                           preferred_element_type=F32)


def _mm_tn(a, b):
    return lax.dot_general(a.astype(MXU_DTYPE), b.astype(MXU_DTYPE), (((0,), (0,)), ((), ())),
                           preferred_element_type=F32)


def _ada_kernel(c_ref, w_ref, b_ref, o_ref):
    s = _silu(c_ref[...])
    o_ref[0] = jnp.dot(s, w_ref[0], precision=HIGHEST, preferred_element_type=F32) + b_ref[0]


def ada_call(crows, ada_w, ada_b):
    depth, d, n6 = ada_w.shape
    tn = 1536
    return pl.pallas_call(
        _ada_kernel,
        grid=(depth, n6 // tn),
        in_specs=[pl.BlockSpec((SUBLANES, d), lambda i, j: (0, 0)),
                  pl.BlockSpec((1, d, tn), lambda i, j: (i, 0, j)),
                  pl.BlockSpec((1, 1, tn), lambda i, j: (i, 0, j))],
        out_specs=pl.BlockSpec((1, SUBLANES, tn), lambda i, j: (i, 0, j)),
        out_shape=jax.ShapeDtypeStruct((depth, SUBLANES, n6), F32),
        compiler_params=_cparams(("arbitrary", "arbitrary")),
        name="ada",
    )(crows, ada_w, ada_b.reshape(depth, 1, n6))


def _inproj0_kernel(x_ref, mod_ref, wb_ref, ws_ref, q_ref, k_ref, v_ref, o_ref, z_ref, xbc_ref, g_ref):
    x = x_ref[0]
    mod = mod_ref[0, 0]
    h = x * (1.0 + mod[1:2]) + mod[0:1]
    hb = h.astype(MXU_DTYPE)
    qk = M_HEADS * M_QK_DIM
    c0 = 0
    q_ref[0] = (jnp.dot(hb, wb_ref[:, c0:c0 + qk], preferred_element_type=F32)
                * (M_QK_DIM ** -0.5)).astype(q_ref.dtype)
    c0 += qk
    k_ref[0] = jnp.dot(hb, wb_ref[:, c0:c0 + qk], preferred_element_type=F32).astype(k_ref.dtype)
    c0 += qk
    v_ref[0] = jnp.dot(hb, wb_ref[:, c0:c0 + M_WIDTH], preferred_element_type=F32).astype(v_ref.dtype)
    c0 += M_WIDTH
    o_ref[0] = jnp.dot(hb, wb_ref[:, c0:c0 + M_WIDTH], preferred_element_type=F32)
    c0 += M_WIDTH
    z_ref[0] = jnp.dot(hb, wb_ref[:, c0:c0 + S_WIDTH], preferred_element_type=F32)
    c0 += S_WIDTH
    xbc_ref[0] = jnp.dot(hb, wb_ref[:, c0:c0 + S_CONV_CH], preferred_element_type=F32)
    g_ref[0] = jnp.dot(h, ws_ref[...], precision=HIGHEST, preferred_element_type=F32)


def inproj0_call(xa, modsel, w_big, w_small, ctx_tiles):
    bsz, ta, d = xa.shape
    nt = ta // TM
    qk = M_HEADS * M_QK_DIM
    widths = (qk, qk, M_WIDTH, M_WIDTH, S_WIDTH, S_CONV_CH, GATE_LANES)
    dtypes = (MXU_DTYPE, MXU_DTYPE, MXU_DTYPE, F32, F32, F32, F32)
    tok = lambda w: pl.BlockSpec((1, TM, w), lambda b, j: (b, j, 0))
    return pl.pallas_call(
        _inproj0_kernel,
        grid=(bsz, nt),
        in_specs=[tok(d),
                  pl.BlockSpec((1, 1, 6, d), lambda b, j: (b, (j >= ctx_tiles).astype(jnp.int32), 0, 0)),
                  pl.BlockSpec(w_big.shape, lambda b, j: (0, 0)),
                  pl.BlockSpec(w_small.shape, lambda b, j: (0, 0))],
        out_specs=[tok(w) for w in widths],
        out_shape=[jax.ShapeDtypeStruct((bsz, ta, w), dt) for w, dt in zip(widths, dtypes)],
        compiler_params=_cparams(("arbitrary", "arbitrary")),
        name="inproj0",
    )(xa, modsel, w_big, w_small)


def _conv_kernel(cur_ref, prev_ref, next_ref, w_ref, b_ref, o_ref, *, ctx_tiles, n_tiles):
    j = pl.program_id(1)
    has_prev = jnp.logical_and(j != 0, j != ctx_tiles)
    has_next = jnp.logical_and(j != ctx_tiles - 1, j != n_tiles - 1)
    prev = jnp.where(has_prev, prev_ref[0], 0.0)
    nxt = jnp.where(has_next, next_ref[0], 0.0)
    ext = jnp.concatenate([prev, cur_ref[0], nxt], axis=0)
    n = TM + 2 * SUBLANES
    w = w_ref[...]
    lo, hi = SUBLANES, SUBLANES + TM
    acc = ext[lo:hi] * w[2:3]
    acc = acc + pltpu.roll(ext, 2, 0)[lo:hi] * w[0:1]
    acc = acc + pltpu.roll(ext, 1, 0)[lo:hi] * w[1:2]
    acc = acc + pltpu.roll(ext, n - 1, 0)[lo:hi] * w[3:4]
    o_ref[0] = _silu(acc + b_ref[...])


def conv_call(xbc, conv_w, conv_b, ctx_tiles):
    bsz, ta, ch = xbc.shape
    nt = ta // TM
    r = TM // SUBLANES
    last = ta // SUBLANES - 1
    return pl.pallas_call(
        functools.partial(_conv_kernel, ctx_tiles=ctx_tiles, n_tiles=nt),
        grid=(bsz, nt),
        in_specs=[pl.BlockSpec((1, TM, ch), lambda b, j: (b, j, 0)),
                  pl.BlockSpec((1, SUBLANES, ch), lambda b, j: (b, jnp.maximum(j * r - 1, 0), 0)),
                  pl.BlockSpec((1, SUBLANES, ch), lambda b, j: (b, jnp.minimum((j + 1) * r, last), 0)),
                  pl.BlockSpec((CONV_K, ch), lambda b, j: (0, 0)),
                  pl.BlockSpec((1, ch), lambda b, j: (0, 0))],
        out_specs=pl.BlockSpec((1, TM, ch), lambda b, j: (b, j, 0)),
        out_shape=jax.ShapeDtypeStruct((bsz, ta, ch), F32),
        compiler_params=_cparams(("arbitrary", "arbitrary")),
        name="conv",
    )(xbc, xbc, xbc, conv_w, conv_b.reshape(1, ch))


def _chunk_order(i, rev, ctx_chunks, n_chunks):
    if not rev:
        return i
    return jnp.where(i < ctx_chunks, ctx_chunks - 1 - i, n_chunks - 1 - (i - ctx_chunks))


def _scan_masks(rev):
    r = lax.broadcasted_iota(jnp.int32, (CHUNK, CHUNK), 0)
    c = lax.broadcasted_iota(jnp.int32, (CHUNK, CHUNK), 1)
    mask = (c >= r) if rev else (c <= r)
    mask_t = (r >= c) if rev else (r <= c)
    return mask, mask.astype(F32), mask_t.astype(F32)


def _mlstm_kernel(*refs, rev, d, add_prev):
    if add_prev:
        (q_ref, k_ref, v_ref, gc_ref, gr_ref, brow_ref, bcol_ref, prev_ref, og_ref, mg_ref,
         o_ref, ct_s, n_s, m_s) = refs
    else:
        q_ref, k_ref, v_ref, gc_ref, gr_ref, brow_ref, bcol_ref, o_ref, ct_s, n_s, m_s = refs
        prev_ref = None

    @pl.when(pl.program_id(1) == 0)
    def _():
        ct_s[...] = jnp.zeros_like(ct_s)
        n_s[...] = jnp.zeros_like(n_s)
        m_s[...] = jnp.zeros_like(m_s)

    mask, mask_f, mask_tf = _scan_masks(rev)
    end = 0 if rev else CHUNK - 1
    gcol = gc_ref[0] + brow_ref[...]
    grow = gr_ref[0] + bcol_ref[...]
    nh2 = N_DIR * M_HEADS
    lf_col = _log_sigmoid(gcol)
    lf_row = _log_sigmoid(grow[FG_OFF:FG_OFF + nh2])
    b_col_all = jnp.dot(mask_f, lf_col, precision=HIGHEST, preferred_element_type=F32)
    b_row_all = jnp.dot(lf_row, mask_tf, precision=HIGHEST, preferred_element_type=F32)

    for j in range(M_HEADS):
        ci = d * M_HEADS + j
        bcol = b_col_all[:, FG_OFF + ci:FG_OFF + ci + 1]
        brow = b_row_all[ci:ci + 1, :]
        igcol = gcol[:, IG_OFF + ci:IG_OFF + ci + 1]
        igrow = grow[IG_OFF + ci:IG_OFF + ci + 1, :]
        b_end = bcol[end:end + 1, :]
        m_prev = m_s[j][:, 0:1]
        n_prev = n_s[j]
        ct_prev = ct_s[j]
        q = q_ref[0, :, j * M_QK_DIM:(j + 1) * M_QK_DIM]
        k = k_ref[0, :, j * M_QK_DIM:(j + 1) * M_QK_DIM]
        v = v_ref[0, :, j * M_V_DIM:(j + 1) * M_V_DIM]
        qf = q.astype(F32)
        kf = k.astype(F32)
        vf = v.astype(F32)

        dmat = jnp.where(mask, bcol - brow + igrow, -jnp.inf)
        inter = bcol + m_prev
        m_t = jnp.maximum(inter, jnp.max(dmat, axis=1, keepdims=True))
        sc = _mm_nt(q, k) * jnp.exp(dmat - m_t)
        a_in = jnp.exp(inter - m_t)
        num = _mm(sc, v) + a_in * _mm(q, ct_prev)
        den = jnp.sum(sc, axis=1, keepdims=True) + a_in * jnp.sum(qf * n_prev, axis=1, keepdims=True)
        h = num / jnp.maximum(jnp.abs(den), jnp.exp(-m_t))
        sl = slice(j * M_V_DIM, (j + 1) * M_V_DIM)
        if add_prev:
            h = h + prev_ref[0, :, sl]
            mu = jnp.mean(h, axis=-1, keepdims=True)
            hc = h - mu
            var = jnp.mean(hc * hc, axis=-1, keepdims=True)
            h = hc * lax.rsqrt(var + LN_EPS) * mg_ref[:, sl] * _sigmoid(og_ref[0, :, sl])
        o_ref[0, :, sl] = h.astype(o_ref.dtype)

        g_col = b_end - bcol + igcol
        g_row = b_end - brow + igrow
        g_max = jnp.max(g_row, axis=1, keepdims=True)
        w_col = jnp.exp(g_col - g_max)
        d_ct = _mm_tn(k, vf * w_col)
        d_n = jnp.sum(kf * w_col, axis=0, keepdims=True)
        m_new = jnp.maximum(b_end + m_prev, g_max)
        a = jnp.exp(b_end + m_prev - m_new)
        s = jnp.exp(g_max - m_new)
        ct_s[j] = a * ct_prev + s * d_ct
        n_s[j] = a * n_prev + s * d_n
        m_s[j] = jnp.broadcast_to(m_new, (1, LANES))


def mlstm_call(q, k, v, gcol, grow, bias_row, bias_col, prev, o_gate, m_norm_g, *, rev, d, ctx_chunks):
    bsz, ta, _ = q.shape
    nc = ta // CHUNK
    order = lambda i: _chunk_order(i, rev, ctx_chunks, nc)
    tok = lambda w: pl.BlockSpec((1, CHUNK, w), lambda b, i: (b, order(i), 0))
    in_specs = [tok(q.shape[-1]), tok(k.shape[-1]), tok(v.shape[-1]), tok(GATE_LANES),
                pl.BlockSpec((1, grow.shape[1], CHUNK), lambda b, i: (b, 0, order(i))),
                pl.BlockSpec(bias_row.shape, lambda b, i: (0, 0)),
                pl.BlockSpec(bias_col.shape, lambda b, i: (0, 0))]
    args = [q, k, v, gcol, grow, bias_row, bias_col]
    last = prev is not None
    if last:
        in_specs += [tok(M_WIDTH), tok(M_WIDTH), pl.BlockSpec(m_norm_g.shape, lambda b, i: (0, 0))]
        args += [prev, o_gate, m_norm_g]
    return pl.pallas_call(
        functools.partial(_mlstm_kernel, rev=rev, d=d, add_prev=last),
        grid=(bsz, nc),
        in_specs=in_specs,
        out_specs=tok(M_WIDTH),
        out_shape=jax.ShapeDtypeStruct((bsz, ta, M_WIDTH), MXU_DTYPE if last else F32),
        scratch_shapes=[pltpu.VMEM((M_HEADS, M_QK_DIM, M_V_DIM), F32),
                        pltpu.VMEM((M_HEADS, 1, M_QK_DIM), F32),
                        pltpu.VMEM((M_HEADS, 1, LANES), F32)],
        compiler_params=_cparams(("arbitrary", "arbitrary")),
        name="mlstm_rev" if rev else "mlstm_fwd",
    )(*args)


def _ssd_kernel(*refs, rev, d, add_prev):
    if add_prev:
        (x_ref, gc_ref, gr_ref, dtb_row_ref, dtb_col_ref, an_row_ref, an_col_ref, prev_ref, z_ref, dsk_ref, sg_ref,
         o_ref, ht_s) = refs
    else:
        x_ref, gc_ref, gr_ref, dtb_row_ref, dtb_col_ref, an_row_ref, an_col_ref, o_ref, ht_s = refs
        prev_ref = None

    @pl.when(pl.program_id(1) == 0)
    def _():
        ht_s[...] = jnp.zeros_like(ht_s)

    mask, mask_f, mask_tf = _scan_masks(rev)
    end = 0 if rev else CHUNK - 1
    lane = lax.broadcasted_iota(jnp.int32, (CHUNK, LANES), 1)
    first_half = lane < S_HEAD_DIM
    dt_col = _softplus(gc_ref[0] + dtb_row_ref[...])
    dt_row = _softplus(gr_ref[0] + dtb_col_ref[...])
    acs_col = jnp.dot(mask_f, dt_col * an_row_ref[...], precision=HIGHEST, preferred_element_type=F32)
    acs_row = jnp.dot(dt_row * an_col_ref[...], mask_tf, precision=HIGHEST, preferred_element_type=F32)
    a_end_row = acs_col[end:end + 1, :]
    e_cs = jnp.exp(acs_col)
    e_rem = jnp.exp(a_end_row - acs_col)
    e_end = jnp.exp(a_end_row)
    gw = S_HEADS_PER_GROUP * S_HEAD_DIM
    pairs = S_HEADS_PER_GROUP // 2

    def pick(arr, la):
        return jnp.where(first_half[:arr.shape[0]], arr[:, la:la + 1], arr[:, la + 1:la + 2])

    y_parts = []
    for g in range(S_GROUPS):
        bm = x_ref[0, :, S_WIDTH + g * S_STATE:S_WIDTH + (g + 1) * S_STATE]
        cm = x_ref[0, :, S_WIDTH + (S_GROUPS + g) * S_STATE:S_WIDTH + (S_GROUPS + g + 1) * S_STATE]
        cb = _mm_nt(cm, bm)
        ht_prev = ht_s[g]
        y_inter = _mm(cm, ht_prev)
        xw_parts = []
        decay_parts = []
        for p in range(pairs):
            h0 = g * S_HEADS_PER_GROUP + 2 * p
            la = DT_OFF + d * S_HEADS + h0
            ra = d * S_HEADS + h0
            lhs = []
            for u in range(2):
                seg = acs_col[:, la + u:la + u + 1] - acs_row[ra + u:ra + u + 1, :]
                dec = jnp.exp(jnp.where(mask, seg, -jnp.inf))
                lhs.append((cb * dec).astype(MXU_DTYPE))
            xs = x_ref[0, :, h0 * S_HEAD_DIM:(h0 + 2) * S_HEAD_DIM]
            xsd = xs * pick(dt_col, la)
            rhs = jnp.concatenate([jnp.where(first_half, xsd, 0.0), jnp.where(first_half, 0.0, xsd)],
                                  axis=0).astype(MXU_DTYPE)
            y = jnp.dot(jnp.concatenate(lhs, axis=1), rhs, preferred_element_type=F32)
            y = y + y_inter[:, p * LANES:(p + 1) * LANES] * pick(e_cs, la)
            sl = slice(h0 * S_HEAD_DIM, (h0 + 2) * S_HEAD_DIM)
            if add_prev:
                y = (y + prev_ref[0, :, sl] + dsk_ref[:, sl] * xs) * _silu(z_ref[0, :, sl])
                y_parts.append(y)
            else:
                o_ref[0, :, sl] = y
            xw_parts.append(xsd * pick(e_rem, la))
            decay_parts.append(pick(e_end, la))
        xw = jnp.concatenate(xw_parts, axis=1)
        decay = jnp.concatenate(decay_parts, axis=1)
        ht_s[g] = decay * ht_prev + _mm_tn(bm, xw)

    if add_prev:
        ys = jnp.concatenate(y_parts, axis=1)
        ys = ys * lax.rsqrt(jnp.mean(ys * ys, axis=-1, keepdims=True) + RMS_EPS) * sg_ref[...]
        o_ref[0] = ys.astype(o_ref.dtype)


def ssd_call(xbc_act, gcol, dtrow, dtb_row, dtb_col, an_row, an_col, prev, z, dskip, s_norm_g, *, rev, d,
             ctx_chunks):
    bsz, ta, ch = xbc_act.shape
    nc = ta // CHUNK
    order = lambda i: _chunk_order(i, rev, ctx_chunks, nc)
    tok = lambda w: pl.BlockSpec((1, CHUNK, w), lambda b, i: (b, order(i), 0))
    const = lambda a: pl.BlockSpec(a.shape, lambda b, i: (0, 0))
    in_specs = [tok(ch), tok(GATE_LANES),
                pl.BlockSpec((1, dtrow.shape[1], CHUNK), lambda b, i: (b, 0, order(i))),
                const(dtb_row), const(dtb_col), const(an_row), const(an_col)]
    args = [xbc_act, gcol, dtrow, dtb_row, dtb_col, an_row, an_col]
    last = prev is not None
    if last:
        in_specs += [tok(S_WIDTH), tok(S_WIDTH), const(dskip), const(s_norm_g)]
        args += [prev, z, dskip, s_norm_g]
    return pl.pallas_call(
        functools.partial(_ssd_kernel, rev=rev, d=d, add_prev=last),
        grid=(bsz, nc),
        in_specs=in_specs,
        out_specs=tok(S_WIDTH),
        out_shape=jax.ShapeDtypeStruct((bsz, ta, S_WIDTH), MXU_DTYPE if last else F32),
        scratch_shapes=[pltpu.VMEM((S_GROUPS, S_STATE, S_HEADS_PER_GROUP * S_HEAD_DIM), F32)],
        compiler_params=_cparams(("arbitrary", "arbitrary")),
        name="ssd_rev" if rev else "ssd_fwd",
    )(*args)


def _store_row_tiles(ref, val):
    for s in range(ROW_CHUNKS):
        ref[pl.ds(s, val.shape[0], stride=ROW_CHUNKS), :] = val[:, s * LANES:(s + 1) * LANES]


def _load_row_tiles(ref, rows):
    return jnp.concatenate([ref[pl.ds(s, rows, stride=ROW_CHUNKS), :] for s in range(ROW_CHUNKS)], axis=1)


def _finish_sublayer(x, y, mod, lng_ref, lnb_ref, rw_ref, rb_ref, x1_ref, h2_ref, e_ref, g_ref):
    x1 = _layer_norm_rows(DEEPNORM_ALPHA * x + mod[2:3] * y, lng_ref[...], lnb_ref[...])
    x1_ref[0] = x1
    h2 = x1 * (1.0 + mod[4:5]) + mod[3:4]
    _store_row_tiles(h2_ref, h2)
    e, g = _route(h2, rw_ref, rb_ref)
    e_ref[...] = e
    g_ref[...] = g


def _outproj0_kernel(ym_ref, ys_ref, x_ref, mod_ref, w_ref, lng_ref, lnb_ref, rw_ref, rb_ref,
                     x1_ref, h2_ref, e_ref, g_ref):
    y = (jnp.dot(ym_ref[0], w_ref[:M_WIDTH, :], preferred_element_type=F32)
         + jnp.dot(ys_ref[0], w_ref[M_WIDTH:, :], preferred_element_type=F32))
    _finish_sublayer(x_ref[0], y, mod_ref[0, 0], lng_ref, lnb_ref, rw_ref, rb_ref, x1_ref, h2_ref, e_ref, g_ref)


def _sublayer_out(bsz, tt, d):
    nt = tt // TM
    n = bsz * tt
    specs = [pl.BlockSpec((1, TM, d), lambda b, j: (b, j, 0)),
             pl.BlockSpec((TM * ROW_CHUNKS, LANES), lambda b, j: (b * nt + j, 0)),
             pl.BlockSpec((TOP_K, TM), lambda b, j: (0, b * nt + j)),
             pl.BlockSpec((TOP_K, TM), lambda b, j: (0, b * nt + j))]
    shapes = [jax.ShapeDtypeStruct((bsz, tt, d), F32), jax.ShapeDtypeStruct((n * ROW_CHUNKS, LANES), F32),
              jax.ShapeDtypeStruct((TOP_K, n), jnp.int32), jax.ShapeDtypeStruct((TOP_K, n), F32)]
    return specs, shapes


def outproj0_call(ym, ys, xa, modsel, w_out, ln_g, ln_b, router_wt, router_b, ctx_tiles):
    bsz, ta, d = xa.shape
    nt = ta // TM
    tok = lambda w: pl.BlockSpec((1, TM, w), lambda b, j: (b, j, 0))
    const = lambda a: pl.BlockSpec(a.shape, lambda b, j: (0, 0))
    out_specs, out_shape = _sublayer_out(bsz, ta, d)
    return pl.pallas_call(
        _outproj0_kernel,
        grid=(bsz, nt),
        in_specs=[tok(M_WIDTH), tok(S_WIDTH), tok(d),
                  pl.BlockSpec((1, 1, 6, d), lambda b, j: (b, (j >= ctx_tiles).astype(jnp.int32), 0, 0)),
                  const(w_out), const(ln_g), const(ln_b), const(router_wt), const(router_b)],
        out_specs=out_specs,
        out_shape=out_shape,
        compiler_params=_cparams(("arbitrary", "arbitrary")),
        name="outproj0",
    )(ym, ys, xa, modsel, w_out, ln_g, ln_b, router_wt, router_b)


def _top2(vals, probs):
    v1, i1, p1 = vals[0], jnp.zeros_like(vals[0], dtype=jnp.int32), probs[0]
    for i in range(1, len(vals)):
        better = vals[i] > v1
        v1 = jnp.where(better, vals[i], v1)
        i1 = jnp.where(better, i, i1)
        p1 = jnp.where(better, probs[i], p1)
    v2 = jnp.full_like(vals[0], -jnp.inf)
    i2 = jnp.zeros_like(i1)
    p2 = jnp.zeros_like(p1)
    for i in range(len(vals)):
        better = jnp.logical_and(i1 != i, vals[i] > v2)
        v2 = jnp.where(better, vals[i], v2)
        i2 = jnp.where(better, i, i2)
        p2 = jnp.where(better, probs[i], p2)
    return v1, i1, p1, v2, i2, p2


def _split_hi_lo(x):
    hi = x.astype(MXU_DTYPE)
    return hi, (x - hi.astype(F32)).astype(MXU_DTYPE)


def _route(h, w_ref, b_ref):
    h_hi, h_lo = _split_hi_lo(h)
    w_hi, w_lo = _split_hi_lo(w_ref[...])
    lg = (jnp.dot(h_hi, w_hi, preferred_element_type=F32) + jnp.dot(h_lo, w_hi, preferred_element_type=F32)
          + jnp.dot(h_hi, w_lo, preferred_element_type=F32))
    logits = lg.T[:N_EXPERTS]
    mx = jnp.max(logits, axis=0, keepdims=True)
    ex = jnp.exp(logits - mx)
    probs = ex / jnp.sum(ex, axis=0, keepdims=True)
    sel = probs + b_ref[...]
    best = None
    for g in range(N_EXPERT_GROUPS):
        rows = range(g * EXPERTS_PER_GROUP, (g + 1) * EXPERTS_PER_GROUP)
        v1, i1, p1, v2, i2, p2 = _top2([sel[r:r + 1] for r in rows], [probs[r:r + 1] for r in rows])
        cand = (v1 + v2, i1 + g * EXPERTS_PER_GROUP, p1, i2 + g * EXPERTS_PER_GROUP, p2)
        if best is None:
            best = cand
        else:
            better = cand[0] > best[0]
            best = tuple(jnp.where(better, c, o) for c, o in zip(cand, best))
    _, e1, p1, e2, p2 = best
    tot = p1 + p2
    return jnp.concatenate([e1, e2], axis=0), jnp.concatenate([p1 / tot, p2 / tot], axis=0)


def _row_tile(ref, r):
    return ref.at[pl.ds(pl.multiple_of(r * ROW_CHUNKS, ROW_CHUNKS), ROW_CHUNKS)]


def _dispatch_kernel(dest_ref, zblk_ref, h_ref, xs_ref, zero_s, sem, zsem, *, n_tok):
    base = pl.program_id(0) * DISPATCH_ROWS
    blk_rows = MOE_BLK * ROW_CHUNKS

    @pl.when(pl.program_id(0) == 0)
    def _():
        zero_s[...] = jnp.zeros_like(zero_s)

        def zero_copy(t):
            start = pl.multiple_of(zblk_ref[t] * blk_rows, blk_rows)
            return pltpu.make_async_copy(zero_s, xs_ref.at[pl.ds(start, blk_rows)], zsem)

        for t in range(2 * N_EXPERTS):
            pl.when(zblk_ref[t] >= 0)(lambda t=t: zero_copy(t).start())
        for t in range(2 * N_EXPERTS):
            pl.when(zblk_ref[t] >= 0)(lambda t=t: zero_copy(t).wait())

    def slot_copy(r, slot):
        return pltpu.make_async_copy(_row_tile(h_ref, r), _row_tile(xs_ref, slot), sem)

    def issue(r, carry):
        for c in range(TOP_K):
            slot_copy(r, dest_ref[c * n_tok + base + r]).start(priority=c)
        return carry

    def drain(r, carry):
        for c in range(TOP_K):
            slot_copy(r, 0).wait()
        return carry

    lax.fori_loop(0, DISPATCH_ROWS, issue, 0, unroll=8)
    lax.fori_loop(0, DISPATCH_ROWS, drain, 0, unroll=8)


def dispatch_call(dest, zero_blocks, h2t, n_slots):
    n_tok = h2t.shape[0] // ROW_CHUNKS
    return pl.pallas_call(
        functools.partial(_dispatch_kernel, n_tok=n_tok),
        grid_spec=pltpu.PrefetchScalarGridSpec(
            num_scalar_prefetch=2,
            grid=(n_tok // DISPATCH_ROWS,),
            in_specs=[pl.BlockSpec((DISPATCH_ROWS * ROW_CHUNKS, LANES), lambda i, dest, zb: (i, 0))],
            out_specs=pl.BlockSpec(memory_space=pl.ANY),
            scratch_shapes=[pltpu.VMEM((MOE_BLK * ROW_CHUNKS, LANES), h2t.dtype),
                            pltpu.SemaphoreType.DMA(()), pltpu.SemaphoreType.DMA(())]),
        out_shape=jax.ShapeDtypeStruct((n_slots * ROW_CHUNKS, LANES), h2t.dtype),
        compiler_params=_cparams(("arbitrary",)),
        name="dispatch",
    )(dest, zero_blocks, h2t)


def _experts_kernel(be_ref, cnt_ref, x_ref, wg_ref, wu_ref, wd_ref, o_ref, wg_s, wu_s, wd_s):
    i = pl.program_id(0)
    e = be_ref[i]
    e_before = be_ref[jnp.maximum(i - 1, 0)]
    cnt = cnt_ref[i]

    @pl.when(jnp.logical_or(i == 0, e != e_before))
    def _():
        wg_s[...] = wg_ref[0, 0].astype(wg_s.dtype)
        wu_s[...] = wu_ref[0, 0].astype(wu_s.dtype)
        wd_s[...] = wd_ref[0, 0].astype(wd_s.dtype)

    @pl.when(cnt > 0)
    def _():
        xb = _load_row_tiles(x_ref, MOE_BLK).astype(MXU_DTYPE)
        gt = jnp.dot(xb, wg_s[...], preferred_element_type=F32)
        up = jnp.dot(xb, wu_s[...], preferred_element_type=F32)
        y = jnp.dot((_silu(gt) * up).astype(MXU_DTYPE), wd_s[...], preferred_element_type=F32)
        _store_row_tiles(o_ref, y)

    @pl.when(cnt == 0)
    def _():
        o_ref[...] = jnp.zeros_like(o_ref)


def experts_call(blk_exp, blk_cnt, xs, w_gate, w_up, w_down, layer):
    n_slots = xs.shape[0] // ROW_CHUNKS
    d, f = w_gate.shape[-2:]
    wspec = lambda a: pl.BlockSpec((1, 1) + a.shape[2:], lambda i, be, cnt: (layer, be[i], 0, 0))
    blk = pl.BlockSpec((MOE_BLK * ROW_CHUNKS, LANES), lambda i, be, cnt: (i, 0))
    return pl.pallas_call(
        _experts_kernel,
        grid_spec=pltpu.PrefetchScalarGridSpec(
            num_scalar_prefetch=2,
            grid=(n_slots // MOE_BLK,),
            in_specs=[blk, wspec(w_gate), wspec(w_up), wspec(w_down)],
            out_specs=blk,
            scratch_shapes=[pltpu.VMEM((d, f), MXU_DTYPE), pltpu.VMEM((d, f), MXU_DTYPE),
                            pltpu.VMEM((f, d), MXU_DTYPE)]),
        out_shape=jax.ShapeDtypeStruct(xs.shape, F32),
        compiler_params=_cparams(("arbitrary",)),
        name="experts",
    )(blk_exp, blk_cnt, xs, w_gate, w_up, w_down)


def _combine_kernel(dest_ref, y_ref, gate_ref, x_ref, mod_ref, lng_ref, lnb_ref, o_ref, ybuf, sems, *, n_tok, nt,
                    n_steps):
    step = pl.program_id(0) * nt + pl.program_id(1)
    slot = step % 2

    def row_copy(buf, r, c, src_slot):
        return pltpu.make_async_copy(_row_tile(y_ref, src_slot), _row_tile(ybuf.at[buf, c], r), sems.at[buf])

    def issue(tile, buf):
        base = tile * TM

        def body(r, carry):
            for c in range(TOP_K):
                row_copy(buf, r, c, dest_ref[c * n_tok + base + r]).start(priority=c)
            return carry

        lax.fori_loop(0, TM, body, 0, unroll=8)

    pl.when(step == 0)(lambda: issue(0, 0))
    pl.when(step + 1 < n_steps)(lambda: issue(step + 1, 1 - slot))

    def drain(r, carry):
        for c in range(TOP_K):
            row_copy(slot, r, c, 0).wait()
        return carry

    lax.fori_loop(0, TM, drain, 0, unroll=8)
    mod = mod_ref[0, 0]
    gate = gate_ref[...]
    y = (gate[:, 0:1] * _load_row_tiles(ybuf.at[slot, 0], TM)
         + gate[:, 1:2] * _load_row_tiles(ybuf.at[slot, 1], TM))
    o_ref[0] = _layer_norm_rows(DEEPNORM_ALPHA * x_ref[0] + mod[5:6] * y, lng_ref[...], lnb_ref[...])


def combine_call(dest, y_slots, gate_cols, x1, modsel, ln_g, ln_b, ctx_tiles):
    bsz, tt, d = x1.shape
    nt = tt // TM
    return pl.pallas_call(
        functools.partial(_combine_kernel, n_tok=bsz * tt, nt=nt, n_steps=bsz * nt),
        grid_spec=pltpu.PrefetchScalarGridSpec(
            num_scalar_prefetch=1,
            grid=(bsz, nt),
            in_specs=[pl.BlockSpec(memory_space=pl.ANY),
                      pl.BlockSpec((TM, TOP_K), lambda b, j, dest: (b * nt + j, 0)),
                      pl.BlockSpec((1, TM, d), lambda b, j, dest: (b, j, 0)),
                      pl.BlockSpec((1, 1, 6, d),
                                   lambda b, j, dest: (b, (j >= ctx_tiles).astype(jnp.int32), 0, 0)),
                      pl.BlockSpec(ln_g.shape, lambda b, j, dest: (0, 0)),
                      pl.BlockSpec(ln_b.shape, lambda b, j, dest: (0, 0))],
            out_specs=pl.BlockSpec((1, TM, d), lambda b, j, dest: (b, j, 0)),
            scratch_shapes=[pltpu.VMEM((2, TOP_K, TM * ROW_CHUNKS, LANES), F32), pltpu.SemaphoreType.DMA((2,))]),
        out_shape=jax.ShapeDtypeStruct((bsz, tt, d), F32),
        compiler_params=_cparams(("arbitrary", "arbitrary")),
        name="combine",
    )(dest, y_slots, gate_cols, x1, modsel, ln_g, ln_b)


def moe_block(x1, h2t, e_idx, gates, modsel, ln_g, ln_b, w_gate, w_up, w_down, layer, ctx_tiles):
    bsz, tt, d = x1.shape
    n = bsz * tt

    n_asg = TOP_K * n
    flat_e = e_idx.reshape(n_asg)
    onehot = (flat_e[:, None] == jnp.arange(N_EXPERTS, dtype=jnp.int32)[None, :]).astype(jnp.int32)
    csum = jnp.cumsum(onehot, axis=0)
    rank = jnp.sum(onehot * csum, axis=1) - 1
    counts = csum[-1]
    padded = (counts + MOE_BLK - 1) // MOE_BLK * MOE_BLK
    pend = jnp.cumsum(padded)
    pstart = pend - padded
    dest = (pstart[flat_e] + rank).astype(jnp.int32)
    n_blocks = -(-n_asg // MOE_BLK) + N_EXPERTS
    blk_start = jnp.arange(n_blocks, dtype=jnp.int32) * MOE_BLK
    blk_exp = jnp.sum((pend[None, :] <= blk_start[:, None]).astype(jnp.int32), axis=1)
    blk_exp = jnp.minimum(blk_exp, N_EXPERTS - 1)
    blk_cnt = jnp.clip(counts[blk_exp] - (blk_start - pstart[blk_exp]), 0, MOE_BLK).astype(jnp.int32)

    part = jnp.where(counts % MOE_BLK != 0, pend // MOE_BLK - 1, -1)
    tail = pend[-1] // MOE_BLK + jnp.arange(N_EXPERTS, dtype=jnp.int32)
    tail = jnp.where(tail < n_blocks, tail, -1)
    zero_blocks = jnp.concatenate([part, tail]).astype(jnp.int32)

    xs = dispatch_call(dest, zero_blocks, h2t, n_blocks * MOE_BLK)
    y_slots = experts_call(blk_exp, blk_cnt, xs, w_gate, w_up, w_down, layer)
    return combine_call(dest, y_slots, gates.T, x1, modsel, ln_g, ln_b, ctx_tiles)


def _inproj1_kernel(x_ref, mod_ref, w_ref, qg_ref, kg_ref, cos_ref, sin_ref, q_ref, k_ref, v_ref):
    mod = mod_ref[0, 0]
    hb = (x_ref[0] * (1.0 + mod[1:2]) + mod[0:1]).astype(MXU_DTYPE)
    cos = cos_ref[...]
    sin = sin_ref[...]

    def norm_rope(t, g):
        t = t * lax.rsqrt(jnp.mean(t * t, axis=-1, keepdims=True) + RMS_EPS) * g
        return t * cos + pltpu.roll(t, A_HEAD_DIM // 2, 1) * sin

    for j in range(A_HEADS):
        sl = slice(j * A_HEAD_DIM, (j + 1) * A_HEAD_DIM)
        t = jnp.dot(hb, w_ref[:, sl], preferred_element_type=F32)
        q_ref[0, :, sl] = (norm_rope(t, qg_ref[...]) * ATT_Q_SCALE).astype(q_ref.dtype)
    for j in range(A_KV_HEADS):
        sl = slice(j * A_HEAD_DIM, (j + 1) * A_HEAD_DIM)
        t = jnp.dot(hb, w_ref[:, A_Q_W + j * A_HEAD_DIM:A_Q_W + (j + 1) * A_HEAD_DIM], preferred_element_type=F32)
        k_ref[0, :, sl] = norm_rope(t, kg_ref[...]).astype(k_ref.dtype)
    v_ref[0] = jnp.dot(hb, w_ref[:, A_Q_W + A_KV_W:], preferred_element_type=F32).astype(v_ref.dtype)


def inproj1_call(xa, modsel, w_in, q_g, k_g, cos2, sin2, ctx_tiles):
    bsz, ta, d = xa.shape
    nt = ta // TM
    tok = lambda w: pl.BlockSpec((1, TM, w), lambda b, j: (b, j, 0))
    const = lambda a: pl.BlockSpec(a.shape, lambda b, j: (0, 0))
    widths = (A_Q_W, A_KV_W, A_KV_W)
    return pl.pallas_call(
        _inproj1_kernel,
        grid=(bsz, nt),
        in_specs=[tok(d),
                  pl.BlockSpec((1, 1, 6, d), lambda b, j: (b, (j >= ctx_tiles).astype(jnp.int32), 0, 0)),
                  const(w_in), const(q_g), const(k_g),
                  pl.BlockSpec((TM, A_HEAD_DIM), lambda b, j: (j, 0)),
                  pl.BlockSpec((TM, A_HEAD_DIM), lambda b, j: (j, 0))],
        out_specs=[tok(w) for w in widths],
        out_shape=[jax.ShapeDtypeStruct((bsz, ta, w), MXU_DTYPE) for w in widths],
        compiler_params=_cparams(("arbitrary", "arbitrary")),
        name="inproj1",
    )(xa, modsel, w_in, q_g, k_g, cos2, sin2)


def _attn_kernel(q_ref, qn_ref, k_ref, vt_ref, o_ref, s_buf):
    rep = A_HEADS // A_KV_HEADS
    n_kv = vt_ref.shape[2]
    assert n_kv % 2 == 0

    def transposed(ref):
        return [ref[0, :, r * A_HEAD_DIM:(r + 1) * A_HEAD_DIM].astype(F32).T.astype(MXU_DTYPE) for r in range(rep)]

    qts = transposed(q_ref)

    def scores(j, slot, qt=qts):
        kb = k_ref[0, pl.ds(pl.multiple_of(j * ATT_TK, ATT_TK), ATT_TK), :]
        for r in range(rep):
            s_buf[slot, r] = jnp.dot(kb, qt[r], preferred_element_type=F32)

    def consume(j, slot, stats):
        vt = vt_ref[0, 0, j]
        new = []
        for r in range(rep):
            m, l, acc = stats[r]
            s = s_buf[slot, r]
            m_new = jnp.maximum(m, jnp.max(s, axis=0, keepdims=True))
            p = jnp.exp2(s - m_new)
            alpha = jnp.exp2(m - m_new)
            l = alpha * l + jnp.sum(p, axis=0, keepdims=True)
            acc = alpha * acc + jnp.dot(vt, p.astype(vt.dtype), preferred_element_type=F32)
            new.append((m_new, l, acc))
        return tuple(new)

    def pair(i, stats):
        j = 2 * i
        scores(j + 1, 1)
        stats = consume(j, 0, stats)
        scores(j + 2, 0)
        return consume(j + 1, 1, stats)

    stats = tuple((jnp.full((1, ATT_TQ), -jnp.inf, F32), jnp.zeros((1, ATT_TQ), F32),
                   jnp.zeros((A_HEAD_DIM, ATT_TQ), F32)) for _ in range(rep))
    pl.when(pl.program_id(2) == 0)(lambda: scores(0, 0))
    stats = lax.fori_loop(0, n_kv // 2 - 1, pair, stats)
    scores(n_kv - 1, 1)
    stats = consume(n_kv - 2, 0, stats)
    scores(0, 0, transposed(qn_ref))
    final = consume(n_kv - 1, 1, stats)
    for r in range(rep):
        _, l, acc = final[r]
        o_ref[0, :, r * A_HEAD_DIM:(r + 1) * A_HEAD_DIM] = (acc / l).T.astype(o_ref.dtype)


def attn_call(q, k, v, n_ctx):
    bsz, ta, _ = q.shape
    t_lat = ta - n_ctx
    gw = (A_HEADS // A_KV_HEADS) * A_HEAD_DIM
    q_off = n_ctx // ATT_TQ
    n_kv = ta // ATT_TK
    vt = v.reshape(bsz, n_kv, ATT_TK, A_KV_HEADS, A_HEAD_DIM).transpose(0, 3, 1, 4, 2)
    nq = t_lat // ATT_TQ
    return pl.pallas_call(
        _attn_kernel,
        grid=(bsz, A_KV_HEADS, nq),
        in_specs=[pl.BlockSpec((1, ATT_TQ, gw), lambda b, g, i: (b, i + q_off, g)),
                  pl.BlockSpec((1, ATT_TQ, gw), lambda b, g, i: (b, jnp.minimum(i + 1, nq - 1) + q_off, g)),
                  pl.BlockSpec((1, ta, A_HEAD_DIM), lambda b, g, i: (b, 0, g)),
                  pl.BlockSpec((1, 1, n_kv, A_HEAD_DIM, ATT_TK), lambda b, g, i: (b, g, 0, 0, 0))],
        out_specs=pl.BlockSpec((1, ATT_TQ, gw), lambda b, g, i: (b, i, g)),
        out_shape=jax.ShapeDtypeStruct((bsz, t_lat, A_Q_W), MXU_DTYPE),
        scratch_shapes=[pltpu.VMEM((2, A_HEADS // A_KV_HEADS, ATT_TK, ATT_TQ), F32)],
        compiler_params=_cparams(("arbitrary", "arbitrary", "arbitrary")),
        name="attention",
    )(q, q, k, vt)


def _outproj1_kernel(a_ref, x_ref, mod_ref, w_ref, lng_ref, lnb_ref, rw_ref, rb_ref, x1_ref, h2_ref, e_ref, g_ref):
    y = jnp.dot(a_ref[0], w_ref[...], preferred_element_type=F32)
    _finish_sublayer(x_ref[0], y, mod_ref[0, 0], lng_ref, lnb_ref, rw_ref, rb_ref, x1_ref, h2_ref, e_ref, g_ref)


def outproj1_call(att, xa, modsel, w_out, ln_g, ln_b, router_wt, router_b, ctx_tiles):
    bsz, t_lat, _ = att.shape
    d = xa.shape[-1]
    const = lambda a: pl.BlockSpec(a.shape, lambda b, j: (0, 0))
    out_specs, out_shape = _sublayer_out(bsz, t_lat, d)
    return pl.pallas_call(
        _outproj1_kernel,
        grid=(bsz, t_lat // TM),
        in_specs=[pl.BlockSpec((1, TM, att.shape[-1]), lambda b, j: (b, j, 0)),
                  pl.BlockSpec((1, TM, d), lambda b, j: (b, j + ctx_tiles, 0)),
                  pl.BlockSpec((1, 1, 6, d), lambda b, j: (b, 1, 0, 0)),
                  const(w_out), const(ln_g), const(ln_b), const(router_wt), const(router_b)],
        out_specs=out_specs,
        out_shape=out_shape,
        compiler_params=_cparams(("arbitrary", "arbitrary")),
        name="outproj1",
    )(att, xa, modsel, w_out, ln_g, ln_b, router_wt, router_b)


def _rope_tables(n_ctx, n_lat):
    rows = n_lat // GRID_W
    row = jnp.repeat(jnp.arange(rows), GRID_W).astype(F32)
    col = jnp.tile(jnp.arange(GRID_W), rows).astype(F32)
    n_freq = A_HEAD_DIM // 4
    inv = ROPE_THETA ** (-jnp.arange(n_freq, dtype=F32) / n_freq)
    ang = jnp.concatenate([row[:, None] * inv, col[:, None] * inv], -1)
    cos, sin = jnp.cos(ang), jnp.sin(ang)
    cos2 = jnp.concatenate([cos, cos], -1)
    sin2 = jnp.concatenate([-sin, sin], -1)
    cos2 = jnp.concatenate([jnp.ones((n_ctx, A_HEAD_DIM), F32), cos2], 0)
    sin2 = jnp.concatenate([jnp.zeros((n_ctx, A_HEAD_DIM), F32), sin2], 0)
    return cos2, sin2


def _lane_row(parts, width=GATE_LANES):
    row = jnp.concatenate([p.reshape(-1).astype(F32) for p in parts])
    return jnp.pad(row, (0, width - row.shape[0])).reshape(1, width)


def kernel(x, c, ctx, c_ctx, ada_w, ada_b, ln_g, ln_b, ab_w_in, ab_w_out, ml_ig_b, ml_fg_b, ml_norm_g,
           ssm_conv_w, ssm_conv_b, ssm_dt_b, ssm_a_log, ssm_d, ssm_norm_g, at_w_in, at_w_out, at_q_g, at_k_g,
           router_w, router_b, moe_w_gate, moe_w_up, moe_w_down):
    bsz, n_lat, d = x.shape
    n_ctx = ctx.shape[1]
    assert d == D_MODEL and bsz + 1 <= SUBLANES
    assert n_ctx % TM == 0 and n_lat % TM == 0 and n_lat % GRID_W == 0
    assert (n_ctx + n_lat) % ATT_TK == 0 and n_ctx % ATT_TQ == 0 and n_lat % ATT_TQ == 0
    assert (bsz * (n_ctx + n_lat)) % DISPATCH_ROWS == 0 and (bsz * n_lat) % DISPATCH_ROWS == 0
    ctx_tiles = n_ctx // TM
    ctx_chunks = n_ctx // CHUNK
    xa = jnp.concatenate([ctx, x], axis=1)

    crows = jnp.zeros((SUBLANES, d), F32).at[:bsz].set(c).at[bsz].set(c_ctx)
    mods = ada_call(crows, ada_w, ada_b)

    def mod_table(i):
        lat = mods[i, :bsz].reshape(bsz, 1, 6, d)
        cx = jnp.broadcast_to(mods[i, bsz].reshape(1, 1, 6, d), (bsz, 1, 6, d))
        return jnp.concatenate([cx, lat], axis=1)

    router_wt = jnp.pad(router_w, ((0, 0), (0, LANES - N_EXPERTS)))
    router_bc = router_b.reshape(N_EXPERTS, 1)

    modsel = mod_table(0)
    w_in = ab_w_in[0]
    s_q, s_k, s_v, s_o, s_ig, s_fg, s_z, s_xbc = (int(v) for v in
        (0, 512, 1024, 2048, 3072, 3072 + 8, 3072 + 16, 3072 + 16 + 1024))
    s_dt = s_xbc + S_CONV_CH
    w_big = jnp.concatenate([w_in[:, :s_ig], w_in[:, s_z:s_dt]], axis=1).astype(MXU_DTYPE)
    w_small = jnp.concatenate([w_in[:, s_ig:s_z], w_in[:, s_dt:]], axis=1)
    w_small = jnp.pad(w_small, ((0, 0), (0, GATE_LANES - w_small.shape[1])))
    q, k, v, o, z, xbc, gates = inproj0_call(xa, modsel, w_big, w_small, ctx_tiles)
    xbc_act = conv_call(xbc, ssm_conv_w[0], ssm_conv_b[0], ctx_tiles)

    grow = jnp.swapaxes(gates[:, :, :DT_OFF], 1, 2)
    dtrow = jnp.swapaxes(gates[:, :, DT_OFF:DT_OFF + N_DIR * S_HEADS], 1, 2)
    gate_b_row = _lane_row([ml_ig_b[0], ml_fg_b[0], ssm_dt_b[0]])
    gate_b_col = jnp.concatenate([ml_ig_b[0].reshape(-1), ml_fg_b[0].reshape(-1)]).reshape(-1, 1)
    a_neg = -jnp.exp(ssm_a_log[0].astype(F32)).reshape(-1)
    an_row = _lane_row([jnp.zeros((DT_OFF,), F32), a_neg])
    an_col = a_neg.reshape(-1, 1)
    dtb_col = ssm_dt_b[0].reshape(-1, 1)

    dskip = jnp.repeat(ssm_d[0].astype(F32), S_HEAD_DIM).reshape(1, S_WIDTH)
    m_norm_g = ml_norm_g[0].reshape(1, -1)
    s_norm_g = ssm_norm_g[0].reshape(1, -1)
    assert N_DIR == 2
    hm = None
    hs = None
    for dd in range(N_DIR):
        hm = mlstm_call(q, k, v, gates, grow, gate_b_row, gate_b_col, hm, o, m_norm_g, rev=dd == 1, d=dd,
                        ctx_chunks=ctx_chunks)
        hs = ssd_call(xbc_act, gates, dtrow, gate_b_row, dtb_col, an_row, an_col, hs, z, dskip, s_norm_g,
                      rev=dd == 1, d=dd, ctx_chunks=ctx_chunks)

    x1, h2t, e_idx, gates = outproj0_call(hm, hs, xa, modsel, ab_w_out[0].astype(MXU_DTYPE),
                                          ln_g[0, 0].reshape(1, d), ln_b[0, 0].reshape(1, d), router_wt, router_bc,
                                          ctx_tiles)
    xa = moe_block(x1, h2t, e_idx, gates, modsel, ln_g[0, 1].reshape(1, d), ln_b[0, 1].reshape(1, d),
                   moe_w_gate, moe_w_up, moe_w_down, 0, ctx_tiles)

    modsel = mod_table(1)
    cos2, sin2 = _rope_tables(n_ctx, n_lat)
    qa, ka, va = inproj1_call(xa, modsel, at_w_in[0].astype(MXU_DTYPE), at_q_g[0].reshape(1, -1),
                              at_k_g[0].reshape(1, -1), cos2, sin2, ctx_tiles)
    att = attn_call(qa, ka, va, n_ctx)
    x1, h2t, e_idx, gates = outproj1_call(att, xa, modsel, at_w_out[0].astype(MXU_DTYPE), ln_g[1, 0].reshape(1, d),
                                          ln_b[1, 0].reshape(1, d), router_wt, router_bc, ctx_tiles)
    return moe_block(x1, h2t, e_idx, gates, modsel, ln_g[1, 1].reshape(1, d), ln_b[1, 1].reshape(1, d),
                     moe_w_gate, moe_w_up, moe_w_down, 1, 0)
```

```python
import functools
import math

import jax
import jax.numpy as jnp
from jax import lax
from jax.experimental import pallas as pl
from jax.experimental.pallas import tpu as pltpu

F32 = jnp.float32
MXU_DTYPE = jnp.bfloat16
HIGHEST = lax.Precision.HIGHEST

D_MODEL = 1024
DEPTH = 2
GRID_W = 64
CHUNK = 128
M_HEADS = 4
M_QK_DIM = D_MODEL // 8
M_V_DIM = D_MODEL // 4
M_WIDTH = M_HEADS * M_V_DIM
S_HEADS = 16
S_HEAD_DIM = D_MODEL // 16
S_GROUPS = 2
S_HEADS_PER_GROUP = S_HEADS // S_GROUPS
S_STATE = 128
S_WIDTH = S_HEADS * S_HEAD_DIM
S_CONV_CH = S_WIDTH + 2 * S_GROUPS * S_STATE
CONV_K = 4
N_DIR = 2
A_HEADS = 8
A_KV_HEADS = 2
A_HEAD_DIM = D_MODEL // A_HEADS
A_Q_W = A_HEADS * A_HEAD_DIM
A_KV_W = A_KV_HEADS * A_HEAD_DIM
ROPE_THETA = 10000.0
N_EXPERTS = 16
N_EXPERT_GROUPS = 4
EXPERTS_PER_GROUP = N_EXPERTS // N_EXPERT_GROUPS
TOP_K = 2
DEEPNORM_ALPHA = (2 * DEPTH) ** 0.25
LN_EPS = 1e-5
RMS_EPS = 1e-6

LANES = 128
SUBLANES = 8
TM = 256
MOE_BLK = 512
DISPATCH_ROWS = 512
ROW_CHUNKS = D_MODEL // LANES
assert ROW_CHUNKS == SUBLANES
ATT_TQ = 256
ATT_TK = 1408
VMEM_LIMIT = 56 * 1024 * 1024
GATE_LANES = 128
IG_OFF, FG_OFF, DT_OFF = 0, N_DIR * M_HEADS, 2 * N_DIR * M_HEADS
ATT_Q_SCALE = A_HEAD_DIM ** -0.5 * math.log2(math.e)


def _cparams(sem):
    return pltpu.CompilerParams(dimension_semantics=sem, vmem_limit_bytes=VMEM_LIMIT)


def _silu(x):
    return x / (1.0 + jnp.exp(-x))


def _sigmoid(x):
    return 1.0 / (1.0 + jnp.exp(-x))


def _softplus(x):
    return jnp.maximum(x, 0.0) + jnp.log(1.0 + jnp.exp(-jnp.abs(x)))


def _log_sigmoid(x):
    return jnp.minimum(x, 0.0) - jnp.log(1.0 + jnp.exp(-jnp.abs(x)))


def _layer_norm_rows(x, g, b):
    mu = jnp.mean(x, axis=-1, keepdims=True)
    xc = x - mu
    var = jnp.mean(xc * xc, axis=-1, keepdims=True)
    return xc * lax.rsqrt(var + LN_EPS) * g + b


def _mm(a, b):
    return jnp.dot(a.astype(MXU_DTYPE), b.astype(MXU_DTYPE), preferred_element_type=F32)


def _mm_nt(a, b):
    return lax.dot_general(a.astype(MXU_DTYPE), b.astype(MXU_DTYPE), (((1,), (1,)), ((), ())),
                           preferred_element_type=F32)


def _mm_tn(a, b):
    return lax.dot_general(a.astype(MXU_DTYPE), b.astype(MXU_DTYPE), (((0,), (0,)), ((), ())),
                           preferred_element_type=F32)


def _ada_kernel(c_ref, w_ref, b_ref, o_ref):
    s = _silu(c_ref[...])
    o_ref[0] = jnp.dot(s, w_ref[0], precision=HIGHEST, preferred_element_type=F32) + b_ref[0]


def ada_call(crows, ada_w, ada_b):
    depth, d, n6 = ada_w.shape
    tn = 1536
    return pl.pallas_call(
        _ada_kernel,
        grid=(depth, n6 // tn),
        in_specs=[pl.BlockSpec((SUBLANES, d), lambda i, j: (0, 0)),
                  pl.BlockSpec((1, d, tn), lambda i, j: (i, 0, j)),
                  pl.BlockSpec((1, 1, tn), lambda i, j: (i, 0, j))],
        out_specs=pl.BlockSpec((1, SUBLANES, tn), lambda i, j: (i, 0, j)),
        out_shape=jax.ShapeDtypeStruct((depth, SUBLANES, n6), F32),
        compiler_params=_cparams(("arbitrary", "arbitrary")),
        name="ada",
    )(crows, ada_w, ada_b.reshape(depth, 1, n6))


def _inproj0_kernel(x_ref, mod_ref, wb_ref, ws_ref, q_ref, k_ref, v_ref, o_ref, z_ref, xbc_ref, g_ref):
    x = x_ref[0]
    mod = mod_ref[0, 0]
    h = x * (1.0 + mod[1:2]) + mod[0:1]
    hb = h.astype(MXU_DTYPE)
    qk = M_HEADS * M_QK_DIM
    c0 = 0
    q_ref[0] = (jnp.dot(hb, wb_ref[:, c0:c0 + qk], preferred_element_type=F32)
                * (M_QK_DIM ** -0.5)).astype(q_ref.dtype)
    c0 += qk
    k_ref[0] = jnp.dot(hb, wb_ref[:, c0:c0 + qk], preferred_element_type=F32).astype(k_ref.dtype)
    c0 += qk
    v_ref[0] = jnp.dot(hb, wb_ref[:, c0:c0 + M_WIDTH], preferred_element_type=F32).astype(v_ref.dtype)
    c0 += M_WIDTH
    o_ref[0] = jnp.dot(hb, wb_ref[:, c0:c0 + M_WIDTH], preferred_element_type=F32)
    c0 += M_WIDTH
    z_ref[0] = jnp.dot(hb, wb_ref[:, c0:c0 + S_WIDTH], preferred_element_type=F32)
    c0 += S_WIDTH
    xbc_ref[0] = jnp.dot(hb, wb_ref[:, c0:c0 + S_CONV_CH], preferred_element_type=F32)
    g_ref[0] = jnp.dot(h, ws_ref[...], precision=HIGHEST, preferred_element_type=F32)


def inproj0_call(xa, modsel, w_big, w_small, ctx_tiles):
    bsz, ta, d = xa.shape
    nt = ta // TM
    qk = M_HEADS * M_QK_DIM
    widths = (qk, qk, M_WIDTH, M_WIDTH, S_WIDTH, S_CONV_CH, GATE_LANES)
    dtypes = (MXU_DTYPE, MXU_DTYPE, MXU_DTYPE, F32, F32, F32, F32)
    tok = lambda w: pl.BlockSpec((1, TM, w), lambda b, j: (b, j, 0))
    return pl.pallas_call(
        _inproj0_kernel,
        grid=(bsz, nt),
        in_specs=[tok(d),
                  pl.BlockSpec((1, 1, 6, d), lambda b, j: (b, (j >= ctx_tiles).astype(jnp.int32), 0, 0)),
                  pl.BlockSpec(w_big.shape, lambda b, j: (0, 0)),
                  pl.BlockSpec(w_small.shape, lambda b, j: (0, 0))],
        out_specs=[tok(w) for w in widths],
        out_shape=[jax.ShapeDtypeStruct((bsz, ta, w), dt) for w, dt in zip(widths, dtypes)],
        compiler_params=_cparams(("arbitrary", "arbitrary")),
        name="inproj0",
    )(xa, modsel, w_big, w_small)


def _conv_kernel(cur_ref, prev_ref, next_ref, w_ref, b_ref, o_ref, *, ctx_tiles, n_tiles):
    j = pl.program_id(1)
    has_prev = jnp.logical_and(j != 0, j != ctx_tiles)
    has_next = jnp.logical_and(j != ctx_tiles - 1, j != n_tiles - 1)
    prev = jnp.where(has_prev, prev_ref[0], 0.0)
    nxt = jnp.where(has_next, next_ref[0], 0.0)
    ext = jnp.concatenate([prev, cur_ref[0], nxt], axis=0)
    n = TM + 2 * SUBLANES
    w = w_ref[...]
    lo, hi = SUBLANES, SUBLANES + TM
    acc = ext[lo:hi] * w[2:3]
    acc = acc + pltpu.roll(ext, 2, 0)[lo:hi] * w[0:1]
    acc = acc + pltpu.roll(ext, 1, 0)[lo:hi] * w[1:2]
    acc = acc + pltpu.roll(ext, n - 1, 0)[lo:hi] * w[3:4]
    o_ref[0] = _silu(acc + b_ref[...])


def conv_call(xbc, conv_w, conv_b, ctx_tiles):
    bsz, ta, ch = xbc.shape
    nt = ta // TM
    r = TM // SUBLANES
    last = ta // SUBLANES - 1
    return pl.pallas_call(
        functools.partial(_conv_kernel, ctx_tiles=ctx_tiles, n_tiles=nt),
        grid=(bsz, nt),
        in_specs=[pl.BlockSpec((1, TM, ch), lambda b, j: (b, j, 0)),
                  pl.BlockSpec((1, SUBLANES, ch), lambda b, j: (b, jnp.maximum(j * r - 1, 0), 0)),
                  pl.BlockSpec((1, SUBLANES, ch), lambda b, j: (b, jnp.minimum((j + 1) * r, last), 0)),
                  pl.BlockSpec((CONV_K, ch), lambda b, j: (0, 0)),
                  pl.BlockSpec((1, ch), lambda b, j: (0, 0))],
        out_specs=pl.BlockSpec((1, TM, ch), lambda b, j: (b, j, 0)),
        out_shape=jax.ShapeDtypeStruct((bsz, ta, ch), F32),
        compiler_params=_cparams(("arbitrary", "arbitrary")),
        name="conv",
    )(xbc, xbc, xbc, conv_w, conv_b.reshape(1, ch))


def _chunk_order(i, rev, ctx_chunks, n_chunks):
    if not rev:
        return i
    return jnp.where(i < ctx_chunks, ctx_chunks - 1 - i, n_chunks - 1 - (i - ctx_chunks))


def _scan_masks(rev):
    r = lax.broadcasted_iota(jnp.int32, (CHUNK, CHUNK), 0)
    c = lax.broadcasted_iota(jnp.int32, (CHUNK, CHUNK), 1)
    mask = (c >= r) if rev else (c <= r)
    mask_t = (r >= c) if rev else (r <= c)
    return mask, mask.astype(F32), mask_t.astype(F32)


def _per_batch(body, kinds):
    def kern(*refs, **kw):
        assert len(refs) == len(kinds)
        for bi in range(refs[0].shape[0]):
            sub = [r.at[pl.ds(bi, 1)] if f == 'b' else r.at[bi] if f == 's' else r for r, f in zip(refs, kinds)]
            body(*sub, **kw)
    return kern


def _mlstm_kernel(*refs, rev, d, add_prev):
    if add_prev:
        (q_ref, k_ref, v_ref, gc_ref, gr_ref, brow_ref, bcol_ref, prev_ref, og_ref, mg_ref,
         o_ref, ct_s, n_s, m_s) = refs
    else:
        q_ref, k_ref, v_ref, gc_ref, gr_ref, brow_ref, bcol_ref, o_ref, ct_s, n_s, m_s = refs
        prev_ref = None

    @pl.when(pl.program_id(0) == 0)
    def _():
        ct_s[...] = jnp.zeros_like(ct_s)
        n_s[...] = jnp.zeros_like(n_s)
        m_s[...] = jnp.zeros_like(m_s)

    mask, mask_f, mask_tf = _scan_masks(rev)
    end = 0 if rev else CHUNK - 1
    gcol = gc_ref[0] + brow_ref[...]
    grow = gr_ref[0] + bcol_ref[...]
    nh2 = N_DIR * M_HEADS
    lf_col = _log_sigmoid(gcol)
    lf_row = _log_sigmoid(grow[FG_OFF:FG_OFF + nh2])
    b_col_all = jnp.dot(mask_f, lf_col, precision=HIGHEST, preferred_element_type=F32)
    b_row_all = jnp.dot(lf_row, mask_tf, precision=HIGHEST, preferred_element_type=F32)

    for j in range(M_HEADS):
        ci = d * M_HEADS + j
        bcol = b_col_all[:, FG_OFF + ci:FG_OFF + ci + 1]
        brow = b_row_all[ci:ci + 1, :]
        igcol = gcol[:, IG_OFF + ci:IG_OFF + ci + 1]
        igrow = grow[IG_OFF + ci:IG_OFF + ci + 1, :]
        b_end = bcol[end:end + 1, :]
        m_prev = m_s[j][:, 0:1]
        n_prev = n_s[j]
        ct_prev = ct_s[j]
        q = q_ref[0, :, j * M_QK_DIM:(j + 1) * M_QK_DIM]
        k = k_ref[0, :, j * M_QK_DIM:(j + 1) * M_QK_DIM]
        v = v_ref[0, :, j * M_V_DIM:(j + 1) * M_V_DIM]
        qf = q.astype(F32)
        kf = k.astype(F32)
        vf = v.astype(F32)

        dmat = jnp.where(mask, bcol - brow + igrow, -jnp.inf)
        inter = bcol + m_prev
        m_t = jnp.maximum(inter, jnp.max(dmat, axis=1, keepdims=True))
        sc = _mm_nt(q, k) * jnp.exp(dmat - m_t)
        a_in = jnp.exp(inter - m_t)
        num = _mm(sc, v) + a_in * _mm(q, ct_prev)
        den = jnp.sum(sc, axis=1, keepdims=True) + a_in * jnp.sum(qf * n_prev, axis=1, keepdims=True)
        h = num / jnp.maximum(jnp.abs(den), jnp.exp(-m_t))
        sl = slice(j * M_V_DIM, (j + 1) * M_V_DIM)
        if add_prev:
            h = h + prev_ref[0, :, sl]
            mu = jnp.mean(h, axis=-1, keepdims=True)
            hc = h - mu
            var = jnp.mean(hc * hc, axis=-1, keepdims=True)
            h = hc * lax.rsqrt(var + LN_EPS) * mg_ref[:, sl] * _sigmoid(og_ref[0, :, sl])
        o_ref[0, :, sl] = h.astype(o_ref.dtype)

        g_col = b_end - bcol + igcol
        g_row = b_end - brow + igrow
        g_max = jnp.max(g_row, axis=1, keepdims=True)
        w_col = jnp.exp(g_col - g_max)
        d_ct = _mm_tn(k, vf * w_col)
        d_n = jnp.sum(kf * w_col, axis=0, keepdims=True)
        m_new = jnp.maximum(b_end + m_prev, g_max)
        a = jnp.exp(b_end + m_prev - m_new)
        s = jnp.exp(g_max - m_new)
        ct_s[j] = a * ct_prev + s * d_ct
        n_s[j] = a * n_prev + s * d_n
        m_s[j] = jnp.broadcast_to(m_new, (1, LANES))


def mlstm_call(q, k, v, gcol, grow, bias_row, bias_col, prev, o_gate, m_norm_g, *, rev, d, ctx_chunks):
    bsz, ta, _ = q.shape
    nc = ta // CHUNK
    order = lambda i: _chunk_order(i, rev, ctx_chunks, nc)
    tok = lambda w: pl.BlockSpec((bsz, CHUNK, w), lambda i: (0, order(i), 0))
    in_specs = [tok(q.shape[-1]), tok(k.shape[-1]), tok(v.shape[-1]), tok(GATE_LANES),
                pl.BlockSpec((bsz, grow.shape[1], CHUNK), lambda i: (0, 0, order(i))),
                pl.BlockSpec(bias_row.shape, lambda i: (0, 0)),
                pl.BlockSpec(bias_col.shape, lambda i: (0, 0))]
    args = [q, k, v, gcol, grow, bias_row, bias_col]
    kinds = "bbbbbcc"
    last = prev is not None
    if last:
        in_specs += [tok(M_WIDTH), tok(M_WIDTH), pl.BlockSpec(m_norm_g.shape, lambda i: (0, 0))]
        args += [prev, o_gate, m_norm_g]
        kinds += "bbc"
    kinds += "b" + "sss"
    return pl.pallas_call(
        functools.partial(_per_batch(_mlstm_kernel, kinds), rev=rev, d=d, add_prev=last),
        grid=(nc,),
        in_specs=in_specs,
        out_specs=tok(M_WIDTH),
        out_shape=jax.ShapeDtypeStruct((bsz, ta, M_WIDTH), MXU_DTYPE if last else F32),
        scratch_shapes=[pltpu.VMEM((bsz, M_HEADS, M_QK_DIM, M_V_DIM), F32),
                        pltpu.VMEM((bsz, M_HEADS, 1, M_QK_DIM), F32),
                        pltpu.VMEM((bsz, M_HEADS, 1, LANES), F32)],
        compiler_params=_cparams(("arbitrary",)),
        name="mlstm_rev" if rev else "mlstm_fwd",
    )(*args)


def _ssd_kernel(*refs, rev, d, add_prev):
    if add_prev:
        (x_ref, gc_ref, gr_ref, dtb_row_ref, dtb_col_ref, an_row_ref, an_col_ref, prev_ref, z_ref, dsk_ref, sg_ref,
         o_ref, ht_s) = refs
    else:
        x_ref, gc_ref, gr_ref, dtb_row_ref, dtb_col_ref, an_row_ref, an_col_ref, o_ref, ht_s = refs
        prev_ref = None

    @pl.when(pl.program_id(0) == 0)
    def _():
        ht_s[...] = jnp.zeros_like(ht_s)

    mask, mask_f, mask_tf = _scan_masks(rev)
    end = 0 if rev else CHUNK - 1
    lane = lax.broadcasted_iota(jnp.int32, (CHUNK, LANES), 1)
    first_half = lane < S_HEAD_DIM
    dt_col = _softplus(gc_ref[0] + dtb_row_ref[...])
    dt_row = _softplus(gr_ref[0] + dtb_col_ref[...])
    acs_col = jnp.dot(mask_f, dt_col * an_row_ref[...], precision=HIGHEST, preferred_element_type=F32)
    acs_row = jnp.dot(dt_row * an_col_ref[...], mask_tf, precision=HIGHEST, preferred_element_type=F32)
    a_end_row = acs_col[end:end + 1, :]
    e_cs = jnp.exp(acs_col)
    e_rem = jnp.exp(a_end_row - acs_col)
    e_end = jnp.exp(a_end_row)
    gw = S_HEADS_PER_GROUP * S_HEAD_DIM
    pairs = S_HEADS_PER_GROUP // 2

    def pick(arr, la):
        return jnp.where(first_half[:arr.shape[0]], arr[:, la:la + 1], arr[:, la + 1:la + 2])

    y_parts = []
    for g in range(S_GROUPS):
        bm = x_ref[0, :, S_WIDTH + g * S_STATE:S_WIDTH + (g + 1) * S_STATE]
        cm = x_ref[0, :, S_WIDTH + (S_GROUPS + g) * S_STATE:S_WIDTH + (S_GROUPS + g + 1) * S_STATE]
        cb = _mm_nt(cm, bm)
        ht_prev = ht_s[g]
        y_inter = _mm(cm, ht_prev)
        xw_parts = []
        decay_parts = []
        for p in range(pairs):
            h0 = g * S_HEADS_PER_GROUP + 2 * p
            la = DT_OFF + d * S_HEADS + h0
            ra = d * S_HEADS + h0
            lhs = []
            for u in range(2):
                seg = acs_col[:, la + u:la + u + 1] - acs_row[ra + u:ra + u + 1, :]
                dec = jnp.exp(jnp.where(mask, seg, -jnp.inf))
                lhs.append((cb * dec).astype(MXU_DTYPE))
            xs = x_ref[0, :, h0 * S_HEAD_DIM:(h0 + 2) * S_HEAD_DIM]
            xsd = xs * pick(dt_col, la)
            rhs = jnp.concatenate([jnp.where(first_half, xsd, 0.0), jnp.where(first_half, 0.0, xsd)],
                                  axis=0).astype(MXU_DTYPE)
            y = jnp.dot(jnp.concatenate(lhs, axis=1), rhs, preferred_element_type=F32)
            y = y + y_inter[:, p * LANES:(p + 1) * LANES] * pick(e_cs, la)
            sl = slice(h0 * S_HEAD_DIM, (h0 + 2) * S_HEAD_DIM)
            if add_prev:
                y = (y + prev_ref[0, :, sl] + dsk_ref[:, sl] * xs) * _silu(z_ref[0, :, sl])
                y_parts.append(y)
            else:
                o_ref[0, :, sl] = y
            xw_parts.append(xsd * pick(e_rem, la))
            decay_parts.append(pick(e_end, la))
        xw = jnp.concatenate(xw_parts, axis=1)
        decay = jnp.concatenate(decay_parts, axis=1)
        ht_s[g] = decay * ht_prev + _mm_tn(bm, xw)

    if add_prev:
        ys = jnp.concatenate(y_parts, axis=1)
        ys = ys * lax.rsqrt(jnp.mean(ys * ys, axis=-1, keepdims=True) + RMS_EPS) * sg_ref[...]
        o_ref[0] = ys.astype(o_ref.dtype)


def ssd_call(xbc_act, gcol, dtrow, dtb_row, dtb_col, an_row, an_col, prev, z, dskip, s_norm_g, *, rev, d,
             ctx_chunks):
    bsz, ta, ch = xbc_act.shape
    nc = ta // CHUNK
    order = lambda i: _chunk_order(i, rev, ctx_chunks, nc)
    tok = lambda w: pl.BlockSpec((bsz, CHUNK, w), lambda i: (0, order(i), 0))
    const = lambda a: pl.BlockSpec(a.shape, lambda i: (0, 0))
    in_specs = [tok(ch), tok(GATE_LANES),
                pl.BlockSpec((bsz, dtrow.shape[1], CHUNK), lambda i: (0, 0, order(i))),
                const(dtb_row), const(dtb_col), const(an_row), const(an_col)]
    args = [xbc_act, gcol, dtrow, dtb_row, dtb_col, an_row, an_col]
    kinds = "bbbcccc"
    last = prev is not None
    if last:
        in_specs += [tok(S_WIDTH), tok(S_WIDTH), const(dskip), const(s_norm_g)]
        args += [prev, z, dskip, s_norm_g]
        kinds += "bbcc"
    kinds += "b" + "s"
    return pl.pallas_call(
        functools.partial(_per_batch(_ssd_kernel, kinds), rev=rev, d=d, add_prev=last),
        grid=(nc,),
        in_specs=in_specs,
        out_specs=tok(S_WIDTH),
        out_shape=jax.ShapeDtypeStruct((bsz, ta, S_WIDTH), MXU_DTYPE if last else F32),
        scratch_shapes=[pltpu.VMEM((bsz, S_GROUPS, S_STATE, S_HEADS_PER_GROUP * S_HEAD_DIM), F32)],
        compiler_params=_cparams(("arbitrary",)),
        name="ssd_rev" if rev else "ssd_fwd",
    )(*args)


def _store_row_tiles(ref, val):
    for s in range(ROW_CHUNKS):
        ref[pl.ds(s, val.shape[0], stride=ROW_CHUNKS), :] = val[:, s * LANES:(s + 1) * LANES]


def _load_row_tiles(ref, rows):
    return jnp.concatenate([ref[pl.ds(s, rows, stride=ROW_CHUNKS), :] for s in range(ROW_CHUNKS)], axis=1)


def _finish_sublayer(x, y, mod, lng_ref, lnb_ref, rw_ref, rb_ref, x1_ref, h2_ref, e_ref, g_ref):
    x1 = _layer_norm_rows(DEEPNORM_ALPHA * x + mod[2:3] * y, lng_ref[...], lnb_ref[...])
    x1_ref[0] = x1
    h2 = x1 * (1.0 + mod[4:5]) + mod[3:4]
    _store_row_tiles(h2_ref, h2)
    e, g = _route(h2, rw_ref, rb_ref)
    e_ref[...] = e
    g_ref[...] = g


def _outproj0_kernel(ym_ref, ys_ref, x_ref, mod_ref, w_ref, lng_ref, lnb_ref, rw_ref, rb_ref,
                     x1_ref, h2_ref, e_ref, g_ref):
    y = (jnp.dot(ym_ref[0], w_ref[:M_WIDTH, :], preferred_element_type=F32)
         + jnp.dot(ys_ref[0], w_ref[M_WIDTH:, :], preferred_element_type=F32))
    _finish_sublayer(x_ref[0], y, mod_ref[0, 0], lng_ref, lnb_ref, rw_ref, rb_ref, x1_ref, h2_ref, e_ref, g_ref)


def _sublayer_out(bsz, tt, d):
    nt = tt // TM
    n = bsz * tt
    specs = [pl.BlockSpec((1, TM, d), lambda b, j: (b, j, 0)),
             pl.BlockSpec((TM * ROW_CHUNKS, LANES), lambda b, j: (b * nt + j, 0)),
             pl.BlockSpec((TOP_K, TM), lambda b, j: (0, b * nt + j)),
             pl.BlockSpec((TOP_K, TM), lambda b, j: (0, b * nt + j))]
    shapes = [jax.ShapeDtypeStruct((bsz, tt, d), F32), jax.ShapeDtypeStruct((n * ROW_CHUNKS, LANES), F32),
              jax.ShapeDtypeStruct((TOP_K, n), jnp.int32), jax.ShapeDtypeStruct((TOP_K, n), F32)]
    return specs, shapes


def outproj0_call(ym, ys, xa, modsel, w_out, ln_g, ln_b, router_wt, router_b, ctx_tiles):
    bsz, ta, d = xa.shape
    nt = ta // TM
    tok = lambda w: pl.BlockSpec((1, TM, w), lambda b, j: (b, j, 0))
    const = lambda a: pl.BlockSpec(a.shape, lambda b, j: (0, 0))
    out_specs, out_shape = _sublayer_out(bsz, ta, d)
    return pl.pallas_call(
        _outproj0_kernel,
        grid=(bsz, nt),
        in_specs=[tok(M_WIDTH), tok(S_WIDTH), tok(d),
                  pl.BlockSpec((1, 1, 6, d), lambda b, j: (b, (j >= ctx_tiles).astype(jnp.int32), 0, 0)),
                  const(w_out), const(ln_g), const(ln_b), const(router_wt), const(router_b)],
        out_specs=out_specs,
        out_shape=out_shape,
        compiler_params=_cparams(("arbitrary", "arbitrary")),
        name="outproj0",
    )(ym, ys, xa, modsel, w_out, ln_g, ln_b, router_wt, router_b)


def _top2(vals, probs):
    v1, i1, p1 = vals[0], jnp.zeros_like(vals[0], dtype=jnp.int32), probs[0]
    for i in range(1, len(vals)):
        better = vals[i] > v1
        v1 = jnp.where(better, vals[i], v1)
        i1 = jnp.where(better, i, i1)
        p1 = jnp.where(better, probs[i], p1)
    v2 = jnp.full_like(vals[0], -jnp.inf)
    i2 = jnp.zeros_like(i1)
    p2 = jnp.zeros_like(p1)
    for i in range(len(vals)):
        better = jnp.logical_and(i1 != i, vals[i] > v2)
        v2 = jnp.where(better, vals[i], v2)
        i2 = jnp.where(better, i, i2)
        p2 = jnp.where(better, probs[i], p2)
    return v1, i1, p1, v2, i2, p2


def _split_hi_lo(x):
    hi = x.astype(MXU_DTYPE)
    return hi, (x - hi.astype(F32)).astype(MXU_DTYPE)


def _route(h, w_ref, b_ref):
    h_hi, h_lo = _split_hi_lo(h)
    w_hi, w_lo = _split_hi_lo(w_ref[...])
    lg = (jnp.dot(h_hi, w_hi, preferred_element_type=F32) + jnp.dot(h_lo, w_hi, preferred_element_type=F32)
          + jnp.dot(h_hi, w_lo, preferred_element_type=F32))
    logits = lg.T[:N_EXPERTS]
    mx = jnp.max(logits, axis=0, keepdims=True)
    ex = jnp.exp(logits - mx)
    probs = ex / jnp.sum(ex, axis=0, keepdims=True)
    sel = probs + b_ref[...]
    best = None
    for g in range(N_EXPERT_GROUPS):
        rows = range(g * EXPERTS_PER_GROUP, (g + 1) * EXPERTS_PER_GROUP)
        v1, i1, p1, v2, i2, p2 = _top2([sel[r:r + 1] for r in rows], [probs[r:r + 1] for r in rows])
        cand = (v1 + v2, i1 + g * EXPERTS_PER_GROUP, p1, i2 + g * EXPERTS_PER_GROUP, p2)
        if best is None:
            best = cand
        else:
            better = cand[0] > best[0]
            best = tuple(jnp.where(better, c, o) for c, o in zip(cand, best))
    _, e1, p1, e2, p2 = best
    tot = p1 + p2
    return jnp.concatenate([e1, e2], axis=0), jnp.concatenate([p1 / tot, p2 / tot], axis=0)


def _row_tile(ref, r):
    return ref.at[pl.ds(pl.multiple_of(r * ROW_CHUNKS, ROW_CHUNKS), ROW_CHUNKS)]


def _dispatch_kernel(dest_ref, zblk_ref, h_ref, xs_ref, zero_s, sem, zsem, *, n_tok):
    base = pl.program_id(0) * DISPATCH_ROWS
    blk_rows = MOE_BLK * ROW_CHUNKS

    @pl.when(pl.program_id(0) == 0)
    def _():
        zero_s[...] = jnp.zeros_like(zero_s)

        def zero_copy(t):
            start = pl.multiple_of(zblk_ref[t] * blk_rows, blk_rows)
            return pltpu.make_async_copy(zero_s, xs_ref.at[pl.ds(start, blk_rows)], zsem)

        for t in range(2 * N_EXPERTS):
            pl.when(zblk_ref[t] >= 0)(lambda t=t: zero_copy(t).start())
        for t in range(2 * N_EXPERTS):
            pl.when(zblk_ref[t] >= 0)(lambda t=t: zero_copy(t).wait())

    def slot_copy(r, slot):
        return pltpu.make_async_copy(_row_tile(h_ref, r), _row_tile(xs_ref, slot), sem)

    def issue(r, carry):
        for c in range(TOP_K):
            slot_copy(r, dest_ref[c * n_tok + base + r]).start(priority=c)
        return carry

    def drain(r, carry):
        for c in range(TOP_K):
            slot_copy(r, 0).wait()
        return carry

    lax.fori_loop(0, DISPATCH_ROWS, issue, 0, unroll=8)
    lax.fori_loop(0, DISPATCH_ROWS, drain, 0, unroll=8)


def dispatch_call(dest, zero_blocks, h2t, n_slots):
    n_tok = h2t.shape[0] // ROW_CHUNKS
    return pl.pallas_call(
        functools.partial(_dispatch_kernel, n_tok=n_tok),
        grid_spec=pltpu.PrefetchScalarGridSpec(
            num_scalar_prefetch=2,
            grid=(n_tok // DISPATCH_ROWS,),
            in_specs=[pl.BlockSpec((DISPATCH_ROWS * ROW_CHUNKS, LANES), lambda i, dest, zb: (i, 0))],
            out_specs=pl.BlockSpec(memory_space=pl.ANY),
            scratch_shapes=[pltpu.VMEM((MOE_BLK * ROW_CHUNKS, LANES), h2t.dtype),
                            pltpu.SemaphoreType.DMA(()), pltpu.SemaphoreType.DMA(())]),
        out_shape=jax.ShapeDtypeStruct((n_slots * ROW_CHUNKS, LANES), h2t.dtype),
        compiler_params=_cparams(("arbitrary",)),
        name="dispatch",
    )(dest, zero_blocks, h2t)


def _experts_kernel(be_ref, cnt_ref, x_ref, wg_ref, wu_ref, wd_ref, o_ref, wg_s, wu_s, wd_s):
    i = pl.program_id(0)
    e = be_ref[i]
    e_before = be_ref[jnp.maximum(i - 1, 0)]
    cnt = cnt_ref[i]

    @pl.when(jnp.logical_or(i == 0, e != e_before))
    def _():
        wg_s[...] = wg_ref[0, 0].astype(wg_s.dtype)
        wu_s[...] = wu_ref[0, 0].astype(wu_s.dtype)
        wd_s[...] = wd_ref[0, 0].astype(wd_s.dtype)

    @pl.when(cnt > 0)
    def _():
        xb = _load_row_tiles(x_ref, MOE_BLK).astype(MXU_DTYPE)
        gt = jnp.dot(xb, wg_s[...], preferred_element_type=F32)
        up = jnp.dot(xb, wu_s[...], preferred_element_type=F32)
        y = jnp.dot((_silu(gt) * up).astype(MXU_DTYPE), wd_s[...], preferred_element_type=F32)
        _store_row_tiles(o_ref, y)

    @pl.when(cnt == 0)
    def _():
        o_ref[...] = jnp.zeros_like(o_ref)


def experts_call(blk_exp, blk_cnt, xs, w_gate, w_up, w_down, layer):
    n_slots = xs.shape[0] // ROW_CHUNKS
    d, f = w_gate.shape[-2:]
    wspec = lambda a: pl.BlockSpec((1, 1) + a.shape[2:], lambda i, be, cnt: (layer, be[i], 0, 0))
    blk = pl.BlockSpec((MOE_BLK * ROW_CHUNKS, LANES), lambda i, be, cnt: (i, 0))
    return pl.pallas_call(
        _experts_kernel,
        grid_spec=pltpu.PrefetchScalarGridSpec(
            num_scalar_prefetch=2,
            grid=(n_slots // MOE_BLK,),
            in_specs=[blk, wspec(w_gate), wspec(w_up), wspec(w_down)],
            out_specs=blk,
            scratch_shapes=[pltpu.VMEM((d, f), MXU_DTYPE), pltpu.VMEM((d, f), MXU_DTYPE),
                            pltpu.VMEM((f, d), MXU_DTYPE)]),
        out_shape=jax.ShapeDtypeStruct(xs.shape, F32),
        compiler_params=_cparams(("arbitrary",)),
        name="experts",
    )(blk_exp, blk_cnt, xs, w_gate, w_up, w_down)


def _combine_kernel(dest_ref, y_ref, gate_ref, x_ref, mod_ref, lng_ref, lnb_ref, o_ref, ybuf, sems, *, n_tok, nt,
                    n_steps):
    step = pl.program_id(0) * nt + pl.program_id(1)
    slot = step % 2

    def row_copy(buf, r, c, src_slot):
        return pltpu.make_async_copy(_row_tile(y_ref, src_slot), _row_tile(ybuf.at[buf, c], r), sems.at[buf])

    def issue(tile, buf):
        base = tile * TM

        def body(r, carry):
            for c in range(TOP_K):
                row_copy(buf, r, c, dest_ref[c * n_tok + base + r]).start(priority=c)
            return carry

        lax.fori_loop(0, TM, body, 0, unroll=8)

    pl.when(step == 0)(lambda: issue(0, 0))
    pl.when(step + 1 < n_steps)(lambda: issue(step + 1, 1 - slot))

    def drain(r, carry):
        for c in range(TOP_K):
            row_copy(slot, r, c, 0).wait()
        return carry

    lax.fori_loop(0, TM, drain, 0, unroll=8)
    mod = mod_ref[0, 0]
    gate = gate_ref[...]
    y = (gate[:, 0:1] * _load_row_tiles(ybuf.at[slot, 0], TM)
         + gate[:, 1:2] * _load_row_tiles(ybuf.at[slot, 1], TM))
    o_ref[0] = _layer_norm_rows(DEEPNORM_ALPHA * x_ref[0] + mod[5:6] * y, lng_ref[...], lnb_ref[...])


def combine_call(dest, y_slots, gate_cols, x1, modsel, ln_g, ln_b, ctx_tiles):
    bsz, tt, d = x1.shape
    nt = tt // TM
    return pl.pallas_call(
        functools.partial(_combine_kernel, n_tok=bsz * tt, nt=nt, n_steps=bsz * nt),
        grid_spec=pltpu.PrefetchScalarGridSpec(
            num_scalar_prefetch=1,
            grid=(bsz, nt),
            in_specs=[pl.BlockSpec(memory_space=pl.ANY),
                      pl.BlockSpec((TM, TOP_K), lambda b, j, dest: (b * nt + j, 0)),
                      pl.BlockSpec((1, TM, d), lambda b, j, dest: (b, j, 0)),
                      pl.BlockSpec((1, 1, 6, d),
                                   lambda b, j, dest: (b, (j >= ctx_tiles).astype(jnp.int32), 0, 0)),
                      pl.BlockSpec(ln_g.shape, lambda b, j, dest: (0, 0)),
                      pl.BlockSpec(ln_b.shape, lambda b, j, dest: (0, 0))],
            out_specs=pl.BlockSpec((1, TM, d), lambda b, j, dest: (b, j, 0)),
            scratch_shapes=[pltpu.VMEM((2, TOP_K, TM * ROW_CHUNKS, LANES), F32), pltpu.SemaphoreType.DMA((2,))]),
        out_shape=jax.ShapeDtypeStruct((bsz, tt, d), F32),
        compiler_params=_cparams(("arbitrary", "arbitrary")),
        name="combine",
    )(dest, y_slots, gate_cols, x1, modsel, ln_g, ln_b)


def moe_block(x1, h2t, e_idx, gates, modsel, ln_g, ln_b, w_gate, w_up, w_down, layer, ctx_tiles):
    bsz, tt, d = x1.shape
    n = bsz * tt

    n_asg = TOP_K * n
    flat_e = e_idx.reshape(n_asg)
    onehot = (flat_e[:, None] == jnp.arange(N_EXPERTS, dtype=jnp.int32)[None, :]).astype(jnp.int32)
    csum = jnp.cumsum(onehot, axis=0)
    rank = jnp.sum(onehot * csum, axis=1) - 1
    counts = csum[-1]
    padded = (counts + MOE_BLK - 1) // MOE_BLK * MOE_BLK
    pend = jnp.cumsum(padded)
    pstart = pend - padded
    dest = (pstart[flat_e] + rank).astype(jnp.int32)
    n_blocks = -(-n_asg // MOE_BLK) + N_EXPERTS
    blk_start = jnp.arange(n_blocks, dtype=jnp.int32) * MOE_BLK
    blk_exp = jnp.sum((pend[None, :] <= blk_start[:, None]).astype(jnp.int32), axis=1)
    blk_exp = jnp.minimum(blk_exp, N_EXPERTS - 1)
    blk_cnt = jnp.clip(counts[blk_exp] - (blk_start - pstart[blk_exp]), 0, MOE_BLK).astype(jnp.int32)

    part = jnp.where(counts % MOE_BLK != 0, pend // MOE_BLK - 1, -1)
    tail = pend[-1] // MOE_BLK + jnp.arange(N_EXPERTS, dtype=jnp.int32)
    tail = jnp.where(tail < n_blocks, tail, -1)
    zero_blocks = jnp.concatenate([part, tail]).astype(jnp.int32)

    xs = dispatch_call(dest, zero_blocks, h2t, n_blocks * MOE_BLK)
    y_slots = experts_call(blk_exp, blk_cnt, xs, w_gate, w_up, w_down, layer)
    return combine_call(dest, y_slots, gates.T, x1, modsel, ln_g, ln_b, ctx_tiles)


def _inproj1_kernel(x_ref, mod_ref, w_ref, qg_ref, kg_ref, cos_ref, sin_ref, q_ref, k_ref, v_ref):
    mod = mod_ref[0, 0]
    hb = (x_ref[0] * (1.0 + mod[1:2]) + mod[0:1]).astype(MXU_DTYPE)
    cos = cos_ref[...]
    sin = sin_ref[...]

    def norm_rope(t, g):
        t = t * lax.rsqrt(jnp.mean(t * t, axis=-1, keepdims=True) + RMS_EPS) * g
        return t * cos + pltpu.roll(t, A_HEAD_DIM // 2, 1) * sin

    qkv = jnp.dot(hb, w_ref[...], preferred_element_type=F32)
    for j in range(A_HEADS):
        sl = slice(j * A_HEAD_DIM, (j + 1) * A_HEAD_DIM)
        q_ref[0, :, sl] = (norm_rope(qkv[:, sl], qg_ref[...]) * ATT_Q_SCALE).astype(q_ref.dtype)
    for j in range(A_KV_HEADS):
        sl = slice(j * A_HEAD_DIM, (j + 1) * A_HEAD_DIM)
        t = qkv[:, A_Q_W + j * A_HEAD_DIM:A_Q_W + (j + 1) * A_HEAD_DIM]
        k_ref[0, :, sl] = norm_rope(t, kg_ref[...]).astype(k_ref.dtype)
    v_ref[0] = qkv[:, A_Q_W + A_KV_W:].astype(v_ref.dtype)


def inproj1_call(xa, modsel, w_in, q_g, k_g, cos2, sin2, ctx_tiles):
    bsz, ta, d = xa.shape
    nt = ta // TM
    tok = lambda w: pl.BlockSpec((1, TM, w), lambda b, j: (b, j, 0))
    const = lambda a: pl.BlockSpec(a.shape, lambda b, j: (0, 0))
    widths = (A_Q_W, A_KV_W, A_KV_W)
    return pl.pallas_call(
        _inproj1_kernel,
        grid=(bsz, nt),
        in_specs=[tok(d),
                  pl.BlockSpec((1, 1, 6, d), lambda b, j: (b, (j >= ctx_tiles).astype(jnp.int32), 0, 0)),
                  const(w_in), const(q_g), const(k_g),
                  pl.BlockSpec((TM, A_HEAD_DIM), lambda b, j: (j, 0)),
                  pl.BlockSpec((TM, A_HEAD_DIM), lambda b, j: (j, 0))],
        out_specs=[tok(w) for w in widths],
        out_shape=[jax.ShapeDtypeStruct((bsz, ta, w), MXU_DTYPE) for w in widths],
        compiler_params=_cparams(("arbitrary", "arbitrary")),
        name="inproj1",
    )(xa, modsel, w_in, q_g, k_g, cos2, sin2)


def _attn_kernel(q_ref, qn_ref, k_ref, vt_ref, o_ref, s_buf):
    rep = A_HEADS // A_KV_HEADS
    n_kv = vt_ref.shape[2]
    assert n_kv % 2 == 0

    def transposed(ref):
        return [ref[0, :, r * A_HEAD_DIM:(r + 1) * A_HEAD_DIM].astype(F32).T.astype(MXU_DTYPE) for r in range(rep)]

    qts = transposed(q_ref)

    def scores(j, slot, qt=qts):
        kb = k_ref[0, pl.ds(pl.multiple_of(j * ATT_TK, ATT_TK), ATT_TK), :]
        for r in range(rep):
            s_buf[slot, r] = jnp.dot(kb, qt[r], preferred_element_type=F32)

    def consume(j, slot, stats):
        vt = vt_ref[0, 0, j]
        new = []
        for r in range(rep):
            m, l, acc = stats[r]
            s = s_buf[slot, r]
            m_new = jnp.maximum(m, jnp.max(s, axis=0, keepdims=True))
            p = jnp.exp2(s - m_new)
            alpha = jnp.exp2(m - m_new)
            l = alpha * l + jnp.sum(p, axis=0, keepdims=True)
            acc = alpha * acc + jnp.dot(vt, p.astype(vt.dtype), preferred_element_type=F32)
            new.append((m_new, l, acc))
        return tuple(new)

    def pair(i, stats):
        j = 2 * i
        scores(j + 1, 1)
        stats = consume(j, 0, stats)
        scores(j + 2, 0)
        return consume(j + 1, 1, stats)

    stats = tuple((jnp.full((1, ATT_TQ), -jnp.inf, F32), jnp.zeros((1, ATT_TQ), F32),
                   jnp.zeros((A_HEAD_DIM, ATT_TQ), F32)) for _ in range(rep))
    pl.when(pl.program_id(2) == 0)(lambda: scores(0, 0))
    stats = lax.fori_loop(0, n_kv // 2 - 1, pair, stats)
    scores(n_kv - 1, 1)
    stats = consume(n_kv - 2, 0, stats)
    scores(0, 0, transposed(qn_ref))
    final = consume(n_kv - 1, 1, stats)
    for r in range(rep):
        _, l, acc = final[r]
        o_ref[0, :, r * A_HEAD_DIM:(r + 1) * A_HEAD_DIM] = (acc / l).T.astype(o_ref.dtype)


def attn_call(q, k, v, n_ctx):
    bsz, ta, _ = q.shape
    t_lat = ta - n_ctx
    gw = (A_HEADS // A_KV_HEADS) * A_HEAD_DIM
    q_off = n_ctx // ATT_TQ
    n_kv = ta // ATT_TK
    vt = v.reshape(bsz, n_kv, ATT_TK, A_KV_HEADS, A_HEAD_DIM).transpose(0, 3, 1, 4, 2)
    nq = t_lat // ATT_TQ
    return pl.pallas_call(
        _attn_kernel,
        grid=(bsz, A_KV_HEADS, nq),
        in_specs=[pl.BlockSpec((1, ATT_TQ, gw), lambda b, g, i: (b, i + q_off, g)),
                  pl.BlockSpec((1, ATT_TQ, gw), lambda b, g, i: (b, jnp.minimum(i + 1, nq - 1) + q_off, g)),
                  pl.BlockSpec((1, ta, A_HEAD_DIM), lambda b, g, i: (b, 0, g)),
                  pl.BlockSpec((1, 1, n_kv, A_HEAD_DIM, ATT_TK), lambda b, g, i: (b, g, 0, 0, 0))],
        out_specs=pl.BlockSpec((1, ATT_TQ, gw), lambda b, g, i: (b, i, g)),
        out_shape=jax.ShapeDtypeStruct((bsz, t_lat, A_Q_W), MXU_DTYPE),
        scratch_shapes=[pltpu.VMEM((2, A_HEADS // A_KV_HEADS, ATT_TK, ATT_TQ), F32)],
        compiler_params=_cparams(("arbitrary", "arbitrary", "arbitrary")),
        name="attention",
    )(q, q, k, vt)


def _outproj1_kernel(a_ref, x_ref, mod_ref, w_ref, lng_ref, lnb_ref, rw_ref, rb_ref, x1_ref, h2_ref, e_ref, g_ref):
    y = jnp.dot(a_ref[0], w_ref[...], preferred_element_type=F32)
    _finish_sublayer(x_ref[0], y, mod_ref[0, 0], lng_ref, lnb_ref, rw_ref, rb_ref, x1_ref, h2_ref, e_ref, g_ref)


def outproj1_call(att, xa, modsel, w_out, ln_g, ln_b, router_wt, router_b, ctx_tiles):
    bsz, t_lat, _ = att.shape
    d = xa.shape[-1]
    const = lambda a: pl.BlockSpec(a.shape, lambda b, j: (0, 0))
    out_specs, out_shape = _sublayer_out(bsz, t_lat, d)
    return pl.pallas_call(
        _outproj1_kernel,
        grid=(bsz, t_lat // TM),
        in_specs=[pl.BlockSpec((1, TM, att.shape[-1]), lambda b, j: (b, j, 0)),
                  pl.BlockSpec((1, TM, d), lambda b, j: (b, j + ctx_tiles, 0)),
                  pl.BlockSpec((1, 1, 6, d), lambda b, j: (b, 1, 0, 0)),
                  const(w_out), const(ln_g), const(ln_b), const(router_wt), const(router_b)],
        out_specs=out_specs,
        out_shape=out_shape,
        compiler_params=_cparams(("arbitrary", "arbitrary")),
        name="outproj1",
    )(att, xa, modsel, w_out, ln_g, ln_b, router_wt, router_b)


def _rope_tables(n_ctx, n_lat):
    rows = n_lat // GRID_W
    row = jnp.repeat(jnp.arange(rows), GRID_W).astype(F32)
    col = jnp.tile(jnp.arange(GRID_W), rows).astype(F32)
    n_freq = A_HEAD_DIM // 4
    inv = ROPE_THETA ** (-jnp.arange(n_freq, dtype=F32) / n_freq)
    ang = jnp.concatenate([row[:, None] * inv, col[:, None] * inv], -1)
    cos, sin = jnp.cos(ang), jnp.sin(ang)
    cos2 = jnp.concatenate([cos, cos], -1)
    sin2 = jnp.concatenate([-sin, sin], -1)
    cos2 = jnp.concatenate([jnp.ones((n_ctx, A_HEAD_DIM), F32), cos2], 0)
    sin2 = jnp.concatenate([jnp.zeros((n_ctx, A_HEAD_DIM), F32), sin2], 0)
    return cos2, sin2


def _lane_row(parts, width=GATE_LANES):
    row = jnp.concatenate([p.reshape(-1).astype(F32) for p in parts])
    return jnp.pad(row, (0, width - row.shape[0])).reshape(1, width)


def kernel(x, c, ctx, c_ctx, ada_w, ada_b, ln_g, ln_b, ab_w_in, ab_w_out, ml_ig_b, ml_fg_b, ml_norm_g,
           ssm_conv_w, ssm_conv_b, ssm_dt_b, ssm_a_log, ssm_d, ssm_norm_g, at_w_in, at_w_out, at_q_g, at_k_g,
           router_w, router_b, moe_w_gate, moe_w_up, moe_w_down):
    bsz, n_lat, d = x.shape
    n_ctx = ctx.shape[1]
    assert d == D_MODEL and bsz + 1 <= SUBLANES
    assert n_ctx % TM == 0 and n_lat % TM == 0 and n_lat % GRID_W == 0
    assert (n_ctx + n_lat) % ATT_TK == 0 and n_ctx % ATT_TQ == 0 and n_lat % ATT_TQ == 0
    assert (bsz * (n_ctx + n_lat)) % DISPATCH_ROWS == 0 and (bsz * n_lat) % DISPATCH_ROWS == 0
    ctx_tiles = n_ctx // TM
    ctx_chunks = n_ctx // CHUNK
    xa = jnp.concatenate([ctx, x], axis=1)

    crows = jnp.zeros((SUBLANES, d), F32).at[:bsz].set(c).at[bsz].set(c_ctx)
    mods = ada_call(crows, ada_w, ada_b)

    def mod_table(i):
        lat = mods[i, :bsz].reshape(bsz, 1, 6, d)
        cx = jnp.broadcast_to(mods[i, bsz].reshape(1, 1, 6, d), (bsz, 1, 6, d))
        return jnp.concatenate([cx, lat], axis=1)

    router_wt = jnp.pad(router_w, ((0, 0), (0, LANES - N_EXPERTS)))
    router_bc = router_b.reshape(N_EXPERTS, 1)

    modsel = mod_table(0)
    w_in = ab_w_in[0]
    s_q, s_k, s_v, s_o, s_ig, s_fg, s_z, s_xbc = (int(v) for v in
        (0, 512, 1024, 2048, 3072, 3072 + 8, 3072 + 16, 3072 + 16 + 1024))
    s_dt = s_xbc + S_CONV_CH
    w_big = jnp.concatenate([w_in[:, :s_ig], w_in[:, s_z:s_dt]], axis=1).astype(MXU_DTYPE)
    w_small = jnp.concatenate([w_in[:, s_ig:s_z], w_in[:, s_dt:]], axis=1)
    w_small = jnp.pad(w_small, ((0, 0), (0, GATE_LANES - w_small.shape[1])))
    q, k, v, o, z, xbc, gates = inproj0_call(xa, modsel, w_big, w_small, ctx_tiles)
    xbc_act = conv_call(xbc, ssm_conv_w[0], ssm_conv_b[0], ctx_tiles)

    grow = jnp.swapaxes(gates[:, :, :DT_OFF], 1, 2)
    dtrow = jnp.swapaxes(gates[:, :, DT_OFF:DT_OFF + N_DIR * S_HEADS], 1, 2)
    gate_b_row = _lane_row([ml_ig_b[0], ml_fg_b[0], ssm_dt_b[0]])
    gate_b_col = jnp.concatenate([ml_ig_b[0].reshape(-1), ml_fg_b[0].reshape(-1)]).reshape(-1, 1)
    a_neg = -jnp.exp(ssm_a_log[0].astype(F32)).reshape(-1)
    an_row = _lane_row([jnp.zeros((DT_OFF,), F32), a_neg])
    an_col = a_neg.reshape(-1, 1)
    dtb_col = ssm_dt_b[0].reshape(-1, 1)

    dskip = jnp.repeat(ssm_d[0].astype(F32), S_HEAD_DIM).reshape(1, S_WIDTH)
    m_norm_g = ml_norm_g[0].reshape(1, -1)
    s_norm_g = ssm_norm_g[0].reshape(1, -1)
    assert N_DIR == 2
    hm = None
    hs = None
    for dd in range(N_DIR):
        hm = mlstm_call(q, k, v, gates, grow, gate_b_row, gate_b_col, hm, o, m_norm_g, rev=dd == 1, d=dd,
                        ctx_chunks=ctx_chunks)
        hs = ssd_call(xbc_act, gates, dtrow, gate_b_row, dtb_col, an_row, an_col, hs, z, dskip, s_norm_g,
                      rev=dd == 1, d=dd, ctx_chunks=ctx_chunks)

    x1, h2t, e_idx, gates = outproj0_call(hm, hs, xa, modsel, ab_w_out[0].astype(MXU_DTYPE),
                                          ln_g[0, 0].reshape(1, d), ln_b[0, 0].reshape(1, d), router_wt, router_bc,
                                          ctx_tiles)
    xa = moe_block(x1, h2t, e_idx, gates, modsel, ln_g[0, 1].reshape(1, d), ln_b[0, 1].reshape(1, d),
                   moe_w_gate, moe_w_up, moe_w_down, 0, ctx_tiles)

    modsel = mod_table(1)
    cos2, sin2 = _rope_tables(n_ctx, n_lat)
    qa, ka, va = inproj1_call(xa, modsel, at_w_in[0].astype(MXU_DTYPE), at_q_g[0].reshape(1, -1),
                              at_k_g[0].reshape(1, -1), cos2, sin2, ctx_tiles)
    att = attn_call(qa, ka, va, n_ctx)
    x1, h2t, e_idx, gates = outproj1_call(att, xa, modsel, at_w_out[0].astype(MXU_DTYPE), ln_g[1, 0].reshape(1, d),
                                          ln_b[1, 0].reshape(1, d), router_wt, router_bc, ctx_tiles)
    return moe_block(x1, h2t, e_idx, gates, modsel, ln_g[1, 1].reshape(1, d), ln_b[1, 1].reshape(1, d),
                     moe_w_gate, moe_w_up, moe_w_down, 1, 0)
```

```python
import functools
import math

import jax
import jax.numpy as jnp
from jax import lax
from jax.experimental import pallas as pl
from jax.experimental.pallas import tpu as pltpu

F32 = jnp.float32
MXU_DTYPE = jnp.bfloat16
HIGHEST = lax.Precision.HIGHEST

D_MODEL = 1024
DEPTH = 2
GRID_W = 64
CHUNK = 128
M_HEADS = 4
M_QK_DIM = D_MODEL // 8
M_V_DIM = D_MODEL // 4
M_WIDTH = M_HEADS * M_V_DIM
S_HEADS = 16
S_HEAD_DIM = D_MODEL // 16
S_GROUPS = 2
S_HEADS_PER_GROUP = S_HEADS // S_GROUPS
S_STATE = 128
S_WIDTH = S_HEADS * S_HEAD_DIM
S_CONV_CH = S_WIDTH + 2 * S_GROUPS * S_STATE
CONV_K = 4
N_DIR = 2
A_HEADS = 8
A_KV_HEADS = 2
A_HEAD_DIM = D_MODEL // A_HEADS
A_Q_W = A_HEADS * A_HEAD_DIM
A_KV_W = A_KV_HEADS * A_HEAD_DIM
ROPE_THETA = 10000.0
N_EXPERTS = 16
N_EXPERT_GROUPS = 4
EXPERTS_PER_GROUP = N_EXPERTS // N_EXPERT_GROUPS
TOP_K = 2
DEEPNORM_ALPHA = (2 * DEPTH) ** 0.25
LN_EPS = 1e-5
RMS_EPS = 1e-6

LANES = 128
SUBLANES = 8
TM = 256
MOE_BLK = 512
DISPATCH_ROWS = 512
ROW_CHUNKS = D_MODEL // LANES
assert ROW_CHUNKS == SUBLANES
ATT_TQ = 256
ATT_TK = 1408
VMEM_LIMIT = 56 * 1024 * 1024
GATE_LANES = 128
IG_OFF, FG_OFF, DT_OFF = 0, N_DIR * M_HEADS, 2 * N_DIR * M_HEADS
ATT_Q_SCALE = A_HEAD_DIM ** -0.5 * math.log2(math.e)


def _cparams(sem):
    return pltpu.CompilerParams(dimension_semantics=sem, vmem_limit_bytes=VMEM_LIMIT)


def _silu(x):
    return x / (1.0 + jnp.exp(-x))


def _sigmoid(x):
    return 1.0 / (1.0 + jnp.exp(-x))


def _softplus(x):
    return jnp.maximum(x, 0.0) + jnp.log(1.0 + jnp.exp(-jnp.abs(x)))


def _log_sigmoid(x):
    return jnp.minimum(x, 0.0) - jnp.log(1.0 + jnp.exp(-jnp.abs(x)))


def _layer_norm_rows(x, g, b):
    mu = jnp.mean(x, axis=-1, keepdims=True)
    xc = x - mu
    var = jnp.mean(xc * xc, axis=-1, keepdims=True)
    return xc * lax.rsqrt(var + LN_EPS) * g + b


def _mm(a, b):
    return jnp.dot(a.astype(MXU_DTYPE), b.astype(MXU_DTYPE), preferred_element_type=F32)


def _mm_nt(a, b):
    return lax.dot_general(a.astype(MXU_DTYPE), b.astype(MXU_DTYPE), (((1,), (1,)), ((), ())),
                           preferred_element_type=F32)


def _mm_tn(a, b):
    return lax.dot_general(a.astype(MXU_DTYPE), b.astype(MXU_DTYPE), (((0,), (0,)), ((), ())),
                           preferred_element_type=F32)


def _ada_kernel(c_ref, w_ref, b_ref, o_ref):
    s = _silu(c_ref[...])
    o_ref[0] = jnp.dot(s, w_ref[0], precision=HIGHEST, preferred_element_type=F32) + b_ref[0]


def ada_call(crows, ada_w, ada_b):
    depth, d, n6 = ada_w.shape
    tn = 1536
    return pl.pallas_call(
        _ada_kernel,
        grid=(depth, n6 // tn),
        in_specs=[pl.BlockSpec((SUBLANES, d), lambda i, j: (0, 0)),
                  pl.BlockSpec((1, d, tn), lambda i, j: (i, 0, j)),
                  pl.BlockSpec((1, 1, tn), lambda i, j: (i, 0, j))],
        out_specs=pl.BlockSpec((1, SUBLANES, tn), lambda i, j: (i, 0, j)),
        out_shape=jax.ShapeDtypeStruct((depth, SUBLANES, n6), F32),
        compiler_params=_cparams(("arbitrary", "arbitrary")),
        name="ada",
    )(crows, ada_w, ada_b.reshape(depth, 1, n6))


def _token_tile(ctx_ref, x_ref, ctx_tiles):
    return jnp.where(pl.program_id(1) < ctx_tiles, ctx_ref[0], x_ref[0])


def _token_specs(d, ctx_tiles):
    return [pl.BlockSpec((1, TM, d), lambda b, j: (b, jnp.minimum(j, ctx_tiles - 1), 0)),
            pl.BlockSpec((1, TM, d), lambda b, j: (b, jnp.maximum(j - ctx_tiles, 0), 0))]


def _inproj0_kernel(ctx_ref, x_ref, mod_ref, wb_ref, ws_ref, q_ref, k_ref, v_ref, o_ref, z_ref, xbc_ref, g_ref, *,
                    ctx_tiles):
    x = _token_tile(ctx_ref, x_ref, ctx_tiles)
    mod = mod_ref[0, 0]
    h = x * (1.0 + mod[1:2]) + mod[0:1]
    hb = h.astype(MXU_DTYPE)
    qk = M_HEADS * M_QK_DIM
    c0 = 0
    q_ref[0] = (jnp.dot(hb, wb_ref[:, c0:c0 + qk], preferred_element_type=F32)
                * (M_QK_DIM ** -0.5)).astype(q_ref.dtype)
    c0 += qk
    k_ref[0] = jnp.dot(hb, wb_ref[:, c0:c0 + qk], preferred_element_type=F32).astype(k_ref.dtype)
    c0 += qk
    v_ref[0] = jnp.dot(hb, wb_ref[:, c0:c0 + M_WIDTH], preferred_element_type=F32).astype(v_ref.dtype)
    c0 += M_WIDTH
    o_ref[0] = jnp.dot(hb, wb_ref[:, c0:c0 + M_WIDTH], preferred_element_type=F32)
    c0 += M_WIDTH
    z_ref[0] = jnp.dot(hb, wb_ref[:, c0:c0 + S_WIDTH], preferred_element_type=F32)
    c0 += S_WIDTH
    xbc_ref[0] = jnp.dot(hb, wb_ref[:, c0:c0 + S_CONV_CH], preferred_element_type=F32)
    g_ref[0] = jnp.dot(h, ws_ref[...], precision=HIGHEST, preferred_element_type=F32)


def inproj0_call(ctx, x, modsel, w_big, w_small, ctx_tiles):
    bsz, _, d = x.shape
    ta = ctx.shape[1] + x.shape[1]
    nt = ta // TM
    qk = M_HEADS * M_QK_DIM
    widths = (qk, qk, M_WIDTH, M_WIDTH, S_WIDTH, S_CONV_CH, GATE_LANES)
    dtypes = (MXU_DTYPE, MXU_DTYPE, MXU_DTYPE, F32, F32, F32, F32)
    tok = lambda w: pl.BlockSpec((1, TM, w), lambda b, j: (b, j, 0))
    return pl.pallas_call(
        functools.partial(_inproj0_kernel, ctx_tiles=ctx_tiles),
        grid=(bsz, nt),
        in_specs=_token_specs(d, ctx_tiles) + [
                  pl.BlockSpec((1, 1, 6, d), lambda b, j: (b, (j >= ctx_tiles).astype(jnp.int32), 0, 0)),
                  pl.BlockSpec(w_big.shape, lambda b, j: (0, 0)),
                  pl.BlockSpec(w_small.shape, lambda b, j: (0, 0))],
        out_specs=[tok(w) for w in widths],
        out_shape=[jax.ShapeDtypeStruct((bsz, ta, w), dt) for w, dt in zip(widths, dtypes)],
        compiler_params=_cparams(("arbitrary", "arbitrary")),
        name="inproj0",
    )(ctx, x, modsel, w_big, w_small)


def _conv_kernel(cur_ref, prev_ref, next_ref, w_ref, b_ref, o_ref, *, ctx_tiles, n_tiles):
    j = pl.program_id(1)
    has_prev = jnp.logical_and(j != 0, j != ctx_tiles)
    has_next = jnp.logical_and(j != ctx_tiles - 1, j != n_tiles - 1)
    prev = jnp.where(has_prev, prev_ref[0], 0.0)
    nxt = jnp.where(has_next, next_ref[0], 0.0)
    ext = jnp.concatenate([prev, cur_ref[0], nxt], axis=0)
    n = TM + 2 * SUBLANES
    w = w_ref[...]
    lo, hi = SUBLANES, SUBLANES + TM
    acc = ext[lo:hi] * w[2:3]
    acc = acc + pltpu.roll(ext, 2, 0)[lo:hi] * w[0:1]
    acc = acc + pltpu.roll(ext, 1, 0)[lo:hi] * w[1:2]
    acc = acc + pltpu.roll(ext, n - 1, 0)[lo:hi] * w[3:4]
    o_ref[0] = _silu(acc + b_ref[...])


def conv_call(xbc, conv_w, conv_b, ctx_tiles):
    bsz, ta, ch = xbc.shape
    nt = ta // TM
    r = TM // SUBLANES
    last = ta // SUBLANES - 1
    return pl.pallas_call(
        functools.partial(_conv_kernel, ctx_tiles=ctx_tiles, n_tiles=nt),
        grid=(bsz, nt),
        in_specs=[pl.BlockSpec((1, TM, ch), lambda b, j: (b, j, 0)),
                  pl.BlockSpec((1, SUBLANES, ch), lambda b, j: (b, jnp.maximum(j * r - 1, 0), 0)),
                  pl.BlockSpec((1, SUBLANES, ch), lambda b, j: (b, jnp.minimum((j + 1) * r, last), 0)),
                  pl.BlockSpec((CONV_K, ch), lambda b, j: (0, 0)),
                  pl.BlockSpec((1, ch), lambda b, j: (0, 0))],
        out_specs=pl.BlockSpec((1, TM, ch), lambda b, j: (b, j, 0)),
        out_shape=jax.ShapeDtypeStruct((bsz, ta, ch), F32),
        compiler_params=_cparams(("arbitrary", "arbitrary")),
        name="conv",
    )(xbc, xbc, xbc, conv_w, conv_b.reshape(1, ch))


def _chunk_order(i, rev, ctx_chunks, n_chunks):
    if not rev:
        return i
    return jnp.where(i < ctx_chunks, ctx_chunks - 1 - i, n_chunks - 1 - (i - ctx_chunks))


def _scan_masks(rev):
    r = lax.broadcasted_iota(jnp.int32, (CHUNK, CHUNK), 0)
    c = lax.broadcasted_iota(jnp.int32, (CHUNK, CHUNK), 1)
    mask = (c >= r) if rev else (c <= r)
    mask_t = (r >= c) if rev else (r <= c)
    return mask, mask.astype(F32), mask_t.astype(F32)


def _per_batch(body, kinds):
    def kern(*refs, **kw):
        assert len(refs) == len(kinds)
        for bi in range(refs[0].shape[0]):
            sub = [r.at[pl.ds(bi, 1)] if f == 'b' else r.at[bi] if f == 's' else r for r, f in zip(refs, kinds)]
            body(*sub, **kw)
    return kern


def _mlstm_kernel(*refs, rev, d, add_prev):
    if add_prev:
        (q_ref, k_ref, v_ref, gc_ref, gr_ref, brow_ref, bcol_ref, prev_ref, og_ref, mg_ref,
         o_ref, ct_s, n_s, m_s) = refs
    else:
        q_ref, k_ref, v_ref, gc_ref, gr_ref, brow_ref, bcol_ref, o_ref, ct_s, n_s, m_s = refs
        prev_ref = None

    @pl.when(pl.program_id(0) == 0)
    def _():
        ct_s[...] = jnp.zeros_like(ct_s)
        n_s[...] = jnp.zeros_like(n_s)
        m_s[...] = jnp.zeros_like(m_s)

    mask, mask_f, mask_tf = _scan_masks(rev)
    end = 0 if rev else CHUNK - 1
    gcol = gc_ref[0] + brow_ref[...]
    grow = gr_ref[0] + bcol_ref[...]
    nh2 = N_DIR * M_HEADS
    lf_col = _log_sigmoid(gcol)
    lf_row = _log_sigmoid(grow[FG_OFF:FG_OFF + nh2])
    b_col_all = jnp.dot(mask_f, lf_col, precision=HIGHEST, preferred_element_type=F32)
    b_row_all = jnp.dot(lf_row, mask_tf, precision=HIGHEST, preferred_element_type=F32)

    for j in range(M_HEADS):
        ci = d * M_HEADS + j
        bcol = b_col_all[:, FG_OFF + ci:FG_OFF + ci + 1]
        brow = b_row_all[ci:ci + 1, :]
        igcol = gcol[:, IG_OFF + ci:IG_OFF + ci + 1]
        igrow = grow[IG_OFF + ci:IG_OFF + ci + 1, :]
        b_end = bcol[end:end + 1, :]
        m_prev = m_s[j][:, 0:1]
        n_prev = n_s[j]
        ct_prev = ct_s[j]
        q = q_ref[0, :, j * M_QK_DIM:(j + 1) * M_QK_DIM]
        k = k_ref[0, :, j * M_QK_DIM:(j + 1) * M_QK_DIM]
        v = v_ref[0, :, j * M_V_DIM:(j + 1) * M_V_DIM]
        qf = q.astype(F32)
        kf = k.astype(F32)
        vf = v.astype(F32)

        dmat = jnp.where(mask, bcol - brow + igrow, -jnp.inf)
        inter = bcol + m_prev
        m_t = jnp.maximum(inter, jnp.max(dmat, axis=1, keepdims=True))
        sc = _mm_nt(q, k) * jnp.exp(dmat - m_t)
        a_in = jnp.exp(inter - m_t)
        num = _mm(sc, v) + a_in * _mm(q, ct_prev)
        den = jnp.sum(sc, axis=1, keepdims=True) + a_in * jnp.sum(qf * n_prev, axis=1, keepdims=True)
        h = num / jnp.maximum(jnp.abs(den), jnp.exp(-m_t))
        sl = slice(j * M_V_DIM, (j + 1) * M_V_DIM)
        if add_prev:
            h = h + prev_ref[0, :, sl]
            mu = jnp.mean(h, axis=-1, keepdims=True)
            hc = h - mu
            var = jnp.mean(hc * hc, axis=-1, keepdims=True)
            h = hc * lax.rsqrt(var + LN_EPS) * mg_ref[:, sl] * _sigmoid(og_ref[0, :, sl])
        o_ref[0, :, sl] = h.astype(o_ref.dtype)

        g_col = b_end - bcol + igcol
        g_row = b_end - brow + igrow
        g_max = jnp.max(g_row, axis=1, keepdims=True)
        w_col = jnp.exp(g_col - g_max)
        d_ct = _mm_tn(k, vf * w_col)
        d_n = jnp.sum(kf * w_col, axis=0, keepdims=True)
        m_new = jnp.maximum(b_end + m_prev, g_max)
        a = jnp.exp(b_end + m_prev - m_new)
        s = jnp.exp(g_max - m_new)
        ct_s[j] = a * ct_prev + s * d_ct
        n_s[j] = a * n_prev + s * d_n
        m_s[j] = jnp.broadcast_to(m_new, (1, LANES))


def mlstm_call(q, k, v, gcol, grow, bias_row, bias_col, prev, o_gate, m_norm_g, *, rev, d, ctx_chunks):
    bsz, ta, _ = q.shape
    nc = ta // CHUNK
    order = lambda i: _chunk_order(i, rev, ctx_chunks, nc)
    tok = lambda w: pl.BlockSpec((bsz, CHUNK, w), lambda i: (0, order(i), 0))
    in_specs = [tok(q.shape[-1]), tok(k.shape[-1]), tok(v.shape[-1]), tok(GATE_LANES),
                pl.BlockSpec((bsz, grow.shape[1], CHUNK), lambda i: (0, 0, order(i))),
                pl.BlockSpec(bias_row.shape, lambda i: (0, 0)),
                pl.BlockSpec(bias_col.shape, lambda i: (0, 0))]
    args = [q, k, v, gcol, grow, bias_row, bias_col]
    kinds = "bbbbbcc"
    last = prev is not None
    if last:
        in_specs += [tok(M_WIDTH), tok(M_WIDTH), pl.BlockSpec(m_norm_g.shape, lambda i: (0, 0))]
        args += [prev, o_gate, m_norm_g]
        kinds += "bbc"
    kinds += "b" + "sss"
    return pl.pallas_call(
        functools.partial(_per_batch(_mlstm_kernel, kinds), rev=rev, d=d, add_prev=last),
        grid=(nc,),
        in_specs=in_specs,
        out_specs=tok(M_WIDTH),
        out_shape=jax.ShapeDtypeStruct((bsz, ta, M_WIDTH), MXU_DTYPE if last else F32),
        scratch_shapes=[pltpu.VMEM((bsz, M_HEADS, M_QK_DIM, M_V_DIM), F32),
                        pltpu.VMEM((bsz, M_HEADS, 1, M_QK_DIM), F32),
                        pltpu.VMEM((bsz, M_HEADS, 1, LANES), F32)],
        compiler_params=_cparams(("arbitrary",)),
        name="mlstm_rev" if rev else "mlstm_fwd",
    )(*args)


def _ssd_kernel(*refs, rev, d, add_prev):
    if add_prev:
        (x_ref, gc_ref, gr_ref, dtb_row_ref, dtb_col_ref, an_row_ref, an_col_ref, prev_ref, z_ref, dsk_ref, sg_ref,
         o_ref, ht_s) = refs
    else:
        x_ref, gc_ref, gr_ref, dtb_row_ref, dtb_col_ref, an_row_ref, an_col_ref, o_ref, ht_s = refs
        prev_ref = None

    @pl.when(pl.program_id(0) == 0)
    def _():
        ht_s[...] = jnp.zeros_like(ht_s)

    mask, mask_f, mask_tf = _scan_masks(rev)
    end = 0 if rev else CHUNK - 1
    lane = lax.broadcasted_iota(jnp.int32, (CHUNK, LANES), 1)
    first_half = lane < S_HEAD_DIM
    dt_col = _softplus(gc_ref[0] + dtb_row_ref[...])
    dt_row = _softplus(gr_ref[0] + dtb_col_ref[...])
    acs_col = jnp.dot(mask_f, dt_col * an_row_ref[...], precision=HIGHEST, preferred_element_type=F32)
    acs_row = jnp.dot(dt_row * an_col_ref[...], mask_tf, precision=HIGHEST, preferred_element_type=F32)
    a_end_row = acs_col[end:end + 1, :]
    e_cs = jnp.exp(acs_col)
    e_rem = jnp.exp(a_end_row - acs_col)
    e_end = jnp.exp(a_end_row)
    gw = S_HEADS_PER_GROUP * S_HEAD_DIM
    pairs = S_HEADS_PER_GROUP // 2

    def pick(arr, la):
        return jnp.where(first_half[:arr.shape[0]], arr[:, la:la + 1], arr[:, la + 1:la + 2])

    y_parts = []
    for g in range(S_GROUPS):
        bm = x_ref[0, :, S_WIDTH + g * S_STATE:S_WIDTH + (g + 1) * S_STATE]
        cm = x_ref[0, :, S_WIDTH + (S_GROUPS + g) * S_STATE:S_WIDTH + (S_GROUPS + g + 1) * S_STATE]
        cb = _mm_nt(cm, bm)
        ht_prev = ht_s[g]
        y_inter = _mm(cm, ht_prev)
        xw_parts = []
        decay_parts = []
        for p in range(pairs):
            h0 = g * S_HEADS_PER_GROUP + 2 * p
            la = DT_OFF + d * S_HEADS + h0
            ra = d * S_HEADS + h0
            lhs = []
            for u in range(2):
                seg = acs_col[:, la + u:la + u + 1] - acs_row[ra + u:ra + u + 1, :]
                dec = jnp.exp(jnp.where(mask, seg, -jnp.inf))
                lhs.append((cb * dec).astype(MXU_DTYPE))
            xs = x_ref[0, :, h0 * S_HEAD_DIM:(h0 + 2) * S_HEAD_DIM]
            xsd = xs * pick(dt_col, la)
            rhs = jnp.concatenate([jnp.where(first_half, xsd, 0.0), jnp.where(first_half, 0.0, xsd)],
                                  axis=0).astype(MXU_DTYPE)
            y = jnp.dot(jnp.concatenate(lhs, axis=1), rhs, preferred_element_type=F32)
            y = y + y_inter[:, p * LANES:(p + 1) * LANES] * pick(e_cs, la)
            sl = slice(h0 * S_HEAD_DIM, (h0 + 2) * S_HEAD_DIM)
            if add_prev:
                y = (y + prev_ref[0, :, sl] + dsk_ref[:, sl] * xs) * _silu(z_ref[0, :, sl])
                y_parts.append(y)
            else:
                o_ref[0, :, sl] = y
            xw_parts.append(xsd * pick(e_rem, la))
            decay_parts.append(pick(e_end, la))
        xw = jnp.concatenate(xw_parts, axis=1)
        decay = jnp.concatenate(decay_parts, axis=1)
        ht_s[g] = decay * ht_prev + _mm_tn(bm, xw)

    if add_prev:
        ys = jnp.concatenate(y_parts, axis=1)
        ys = ys * lax.rsqrt(jnp.mean(ys * ys, axis=-1, keepdims=True) + RMS_EPS) * sg_ref[...]
        o_ref[0] = ys.astype(o_ref.dtype)


def ssd_call(xbc_act, gcol, dtrow, dtb_row, dtb_col, an_row, an_col, prev, z, dskip, s_norm_g, *, rev, d,
             ctx_chunks):
    bsz, ta, ch = xbc_act.shape
    nc = ta // CHUNK
    order = lambda i: _chunk_order(i, rev, ctx_chunks, nc)
    tok = lambda w: pl.BlockSpec((bsz, CHUNK, w), lambda i: (0, order(i), 0))
    const = lambda a: pl.BlockSpec(a.shape, lambda i: (0, 0))
    in_specs = [tok(ch), tok(GATE_LANES),
                pl.BlockSpec((bsz, dtrow.shape[1], CHUNK), lambda i: (0, 0, order(i))),
                const(dtb_row), const(dtb_col), const(an_row), const(an_col)]
    args = [xbc_act, gcol, dtrow, dtb_row, dtb_col, an_row, an_col]
    kinds = "bbbcccc"
    last = prev is not None
    if last:
        in_specs += [tok(S_WIDTH), tok(S_WIDTH), const(dskip), const(s_norm_g)]
        args += [prev, z, dskip, s_norm_g]
        kinds += "bbcc"
    kinds += "b" + "s"
    return pl.pallas_call(
        functools.partial(_per_batch(_ssd_kernel, kinds), rev=rev, d=d, add_prev=last),
        grid=(nc,),
        in_specs=in_specs,
        out_specs=tok(S_WIDTH),
        out_shape=jax.ShapeDtypeStruct((bsz, ta, S_WIDTH), MXU_DTYPE if last else F32),
        scratch_shapes=[pltpu.VMEM((bsz, S_GROUPS, S_STATE, S_HEADS_PER_GROUP * S_HEAD_DIM), F32)],
        compiler_params=_cparams(("arbitrary",)),
        name="ssd_rev" if rev else "ssd_fwd",
    )(*args)


def _store_row_tiles(ref, val):
    for s in range(ROW_CHUNKS):
        ref[pl.ds(s, val.shape[0], stride=ROW_CHUNKS), :] = val[:, s * LANES:(s + 1) * LANES]


def _load_row_tiles(ref, rows):
    return jnp.concatenate([ref[pl.ds(s, rows, stride=ROW_CHUNKS), :] for s in range(ROW_CHUNKS)], axis=1)


def _finish_sublayer(x, y, mod, lng_ref, lnb_ref, rw_ref, rb_ref, x1_ref, h2_ref, e_ref, g_ref):
    x1 = _layer_norm_rows(DEEPNORM_ALPHA * x + mod[2:3] * y, lng_ref[...], lnb_ref[...])
    x1_ref[0] = x1
    h2 = x1 * (1.0 + mod[4:5]) + mod[3:4]
    _store_row_tiles(h2_ref, h2)
    e, g = _route(h2, rw_ref, rb_ref)
    e_ref[...] = e
    g_ref[...] = g


def _outproj0_kernel(ym_ref, ys_ref, ctx_ref, x_ref, mod_ref, w_ref, lng_ref, lnb_ref, rw_ref, rb_ref,
                     x1_ref, h2_ref, e_ref, g_ref, *, ctx_tiles):
    y = (jnp.dot(ym_ref[0], w_ref[:M_WIDTH, :], preferred_element_type=F32)
         + jnp.dot(ys_ref[0], w_ref[M_WIDTH:, :], preferred_element_type=F32))
    _finish_sublayer(_token_tile(ctx_ref, x_ref, ctx_tiles), y, mod_ref[0, 0], lng_ref, lnb_ref, rw_ref, rb_ref,
                     x1_ref, h2_ref, e_ref, g_ref)


def _sublayer_out(bsz, tt, d):
    nt = tt // TM
    n = bsz * tt
    specs = [pl.BlockSpec((1, TM, d), lambda b, j: (b, j, 0)),
             pl.BlockSpec((TM * ROW_CHUNKS, LANES), lambda b, j: (b * nt + j, 0)),
             pl.BlockSpec((TOP_K, TM), lambda b, j: (0, b * nt + j)),
             pl.BlockSpec((TOP_K, TM), lambda b, j: (0, b * nt + j))]
    shapes = [jax.ShapeDtypeStruct((bsz, tt, d), F32), jax.ShapeDtypeStruct((n * ROW_CHUNKS, LANES), F32),
              jax.ShapeDtypeStruct((TOP_K, n), jnp.int32), jax.ShapeDtypeStruct((TOP_K, n), F32)]
    return specs, shapes


def outproj0_call(ym, ys, ctx, x, modsel, w_out, ln_g, ln_b, router_wt, router_b, ctx_tiles):
    bsz, ta, _ = ym.shape
    d = x.shape[-1]
    nt = ta // TM
    tok = lambda w: pl.BlockSpec((1, TM, w), lambda b, j: (b, j, 0))
    const = lambda a: pl.BlockSpec(a.shape, lambda b, j: (0, 0))
    out_specs, out_shape = _sublayer_out(bsz, ta, d)
    return pl.pallas_call(
        functools.partial(_outproj0_kernel, ctx_tiles=ctx_tiles),
        grid=(bsz, nt),
        in_specs=[tok(M_WIDTH), tok(S_WIDTH)] + _token_specs(d, ctx_tiles) + [
                  pl.BlockSpec((1, 1, 6, d), lambda b, j: (b, (j >= ctx_tiles).astype(jnp.int32), 0, 0)),
                  const(w_out), const(ln_g), const(ln_b), const(router_wt), const(router_b)],
        out_specs=out_specs,
        out_shape=out_shape,
        compiler_params=_cparams(("arbitrary", "arbitrary")),
        name="outproj0",
    )(ym, ys, ctx, x, modsel, w_out, ln_g, ln_b, router_wt, router_b)


def _top2(vals, probs):
    v1, i1, p1 = vals[0], jnp.zeros_like(vals[0], dtype=jnp.int32), probs[0]
    for i in range(1, len(vals)):
        better = vals[i] > v1
        v1 = jnp.where(better, vals[i], v1)
        i1 = jnp.where(better, i, i1)
        p1 = jnp.where(better, probs[i], p1)
    v2 = jnp.full_like(vals[0], -jnp.inf)
    i2 = jnp.zeros_like(i1)
    p2 = jnp.zeros_like(p1)
    for i in range(len(vals)):
        better = jnp.logical_and(i1 != i, vals[i] > v2)
        v2 = jnp.where(better, vals[i], v2)
        i2 = jnp.where(better, i, i2)
        p2 = jnp.where(better, probs[i], p2)
    return v1, i1, p1, v2, i2, p2


def _split_hi_lo(x):
    hi = x.astype(MXU_DTYPE)
    return hi, (x - hi.astype(F32)).astype(MXU_DTYPE)


def _route(h, w_ref, b_ref):
    h_hi, h_lo = _split_hi_lo(h)
    w_hi, w_lo = _split_hi_lo(w_ref[...])
    lg = (jnp.dot(h_hi, w_hi, preferred_element_type=F32) + jnp.dot(h_lo, w_hi, preferred_element_type=F32)
          + jnp.dot(h_hi, w_lo, preferred_element_type=F32))
    logits = lg.T[:N_EXPERTS]
    mx = jnp.max(logits, axis=0, keepdims=True)
    ex = jnp.exp(logits - mx)
    probs = ex / jnp.sum(ex, axis=0, keepdims=True)
    sel = probs + b_ref[...]
    best = None
    for g in range(N_EXPERT_GROUPS):
        rows = range(g * EXPERTS_PER_GROUP, (g + 1) * EXPERTS_PER_GROUP)
        v1, i1, p1, v2, i2, p2 = _top2([sel[r:r + 1] for r in rows], [probs[r:r + 1] for r in rows])
        cand = (v1 + v2, i1 + g * EXPERTS_PER_GROUP, p1, i2 + g * EXPERTS_PER_GROUP, p2)
        if best is None:
            best = cand
        else:
            better = cand[0] > best[0]
            best = tuple(jnp.where(better, c, o) for c, o in zip(cand, best))
    _, e1, p1, e2, p2 = best
    tot = p1 + p2
    return jnp.concatenate([e1, e2], axis=0), jnp.concatenate([p1 / tot, p2 / tot], axis=0)


def _row_tile(ref, r):
    return ref.at[pl.ds(pl.multiple_of(r * ROW_CHUNKS, ROW_CHUNKS), ROW_CHUNKS)]


def _dispatch_kernel(dest_ref, zblk_ref, h_ref, xs_ref, zero_s, sem, zsem, *, n_tok):
    base = pl.program_id(0) * DISPATCH_ROWS
    blk_rows = MOE_BLK * ROW_CHUNKS

    @pl.when(pl.program_id(0) == 0)
    def _():
        zero_s[...] = jnp.zeros_like(zero_s)

        def zero_copy(t):
            start = pl.multiple_of(zblk_ref[t] * blk_rows, blk_rows)
            return pltpu.make_async_copy(zero_s, xs_ref.at[pl.ds(start, blk_rows)], zsem)

        for t in range(2 * N_EXPERTS):
            pl.when(zblk_ref[t] >= 0)(lambda t=t: zero_copy(t).start())
        for t in range(2 * N_EXPERTS):
            pl.when(zblk_ref[t] >= 0)(lambda t=t: zero_copy(t).wait())

    def slot_copy(r, slot):
        return pltpu.make_async_copy(_row_tile(h_ref, r), _row_tile(xs_ref, slot), sem)

    def issue(r, carry):
        for c in range(TOP_K):
            slot_copy(r, dest_ref[c * n_tok + base + r]).start(priority=c)
        return carry

    def drain(r, carry):
        for c in range(TOP_K):
            slot_copy(r, 0).wait()
        return carry

    lax.fori_loop(0, DISPATCH_ROWS, issue, 0, unroll=8)
    lax.fori_loop(0, DISPATCH_ROWS, drain, 0, unroll=8)


def dispatch_call(dest, zero_blocks, h2t, n_slots):
    n_tok = h2t.shape[0] // ROW_CHUNKS
    return pl.pallas_call(
        functools.partial(_dispatch_kernel, n_tok=n_tok),
        grid_spec=pltpu.PrefetchScalarGridSpec(
            num_scalar_prefetch=2,
            grid=(n_tok // DISPATCH_ROWS,),
            in_specs=[pl.BlockSpec((DISPATCH_ROWS * ROW_CHUNKS, LANES), lambda i, dest, zb: (i, 0))],
            out_specs=pl.BlockSpec(memory_space=pl.ANY),
            scratch_shapes=[pltpu.VMEM((MOE_BLK * ROW_CHUNKS, LANES), h2t.dtype),
                            pltpu.SemaphoreType.DMA(()), pltpu.SemaphoreType.DMA(())]),
        out_shape=jax.ShapeDtypeStruct((n_slots * ROW_CHUNKS, LANES), h2t.dtype),
        compiler_params=_cparams(("arbitrary",)),
        name="dispatch",
    )(dest, zero_blocks, h2t)


def _experts_kernel(be_ref, cnt_ref, x_ref, wg_ref, wu_ref, wd_ref, o_ref, wg_s, wu_s, wd_s):
    i = pl.program_id(0)
    e = be_ref[i]
    e_before = be_ref[jnp.maximum(i - 1, 0)]
    cnt = cnt_ref[i]

    @pl.when(jnp.logical_or(i == 0, e != e_before))
    def _():
        wg_s[...] = wg_ref[0, 0].astype(wg_s.dtype)
        wu_s[...] = wu_ref[0, 0].astype(wu_s.dtype)
        wd_s[...] = wd_ref[0, 0].astype(wd_s.dtype)

    @pl.when(cnt > 0)
    def _():
        xb = _load_row_tiles(x_ref, MOE_BLK).astype(MXU_DTYPE)
        gt = jnp.dot(xb, wg_s[...], preferred_element_type=F32)
        up = jnp.dot(xb, wu_s[...], preferred_element_type=F32)
        y = jnp.dot((_silu(gt) * up).astype(MXU_DTYPE), wd_s[...], preferred_element_type=F32)
        _store_row_tiles(o_ref, y)

    @pl.when(cnt == 0)
    def _():
        o_ref[...] = jnp.zeros_like(o_ref)


def experts_call(blk_exp, blk_cnt, xs, w_gate, w_up, w_down, layer):
    n_slots = xs.shape[0] // ROW_CHUNKS
    d, f = w_gate.shape[-2:]
    wspec = lambda a: pl.BlockSpec((1, 1) + a.shape[2:], lambda i, be, cnt: (layer, be[i], 0, 0))
    blk = pl.BlockSpec((MOE_BLK * ROW_CHUNKS, LANES), lambda i, be, cnt: (i, 0))
    return pl.pallas_call(
        _experts_kernel,
        grid_spec=pltpu.PrefetchScalarGridSpec(
            num_scalar_prefetch=2,
            grid=(n_slots // MOE_BLK,),
            in_specs=[blk, wspec(w_gate), wspec(w_up), wspec(w_down)],
            out_specs=blk,
            scratch_shapes=[pltpu.VMEM((d, f), MXU_DTYPE), pltpu.VMEM((d, f), MXU_DTYPE),
                            pltpu.VMEM((f, d), MXU_DTYPE)]),
        out_shape=jax.ShapeDtypeStruct(xs.shape, F32),
        compiler_params=_cparams(("arbitrary",)),
        name="experts",
    )(blk_exp, blk_cnt, xs, w_gate, w_up, w_down)


def _combined_tile(dest_ref, y_ref, gate_ref, x_ref, mod_ref, lng_ref, lnb_ref, ybuf, sems, *, n_tok, nt, n_steps):
    step = pl.program_id(0) * nt + pl.program_id(1)
    slot = step % 2

    def row_copy(buf, r, c, src_slot):
        return pltpu.make_async_copy(_row_tile(y_ref, src_slot), _row_tile(ybuf.at[buf, c], r), sems.at[buf])

    def issue(tile, buf):
        base = tile * TM

        def body(r, carry):
            for c in range(TOP_K):
                row_copy(buf, r, c, dest_ref[c * n_tok + base + r]).start(priority=c)
            return carry

        lax.fori_loop(0, TM, body, 0, unroll=8)

    pl.when(step == 0)(lambda: issue(0, 0))
    pl.when(step + 1 < n_steps)(lambda: issue(step + 1, 1 - slot))

    def drain(r, carry):
        for c in range(TOP_K):
            row_copy(slot, r, c, 0).wait()
        return carry

    lax.fori_loop(0, TM, drain, 0, unroll=8)
    mod = mod_ref[0, 0]
    gate = gate_ref[...]
    y = (gate[:, 0:1] * _load_row_tiles(ybuf.at[slot, 0], TM)
         + gate[:, 1:2] * _load_row_tiles(ybuf.at[slot, 1], TM))
    return _layer_norm_rows(DEEPNORM_ALPHA * x_ref[0] + mod[5:6] * y, lng_ref[...], lnb_ref[...])


def _combine_kernel(dest_ref, y_ref, gate_ref, x_ref, mod_ref, lng_ref, lnb_ref, o_ref, ybuf, sems, **kw):
    o_ref[0] = _combined_tile(dest_ref, y_ref, gate_ref, x_ref, mod_ref, lng_ref, lnb_ref, ybuf, sems, **kw)


def _combine_inproj1_kernel(dest_ref, y_ref, gate_ref, x_ref, mod_ref, lng_ref, lnb_ref,
                            mod1_ref, w_ref, qg_ref, kg_ref, cos_ref, sin_ref,
                            o_ref, q_ref, k_ref, v_ref, ybuf, sems, **kw):
    x = _combined_tile(dest_ref, y_ref, gate_ref, x_ref, mod_ref, lng_ref, lnb_ref, ybuf, sems, **kw)
    o_ref[0] = x
    _qkv_project(x, mod1_ref[0, 0], w_ref, qg_ref, kg_ref, cos_ref, sin_ref, q_ref, k_ref, v_ref)


def combine_call(dest, y_slots, gate_cols, x1, modsel, ln_g, ln_b, ctx_tiles, proj=None):
    bsz, tt, d = x1.shape
    nt = tt // TM
    tok = lambda w: pl.BlockSpec((1, TM, w), lambda b, j, dest: (b, j, 0))
    const = lambda a: pl.BlockSpec(a.shape, lambda b, j, dest: (0, 0))
    modspec = pl.BlockSpec((1, 1, 6, d), lambda b, j, dest: (b, (j >= ctx_tiles).astype(jnp.int32), 0, 0))
    in_specs = [pl.BlockSpec(memory_space=pl.ANY),
                pl.BlockSpec((TM, TOP_K), lambda b, j, dest: (b * nt + j, 0)),
                tok(d), modspec, const(ln_g), const(ln_b)]
    args = [dest, y_slots, gate_cols, x1, modsel, ln_g, ln_b]
    out_specs = [tok(d)]
    out_shape = [jax.ShapeDtypeStruct((bsz, tt, d), F32)]
    body = _combine_kernel
    if proj is not None:
        modsel1, w_in, q_g, k_g, cos2, sin2 = proj
        rope = pl.BlockSpec((TM, A_HEAD_DIM), lambda b, j, dest: (j, 0))
        in_specs += [modspec, const(w_in), const(q_g), const(k_g), rope, rope]
        args += [modsel1, w_in, q_g, k_g, cos2, sin2]
        widths = (A_Q_W, A_KV_W, A_KV_W)
        out_specs += [tok(w) for w in widths]
        out_shape += [jax.ShapeDtypeStruct((bsz, tt, w), MXU_DTYPE) for w in widths]
        body = _combine_inproj1_kernel
    res = pl.pallas_call(
        functools.partial(body, n_tok=bsz * tt, nt=nt, n_steps=bsz * nt),
        grid_spec=pltpu.PrefetchScalarGridSpec(
            num_scalar_prefetch=1,
            grid=(bsz, nt),
            in_specs=in_specs,
            out_specs=out_specs,
            scratch_shapes=[pltpu.VMEM((2, TOP_K, TM * ROW_CHUNKS, LANES), F32), pltpu.SemaphoreType.DMA((2,))]),
        out_shape=out_shape,
        compiler_params=_cparams(("arbitrary", "arbitrary")),
        name="combine" if proj is None else "combine_inproj1",
    )(*args)
    return res[0] if proj is None else res


def moe_block(x1, h2t, e_idx, gates, modsel, ln_g, ln_b, w_gate, w_up, w_down, layer, ctx_tiles, proj=None):
    bsz, tt, d = x1.shape
    n = bsz * tt

    n_asg = TOP_K * n
    flat_e = e_idx.reshape(n_asg)
    onehot = (flat_e[:, None] == jnp.arange(N_EXPERTS, dtype=jnp.int32)[None, :]).astype(jnp.int32)
    csum = jnp.cumsum(onehot, axis=0)
    rank = jnp.sum(onehot * csum, axis=1) - 1
    counts = csum[-1]
    padded = (counts + MOE_BLK - 1) // MOE_BLK * MOE_BLK
    pend = jnp.cumsum(padded)
    pstart = pend - padded
    dest = (pstart[flat_e] + rank).astype(jnp.int32)
    n_blocks = -(-n_asg // MOE_BLK) + N_EXPERTS
    blk_start = jnp.arange(n_blocks, dtype=jnp.int32) * MOE_BLK
    blk_exp = jnp.sum((pend[None, :] <= blk_start[:, None]).astype(jnp.int32), axis=1)
    blk_exp = jnp.minimum(blk_exp, N_EXPERTS - 1)
    blk_cnt = jnp.clip(counts[blk_exp] - (blk_start - pstart[blk_exp]), 0, MOE_BLK).astype(jnp.int32)

    part = jnp.where(counts % MOE_BLK != 0, pend // MOE_BLK - 1, -1)
    tail = pend[-1] // MOE_BLK + jnp.arange(N_EXPERTS, dtype=jnp.int32)
    tail = jnp.where(tail < n_blocks, tail, -1)
    zero_blocks = jnp.concatenate([part, tail]).astype(jnp.int32)

    xs = dispatch_call(dest, zero_blocks, h2t, n_blocks * MOE_BLK)
    y_slots = experts_call(blk_exp, blk_cnt, xs, w_gate, w_up, w_down, layer)
    return combine_call(dest, y_slots, gates.T, x1, modsel, ln_g, ln_b, ctx_tiles, proj)


def _qkv_project(x, mod, w_ref, qg_ref, kg_ref, cos_ref, sin_ref, q_ref, k_ref, v_ref):
    hb = (x * (1.0 + mod[1:2]) + mod[0:1]).astype(MXU_DTYPE)
    cos = cos_ref[...]
    sin = sin_ref[...]

    def norm_rope(t, g):
        t = t * lax.rsqrt(jnp.mean(t * t, axis=-1, keepdims=True) + RMS_EPS) * g
        return t * cos + pltpu.roll(t, A_HEAD_DIM // 2, 1) * sin

    qkv = jnp.dot(hb, w_ref[...], preferred_element_type=F32)
    for j in range(A_HEADS):
        sl = slice(j * A_HEAD_DIM, (j + 1) * A_HEAD_DIM)
        q_ref[0, :, sl] = (norm_rope(qkv[:, sl], qg_ref[...]) * ATT_Q_SCALE).astype(q_ref.dtype)
    for j in range(A_KV_HEADS):
        sl = slice(j * A_HEAD_DIM, (j + 1) * A_HEAD_DIM)
        t = qkv[:, A_Q_W + j * A_HEAD_DIM:A_Q_W + (j + 1) * A_HEAD_DIM]
        k_ref[0, :, sl] = norm_rope(t, kg_ref[...]).astype(k_ref.dtype)
    v_ref[0] = qkv[:, A_Q_W + A_KV_W:].astype(v_ref.dtype)


def _attn_kernel(q_ref, qn_ref, k_ref, vt_ref, o_ref, s_buf):
    rep = A_HEADS // A_KV_HEADS
    n_kv = vt_ref.shape[2]
    assert n_kv % 2 == 0

    def transposed(ref):
        return [ref[0, :, r * A_HEAD_DIM:(r + 1) * A_HEAD_DIM].astype(F32).T.astype(MXU_DTYPE) for r in range(rep)]

    qts = transposed(q_ref)

    def scores(j, slot, qt=qts):
        kb = k_ref[0, pl.ds(pl.multiple_of(j * ATT_TK, ATT_TK), ATT_TK), :]
        for r in range(rep):
            s_buf[slot, r] = jnp.dot(kb, qt[r], preferred_element_type=F32)

    def consume(j, slot, stats):
        vt = vt_ref[0, 0, j]
        new = []
        for r in range(rep):
            m, l, acc = stats[r]
            s = s_buf[slot, r]
            m_new = jnp.maximum(m, jnp.max(s, axis=0, keepdims=True))
            p = jnp.exp2(s - m_new)
            alpha = jnp.exp2(m - m_new)
            l = alpha * l + jnp.sum(p, axis=0, keepdims=True)
            acc = alpha * acc + jnp.dot(vt, p.astype(vt.dtype), preferred_element_type=F32)
            new.append((m_new, l, acc))
        return tuple(new)

    def pair(i, stats):
        j = 2 * i
        scores(j + 1, 1)
        stats = consume(j, 0, stats)
        scores(j + 2, 0)
        return consume(j + 1, 1, stats)

    stats = tuple((jnp.full((1, ATT_TQ), -jnp.inf, F32), jnp.zeros((1, ATT_TQ), F32),
                   jnp.zeros((A_HEAD_DIM, ATT_TQ), F32)) for _ in range(rep))
    pl.when(pl.program_id(2) == 0)(lambda: scores(0, 0))
    stats = lax.fori_loop(0, n_kv // 2 - 1, pair, stats)
    scores(n_kv - 1, 1)
    stats = consume(n_kv - 2, 0, stats)
    scores(0, 0, transposed(qn_ref))
    final = consume(n_kv - 1, 1, stats)
    for r in range(rep):
        _, l, acc = final[r]
        o_ref[0, :, r * A_HEAD_DIM:(r + 1) * A_HEAD_DIM] = (acc / l).T.astype(o_ref.dtype)


def attn_call(q, k, v, n_ctx):
    bsz, ta, _ = q.shape
    t_lat = ta - n_ctx
    gw = (A_HEADS // A_KV_HEADS) * A_HEAD_DIM
    q_off = n_ctx // ATT_TQ
    n_kv = ta // ATT_TK
    vt = v.reshape(bsz, n_kv, ATT_TK, A_KV_HEADS, A_HEAD_DIM).transpose(0, 3, 1, 4, 2)
    nq = t_lat // ATT_TQ
    return pl.pallas_call(
        _attn_kernel,
        grid=(bsz, A_KV_HEADS, nq),
        in_specs=[pl.BlockSpec((1, ATT_TQ, gw), lambda b, g, i: (b, i + q_off, g)),
                  pl.BlockSpec((1, ATT_TQ, gw), lambda b, g, i: (b, jnp.minimum(i + 1, nq - 1) + q_off, g)),
                  pl.BlockSpec((1, ta, A_HEAD_DIM), lambda b, g, i: (b, 0, g)),
                  pl.BlockSpec((1, 1, n_kv, A_HEAD_DIM, ATT_TK), lambda b, g, i: (b, g, 0, 0, 0))],
        out_specs=pl.BlockSpec((1, ATT_TQ, gw), lambda b, g, i: (b, i, g)),
        out_shape=jax.ShapeDtypeStruct((bsz, t_lat, A_Q_W), MXU_DTYPE),
        scratch_shapes=[pltpu.VMEM((2, A_HEADS // A_KV_HEADS, ATT_TK, ATT_TQ), F32)],
        compiler_params=_cparams(("arbitrary", "arbitrary", "arbitrary")),
        name="attention",
    )(q, q, k, vt)


def _outproj1_kernel(a_ref, x_ref, mod_ref, w_ref, lng_ref, lnb_ref, rw_ref, rb_ref, x1_ref, h2_ref, e_ref, g_ref):
    y = jnp.dot(a_ref[0], w_ref[...], preferred_element_type=F32)
    _finish_sublayer(x_ref[0], y, mod_ref[0, 0], lng_ref, lnb_ref, rw_ref, rb_ref, x1_ref, h2_ref, e_ref, g_ref)


def outproj1_call(att, xa, modsel, w_out, ln_g, ln_b, router_wt, router_b, ctx_tiles):
    bsz, t_lat, _ = att.shape
    d = xa.shape[-1]
    const = lambda a: pl.BlockSpec(a.shape, lambda b, j: (0, 0))
    out_specs, out_shape = _sublayer_out(bsz, t_lat, d)
    return pl.pallas_call(
        _outproj1_kernel,
        grid=(bsz, t_lat // TM),
        in_specs=[pl.BlockSpec((1, TM, att.shape[-1]), lambda b, j: (b, j, 0)),
                  pl.BlockSpec((1, TM, d), lambda b, j: (b, j + ctx_tiles, 0)),
                  pl.BlockSpec((1, 1, 6, d), lambda b, j: (b, 1, 0, 0)),
                  const(w_out), const(ln_g), const(ln_b), const(router_wt), const(router_b)],
        out_specs=out_specs,
        out_shape=out_shape,
        compiler_params=_cparams(("arbitrary", "arbitrary")),
        name="outproj1",
    )(att, xa, modsel, w_out, ln_g, ln_b, router_wt, router_b)


def _rope_tables(n_ctx, n_lat):
    rows = n_lat // GRID_W
    row = jnp.repeat(jnp.arange(rows), GRID_W).astype(F32)
    col = jnp.tile(jnp.arange(GRID_W), rows).astype(F32)
    n_freq = A_HEAD_DIM // 4
    inv = ROPE_THETA ** (-jnp.arange(n_freq, dtype=F32) / n_freq)
    ang = jnp.concatenate([row[:, None] * inv, col[:, None] * inv], -1)
    cos, sin = jnp.cos(ang), jnp.sin(ang)
    cos2 = jnp.concatenate([cos, cos], -1)
    sin2 = jnp.concatenate([-sin, sin], -1)
    cos2 = jnp.concatenate([jnp.ones((n_ctx, A_HEAD_DIM), F32), cos2], 0)
    sin2 = jnp.concatenate([jnp.zeros((n_ctx, A_HEAD_DIM), F32), sin2], 0)
    return cos2, sin2


def _lane_row(parts, width=GATE_LANES):
    row = jnp.concatenate([p.reshape(-1).astype(F32) for p in parts])
    return jnp.pad(row, (0, width - row.shape[0])).reshape(1, width)


def kernel(x, c, ctx, c_ctx, ada_w, ada_b, ln_g, ln_b, ab_w_in, ab_w_out, ml_ig_b, ml_fg_b, ml_norm_g,
           ssm_conv_w, ssm_conv_b, ssm_dt_b, ssm_a_log, ssm_d, ssm_norm_g, at_w_in, at_w_out, at_q_g, at_k_g,
           router_w, router_b, moe_w_gate, moe_w_up, moe_w_down):
    bsz, n_lat, d = x.shape
    n_ctx = ctx.shape[1]
    assert d == D_MODEL and bsz + 1 <= SUBLANES
    assert n_ctx % TM == 0 and n_lat % TM == 0 and n_lat % GRID_W == 0
    assert (n_ctx + n_lat) % ATT_TK == 0 and n_ctx % ATT_TQ == 0 and n_lat % ATT_TQ == 0
    assert (bsz * (n_ctx + n_lat)) % DISPATCH_ROWS == 0 and (bsz * n_lat) % DISPATCH_ROWS == 0
    ctx_tiles = n_ctx // TM
    ctx_chunks = n_ctx // CHUNK

    crows = jnp.zeros((SUBLANES, d), F32).at[:bsz].set(c).at[bsz].set(c_ctx)
    mods = ada_call(crows, ada_w, ada_b)

    def mod_table(i):
        lat = mods[i, :bsz].reshape(bsz, 1, 6, d)
        cx = jnp.broadcast_to(mods[i, bsz].reshape(1, 1, 6, d), (bsz, 1, 6, d))
        return jnp.concatenate([cx, lat], axis=1)

    router_wt = jnp.pad(router_w, ((0, 0), (0, LANES - N_EXPERTS)))
    router_bc = router_b.reshape(N_EXPERTS, 1)

    modsel = mod_table(0)
    w_in = ab_w_in[0]
    s_q, s_k, s_v, s_o, s_ig, s_fg, s_z, s_xbc = (int(v) for v in
        (0, 512, 1024, 2048, 3072, 3072 + 8, 3072 + 16, 3072 + 16 + 1024))
    s_dt = s_xbc + S_CONV_CH
    w_big = jnp.concatenate([w_in[:, :s_ig], w_in[:, s_z:s_dt]], axis=1).astype(MXU_DTYPE)
    w_small = jnp.concatenate([w_in[:, s_ig:s_z], w_in[:, s_dt:]], axis=1)
    w_small = jnp.pad(w_small, ((0, 0), (0, GATE_LANES - w_small.shape[1])))
    q, k, v, o, z, xbc, gates = inproj0_call(ctx, x, modsel, w_big, w_small, ctx_tiles)
    xbc_act = conv_call(xbc, ssm_conv_w[0], ssm_conv_b[0], ctx_tiles)

    grow = jnp.swapaxes(gates[:, :, :DT_OFF], 1, 2)
    dtrow = jnp.swapaxes(gates[:, :, DT_OFF:DT_OFF + N_DIR * S_HEADS], 1, 2)
    gate_b_row = _lane_row([ml_ig_b[0], ml_fg_b[0], ssm_dt_b[0]])
    gate_b_col = jnp.concatenate([ml_ig_b[0].reshape(-1), ml_fg_b[0].reshape(-1)]).reshape(-1, 1)
    a_neg = -jnp.exp(ssm_a_log[0].astype(F32)).reshape(-1)
    an_row = _lane_row([jnp.zeros((DT_OFF,), F32), a_neg])
    an_col = a_neg.reshape(-1, 1)
    dtb_col = ssm_dt_b[0].reshape(-1, 1)

    dskip = jnp.repeat(ssm_d[0].astype(F32), S_HEAD_DIM).reshape(1, S_WIDTH)
    m_norm_g = ml_norm_g[0].reshape(1, -1)
    s_norm_g = ssm_norm_g[0].reshape(1, -1)
    assert N_DIR == 2
    hm = None
    hs = None
    for dd in range(N_DIR):
        hm = mlstm_call(q, k, v, gates, grow, gate_b_row, gate_b_col, hm, o, m_norm_g, rev=dd == 1, d=dd,
                        ctx_chunks=ctx_chunks)
        hs = ssd_call(xbc_act, gates, dtrow, gate_b_row, dtb_col, an_row, an_col, hs, z, dskip, s_norm_g,
                      rev=dd == 1, d=dd, ctx_chunks=ctx_chunks)

    x1, h2t, e_idx, gates = outproj0_call(hm, hs, ctx, x, modsel, ab_w_out[0].astype(MXU_DTYPE),
                                          ln_g[0, 0].reshape(1, d), ln_b[0, 0].reshape(1, d), router_wt, router_bc,
                                          ctx_tiles)
    modsel1 = mod_table(1)
    cos2, sin2 = _rope_tables(n_ctx, n_lat)
    proj = (modsel1, at_w_in[0].astype(MXU_DTYPE), at_q_g[0].reshape(1, -1), at_k_g[0].reshape(1, -1), cos2, sin2)
    xa, qa, ka, va = moe_block(x1, h2t, e_idx, gates, modsel, ln_g[0, 1].reshape(1, d), ln_b[0, 1].reshape(1, d),
                               moe_w_gate, moe_w_up, moe_w_down, 0, ctx_tiles, proj)
    modsel = modsel1
    att = attn_call(qa, ka, va, n_ctx)
    x1, h2t, e_idx, gates = outproj1_call(att, xa, modsel, at_w_out[0].astype(MXU_DTYPE), ln_g[1, 0].reshape(1, d),
                                          ln_b[1, 0].reshape(1, d), router_wt, router_bc, ctx_tiles)
    return moe_block(x1, h2t, e_idx, gates, modsel, ln_g[1, 1].reshape(1, d), ln_b[1, 1].reshape(1, d),
                     moe_w_gate, moe_w_up, moe_w_down, 1, 0)
```

```python
import functools
import math

import jax
import jax.numpy as jnp
from jax import lax
from jax.experimental import pallas as pl
from jax.experimental.pallas import tpu as pltpu

F32 = jnp.float32
MXU_DTYPE = jnp.bfloat16
HIGHEST = lax.Precision.HIGHEST

D_MODEL = 1024
DEPTH = 2
GRID_W = 64
CHUNK = 128
M_HEADS = 4
M_QK_DIM = D_MODEL // 8
M_V_DIM = D_MODEL // 4
M_WIDTH = M_HEADS * M_V_DIM
S_HEADS = 16
S_HEAD_DIM = D_MODEL // 16
S_GROUPS = 2
S_HEADS_PER_GROUP = S_HEADS // S_GROUPS
S_STATE = 128
S_WIDTH = S_HEADS * S_HEAD_DIM
S_CONV_CH = S_WIDTH + 2 * S_GROUPS * S_STATE
CONV_K = 4
N_DIR = 2
A_HEADS = 8
A_KV_HEADS = 2
A_HEAD_DIM = D_MODEL // A_HEADS
A_Q_W = A_HEADS * A_HEAD_DIM
A_KV_W = A_KV_HEADS * A_HEAD_DIM
ROPE_THETA = 10000.0
N_EXPERTS = 16
N_EXPERT_GROUPS = 4
EXPERTS_PER_GROUP = N_EXPERTS // N_EXPERT_GROUPS
TOP_K = 2
DEEPNORM_ALPHA = (2 * DEPTH) ** 0.25
LN_EPS = 1e-5
RMS_EPS = 1e-6

LANES = 128
SUBLANES = 8
TM = 256
MOE_BLK = 512
DISPATCH_ROWS = 512
ROW_CHUNKS = D_MODEL // LANES
assert ROW_CHUNKS == SUBLANES
ATT_TQ = 256
ATT_TK = 1408
VMEM_LIMIT = 56 * 1024 * 1024
GATE_LANES = 128
IG_OFF, FG_OFF, DT_OFF = 0, N_DIR * M_HEADS, 2 * N_DIR * M_HEADS
ATT_Q_SCALE = A_HEAD_DIM ** -0.5 * math.log2(math.e)


def _cparams(sem):
    return pltpu.CompilerParams(dimension_semantics=sem, vmem_limit_bytes=VMEM_LIMIT)


def _silu(x):
    return x / (1.0 + jnp.exp(-x))


def _sigmoid(x):
    return 1.0 / (1.0 + jnp.exp(-x))


def _softplus(x):
    return jnp.maximum(x, 0.0) + jnp.log(1.0 + jnp.exp(-jnp.abs(x)))


def _log_sigmoid(x):
    return jnp.minimum(x, 0.0) - jnp.log(1.0 + jnp.exp(-jnp.abs(x)))


def _layer_norm_rows(x, g, b):
    mu = jnp.mean(x, axis=-1, keepdims=True)
    xc = x - mu
    var = jnp.mean(xc * xc, axis=-1, keepdims=True)
    return xc * lax.rsqrt(var + LN_EPS) * g + b


def _mm(a, b):
    return jnp.dot(a.astype(MXU_DTYPE), b.astype(MXU_DTYPE), preferred_element_type=F32)


def _mm_nt(a, b):
    return lax.dot_general(a.astype(MXU_DTYPE), b.astype(MXU_DTYPE), (((1,), (1,)), ((), ())),
                           preferred_element_type=F32)


def _mm_tn(a, b):
    return lax.dot_general(a.astype(MXU_DTYPE), b.astype(MXU_DTYPE), (((0,), (0,)), ((), ())),
                           preferred_element_type=F32)


def _ada_kernel(c_ref, w_ref, b_ref, o_ref):
    s = _silu(c_ref[...])
    o_ref[0] = jnp.dot(s, w_ref[0], precision=HIGHEST, preferred_element_type=F32) + b_ref[0]


def ada_call(crows, ada_w, ada_b):
    depth, d, n6 = ada_w.shape
    tn = 1536
    return pl.pallas_call(
        _ada_kernel,
        grid=(depth, n6 // tn),
        in_specs=[pl.BlockSpec((SUBLANES, d), lambda i, j: (0, 0)),
                  pl.BlockSpec((1, d, tn), lambda i, j: (i, 0, j)),
                  pl.BlockSpec((1, 1, tn), lambda i, j: (i, 0, j))],
        out_specs=pl.BlockSpec((1, SUBLANES, tn), lambda i, j: (i, 0, j)),
        out_shape=jax.ShapeDtypeStruct((depth, SUBLANES, n6), F32),
        compiler_params=_cparams(("arbitrary", "arbitrary")),
        name="ada",
    )(crows, ada_w, ada_b.reshape(depth, 1, n6))


def _token_tile(ctx_ref, x_ref, ctx_tiles):
    return jnp.where(pl.program_id(1) < ctx_tiles, ctx_ref[0], x_ref[0])


def _token_specs(d, ctx_tiles):
    return [pl.BlockSpec((1, TM, d), lambda b, j: (b, jnp.minimum(j, ctx_tiles - 1), 0)),
            pl.BlockSpec((1, TM, d), lambda b, j: (b, jnp.maximum(j - ctx_tiles, 0), 0))]


def _inproj0_kernel(ctx_ref, x_ref, mod_ref, wb_ref, ws_ref, q_ref, k_ref, v_ref, o_ref, z_ref, xbc_ref, g_ref, *,
                    ctx_tiles):
    x = _token_tile(ctx_ref, x_ref, ctx_tiles)
    mod = mod_ref[0, 0]
    h = x * (1.0 + mod[1:2]) + mod[0:1]
    hb = h.astype(MXU_DTYPE)
    qk = M_HEADS * M_QK_DIM
    c0 = 0
    q_ref[0] = (jnp.dot(hb, wb_ref[:, c0:c0 + qk], preferred_element_type=F32)
                * (M_QK_DIM ** -0.5)).astype(q_ref.dtype)
    c0 += qk
    k_ref[0] = jnp.dot(hb, wb_ref[:, c0:c0 + qk], preferred_element_type=F32).astype(k_ref.dtype)
    c0 += qk
    v_ref[0] = jnp.dot(hb, wb_ref[:, c0:c0 + M_WIDTH], preferred_element_type=F32).astype(v_ref.dtype)
    c0 += M_WIDTH
    o_ref[0] = jnp.dot(hb, wb_ref[:, c0:c0 + M_WIDTH], preferred_element_type=F32)
    c0 += M_WIDTH
    z_ref[0] = jnp.dot(hb, wb_ref[:, c0:c0 + S_WIDTH], preferred_element_type=F32)
    c0 += S_WIDTH
    xbc_ref[0] = jnp.dot(hb, wb_ref[:, c0:c0 + S_CONV_CH], preferred_element_type=F32)
    h_hi, h_lo = _split_hi_lo(h)
    w_hi, w_lo = _split_hi_lo(ws_ref[...])
    g_ref[0] = (jnp.dot(h_hi, w_hi, preferred_element_type=F32) + jnp.dot(h_lo, w_hi, preferred_element_type=F32)
                + jnp.dot(h_hi, w_lo, preferred_element_type=F32))


def inproj0_call(ctx, x, modsel, w_big, w_small, ctx_tiles):
    bsz, _, d = x.shape
    ta = ctx.shape[1] + x.shape[1]
    nt = ta // TM
    qk = M_HEADS * M_QK_DIM
    widths = (qk, qk, M_WIDTH, M_WIDTH, S_WIDTH, S_CONV_CH, GATE_LANES)
    dtypes = (MXU_DTYPE, MXU_DTYPE, MXU_DTYPE, F32, F32, F32, F32)
    tok = lambda w: pl.BlockSpec((1, TM, w), lambda b, j: (b, j, 0))
    return pl.pallas_call(
        functools.partial(_inproj0_kernel, ctx_tiles=ctx_tiles),
        grid=(bsz, nt),
        in_specs=_token_specs(d, ctx_tiles) + [
                  pl.BlockSpec((1, 1, 6, d), lambda b, j: (b, (j >= ctx_tiles).astype(jnp.int32), 0, 0)),
                  pl.BlockSpec(w_big.shape, lambda b, j: (0, 0)),
                  pl.BlockSpec(w_small.shape, lambda b, j: (0, 0))],
        out_specs=[tok(w) for w in widths],
        out_shape=[jax.ShapeDtypeStruct((bsz, ta, w), dt) for w, dt in zip(widths, dtypes)],
        compiler_params=_cparams(("arbitrary", "arbitrary")),
        name="inproj0",
    )(ctx, x, modsel, w_big, w_small)


def _conv_kernel(cur_ref, prev_ref, next_ref, w_ref, b_ref, o_ref, *, ctx_tiles, n_tiles):
    j = pl.program_id(1)
    has_prev = jnp.logical_and(j != 0, j != ctx_tiles)
    has_next = jnp.logical_and(j != ctx_tiles - 1, j != n_tiles - 1)
    prev = jnp.where(has_prev, prev_ref[0], 0.0)
    nxt = jnp.where(has_next, next_ref[0], 0.0)
    ext = jnp.concatenate([prev, cur_ref[0], nxt], axis=0)
    n = TM + 2 * SUBLANES
    w = w_ref[...]
    lo, hi = SUBLANES, SUBLANES + TM
    acc = ext[lo:hi] * w[2:3]
    acc = acc + pltpu.roll(ext, 2, 0)[lo:hi] * w[0:1]
    acc = acc + pltpu.roll(ext, 1, 0)[lo:hi] * w[1:2]
    acc = acc + pltpu.roll(ext, n - 1, 0)[lo:hi] * w[3:4]
    o_ref[0] = _silu(acc + b_ref[...])


def conv_call(xbc, conv_w, conv_b, ctx_tiles):
    bsz, ta, ch = xbc.shape
    nt = ta // TM
    r = TM // SUBLANES
    last = ta // SUBLANES - 1
    return pl.pallas_call(
        functools.partial(_conv_kernel, ctx_tiles=ctx_tiles, n_tiles=nt),
        grid=(bsz, nt),
        in_specs=[pl.BlockSpec((1, TM, ch), lambda b, j: (b, j, 0)),
                  pl.BlockSpec((1, SUBLANES, ch), lambda b, j: (b, jnp.maximum(j * r - 1, 0), 0)),
                  pl.BlockSpec((1, SUBLANES, ch), lambda b, j: (b, jnp.minimum((j + 1) * r, last), 0)),
                  pl.BlockSpec((CONV_K, ch), lambda b, j: (0, 0)),
                  pl.BlockSpec((1, ch), lambda b, j: (0, 0))],
        out_specs=pl.BlockSpec((1, TM, ch), lambda b, j: (b, j, 0)),
        out_shape=jax.ShapeDtypeStruct((bsz, ta, ch), F32),
        compiler_params=_cparams(("arbitrary", "arbitrary")),
        name="conv",
    )(xbc, xbc, xbc, conv_w, conv_b.reshape(1, ch))


def _chunk_order(i, rev, ctx_chunks, n_chunks):
    if not rev:
        return i
    return jnp.where(i < ctx_chunks, ctx_chunks - 1 - i, n_chunks - 1 - (i - ctx_chunks))


def _scan_masks(rev):
    r = lax.broadcasted_iota(jnp.int32, (CHUNK, CHUNK), 0)
    c = lax.broadcasted_iota(jnp.int32, (CHUNK, CHUNK), 1)
    mask = (c >= r) if rev else (c <= r)
    mask_t = (r >= c) if rev else (r <= c)
    return mask, mask.astype(F32).astype(MXU_DTYPE), mask_t.astype(F32).astype(MXU_DTYPE)


def _split3(x):
    hi = x.astype(MXU_DTYPE)
    r = x - hi.astype(F32)
    mid = r.astype(MXU_DTYPE)
    lo = (r - mid.astype(F32)).astype(MXU_DTYPE)
    return hi, mid, lo


def _prefix_cols(tri, x):
    return sum(jnp.dot(tri, p, preferred_element_type=F32) for p in _split3(x))


def _prefix_rows(x, tri):
    return sum(jnp.dot(p, tri, preferred_element_type=F32) for p in _split3(x))


def _per_batch(body, kinds):
    def kern(*refs, **kw):
        assert len(refs) == len(kinds)
        for bi in range(refs[0].shape[0]):
            sub = [r.at[pl.ds(bi, 1)] if f == 'b' else r.at[bi] if f == 's' else r for r, f in zip(refs, kinds)]
            body(*sub, **kw)
    return kern


def _mlstm_kernel(*refs, rev, d, add_prev):
    if add_prev:
        (q_ref, k_ref, v_ref, gc_ref, gr_ref, brow_ref, bcol_ref, prev_ref, og_ref, mg_ref,
         o_ref, ct_s, n_s, m_s) = refs
    else:
        q_ref, k_ref, v_ref, gc_ref, gr_ref, brow_ref, bcol_ref, o_ref, ct_s, n_s, m_s = refs
        prev_ref = None

    @pl.when(pl.program_id(0) == 0)
    def _():
        ct_s[...] = jnp.zeros_like(ct_s)
        n_s[...] = jnp.zeros_like(n_s)
        m_s[...] = jnp.zeros_like(m_s)

    mask, mask_f, mask_tf = _scan_masks(rev)
    end = 0 if rev else CHUNK - 1
    gcol = gc_ref[0] + brow_ref[...]
    grow = gr_ref[0] + bcol_ref[...]
    nh2 = N_DIR * M_HEADS
    lf_col = _log_sigmoid(gcol)
    lf_row = _log_sigmoid(grow[FG_OFF:FG_OFF + nh2])
    b_col_all = _prefix_cols(mask_f, lf_col)
    b_row_all = _prefix_rows(lf_row, mask_tf)

    for j in range(M_HEADS):
        ci = d * M_HEADS + j
        bcol = b_col_all[:, FG_OFF + ci:FG_OFF + ci + 1]
        brow = b_row_all[ci:ci + 1, :]
        igcol = gcol[:, IG_OFF + ci:IG_OFF + ci + 1]
        igrow = grow[IG_OFF + ci:IG_OFF + ci + 1, :]
        b_end = bcol[end:end + 1, :]
        m_prev = m_s[j][:, 0:1]
        n_prev = n_s[j]
        ct_prev = ct_s[j]
        q = q_ref[0, :, j * M_QK_DIM:(j + 1) * M_QK_DIM]
        k = k_ref[0, :, j * M_QK_DIM:(j + 1) * M_QK_DIM]
        v = v_ref[0, :, j * M_V_DIM:(j + 1) * M_V_DIM]
        qf = q.astype(F32)
        kf = k.astype(F32)
        vf = v.astype(F32)

        dmat = jnp.where(mask, bcol - brow + igrow, -jnp.inf)
        inter = bcol + m_prev
        m_t = jnp.maximum(inter, jnp.max(dmat, axis=1, keepdims=True))
        sc = _mm_nt(q, k) * jnp.exp(dmat - m_t)
        a_in = jnp.exp(inter - m_t)
        num = _mm(sc, v) + a_in * _mm(q, ct_prev)
        den = jnp.sum(sc, axis=1, keepdims=True) + a_in * jnp.sum(qf * n_prev, axis=1, keepdims=True)
        h = num / jnp.maximum(jnp.abs(den), jnp.exp(-m_t))
        sl = slice(j * M_V_DIM, (j + 1) * M_V_DIM)
        if add_prev:
            h = h + prev_ref[0, :, sl]
            mu = jnp.mean(h, axis=-1, keepdims=True)
            hc = h - mu
            var = jnp.mean(hc * hc, axis=-1, keepdims=True)
            h = hc * lax.rsqrt(var + LN_EPS) * mg_ref[:, sl] * _sigmoid(og_ref[0, :, sl])
        o_ref[0, :, sl] = h.astype(o_ref.dtype)

        g_col = b_end - bcol + igcol
        g_row = b_end - brow + igrow
        g_max = jnp.max(g_row, axis=1, keepdims=True)
        w_col = jnp.exp(g_col - g_max)
        d_ct = _mm_tn(k, vf * w_col)
        d_n = jnp.sum(kf * w_col, axis=0, keepdims=True)
        m_new = jnp.maximum(b_end + m_prev, g_max)
        a = jnp.exp(b_end + m_prev - m_new)
        s = jnp.exp(g_max - m_new)
        ct_s[j] = a * ct_prev + s * d_ct
        n_s[j] = a * n_prev + s * d_n
        m_s[j] = jnp.broadcast_to(m_new, (1, LANES))


def mlstm_call(q, k, v, gcol, grow, bias_row, bias_col, prev, o_gate, m_norm_g, *, rev, d, ctx_chunks):
    bsz, ta, _ = q.shape
    nc = ta // CHUNK
    order = lambda i: _chunk_order(i, rev, ctx_chunks, nc)
    tok = lambda w: pl.BlockSpec((bsz, CHUNK, w), lambda i: (0, order(i), 0))
    in_specs = [tok(q.shape[-1]), tok(k.shape[-1]), tok(v.shape[-1]), tok(GATE_LANES),
                pl.BlockSpec((bsz, grow.shape[1], CHUNK), lambda i: (0, 0, order(i))),
                pl.BlockSpec(bias_row.shape, lambda i: (0, 0)),
                pl.BlockSpec(bias_col.shape, lambda i: (0, 0))]
    args = [q, k, v, gcol, grow, bias_row, bias_col]
    kinds = "bbbbbcc"
    last = prev is not None
    if last:
        in_specs += [tok(M_WIDTH), tok(M_WIDTH), pl.BlockSpec(m_norm_g.shape, lambda i: (0, 0))]
        args += [prev, o_gate, m_norm_g]
        kinds += "bbc"
    kinds += "b" + "sss"
    return pl.pallas_call(
        functools.partial(_per_batch(_mlstm_kernel, kinds), rev=rev, d=d, add_prev=last),
        grid=(nc,),
        in_specs=in_specs,
        out_specs=tok(M_WIDTH),
        out_shape=jax.ShapeDtypeStruct((bsz, ta, M_WIDTH), MXU_DTYPE if last else F32),
        scratch_shapes=[pltpu.VMEM((bsz, M_HEADS, M_QK_DIM, M_V_DIM), F32),
                        pltpu.VMEM((bsz, M_HEADS, 1, M_QK_DIM), F32),
                        pltpu.VMEM((bsz, M_HEADS, 1, LANES), F32)],
        compiler_params=_cparams(("arbitrary",)),
        name="mlstm_rev" if rev else "mlstm_fwd",
    )(*args)


def _ssd_kernel(*refs, rev, d, add_prev):
    if add_prev:
        (x_ref, gc_ref, gr_ref, dtb_row_ref, dtb_col_ref, an_row_ref, an_col_ref, prev_ref, z_ref, dsk_ref, sg_ref,
         o_ref, ht_s) = refs
    else:
        x_ref, gc_ref, gr_ref, dtb_row_ref, dtb_col_ref, an_row_ref, an_col_ref, o_ref, ht_s = refs
        prev_ref = None

    @pl.when(pl.program_id(0) == 0)
    def _():
        ht_s[...] = jnp.zeros_like(ht_s)

    mask, mask_f, mask_tf = _scan_masks(rev)
    end = 0 if rev else CHUNK - 1
    lane = lax.broadcasted_iota(jnp.int32, (CHUNK, LANES), 1)
    first_half = lane < S_HEAD_DIM
    dt_col = _softplus(gc_ref[0] + dtb_row_ref[...])
    dt_row = _softplus(gr_ref[0] + dtb_col_ref[...])
    acs_col = _prefix_cols(mask_f, dt_col * an_row_ref[...])
    acs_row = _prefix_rows(dt_row * an_col_ref[...], mask_tf)
    a_end_row = acs_col[end:end + 1, :]
    e_cs = jnp.exp(acs_col)
    e_rem = jnp.exp(a_end_row - acs_col)
    e_end = jnp.exp(a_end_row)
    gw = S_HEADS_PER_GROUP * S_HEAD_DIM
    pairs = S_HEADS_PER_GROUP // 2

    def pick(arr, la):
        return jnp.where(first_half[:arr.shape[0]], arr[:, la:la + 1], arr[:, la + 1:la + 2])

    y_parts = []
    for g in range(S_GROUPS):
        bm = x_ref[0, :, S_WIDTH + g * S_STATE:S_WIDTH + (g + 1) * S_STATE]
        cm = x_ref[0, :, S_WIDTH + (S_GROUPS + g) * S_STATE:S_WIDTH + (S_GROUPS + g + 1) * S_STATE]
        cb = _mm_nt(cm, bm)
        ht_prev = ht_s[g]
        y_inter = _mm(cm, ht_prev)
        xw_parts = []
        decay_parts = []
        for p in range(pairs):
            h0 = g * S_HEADS_PER_GROUP + 2 * p
            la = DT_OFF + d * S_HEADS + h0
            ra = d * S_HEADS + h0
            lhs = []
            for u in range(2):
                seg = acs_col[:, la + u:la + u + 1] - acs_row[ra + u:ra + u + 1, :]
                dec = jnp.exp(jnp.where(mask, seg, -jnp.inf))
                lhs.append((cb * dec).astype(MXU_DTYPE))
            xs = x_ref[0, :, h0 * S_HEAD_DIM:(h0 + 2) * S_HEAD_DIM]
            xsd = xs * pick(dt_col, la)
            rhs = jnp.concatenate([jnp.where(first_half, xsd, 0.0), jnp.where(first_half, 0.0, xsd)],
                                  axis=0).astype(MXU_DTYPE)
            y = jnp.dot(jnp.concatenate(lhs, axis=1), rhs, preferred_element_type=F32)
            y = y + y_inter[:, p * LANES:(p + 1) * LANES] * pick(e_cs, la)
            sl = slice(h0 * S_HEAD_DIM, (h0 + 2) * S_HEAD_DIM)
            if add_prev:
                y = (y + prev_ref[0, :, sl] + dsk_ref[:, sl] * xs) * _silu(z_ref[0, :, sl])
                y_parts.append(y)
            else:
                o_ref[0, :, sl] = y
            xw_parts.append(xsd * pick(e_rem, la))
            decay_parts.append(pick(e_end, la))
        xw = jnp.concatenate(xw_parts, axis=1)
        decay = jnp.concatenate(decay_parts, axis=1)
        ht_s[g] = decay * ht_prev + _mm_tn(bm, xw)

    if add_prev:
        ys = jnp.concatenate(y_parts, axis=1)
        ys = ys * lax.rsqrt(jnp.mean(ys * ys, axis=-1, keepdims=True) + RMS_EPS) * sg_ref[...]
        o_ref[0] = ys.astype(o_ref.dtype)


def ssd_call(xbc_act, gcol, dtrow, dtb_row, dtb_col, an_row, an_col, prev, z, dskip, s_norm_g, *, rev, d,
             ctx_chunks):
    bsz, ta, ch = xbc_act.shape
    nc = ta // CHUNK
    order = lambda i: _chunk_order(i, rev, ctx_chunks, nc)
    tok = lambda w: pl.BlockSpec((bsz, CHUNK, w), lambda i: (0, order(i), 0))
    const = lambda a: pl.BlockSpec(a.shape, lambda i: (0, 0))
    in_specs = [tok(ch), tok(GATE_LANES),
                pl.BlockSpec((bsz, dtrow.shape[1], CHUNK), lambda i: (0, 0, order(i))),
                const(dtb_row), const(dtb_col), const(an_row), const(an_col)]
    args = [xbc_act, gcol, dtrow, dtb_row, dtb_col, an_row, an_col]
    kinds = "bbbcccc"
    last = prev is not None
    if last:
        in_specs += [tok(S_WIDTH), tok(S_WIDTH), const(dskip), const(s_norm_g)]
        args += [prev, z, dskip, s_norm_g]
        kinds += "bbcc"
    kinds += "b" + "s"
    return pl.pallas_call(
        functools.partial(_per_batch(_ssd_kernel, kinds), rev=rev, d=d, add_prev=last),
        grid=(nc,),
        in_specs=in_specs,
        out_specs=tok(S_WIDTH),
        out_shape=jax.ShapeDtypeStruct((bsz, ta, S_WIDTH), MXU_DTYPE if last else F32),
        scratch_shapes=[pltpu.VMEM((bsz, S_GROUPS, S_STATE, S_HEADS_PER_GROUP * S_HEAD_DIM), F32)],
        compiler_params=_cparams(("arbitrary",)),
        name="ssd_rev" if rev else "ssd_fwd",
    )(*args)


def _store_row_tiles(ref, val):
    for s in range(ROW_CHUNKS):
        ref[pl.ds(s, val.shape[0], stride=ROW_CHUNKS), :] = val[:, s * LANES:(s + 1) * LANES]


def _load_row_tiles(ref, rows):
    return jnp.concatenate([ref[pl.ds(s, rows, stride=ROW_CHUNKS), :] for s in range(ROW_CHUNKS)], axis=1)


def _finish_sublayer(x, y, mod, lng_ref, lnb_ref, rw_ref, rb_ref, x1_ref, h2_ref, e_ref, g_ref):
    x1 = _layer_norm_rows(DEEPNORM_ALPHA * x + mod[2:3] * y, lng_ref[...], lnb_ref[...])
    x1_ref[0] = x1
    h2 = x1 * (1.0 + mod[4:5]) + mod[3:4]
    _store_row_tiles(h2_ref, h2)
    e, g = _route(h2, rw_ref, rb_ref)
    e_ref[...] = e
    g_ref[...] = g


def _outproj0_kernel(ym_ref, ys_ref, ctx_ref, x_ref, mod_ref, w_ref, lng_ref, lnb_ref, rw_ref, rb_ref,
                     x1_ref, h2_ref, e_ref, g_ref, *, ctx_tiles):
    y = (jnp.dot(ym_ref[0], w_ref[:M_WIDTH, :], preferred_element_type=F32)
         + jnp.dot(ys_ref[0], w_ref[M_WIDTH:, :], preferred_element_type=F32))
    _finish_sublayer(_token_tile(ctx_ref, x_ref, ctx_tiles), y, mod_ref[0, 0], lng_ref, lnb_ref, rw_ref, rb_ref,
                     x1_ref, h2_ref, e_ref, g_ref)


def _sublayer_out(bsz, tt, d):
    nt = tt // TM
    n = bsz * tt
    specs = [pl.BlockSpec((1, TM, d), lambda b, j: (b, j, 0)),
             pl.BlockSpec((TM * ROW_CHUNKS, LANES), lambda b, j: (b * nt + j, 0)),
             pl.BlockSpec((TOP_K, TM), lambda b, j: (0, b * nt + j)),
             pl.BlockSpec((TOP_K, TM), lambda b, j: (0, b * nt + j))]
    shapes = [jax.ShapeDtypeStruct((bsz, tt, d), F32), jax.ShapeDtypeStruct((n * ROW_CHUNKS, LANES), F32),
              jax.ShapeDtypeStruct((TOP_K, n), jnp.int32), jax.ShapeDtypeStruct((TOP_K, n), F32)]
    return specs, shapes


def outproj0_call(ym, ys, ctx, x, modsel, w_out, ln_g, ln_b, router_wt, router_b, ctx_tiles):
    bsz, ta, _ = ym.shape
    d = x.shape[-1]
    nt = ta // TM
    tok = lambda w: pl.BlockSpec((1, TM, w), lambda b, j: (b, j, 0))
    const = lambda a: pl.BlockSpec(a.shape, lambda b, j: (0, 0))
    out_specs, out_shape = _sublayer_out(bsz, ta, d)
    return pl.pallas_call(
        functools.partial(_outproj0_kernel, ctx_tiles=ctx_tiles),
        grid=(bsz, nt),
        in_specs=[tok(M_WIDTH), tok(S_WIDTH)] + _token_specs(d, ctx_tiles) + [
                  pl.BlockSpec((1, 1, 6, d), lambda b, j: (b, (j >= ctx_tiles).astype(jnp.int32), 0, 0)),
                  const(w_out), const(ln_g), const(ln_b), const(router_wt), const(router_b)],
        out_specs=out_specs,
        out_shape=out_shape,
        compiler_params=_cparams(("arbitrary", "arbitrary")),
        name="outproj0",
    )(ym, ys, ctx, x, modsel, w_out, ln_g, ln_b, router_wt, router_b)


def _top2(vals, probs):
    v1, i1, p1 = vals[0], jnp.zeros_like(vals[0], dtype=jnp.int32), probs[0]
    for i in range(1, len(vals)):
        better = vals[i] > v1
        v1 = jnp.where(better, vals[i], v1)
        i1 = jnp.where(better, i, i1)
        p1 = jnp.where(better, probs[i], p1)
    v2 = jnp.full_like(vals[0], -jnp.inf)
    i2 = jnp.zeros_like(i1)
    p2 = jnp.zeros_like(p1)
    for i in range(len(vals)):
        better = jnp.logical_and(i1 != i, vals[i] > v2)
        v2 = jnp.where(better, vals[i], v2)
        i2 = jnp.where(better, i, i2)
        p2 = jnp.where(better, probs[i], p2)
    return v1, i1, p1, v2, i2, p2


def _split_hi_lo(x):
    hi = x.astype(MXU_DTYPE)
    return hi, (x - hi.astype(F32)).astype(MXU_DTYPE)


def _route(h, w_ref, b_ref):
    h_hi, h_lo = _split_hi_lo(h)
    w_hi, w_lo = _split_hi_lo(w_ref[...])
    lg = (jnp.dot(h_hi, w_hi, preferred_element_type=F32) + jnp.dot(h_lo, w_hi, preferred_element_type=F32)
          + jnp.dot(h_hi, w_lo, preferred_element_type=F32))
    logits = lg.T[:N_EXPERTS]
    mx = jnp.max(logits, axis=0, keepdims=True)
    ex = jnp.exp(logits - mx)
    probs = ex / jnp.sum(ex, axis=0, keepdims=True)
    sel = probs + b_ref[...]
    best = None
    for g in range(N_EXPERT_GROUPS):
        rows = range(g * EXPERTS_PER_GROUP, (g + 1) * EXPERTS_PER_GROUP)
        v1, i1, p1, v2, i2, p2 = _top2([sel[r:r + 1] for r in rows], [probs[r:r + 1] for r in rows])
        cand = (v1 + v2, i1 + g * EXPERTS_PER_GROUP, p1, i2 + g * EXPERTS_PER_GROUP, p2)
        if best is None:
            best = cand
        else:
            better = cand[0] > best[0]
            best = tuple(jnp.where(better, c, o) for c, o in zip(cand, best))
    _, e1, p1, e2, p2 = best
    tot = p1 + p2
    return jnp.concatenate([e1, e2], axis=0), jnp.concatenate([p1 / tot, p2 / tot], axis=0)


def _row_tile(ref, r):
    return ref.at[pl.ds(pl.multiple_of(r * ROW_CHUNKS, ROW_CHUNKS), ROW_CHUNKS)]


def _dispatch_kernel(dest_ref, zblk_ref, h_ref, xs_ref, zero_s, sem, zsem, *, n_tok):
    base = pl.program_id(0) * DISPATCH_ROWS
    blk_rows = MOE_BLK * ROW_CHUNKS

    @pl.when(pl.program_id(0) == 0)
    def _():
        zero_s[...] = jnp.zeros_like(zero_s)

        def zero_copy(t):
            start = pl.multiple_of(zblk_ref[t] * blk_rows, blk_rows)
            return pltpu.make_async_copy(zero_s, xs_ref.at[pl.ds(start, blk_rows)], zsem)

        for t in range(2 * N_EXPERTS):
            pl.when(zblk_ref[t] >= 0)(lambda t=t: zero_copy(t).start())
        for t in range(2 * N_EXPERTS):
            pl.when(zblk_ref[t] >= 0)(lambda t=t: zero_copy(t).wait())

    def slot_copy(r, slot):
        return pltpu.make_async_copy(_row_tile(h_ref, r), _row_tile(xs_ref, slot), sem)

    def issue(r, carry):
        for c in range(TOP_K):
            slot_copy(r, dest_ref[c * n_tok + base + r]).start(priority=c)
        return carry

    def drain(r, carry):
        for c in range(TOP_K):
            slot_copy(r, 0).wait()
        return carry

    lax.fori_loop(0, DISPATCH_ROWS, issue, 0, unroll=8)
    lax.fori_loop(0, DISPATCH_ROWS, drain, 0, unroll=8)


def dispatch_call(dest, zero_blocks, h2t, n_slots):
    n_tok = h2t.shape[0] // ROW_CHUNKS
    return pl.pallas_call(
        functools.partial(_dispatch_kernel, n_tok=n_tok),
        grid_spec=pltpu.PrefetchScalarGridSpec(
            num_scalar_prefetch=2,
            grid=(n_tok // DISPATCH_ROWS,),
            in_specs=[pl.BlockSpec((DISPATCH_ROWS * ROW_CHUNKS, LANES), lambda i, dest, zb: (i, 0))],
            out_specs=pl.BlockSpec(memory_space=pl.ANY),
            scratch_shapes=[pltpu.VMEM((MOE_BLK * ROW_CHUNKS, LANES), h2t.dtype),
                            pltpu.SemaphoreType.DMA(()), pltpu.SemaphoreType.DMA(())]),
        out_shape=jax.ShapeDtypeStruct((n_slots * ROW_CHUNKS, LANES), h2t.dtype),
        compiler_params=_cparams(("arbitrary",)),
        name="dispatch",
    )(dest, zero_blocks, h2t)


def _experts_kernel(be_ref, cnt_ref, x_ref, wg_ref, wu_ref, wd_ref, o_ref, wg_s, wu_s, wd_s):
    i = pl.program_id(0)
    e = be_ref[i]
    e_before = be_ref[jnp.maximum(i - 1, 0)]
    cnt = cnt_ref[i]

    @pl.when(jnp.logical_or(i == 0, e != e_before))
    def _():
        wg_s[...] = wg_ref[0, 0].astype(wg_s.dtype)
        wu_s[...] = wu_ref[0, 0].astype(wu_s.dtype)
        wd_s[...] = wd_ref[0, 0].astype(wd_s.dtype)

    @pl.when(cnt > 0)
    def _():
        xb = _load_row_tiles(x_ref, MOE_BLK).astype(MXU_DTYPE)
        gt = jnp.dot(xb, wg_s[...], preferred_element_type=F32)
        up = jnp.dot(xb, wu_s[...], preferred_element_type=F32)
        y = jnp.dot((_silu(gt) * up).astype(MXU_DTYPE), wd_s[...], preferred_element_type=F32)
        _store_row_tiles(o_ref, y)

    @pl.when(cnt == 0)
    def _():
        o_ref[...] = jnp.zeros_like(o_ref)


def experts_call(blk_exp, blk_cnt, xs, w_gate, w_up, w_down, layer):
    n_slots = xs.shape[0] // ROW_CHUNKS
    d, f = w_gate.shape[-2:]
    wspec = lambda a: pl.BlockSpec((1, 1) + a.shape[2:], lambda i, be, cnt: (layer, be[i], 0, 0))
    blk = pl.BlockSpec((MOE_BLK * ROW_CHUNKS, LANES), lambda i, be, cnt: (i, 0))
    return pl.pallas_call(
        _experts_kernel,
        grid_spec=pltpu.PrefetchScalarGridSpec(
            num_scalar_prefetch=2,
            grid=(n_slots // MOE_BLK,),
            in_specs=[blk, wspec(w_gate), wspec(w_up), wspec(w_down)],
            out_specs=blk,
            scratch_shapes=[pltpu.VMEM((d, f), MXU_DTYPE), pltpu.VMEM((d, f), MXU_DTYPE),
                            pltpu.VMEM((f, d), MXU_DTYPE)]),
        out_shape=jax.ShapeDtypeStruct(xs.shape, F32),
        compiler_params=_cparams(("arbitrary",)),
        name="experts",
    )(blk_exp, blk_cnt, xs, w_gate, w_up, w_down)


def _combined_tile(dest_ref, y_ref, gate_ref, x_ref, mod_ref, lng_ref, lnb_ref, ybuf, sems, *, n_tok, nt, n_steps):
    step = pl.program_id(0) * nt + pl.program_id(1)
    slot = step % 2

    def row_copy(buf, r, c, src_slot):
        return pltpu.make_async_copy(_row_tile(y_ref, src_slot), _row_tile(ybuf.at[buf, c], r), sems.at[buf])

    def issue(tile, buf):
        base = tile * TM

        def body(r, carry):
            for c in range(TOP_K):
                row_copy(buf, r, c, dest_ref[c * n_tok + base + r]).start(priority=c)
            return carry

        lax.fori_loop(0, TM, body, 0, unroll=8)

    def drain(buf):
        def body(r, carry):
            for c in range(TOP_K):
                row_copy(buf, r, c, 0).wait()
            return carry

        lax.fori_loop(0, TM, body, 0, unroll=8)

    pl.when(step == 0)(lambda: issue(0, 0))
    drain(slot)
    mod = mod_ref[0, 0]
    gate = gate_ref[...]
    y = (gate[:, 0:1] * _load_row_tiles(ybuf.at[slot, 0], TM)
         + gate[:, 1:2] * _load_row_tiles(ybuf.at[slot, 1], TM))
    nxt_base = jnp.minimum(step + 1, n_steps - 1) * TM
    for r in range(TM):
        for c in range(TOP_K):
            row_copy(1 - slot, r, c, dest_ref[c * n_tok + nxt_base + r]).start(priority=c)
    out = _layer_norm_rows(DEEPNORM_ALPHA * x_ref[0] + mod[5:6] * y, lng_ref[...], lnb_ref[...])
    return out, lambda: pl.when(step == n_steps - 1)(lambda: drain(1 - slot))


def _combine_kernel(dest_ref, y_ref, gate_ref, x_ref, mod_ref, lng_ref, lnb_ref, o_ref, ybuf, sems, **kw):
    o_ref[0], finish = _combined_tile(dest_ref, y_ref, gate_ref, x_ref, mod_ref, lng_ref, lnb_ref, ybuf, sems, **kw)
    finish()


def _combine_inproj1_kernel(dest_ref, y_ref, gate_ref, x_ref, mod_ref, lng_ref, lnb_ref,
                            mod1_ref, w_ref, qg_ref, kg_ref, cos_ref, sin_ref,
                            o_ref, q_ref, k_ref, v_ref, ybuf, sems, **kw):
    x, finish = _combined_tile(dest_ref, y_ref, gate_ref, x_ref, mod_ref, lng_ref, lnb_ref, ybuf, sems, **kw)
    o_ref[0] = x
    _qkv_project(x, mod1_ref[0, 0], w_ref, qg_ref, kg_ref, cos_ref, sin_ref, q_ref, k_ref, v_ref)
    finish()


def combine_call(dest, y_slots, gate_cols, x1, modsel, ln_g, ln_b, ctx_tiles, proj=None):
    bsz, tt, d = x1.shape
    nt = tt // TM
    tok = lambda w: pl.BlockSpec((1, TM, w), lambda b, j, dest: (b, j, 0))
    const = lambda a: pl.BlockSpec(a.shape, lambda b, j, dest: (0, 0))
    modspec = pl.BlockSpec((1, 1, 6, d), lambda b, j, dest: (b, (j >= ctx_tiles).astype(jnp.int32), 0, 0))
    in_specs = [pl.BlockSpec(memory_space=pl.ANY),
                pl.BlockSpec((TM, TOP_K), lambda b, j, dest: (b * nt + j, 0)),
                tok(d), modspec, const(ln_g), const(ln_b)]
    args = [dest, y_slots, gate_cols, x1, modsel, ln_g, ln_b]
    out_specs = [tok(d)]
    out_shape = [jax.ShapeDtypeStruct((bsz, tt, d), F32)]
    body = _combine_kernel
    if proj is not None:
        modsel1, w_in, q_g, k_g, cos2, sin2 = proj
        rope = pl.BlockSpec((TM, A_HEAD_DIM), lambda b, j, dest: (j, 0))
        in_specs += [modspec, const(w_in), const(q_g), const(k_g), rope, rope]
        args += [modsel1, w_in, q_g, k_g, cos2, sin2]
        widths = (A_Q_W, A_KV_W, A_KV_W)
        out_specs += [tok(w) for w in widths]
        out_shape += [jax.ShapeDtypeStruct((bsz, tt, w), MXU_DTYPE) for w in widths]
        body = _combine_inproj1_kernel
    res = pl.pallas_call(
        functools.partial(body, n_tok=bsz * tt, nt=nt, n_steps=bsz * nt),
        grid_spec=pltpu.PrefetchScalarGridSpec(
            num_scalar_prefetch=1,
            grid=(bsz, nt),
            in_specs=in_specs,
            out_specs=out_specs,
            scratch_shapes=[pltpu.VMEM((2, TOP_K, TM * ROW_CHUNKS, LANES), F32), pltpu.SemaphoreType.DMA((2,))]),
        out_shape=out_shape,
        compiler_params=_cparams(("arbitrary", "arbitrary")),
        name="combine" if proj is None else "combine_inproj1",
    )(*args)
    return res[0] if proj is None else res


def moe_block(x1, h2t, e_idx, gates, modsel, ln_g, ln_b, w_gate, w_up, w_down, layer, ctx_tiles, proj=None):
    bsz, tt, d = x1.shape
    n = bsz * tt

    n_asg = TOP_K * n
    flat_e = e_idx.reshape(n_asg)
    onehot = (flat_e[:, None] == jnp.arange(N_EXPERTS, dtype=jnp.int32)[None, :]).astype(jnp.int32)
    csum = jnp.cumsum(onehot, axis=0)
    rank = jnp.sum(onehot * csum, axis=1) - 1
    counts = csum[-1]
    padded = (counts + MOE_BLK - 1) // MOE_BLK * MOE_BLK
    pend = jnp.cumsum(padded)
    pstart = pend - padded
    dest = (pstart[flat_e] + rank).astype(jnp.int32)
    n_blocks = -(-n_asg // MOE_BLK) + N_EXPERTS
    blk_start = jnp.arange(n_blocks, dtype=jnp.int32) * MOE_BLK
    blk_exp = jnp.sum((pend[None, :] <= blk_start[:, None]).astype(jnp.int32), axis=1)
    blk_exp = jnp.minimum(blk_exp, N_EXPERTS - 1)
    blk_cnt = jnp.clip(counts[blk_exp] - (blk_start - pstart[blk_exp]), 0, MOE_BLK).astype(jnp.int32)

    part = jnp.where(counts % MOE_BLK != 0, pend // MOE_BLK - 1, -1)
    tail = pend[-1] // MOE_BLK + jnp.arange(N_EXPERTS, dtype=jnp.int32)
    tail = jnp.where(tail < n_blocks, tail, -1)
    zero_blocks = jnp.concatenate([part, tail]).astype(jnp.int32)

    xs = dispatch_call(dest, zero_blocks, h2t, n_blocks * MOE_BLK)
    y_slots = experts_call(blk_exp, blk_cnt, xs, w_gate, w_up, w_down, layer)
    return combine_call(dest, y_slots, gates.T, x1, modsel, ln_g, ln_b, ctx_tiles, proj)


def _qkv_project(x, mod, w_ref, qg_ref, kg_ref, cos_ref, sin_ref, q_ref, k_ref, v_ref):
    hb = (x * (1.0 + mod[1:2]) + mod[0:1]).astype(MXU_DTYPE)
    cos = cos_ref[...]
    sin = sin_ref[...]

    def norm_rope(t, g):
        t = t * lax.rsqrt(jnp.mean(t * t, axis=-1, keepdims=True) + RMS_EPS) * g
        return t * cos + pltpu.roll(t, A_HEAD_DIM // 2, 1) * sin

    qkv = jnp.dot(hb, w_ref[...], preferred_element_type=F32)
    for j in range(A_HEADS):
        sl = slice(j * A_HEAD_DIM, (j + 1) * A_HEAD_DIM)
        q_ref[0, :, sl] = (norm_rope(qkv[:, sl], qg_ref[...]) * ATT_Q_SCALE).astype(q_ref.dtype)
    for j in range(A_KV_HEADS):
        sl = slice(j * A_HEAD_DIM, (j + 1) * A_HEAD_DIM)
        t = qkv[:, A_Q_W + j * A_HEAD_DIM:A_Q_W + (j + 1) * A_HEAD_DIM]
        k_ref[0, :, sl] = norm_rope(t, kg_ref[...]).astype(k_ref.dtype)
    v_ref[0] = qkv[:, A_Q_W + A_KV_W:].astype(v_ref.dtype)


def _attn_kernel(q_ref, qn_ref, k_ref, vt_ref, o_ref, s_buf):
    rep = A_HEADS // A_KV_HEADS
    n_kv = vt_ref.shape[2]
    assert n_kv % 2 == 0

    def transposed(ref):
        return [ref[0, :, r * A_HEAD_DIM:(r + 1) * A_HEAD_DIM].astype(F32).T.astype(MXU_DTYPE) for r in range(rep)]

    qts = transposed(q_ref)

    def scores(j, slot, qt=qts):
        kb = k_ref[0, pl.ds(pl.multiple_of(j * ATT_TK, ATT_TK), ATT_TK), :]
        for r in range(rep):
            s_buf[slot, r] = jnp.dot(kb, qt[r], preferred_element_type=F32)

    def consume(j, slot, stats):
        vt = vt_ref[0, 0, j]
        new = []
        for r in range(rep):
            m, l, acc = stats[r]
            s = s_buf[slot, r]
            m_new = jnp.maximum(m, jnp.max(s, axis=0, keepdims=True))
            p = jnp.exp2(s - m_new)
            alpha = jnp.exp2(m - m_new)
            l = alpha * l + jnp.sum(p, axis=0, keepdims=True)
            acc = alpha * acc + jnp.dot(vt, p.astype(vt.dtype), preferred_element_type=F32)
            new.append((m_new, l, acc))
        return tuple(new)

    def pair(i, stats):
        j = 2 * i
        scores(j + 1, 1)
        stats = consume(j, 0, stats)
        scores(j + 2, 0)
        return consume(j + 1, 1, stats)

    stats = tuple((jnp.full((1, ATT_TQ), -jnp.inf, F32), jnp.zeros((1, ATT_TQ), F32),
                   jnp.zeros((A_HEAD_DIM, ATT_TQ), F32)) for _ in range(rep))
    pl.when(pl.program_id(2) == 0)(lambda: scores(0, 0))
    stats = lax.fori_loop(0, n_kv // 2 - 1, pair, stats)
    scores(n_kv - 1, 1)
    stats = consume(n_kv - 2, 0, stats)
    scores(0, 0, transposed(qn_ref))
    final = consume(n_kv - 1, 1, stats)
    for r in range(rep):
        _, l, acc = final[r]
        o_ref[0, :, r * A_HEAD_DIM:(r + 1) * A_HEAD_DIM] = (acc / l).T.astype(o_ref.dtype)


def attn_call(q, k, v, n_ctx):
    bsz, ta, _ = q.shape
    t_lat = ta - n_ctx
    gw = (A_HEADS // A_KV_HEADS) * A_HEAD_DIM
    q_off = n_ctx // ATT_TQ
    n_kv = ta // ATT_TK
    vt = v.reshape(bsz, n_kv, ATT_TK, A_KV_HEADS, A_HEAD_DIM).transpose(0, 3, 1, 4, 2)
    nq = t_lat // ATT_TQ
    return pl.pallas_call(
        _attn_kernel,
        grid=(bsz, A_KV_HEADS, nq),
        in_specs=[pl.BlockSpec((1, ATT_TQ, gw), lambda b, g, i: (b, i + q_off, g)),
                  pl.BlockSpec((1, ATT_TQ, gw), lambda b, g, i: (b, jnp.minimum(i + 1, nq - 1) + q_off, g)),
                  pl.BlockSpec((1, ta, A_HEAD_DIM), lambda b, g, i: (b, 0, g)),
                  pl.BlockSpec((1, 1, n_kv, A_HEAD_DIM, ATT_TK), lambda b, g, i: (b, g, 0, 0, 0))],
        out_specs=pl.BlockSpec((1, ATT_TQ, gw), lambda b, g, i: (b, i, g)),
        out_shape=jax.ShapeDtypeStruct((bsz, t_lat, A_Q_W), MXU_DTYPE),
        scratch_shapes=[pltpu.VMEM((2, A_HEADS // A_KV_HEADS, ATT_TK, ATT_TQ), F32)],
        compiler_params=_cparams(("arbitrary", "arbitrary", "arbitrary")),
        name="attention",
    )(q, q, k, vt)


def _outproj1_kernel(a_ref, x_ref, mod_ref, w_ref, lng_ref, lnb_ref, rw_ref, rb_ref, x1_ref, h2_ref, e_ref, g_ref):
    y = jnp.dot(a_ref[0], w_ref[...], preferred_element_type=F32)
    _finish_sublayer(x_ref[0], y, mod_ref[0, 0], lng_ref, lnb_ref, rw_ref, rb_ref, x1_ref, h2_ref, e_ref, g_ref)


def outproj1_call(att, xa, modsel, w_out, ln_g, ln_b, router_wt, router_b, ctx_tiles):
    bsz, t_lat, _ = att.shape
    d = xa.shape[-1]
    const = lambda a: pl.BlockSpec(a.shape, lambda b, j: (0, 0))
    out_specs, out_shape = _sublayer_out(bsz, t_lat, d)
    return pl.pallas_call(
        _outproj1_kernel,
        grid=(bsz, t_lat // TM),
        in_specs=[pl.BlockSpec((1, TM, att.shape[-1]), lambda b, j: (b, j, 0)),
                  pl.BlockSpec((1, TM, d), lambda b, j: (b, j + ctx_tiles, 0)),
                  pl.BlockSpec((1, 1, 6, d), lambda b, j: (b, 1, 0, 0)),
                  const(w_out), const(ln_g), const(ln_b), const(router_wt), const(router_b)],
        out_specs=out_specs,
        out_shape=out_shape,
        compiler_params=_cparams(("arbitrary", "arbitrary")),
        name="outproj1",
    )(att, xa, modsel, w_out, ln_g, ln_b, router_wt, router_b)


def _rope_tables(n_ctx, n_lat):
    rows = n_lat // GRID_W
    row = jnp.repeat(jnp.arange(rows), GRID_W).astype(F32)
    col = jnp.tile(jnp.arange(GRID_W), rows).astype(F32)
    n_freq = A_HEAD_DIM // 4
    inv = ROPE_THETA ** (-jnp.arange(n_freq, dtype=F32) / n_freq)
    ang = jnp.concatenate([row[:, None] * inv, col[:, None] * inv], -1)
    cos, sin = jnp.cos(ang), jnp.sin(ang)
    cos2 = jnp.concatenate([cos, cos], -1)
    sin2 = jnp.concatenate([-sin, sin], -1)
    cos2 = jnp.concatenate([jnp.ones((n_ctx, A_HEAD_DIM), F32), cos2], 0)
    sin2 = jnp.concatenate([jnp.zeros((n_ctx, A_HEAD_DIM), F32), sin2], 0)
    return cos2, sin2


def _lane_row(parts, width=GATE_LANES):
    row = jnp.concatenate([p.reshape(-1).astype(F32) for p in parts])
    return jnp.pad(row, (0, width - row.shape[0])).reshape(1, width)


def kernel(x, c, ctx, c_ctx, ada_w, ada_b, ln_g, ln_b, ab_w_in, ab_w_out, ml_ig_b, ml_fg_b, ml_norm_g,
           ssm_conv_w, ssm_conv_b, ssm_dt_b, ssm_a_log, ssm_d, ssm_norm_g, at_w_in, at_w_out, at_q_g, at_k_g,
           router_w, router_b, moe_w_gate, moe_w_up, moe_w_down):
    bsz, n_lat, d = x.shape
    n_ctx = ctx.shape[1]
    assert d == D_MODEL and bsz + 1 <= SUBLANES
    assert n_ctx % TM == 0 and n_lat % TM == 0 and n_lat % GRID_W == 0
    assert (n_ctx + n_lat) % ATT_TK == 0 and n_ctx % ATT_TQ == 0 and n_lat % ATT_TQ == 0
    assert (bsz * (n_ctx + n_lat)) % DISPATCH_ROWS == 0 and (bsz * n_lat) % DISPATCH_ROWS == 0
    ctx_tiles = n_ctx // TM
    ctx_chunks = n_ctx // CHUNK

    crows = jnp.zeros((SUBLANES, d), F32).at[:bsz].set(c).at[bsz].set(c_ctx)
    mods = ada_call(crows, ada_w, ada_b)

    def mod_table(i):
        lat = mods[i, :bsz].reshape(bsz, 1, 6, d)
        cx = jnp.broadcast_to(mods[i, bsz].reshape(1, 1, 6, d), (bsz, 1, 6, d))
        return jnp.concatenate([cx, lat], axis=1)

    router_wt = jnp.pad(router_w, ((0, 0), (0, LANES - N_EXPERTS)))
    router_bc = router_b.reshape(N_EXPERTS, 1)

    modsel = mod_table(0)
    w_in = ab_w_in[0]
    s_q, s_k, s_v, s_o, s_ig, s_fg, s_z, s_xbc = (int(v) for v in
        (0, 512, 1024, 2048, 3072, 3072 + 8, 3072 + 16, 3072 + 16 + 1024))
    s_dt = s_xbc + S_CONV_CH
    w_big = jnp.concatenate([w_in[:, :s_ig], w_in[:, s_z:s_dt]], axis=1).astype(MXU_DTYPE)
    w_small = jnp.concatenate([w_in[:, s_ig:s_z], w_in[:, s_dt:]], axis=1)
    w_small = jnp.pad(w_small, ((0, 0), (0, GATE_LANES - w_small.shape[1])))
    q, k, v, o, z, xbc, gates = inproj0_call(ctx, x, modsel, w_big, w_small, ctx_tiles)
    xbc_act = conv_call(xbc, ssm_conv_w[0], ssm_conv_b[0], ctx_tiles)

    grow = jnp.swapaxes(gates[:, :, :DT_OFF], 1, 2)
    dtrow = jnp.swapaxes(gates[:, :, DT_OFF:DT_OFF + N_DIR * S_HEADS], 1, 2)
    gate_b_row = _lane_row([ml_ig_b[0], ml_fg_b[0], ssm_dt_b[0]])
    gate_b_col = jnp.concatenate([ml_ig_b[0].reshape(-1), ml_fg_b[0].reshape(-1)]).reshape(-1, 1)
    a_neg = -jnp.exp(ssm_a_log[0].astype(F32)).reshape(-1)
    an_row = _lane_row([jnp.zeros((DT_OFF,), F32), a_neg])
    an_col = a_neg.reshape(-1, 1)
    dtb_col = ssm_dt_b[0].reshape(-1, 1)

    dskip = jnp.repeat(ssm_d[0].astype(F32), S_HEAD_DIM).reshape(1, S_WIDTH)
    m_norm_g = ml_norm_g[0].reshape(1, -1)
    s_norm_g = ssm_norm_g[0].reshape(1, -1)
    assert N_DIR == 2
    hm = None
    hs = None
    for dd in range(N_DIR):
        hm = mlstm_call(q, k, v, gates, grow, gate_b_row, gate_b_col, hm, o, m_norm_g, rev=dd == 1, d=dd,
                        ctx_chunks=ctx_chunks)
        hs = ssd_call(xbc_act, gates, dtrow, gate_b_row, dtb_col, an_row, an_col, hs, z, dskip, s_norm_g,
                      rev=dd == 1, d=dd, ctx_chunks=ctx_chunks)

    x1, h2t, e_idx, gates = outproj0_call(hm, hs, ctx, x, modsel, ab_w_out[0].astype(MXU_DTYPE),
                                          ln_g[0, 0].reshape(1, d), ln_b[0, 0].reshape(1, d), router_wt, router_bc,
                                          ctx_tiles)
    modsel1 = mod_table(1)
    cos2, sin2 = _rope_tables(n_ctx, n_lat)
    proj = (modsel1, at_w_in[0].astype(MXU_DTYPE), at_q_g[0].reshape(1, -1), at_k_g[0].reshape(1, -1), cos2, sin2)
    xa, qa, ka, va = moe_block(x1, h2t, e_idx, gates, modsel, ln_g[0, 1].reshape(1, d), ln_b[0, 1].reshape(1, d),
                               moe_w_gate, moe_w_up, moe_w_down, 0, ctx_tiles, proj)
    modsel = modsel1
    att = attn_call(qa, ka, va, n_ctx)
    x1, h2t, e_idx, gates = outproj1_call(att, xa, modsel, at_w_out[0].astype(MXU_DTYPE), ln_g[1, 0].reshape(1, d),
                                          ln_b[1, 0].reshape(1, d), router_wt, router_bc, ctx_tiles)
    return moe_block(x1, h2t, e_idx, gates, modsel, ln_g[1, 1].reshape(1, d), ln_b[1, 1].reshape(1, d),
                     moe_w_gate, moe_w_up, moe_w_down, 1, 0)
```

```python
import functools
import math

import jax
import jax.numpy as jnp
from jax import lax
from jax.experimental import pallas as pl
from jax.experimental.pallas import tpu as pltpu

F32 = jnp.float32
MXU_DTYPE = jnp.bfloat16
HIGHEST = lax.Precision.HIGHEST

D_MODEL = 1024
DEPTH = 2
GRID_W = 64
CHUNK = 128
M_HEADS = 4
M_QK_DIM = D_MODEL // 8
M_V_DIM = D_MODEL // 4
M_WIDTH = M_HEADS * M_V_DIM
S_HEADS = 16
S_HEAD_DIM = D_MODEL // 16
S_GROUPS = 2
S_HEADS_PER_GROUP = S_HEADS // S_GROUPS
S_STATE = 128
S_WIDTH = S_HEADS * S_HEAD_DIM
S_CONV_CH = S_WIDTH + 2 * S_GROUPS * S_STATE
CONV_K = 4
N_DIR = 2
A_HEADS = 8
A_KV_HEADS = 2
A_HEAD_DIM = D_MODEL // A_HEADS
A_Q_W = A_HEADS * A_HEAD_DIM
A_KV_W = A_KV_HEADS * A_HEAD_DIM
ROPE_THETA = 10000.0
N_EXPERTS = 16
N_EXPERT_GROUPS = 4
EXPERTS_PER_GROUP = N_EXPERTS // N_EXPERT_GROUPS
TOP_K = 2
DEEPNORM_ALPHA = (2 * DEPTH) ** 0.25
LN_EPS = 1e-5
RMS_EPS = 1e-6

LANES = 128
SUBLANES = 8
TM = 256
MOE_BLK = 512
DISPATCH_ROWS = 512
ROW_CHUNKS = D_MODEL // LANES
assert ROW_CHUNKS == SUBLANES
ATT_TQ = 256
ATT_TK = 1408
VMEM_LIMIT = 56 * 1024 * 1024
GATE_LANES = 128
IG_OFF, FG_OFF, DT_OFF = 0, N_DIR * M_HEADS, 2 * N_DIR * M_HEADS
ATT_Q_SCALE = A_HEAD_DIM ** -0.5 * math.log2(math.e)


def _cparams(sem):
    return pltpu.CompilerParams(dimension_semantics=sem, vmem_limit_bytes=VMEM_LIMIT)


def _silu(x):
    return x / (1.0 + jnp.exp(-x))


def _sigmoid(x):
    return 1.0 / (1.0 + jnp.exp(-x))


def _softplus(x):
    return jnp.maximum(x, 0.0) + jnp.log(1.0 + jnp.exp(-jnp.abs(x)))


def _log_sigmoid(x):
    return jnp.minimum(x, 0.0) - jnp.log(1.0 + jnp.exp(-jnp.abs(x)))


def _layer_norm_rows(x, g, b):
    mu = jnp.mean(x, axis=-1, keepdims=True)
    xc = x - mu
    var = jnp.mean(xc * xc, axis=-1, keepdims=True)
    return xc * lax.rsqrt(var + LN_EPS) * g + b


def _mm(a, b):
    return jnp.dot(a.astype(MXU_DTYPE), b.astype(MXU_DTYPE), preferred_element_type=F32)


def _mm_nt(a, b):
    return lax.dot_general(a.astype(MXU_DTYPE), b.astype(MXU_DTYPE), (((1,), (1,)), ((), ())),
                           preferred_element_type=F32)


def _mm_tn(a, b):
    return lax.dot_general(a.astype(MXU_DTYPE), b.astype(MXU_DTYPE), (((0,), (0,)), ((), ())),
                           preferred_element_type=F32)


def _ada_kernel(c_ref, w_ref, b_ref, o_ref):
    s = _silu(c_ref[...])
    o_ref[0] = jnp.dot(s, w_ref[0], precision=HIGHEST, preferred_element_type=F32) + b_ref[0]


def ada_call(crows, ada_w, ada_b):
    depth, d, n6 = ada_w.shape
    tn = 1536
    return pl.pallas_call(
        _ada_kernel,
        grid=(depth, n6 // tn),
        in_specs=[pl.BlockSpec((SUBLANES, d), lambda i, j: (0, 0)),
                  pl.BlockSpec((1, d, tn), lambda i, j: (i, 0, j)),
                  pl.BlockSpec((1, 1, tn), lambda i, j: (i, 0, j))],
        out_specs=pl.BlockSpec((1, SUBLANES, tn), lambda i, j: (i, 0, j)),
        out_shape=jax.ShapeDtypeStruct((depth, SUBLANES, n6), F32),
        compiler_params=_cparams(("arbitrary", "arbitrary")),
        name="ada",
    )(crows, ada_w, ada_b.reshape(depth, 1, n6))


def _token_tile(ctx_ref, x_ref, ctx_tiles):
    return jnp.where(pl.program_id(1) < ctx_tiles, ctx_ref[0], x_ref[0])


def _token_specs(d, ctx_tiles):
    return [pl.BlockSpec((1, TM, d), lambda b, j: (b, jnp.minimum(j, ctx_tiles - 1), 0)),
            pl.BlockSpec((1, TM, d), lambda b, j: (b, jnp.maximum(j - ctx_tiles, 0), 0))]


def _inproj0_kernel(ctx_ref, x_ref, mod_ref, wb_ref, ws_ref, q_ref, k_ref, v_ref, o_ref, z_ref, xbc_ref, g_ref, *,
                    ctx_tiles):
    x = _token_tile(ctx_ref, x_ref, ctx_tiles)
    mod = mod_ref[0, 0]
    h = x * (1.0 + mod[1:2]) + mod[0:1]
    hb = h.astype(MXU_DTYPE)
    qk = M_HEADS * M_QK_DIM
    c0 = 0
    q_ref[0] = (jnp.dot(hb, wb_ref[:, c0:c0 + qk], preferred_element_type=F32)
                * (M_QK_DIM ** -0.5)).astype(q_ref.dtype)
    c0 += qk
    k_ref[0] = jnp.dot(hb, wb_ref[:, c0:c0 + qk], preferred_element_type=F32).astype(k_ref.dtype)
    c0 += qk
    v_ref[0] = jnp.dot(hb, wb_ref[:, c0:c0 + M_WIDTH], preferred_element_type=F32).astype(v_ref.dtype)
    c0 += M_WIDTH
    o_ref[0] = jnp.dot(hb, wb_ref[:, c0:c0 + M_WIDTH], preferred_element_type=F32)
    c0 += M_WIDTH
    z_ref[0] = jnp.dot(hb, wb_ref[:, c0:c0 + S_WIDTH], preferred_element_type=F32)
    c0 += S_WIDTH
    xbc_ref[0] = jnp.dot(hb, wb_ref[:, c0:c0 + S_CONV_CH], preferred_element_type=F32)
    h_hi, h_lo = _split_hi_lo(h)
    w_hi, w_lo = _split_hi_lo(ws_ref[...])
    g_ref[0] = (jnp.dot(h_hi, w_hi, preferred_element_type=F32) + jnp.dot(h_lo, w_hi, preferred_element_type=F32)
                + jnp.dot(h_hi, w_lo, preferred_element_type=F32))


def inproj0_call(ctx, x, modsel, w_big, w_small, ctx_tiles):
    bsz, _, d = x.shape
    ta = ctx.shape[1] + x.shape[1]
    nt = ta // TM
    qk = M_HEADS * M_QK_DIM
    widths = (qk, qk, M_WIDTH, M_WIDTH, S_WIDTH, S_CONV_CH, GATE_LANES)
    dtypes = (MXU_DTYPE, MXU_DTYPE, MXU_DTYPE, F32, F32, F32, F32)
    tok = lambda w: pl.BlockSpec((1, TM, w), lambda b, j: (b, j, 0))
    return pl.pallas_call(
        functools.partial(_inproj0_kernel, ctx_tiles=ctx_tiles),
        grid=(bsz, nt),
        in_specs=_token_specs(d, ctx_tiles) + [
                  pl.BlockSpec((1, 1, 6, d), lambda b, j: (b, (j >= ctx_tiles).astype(jnp.int32), 0, 0)),
                  pl.BlockSpec(w_big.shape, lambda b, j: (0, 0)),
                  pl.BlockSpec(w_small.shape, lambda b, j: (0, 0))],
        out_specs=[tok(w) for w in widths],
        out_shape=[jax.ShapeDtypeStruct((bsz, ta, w), dt) for w, dt in zip(widths, dtypes)],
        compiler_params=_cparams(("arbitrary", "arbitrary")),
        name="inproj0",
    )(ctx, x, modsel, w_big, w_small)


def _conv_kernel(cur_ref, prev_ref, next_ref, w_ref, b_ref, o_ref, *, ctx_tiles, n_tiles):
    j = pl.program_id(1)
    has_prev = jnp.logical_and(j != 0, j != ctx_tiles)
    has_next = jnp.logical_and(j != ctx_tiles - 1, j != n_tiles - 1)
    prev = jnp.where(has_prev, prev_ref[0], 0.0)
    nxt = jnp.where(has_next, next_ref[0], 0.0)
    ext = jnp.concatenate([prev, cur_ref[0], nxt], axis=0)
    n = TM + 2 * SUBLANES
    w = w_ref[...]
    lo, hi = SUBLANES, SUBLANES + TM
    acc = ext[lo:hi] * w[2:3]
    acc = acc + pltpu.roll(ext, 2, 0)[lo:hi] * w[0:1]
    acc = acc + pltpu.roll(ext, 1, 0)[lo:hi] * w[1:2]
    acc = acc + pltpu.roll(ext, n - 1, 0)[lo:hi] * w[3:4]
    o_ref[0] = _silu(acc + b_ref[...])


def conv_call(xbc, conv_w, conv_b, ctx_tiles):
    bsz, ta, ch = xbc.shape
    nt = ta // TM
    r = TM // SUBLANES
    last = ta // SUBLANES - 1
    return pl.pallas_call(
        functools.partial(_conv_kernel, ctx_tiles=ctx_tiles, n_tiles=nt),
        grid=(bsz, nt),
        in_specs=[pl.BlockSpec((1, TM, ch), lambda b, j: (b, j, 0)),
                  pl.BlockSpec((1, SUBLANES, ch), lambda b, j: (b, jnp.maximum(j * r - 1, 0), 0)),
                  pl.BlockSpec((1, SUBLANES, ch), lambda b, j: (b, jnp.minimum((j + 1) * r, last), 0)),
                  pl.BlockSpec((CONV_K, ch), lambda b, j: (0, 0)),
                  pl.BlockSpec((1, ch), lambda b, j: (0, 0))],
        out_specs=pl.BlockSpec((1, TM, ch), lambda b, j: (b, j, 0)),
        out_shape=jax.ShapeDtypeStruct((bsz, ta, ch), F32),
        compiler_params=_cparams(("arbitrary", "arbitrary")),
        name="conv",
    )(xbc, xbc, xbc, conv_w, conv_b.reshape(1, ch))


def _chunk_order(i, rev, ctx_chunks, n_chunks):
    if not rev:
        return i
    return jnp.where(i < ctx_chunks, ctx_chunks - 1 - i, n_chunks - 1 - (i - ctx_chunks))


def _scan_masks(rev):
    r = lax.broadcasted_iota(jnp.int32, (CHUNK, CHUNK), 0)
    c = lax.broadcasted_iota(jnp.int32, (CHUNK, CHUNK), 1)
    mask = (c >= r) if rev else (c <= r)
    mask_t = (r >= c) if rev else (r <= c)
    return mask, mask.astype(F32).astype(MXU_DTYPE), mask_t.astype(F32).astype(MXU_DTYPE)


def _split3(x):
    hi = x.astype(MXU_DTYPE)
    r = x - hi.astype(F32)
    mid = r.astype(MXU_DTYPE)
    lo = (r - mid.astype(F32)).astype(MXU_DTYPE)
    return hi, mid, lo


def _prefix_cols(tri, x):
    return sum(jnp.dot(tri, p, preferred_element_type=F32) for p in _split3(x))


def _prefix_rows(x, tri):
    return sum(jnp.dot(p, tri, preferred_element_type=F32) for p in _split3(x))


def _per_batch(body, kinds):
    def kern(*refs, **kw):
        assert len(refs) == len(kinds)
        for bi in range(refs[0].shape[0]):
            sub = [r.at[pl.ds(bi, 1)] if f == 'b' else r.at[bi] if f == 's' else r for r, f in zip(refs, kinds)]
            body(*sub, **kw)
    return kern


def _mlstm_kernel(*refs, rev, d, add_prev):
    if add_prev:
        (q_ref, k_ref, v_ref, gc_ref, gr_ref, brow_ref, bcol_ref, prev_ref, og_ref, mg_ref,
         o_ref, ct_s, n_s, m_s) = refs
    else:
        q_ref, k_ref, v_ref, gc_ref, gr_ref, brow_ref, bcol_ref, o_ref, ct_s, n_s, m_s = refs
        prev_ref = None

    @pl.when(pl.program_id(0) == 0)
    def _():
        ct_s[...] = jnp.zeros_like(ct_s)
        n_s[...] = jnp.zeros_like(n_s)
        m_s[...] = jnp.zeros_like(m_s)

    mask, mask_f, mask_tf = _scan_masks(rev)
    end = 0 if rev else CHUNK - 1
    gcol = gc_ref[0] + brow_ref[...]
    grow = gr_ref[0] + bcol_ref[...]
    nh2 = N_DIR * M_HEADS
    lf_col = _log_sigmoid(gcol)
    lf_row = _log_sigmoid(grow[FG_OFF:FG_OFF + nh2])
    b_col_all = _prefix_cols(mask_f, lf_col)
    b_row_all = _prefix_rows(lf_row, mask_tf)

    for j in range(M_HEADS):
        ci = d * M_HEADS + j
        bcol = b_col_all[:, FG_OFF + ci:FG_OFF + ci + 1]
        brow = b_row_all[ci:ci + 1, :]
        igcol = gcol[:, IG_OFF + ci:IG_OFF + ci + 1]
        igrow = grow[IG_OFF + ci:IG_OFF + ci + 1, :]
        b_end = bcol[end:end + 1, :]
        m_prev = m_s[j][:, 0:1]
        n_prev = n_s[j]
        ct_prev = ct_s[j]
        q = q_ref[0, :, j * M_QK_DIM:(j + 1) * M_QK_DIM]
        k = k_ref[0, :, j * M_QK_DIM:(j + 1) * M_QK_DIM]
        v = v_ref[0, :, j * M_V_DIM:(j + 1) * M_V_DIM]
        qf = q.astype(F32)
        kf = k.astype(F32)
        vf = v.astype(F32)

        dmat = jnp.where(mask, bcol - brow + igrow, -jnp.inf)
        inter = bcol + m_prev
        m_t = jnp.maximum(inter, jnp.max(dmat, axis=1, keepdims=True))
        sc = _mm_nt(q, k) * jnp.exp(dmat - m_t)
        a_in = jnp.exp(inter - m_t)
        num = _mm(sc, v) + a_in * _mm(q, ct_prev)
        den = jnp.sum(sc, axis=1, keepdims=True) + a_in * jnp.sum(qf * n_prev, axis=1, keepdims=True)
        h = num / jnp.maximum(jnp.abs(den), jnp.exp(-m_t))
        sl = slice(j * M_V_DIM, (j + 1) * M_V_DIM)
        if add_prev:
            h = h + prev_ref[0, :, sl]
            mu = jnp.mean(h, axis=-1, keepdims=True)
            hc = h - mu
            var = jnp.mean(hc * hc, axis=-1, keepdims=True)
            h = hc * lax.rsqrt(var + LN_EPS) * mg_ref[:, sl] * _sigmoid(og_ref[0, :, sl])
        o_ref[0, :, sl] = h.astype(o_ref.dtype)

        g_col = b_end - bcol + igcol
        g_row = b_end - brow + igrow
        g_max = jnp.max(g_row, axis=1, keepdims=True)
        w_col = jnp.exp(g_col - g_max)
        d_ct = _mm_tn(k, vf * w_col)
        d_n = jnp.sum(kf * w_col, axis=0, keepdims=True)
        m_new = jnp.maximum(b_end + m_prev, g_max)
        a = jnp.exp(b_end + m_prev - m_new)
        s = jnp.exp(g_max - m_new)
        ct_s[j] = a * ct_prev + s * d_ct
        n_s[j] = a * n_prev + s * d_n
        m_s[j] = jnp.broadcast_to(m_new, (1, LANES))


def mlstm_call(q, k, v, gcol, grow, bias_row, bias_col, prev, o_gate, m_norm_g, *, rev, d, ctx_chunks):
    bsz, ta, _ = q.shape
    nc = ta // CHUNK
    order = lambda i: _chunk_order(i, rev, ctx_chunks, nc)
    tok = lambda w: pl.BlockSpec((bsz, CHUNK, w), lambda i: (0, order(i), 0))
    in_specs = [tok(q.shape[-1]), tok(k.shape[-1]), tok(v.shape[-1]), tok(GATE_LANES),
                pl.BlockSpec((bsz, grow.shape[1], CHUNK), lambda i: (0, 0, order(i))),
                pl.BlockSpec(bias_row.shape, lambda i: (0, 0)),
                pl.BlockSpec(bias_col.shape, lambda i: (0, 0))]
    args = [q, k, v, gcol, grow, bias_row, bias_col]
    kinds = "bbbbbcc"
    last = prev is not None
    if last:
        in_specs += [tok(M_WIDTH), tok(M_WIDTH), pl.BlockSpec(m_norm_g.shape, lambda i: (0, 0))]
        args += [prev, o_gate, m_norm_g]
        kinds += "bbc"
    kinds += "b" + "sss"
    return pl.pallas_call(
        functools.partial(_per_batch(_mlstm_kernel, kinds), rev=rev, d=d, add_prev=last),
        grid=(nc,),
        in_specs=in_specs,
        out_specs=tok(M_WIDTH),
        out_shape=jax.ShapeDtypeStruct((bsz, ta, M_WIDTH), MXU_DTYPE if last else F32),
        scratch_shapes=[pltpu.VMEM((bsz, M_HEADS, M_QK_DIM, M_V_DIM), F32),
                        pltpu.VMEM((bsz, M_HEADS, 1, M_QK_DIM), F32),
                        pltpu.VMEM((bsz, M_HEADS, 1, LANES), F32)],
        compiler_params=_cparams(("arbitrary",)),
        name="mlstm_rev" if rev else "mlstm_fwd",
    )(*args)


def _ssd_kernel(*refs, rev, d, add_prev):
    if add_prev:
        (x_ref, gc_ref, gr_ref, dtb_row_ref, dtb_col_ref, an_row_ref, an_col_ref, prev_ref, z_ref, dsk_ref, sg_ref,
         o_ref, ht_s) = refs
    else:
        x_ref, gc_ref, gr_ref, dtb_row_ref, dtb_col_ref, an_row_ref, an_col_ref, o_ref, ht_s = refs
        prev_ref = None

    @pl.when(pl.program_id(0) == 0)
    def _():
        ht_s[...] = jnp.zeros_like(ht_s)

    mask, mask_f, mask_tf = _scan_masks(rev)
    end = 0 if rev else CHUNK - 1
    lane = lax.broadcasted_iota(jnp.int32, (CHUNK, LANES), 1)
    first_half = lane < S_HEAD_DIM
    dt_col = _softplus(gc_ref[0] + dtb_row_ref[...])
    dt_row = _softplus(gr_ref[0] + dtb_col_ref[...])
    acs_col = _prefix_cols(mask_f, dt_col * an_row_ref[...])
    acs_row = _prefix_rows(dt_row * an_col_ref[...], mask_tf)
    a_end_row = acs_col[end:end + 1, :]
    e_cs = jnp.exp(acs_col)
    e_rem = jnp.exp(a_end_row - acs_col)
    e_end = jnp.exp(a_end_row)
    gw = S_HEADS_PER_GROUP * S_HEAD_DIM
    pairs = S_HEADS_PER_GROUP // 2

    def pick(arr, la):
        return jnp.where(first_half[:arr.shape[0]], arr[:, la:la + 1], arr[:, la + 1:la + 2])

    y_parts = []
    for g in range(S_GROUPS):
        bm = x_ref[0, :, S_WIDTH + g * S_STATE:S_WIDTH + (g + 1) * S_STATE]
        cm = x_ref[0, :, S_WIDTH + (S_GROUPS + g) * S_STATE:S_WIDTH + (S_GROUPS + g + 1) * S_STATE]
        cb = _mm_nt(cm, bm)
        ht_prev = ht_s[g]
        y_inter = _mm(cm, ht_prev)
        xw_parts = []
        decay_parts = []
        for p in range(pairs):
            h0 = g * S_HEADS_PER_GROUP + 2 * p
            la = DT_OFF + d * S_HEADS + h0
            ra = d * S_HEADS + h0
            lhs = []
            for u in range(2):
                seg = acs_col[:, la + u:la + u + 1] - acs_row[ra + u:ra + u + 1, :]
                dec = jnp.exp(jnp.where(mask, seg, -jnp.inf))
                lhs.append((cb * dec).astype(MXU_DTYPE))
            xs = x_ref[0, :, h0 * S_HEAD_DIM:(h0 + 2) * S_HEAD_DIM]
            xsd = xs * pick(dt_col, la)
            rhs = jnp.concatenate([jnp.where(first_half, xsd, 0.0), jnp.where(first_half, 0.0, xsd)],
                                  axis=0).astype(MXU_DTYPE)
            y = jnp.dot(jnp.concatenate(lhs, axis=1), rhs, preferred_element_type=F32)
            y = y + y_inter[:, p * LANES:(p + 1) * LANES] * pick(e_cs, la)
            sl = slice(h0 * S_HEAD_DIM, (h0 + 2) * S_HEAD_DIM)
            if add_prev:
                y = (y + prev_ref[0, :, sl] + dsk_ref[:, sl] * xs) * _silu(z_ref[0, :, sl])
                y_parts.append(y)
            else:
                o_ref[0, :, sl] = y
            xw_parts.append(xsd * pick(e_rem, la))
            decay_parts.append(pick(e_end, la))
        xw = jnp.concatenate(xw_parts, axis=1)
        decay = jnp.concatenate(decay_parts, axis=1)
        ht_s[g] = decay * ht_prev + _mm_tn(bm, xw)

    if add_prev:
        ys = jnp.concatenate(y_parts, axis=1)
        ys = ys * lax.rsqrt(jnp.mean(ys * ys, axis=-1, keepdims=True) + RMS_EPS) * sg_ref[...]
        o_ref[0] = ys.astype(o_ref.dtype)


def ssd_call(xbc_act, gcol, dtrow, dtb_row, dtb_col, an_row, an_col, prev, z, dskip, s_norm_g, *, rev, d,
             ctx_chunks):
    bsz, ta, ch = xbc_act.shape
    nc = ta // CHUNK
    order = lambda i: _chunk_order(i, rev, ctx_chunks, nc)
    tok = lambda w: pl.BlockSpec((bsz, CHUNK, w), lambda i: (0, order(i), 0))
    const = lambda a: pl.BlockSpec(a.shape, lambda i: (0, 0))
    in_specs = [tok(ch), tok(GATE_LANES),
                pl.BlockSpec((bsz, dtrow.shape[1], CHUNK), lambda i: (0, 0, order(i))),
                const(dtb_row), const(dtb_col), const(an_row), const(an_col)]
    args = [xbc_act, gcol, dtrow, dtb_row, dtb_col, an_row, an_col]
    kinds = "bbbcccc"
    last = prev is not None
    if last:
        in_specs += [tok(S_WIDTH), tok(S_WIDTH), const(dskip), const(s_norm_g)]
        args += [prev, z, dskip, s_norm_g]
        kinds += "bbcc"
    kinds += "b" + "s"
    return pl.pallas_call(
        functools.partial(_per_batch(_ssd_kernel, kinds), rev=rev, d=d, add_prev=last),
        grid=(nc,),
        in_specs=in_specs,
        out_specs=tok(S_WIDTH),
        out_shape=jax.ShapeDtypeStruct((bsz, ta, S_WIDTH), MXU_DTYPE if last else F32),
        scratch_shapes=[pltpu.VMEM((bsz, S_GROUPS, S_STATE, S_HEADS_PER_GROUP * S_HEAD_DIM), F32)],
        compiler_params=_cparams(("arbitrary",)),
        name="ssd_rev" if rev else "ssd_fwd",
    )(*args)


def _store_row_tiles(ref, val):
    for s in range(ROW_CHUNKS):
        ref[pl.ds(s, val.shape[0], stride=ROW_CHUNKS), :] = val[:, s * LANES:(s + 1) * LANES]


def _load_row_tiles(ref, rows):
    return jnp.concatenate([ref[pl.ds(s, rows, stride=ROW_CHUNKS), :] for s in range(ROW_CHUNKS)], axis=1)


def _finish_sublayer(x, y, mod, lng_ref, lnb_ref, rw_ref, rb_ref, x1_ref, h2_ref, e_ref, g_ref):
    x1 = _layer_norm_rows(DEEPNORM_ALPHA * x + mod[2:3] * y, lng_ref[...], lnb_ref[...])
    x1_ref[0] = x1
    h2 = x1 * (1.0 + mod[4:5]) + mod[3:4]
    _store_row_tiles(h2_ref, h2)
    e, g = _route(h2, rw_ref, rb_ref)
    e_ref[...] = e
    g_ref[...] = g


def _outproj0_kernel(ym_ref, ys_ref, ctx_ref, x_ref, mod_ref, w_ref, lng_ref, lnb_ref, rw_ref, rb_ref,
                     x1_ref, h2_ref, e_ref, g_ref, *, ctx_tiles):
    y = (jnp.dot(ym_ref[0], w_ref[:M_WIDTH, :], preferred_element_type=F32)
         + jnp.dot(ys_ref[0], w_ref[M_WIDTH:, :], preferred_element_type=F32))
    _finish_sublayer(_token_tile(ctx_ref, x_ref, ctx_tiles), y, mod_ref[0, 0], lng_ref, lnb_ref, rw_ref, rb_ref,
                     x1_ref, h2_ref, e_ref, g_ref)


def _sublayer_out(bsz, tt, d):
    nt = tt // TM
    n = bsz * tt
    specs = [pl.BlockSpec((1, TM, d), lambda b, j: (b, j, 0)),
             pl.BlockSpec((TM * ROW_CHUNKS, LANES), lambda b, j: (b * nt + j, 0)),
             pl.BlockSpec((TOP_K, TM), lambda b, j: (0, b * nt + j)),
             pl.BlockSpec((TOP_K, TM), lambda b, j: (0, b * nt + j))]
    shapes = [jax.ShapeDtypeStruct((bsz, tt, d), F32), jax.ShapeDtypeStruct((n * ROW_CHUNKS, LANES), F32),
              jax.ShapeDtypeStruct((TOP_K, n), jnp.int32), jax.ShapeDtypeStruct((TOP_K, n), F32)]
    return specs, shapes


def outproj0_call(ym, ys, ctx, x, modsel, w_out, ln_g, ln_b, router_wt, router_b, ctx_tiles):
    bsz, ta, _ = ym.shape
    d = x.shape[-1]
    nt = ta // TM
    tok = lambda w: pl.BlockSpec((1, TM, w), lambda b, j: (b, j, 0))
    const = lambda a: pl.BlockSpec(a.shape, lambda b, j: (0, 0))
    out_specs, out_shape = _sublayer_out(bsz, ta, d)
    return pl.pallas_call(
        functools.partial(_outproj0_kernel, ctx_tiles=ctx_tiles),
        grid=(bsz, nt),
        in_specs=[tok(M_WIDTH), tok(S_WIDTH)] + _token_specs(d, ctx_tiles) + [
                  pl.BlockSpec((1, 1, 6, d), lambda b, j: (b, (j >= ctx_tiles).astype(jnp.int32), 0, 0)),
                  const(w_out), const(ln_g), const(ln_b), const(router_wt), const(router_b)],
        out_specs=out_specs,
        out_shape=out_shape,
        compiler_params=_cparams(("arbitrary", "arbitrary")),
        name="outproj0",
    )(ym, ys, ctx, x, modsel, w_out, ln_g, ln_b, router_wt, router_b)


def _top2(vals, probs):
    v1, i1, p1 = vals[0], jnp.zeros_like(vals[0], dtype=jnp.int32), probs[0]
    for i in range(1, len(vals)):
        better = vals[i] > v1
        v1 = jnp.where(better, vals[i], v1)
        i1 = jnp.where(better, i, i1)
        p1 = jnp.where(better, probs[i], p1)
    v2 = jnp.full_like(vals[0], -jnp.inf)
    i2 = jnp.zeros_like(i1)
    p2 = jnp.zeros_like(p1)
    for i in range(len(vals)):
        better = jnp.logical_and(i1 != i, vals[i] > v2)
        v2 = jnp.where(better, vals[i], v2)
        i2 = jnp.where(better, i, i2)
        p2 = jnp.where(better, probs[i], p2)
    return v1, i1, p1, v2, i2, p2


def _split_hi_lo(x):
    hi = x.astype(MXU_DTYPE)
    return hi, (x - hi.astype(F32)).astype(MXU_DTYPE)


def _route(h, w_ref, b_ref):
    h_hi, h_lo = _split_hi_lo(h)
    w_hi, w_lo = _split_hi_lo(w_ref[...])
    lg = (jnp.dot(h_hi, w_hi, preferred_element_type=F32) + jnp.dot(h_lo, w_hi, preferred_element_type=F32)
          + jnp.dot(h_hi, w_lo, preferred_element_type=F32))
    logits = lg.T[:N_EXPERTS]
    mx = jnp.max(logits, axis=0, keepdims=True)
    ex = jnp.exp(logits - mx)
    probs = ex / jnp.sum(ex, axis=0, keepdims=True)
    sel = probs + b_ref[...]
    best = None
    for g in range(N_EXPERT_GROUPS):
        rows = range(g * EXPERTS_PER_GROUP, (g + 1) * EXPERTS_PER_GROUP)
        v1, i1, p1, v2, i2, p2 = _top2([sel[r:r + 1] for r in rows], [probs[r:r + 1] for r in rows])
        cand = (v1 + v2, i1 + g * EXPERTS_PER_GROUP, p1, i2 + g * EXPERTS_PER_GROUP, p2)
        if best is None:
            best = cand
        else:
            better = cand[0] > best[0]
            best = tuple(jnp.where(better, c, o) for c, o in zip(cand, best))
    _, e1, p1, e2, p2 = best
    tot = p1 + p2
    return jnp.concatenate([e1, e2], axis=0), jnp.concatenate([p1 / tot, p2 / tot], axis=0)


def _row_tile(ref, r):
    return ref.at[pl.ds(pl.multiple_of(r * ROW_CHUNKS, ROW_CHUNKS), ROW_CHUNKS)]


def _dispatch_kernel(dest_ref, zblk_ref, h_ref, xs_ref, zero_s, sem, zsem, *, n_tok):
    base = pl.program_id(0) * DISPATCH_ROWS
    blk_rows = MOE_BLK * ROW_CHUNKS

    @pl.when(pl.program_id(0) == 0)
    def _():
        zero_s[...] = jnp.zeros_like(zero_s)

        def zero_copy(t):
            start = pl.multiple_of(zblk_ref[t] * blk_rows, blk_rows)
            return pltpu.make_async_copy(zero_s, xs_ref.at[pl.ds(start, blk_rows)], zsem)

        for t in range(2 * N_EXPERTS):
            pl.when(zblk_ref[t] >= 0)(lambda t=t: zero_copy(t).start())
        for t in range(2 * N_EXPERTS):
            pl.when(zblk_ref[t] >= 0)(lambda t=t: zero_copy(t).wait())

    def slot_copy(r, slot):
        return pltpu.make_async_copy(_row_tile(h_ref, r), _row_tile(xs_ref, slot), sem)

    def issue(r, carry):
        for c in range(TOP_K):
            slot_copy(r, dest_ref[c * n_tok + base + r]).start(priority=c)
        return carry

    def drain(r, carry):
        for c in range(TOP_K):
            slot_copy(r, 0).wait()
        return carry

    lax.fori_loop(0, DISPATCH_ROWS, issue, 0, unroll=8)
    lax.fori_loop(0, DISPATCH_ROWS, drain, 0, unroll=8)


def dispatch_call(dest, zero_blocks, h2t, n_slots):
    n_tok = h2t.shape[0] // ROW_CHUNKS
    return pl.pallas_call(
        functools.partial(_dispatch_kernel, n_tok=n_tok),
        grid_spec=pltpu.PrefetchScalarGridSpec(
            num_scalar_prefetch=2,
            grid=(n_tok // DISPATCH_ROWS,),
            in_specs=[pl.BlockSpec((DISPATCH_ROWS * ROW_CHUNKS, LANES), lambda i, dest, zb: (i, 0))],
            out_specs=pl.BlockSpec(memory_space=pl.ANY),
            scratch_shapes=[pltpu.VMEM((MOE_BLK * ROW_CHUNKS, LANES), h2t.dtype),
                            pltpu.SemaphoreType.DMA(()), pltpu.SemaphoreType.DMA(())]),
        out_shape=jax.ShapeDtypeStruct((n_slots * ROW_CHUNKS, LANES), h2t.dtype),
        compiler_params=_cparams(("arbitrary",)),
        name="dispatch",
    )(dest, zero_blocks, h2t)


def _experts_kernel(be_ref, cnt_ref, x_ref, wg_ref, wu_ref, wd_ref, o_ref, wg_s, wu_s, wd_s):
    i = pl.program_id(0)
    e = be_ref[i]
    e_before = be_ref[jnp.maximum(i - 1, 0)]
    cnt = cnt_ref[i]

    @pl.when(jnp.logical_or(i == 0, e != e_before))
    def _():
        wg_s[...] = wg_ref[0, 0].astype(wg_s.dtype)
        wu_s[...] = wu_ref[0, 0].astype(wu_s.dtype)
        wd_s[...] = wd_ref[0, 0].astype(wd_s.dtype)

    @pl.when(cnt > 0)
    def _():
        xb = _load_row_tiles(x_ref, MOE_BLK).astype(MXU_DTYPE)
        gt = jnp.dot(xb, wg_s[...], preferred_element_type=F32)
        up = jnp.dot(xb, wu_s[...], preferred_element_type=F32)
        y = jnp.dot((_silu(gt) * up).astype(MXU_DTYPE), wd_s[...], preferred_element_type=F32)
        _store_row_tiles(o_ref, y)

    @pl.when(cnt == 0)
    def _():
        o_ref[...] = jnp.zeros_like(o_ref)


def experts_call(blk_exp, blk_cnt, xs, w_gate, w_up, w_down, layer):
    n_slots = xs.shape[0] // ROW_CHUNKS
    d, f = w_gate.shape[-2:]
    wspec = lambda a: pl.BlockSpec((1, 1) + a.shape[2:], lambda i, be, cnt: (layer, be[i], 0, 0))
    blk = pl.BlockSpec((MOE_BLK * ROW_CHUNKS, LANES), lambda i, be, cnt: (i, 0))
    return pl.pallas_call(
        _experts_kernel,
        grid_spec=pltpu.PrefetchScalarGridSpec(
            num_scalar_prefetch=2,
            grid=(n_slots // MOE_BLK,),
            in_specs=[blk, wspec(w_gate), wspec(w_up), wspec(w_down)],
            out_specs=blk,
            scratch_shapes=[pltpu.VMEM((d, f), MXU_DTYPE), pltpu.VMEM((d, f), MXU_DTYPE),
                            pltpu.VMEM((f, d), MXU_DTYPE)]),
        out_shape=jax.ShapeDtypeStruct(xs.shape, F32),
        compiler_params=_cparams(("arbitrary",)),
        name="experts",
    )(blk_exp, blk_cnt, xs, w_gate, w_up, w_down)


def _combined_tile(dest_ref, y_ref, gate_ref, x_ref, mod_ref, lng_ref, lnb_ref, ybuf, sems, *, n_tok, nt, n_steps):
    step = pl.program_id(0) * nt + pl.program_id(1)
    slot = step % 2

    def row_copy(buf, r, c, src_slot):
        return pltpu.make_async_copy(_row_tile(y_ref, src_slot), _row_tile(ybuf.at[buf, c], r), sems.at[buf])

    def issue(tile, buf):
        base = tile * TM

        def body(r, carry):
            for c in range(TOP_K):
                row_copy(buf, r, c, dest_ref[c * n_tok + base + r]).start(priority=c)
            return carry

        lax.fori_loop(0, TM, body, 0, unroll=8)

    def drain(buf):
        def body(r, carry):
            for c in range(TOP_K):
                row_copy(buf, r, c, 0).wait()
            return carry

        lax.fori_loop(0, TM, body, 0, unroll=8)

    pl.when(step == 0)(lambda: issue(0, 0))
    pl.when(step + 1 < n_steps)(lambda: issue(step + 1, 1 - slot))
    drain(slot)
    mod = mod_ref[0, 0]
    gate = gate_ref[...]
    y = (gate[:, 0:1] * _load_row_tiles(ybuf.at[slot, 0], TM)
         + gate[:, 1:2] * _load_row_tiles(ybuf.at[slot, 1], TM))
    return _layer_norm_rows(DEEPNORM_ALPHA * x_ref[0] + mod[5:6] * y, lng_ref[...], lnb_ref[...])


def _combine_kernel(dest_ref, y_ref, gate_ref, x_ref, mod_ref, lng_ref, lnb_ref, o_ref, ybuf, sems, **kw):
    o_ref[0] = _combined_tile(dest_ref, y_ref, gate_ref, x_ref, mod_ref, lng_ref, lnb_ref, ybuf, sems, **kw)


def _combine_inproj1_kernel(dest_ref, y_ref, gate_ref, x_ref, mod_ref, lng_ref, lnb_ref,
                            mod1_ref, w_ref, qg_ref, kg_ref, cos_ref, sin_ref,
                            o_ref, q_ref, k_ref, v_ref, ybuf, sems, **kw):
    x = _combined_tile(dest_ref, y_ref, gate_ref, x_ref, mod_ref, lng_ref, lnb_ref, ybuf, sems, **kw)
    o_ref[0] = x
    _qkv_project(x, mod1_ref[0, 0], w_ref, qg_ref, kg_ref, cos_ref, sin_ref, q_ref, k_ref, v_ref)


def combine_call(dest, y_slots, gate_cols, x1, modsel, ln_g, ln_b, ctx_tiles, proj=None):
    bsz, tt, d = x1.shape
    nt = tt // TM
    tok = lambda w: pl.BlockSpec((1, TM, w), lambda b, j, dest: (b, j, 0))
    const = lambda a: pl.BlockSpec(a.shape, lambda b, j, dest: (0, 0))
    modspec = pl.BlockSpec((1, 1, 6, d), lambda b, j, dest: (b, (j >= ctx_tiles).astype(jnp.int32), 0, 0))
    in_specs = [pl.BlockSpec(memory_space=pl.ANY),
                pl.BlockSpec((TM, TOP_K), lambda b, j, dest: (b * nt + j, 0)),
                tok(d), modspec, const(ln_g), const(ln_b)]
    args = [dest, y_slots, gate_cols, x1, modsel, ln_g, ln_b]
    out_specs = [tok(d)]
    out_shape = [jax.ShapeDtypeStruct((bsz, tt, d), F32)]
    body = _combine_kernel
    if proj is not None:
        modsel1, w_in, q_g, k_g, cos2, sin2 = proj
        rope = pl.BlockSpec((TM, A_HEAD_DIM), lambda b, j, dest: (j, 0))
        in_specs += [modspec, const(w_in), const(q_g), const(k_g), rope, rope]
        args += [modsel1, w_in, q_g, k_g, cos2, sin2]
        widths = (A_Q_W, A_KV_W, A_KV_W)
        out_specs += [tok(w) for w in widths]
        out_shape += [jax.ShapeDtypeStruct((bsz, tt, w), MXU_DTYPE) for w in widths]
        body = _combine_inproj1_kernel
    res = pl.pallas_call(
        functools.partial(body, n_tok=bsz * tt, nt=nt, n_steps=bsz * nt),
        grid_spec=pltpu.PrefetchScalarGridSpec(
            num_scalar_prefetch=1,
            grid=(bsz, nt),
            in_specs=in_specs,
            out_specs=out_specs,
            scratch_shapes=[pltpu.VMEM((2, TOP_K, TM * ROW_CHUNKS, LANES), F32), pltpu.SemaphoreType.DMA((2,))]),
        out_shape=out_shape,
        compiler_params=_cparams(("arbitrary", "arbitrary")),
        name="combine" if proj is None else "combine_inproj1",
    )(*args)
    return res[0] if proj is None else res


def moe_block(x1, h2t, e_idx, gates, modsel, ln_g, ln_b, w_gate, w_up, w_down, layer, ctx_tiles, proj=None):
    bsz, tt, d = x1.shape
    n = bsz * tt

    n_asg = TOP_K * n
    flat_e = e_idx.reshape(n_asg)
    onehot = (flat_e[:, None] == jnp.arange(N_EXPERTS, dtype=jnp.int32)[None, :]).astype(jnp.int32)
    csum = jnp.cumsum(onehot, axis=0)
    rank = jnp.sum(onehot * csum, axis=1) - 1
    counts = csum[-1]
    padded = (counts + MOE_BLK - 1) // MOE_BLK * MOE_BLK
    pend = jnp.cumsum(padded)
    pstart = pend - padded
    dest = (pstart[flat_e] + rank).astype(jnp.int32)
    n_blocks = -(-n_asg // MOE_BLK) + N_EXPERTS
    blk_start = jnp.arange(n_blocks, dtype=jnp.int32) * MOE_BLK
    blk_exp = jnp.sum((pend[None, :] <= blk_start[:, None]).astype(jnp.int32), axis=1)
    blk_exp = jnp.minimum(blk_exp, N_EXPERTS - 1)
    blk_cnt = jnp.clip(counts[blk_exp] - (blk_start - pstart[blk_exp]), 0, MOE_BLK).astype(jnp.int32)

    part = jnp.where(counts % MOE_BLK != 0, pend // MOE_BLK - 1, -1)
    tail = pend[-1] // MOE_BLK + jnp.arange(N_EXPERTS, dtype=jnp.int32)
    tail = jnp.where(tail < n_blocks, tail, -1)
    zero_blocks = jnp.concatenate([part, tail]).astype(jnp.int32)

    xs = dispatch_call(dest, zero_blocks, h2t, n_blocks * MOE_BLK)
    y_slots = experts_call(blk_exp, blk_cnt, xs, w_gate, w_up, w_down, layer)
    return combine_call(dest, y_slots, gates.T, x1, modsel, ln_g, ln_b, ctx_tiles, proj)


def _qkv_project(x, mod, w_ref, qg_ref, kg_ref, cos_ref, sin_ref, q_ref, k_ref, v_ref):
    hb = (x * (1.0 + mod[1:2]) + mod[0:1]).astype(MXU_DTYPE)
    cos = cos_ref[...]
    sin = sin_ref[...]

    def norm_rope(t, g):
        t = t * lax.rsqrt(jnp.mean(t * t, axis=-1, keepdims=True) + RMS_EPS) * g
        return t * cos + pltpu.roll(t, A_HEAD_DIM // 2, 1) * sin

    qkv = jnp.dot(hb, w_ref[...], preferred_element_type=F32)
    for j in range(A_HEADS):
        sl = slice(j * A_HEAD_DIM, (j + 1) * A_HEAD_DIM)
        q_ref[0, :, sl] = (norm_rope(qkv[:, sl], qg_ref[...]) * ATT_Q_SCALE).astype(q_ref.dtype)
    for j in range(A_KV_HEADS):
        sl = slice(j * A_HEAD_DIM, (j + 1) * A_HEAD_DIM)
        t = qkv[:, A_Q_W + j * A_HEAD_DIM:A_Q_W + (j + 1) * A_HEAD_DIM]
        k_ref[0, :, sl] = norm_rope(t, kg_ref[...]).astype(k_ref.dtype)
    v_ref[0] = qkv[:, A_Q_W + A_KV_W:].astype(v_ref.dtype)


def _attn_kernel(q_ref, qn_ref, k_ref, vt_ref, o_ref, s_buf):
    rep = A_HEADS // A_KV_HEADS
    n_kv = vt_ref.shape[2]
    assert n_kv % 2 == 0

    def transposed(ref):
        return [ref[0, :, r * A_HEAD_DIM:(r + 1) * A_HEAD_DIM].astype(F32).T.astype(MXU_DTYPE) for r in range(rep)]

    qts = transposed(q_ref)

    def scores(j, slot, qt=qts):
        kb = k_ref[0, pl.ds(pl.multiple_of(j * ATT_TK, ATT_TK), ATT_TK), :]
        for r in range(rep):
            s_buf[slot, r] = jnp.dot(kb, qt[r], preferred_element_type=F32)

    def consume(j, slot, stats):
        vt = vt_ref[0, 0, j]
        new = []
        for r in range(rep):
            m, l, acc = stats[r]
            s = s_buf[slot, r]
            m_new = jnp.maximum(m, jnp.max(s, axis=0, keepdims=True))
            p = jnp.exp2(s - m_new)
            alpha = jnp.exp2(m - m_new)
            l = alpha * l + jnp.sum(p, axis=0, keepdims=True)
            acc = alpha * acc + jnp.dot(vt, p.astype(vt.dtype), preferred_element_type=F32)
            new.append((m_new, l, acc))
        return tuple(new)

    def pair(i, stats):
        j = 2 * i
        scores(j + 1, 1)
        stats = consume(j, 0, stats)
        scores(j + 2, 0)
        return consume(j + 1, 1, stats)

    stats = tuple((jnp.full((1, ATT_TQ), -jnp.inf, F32), jnp.zeros((1, ATT_TQ), F32),
                   jnp.zeros((A_HEAD_DIM, ATT_TQ), F32)) for _ in range(rep))
    pl.when(pl.program_id(2) == 0)(lambda: scores(0, 0))
    stats = lax.fori_loop(0, n_kv // 2 - 1, pair, stats)
    scores(n_kv - 1, 1)
    stats = consume(n_kv - 2, 0, stats)
    scores(0, 0, transposed(qn_ref))
    final = consume(n_kv - 1, 1, stats)
    for r in range(rep):
        _, l, acc = final[r]
        o_ref[0, :, r * A_HEAD_DIM:(r + 1) * A_HEAD_DIM] = (acc / l).T.astype(o_ref.dtype)


def attn_call(q, k, v, n_ctx):
    bsz, ta, _ = q.shape
    t_lat = ta - n_ctx
    gw = (A_HEADS // A_KV_HEADS) * A_HEAD_DIM
    q_off = n_ctx // ATT_TQ
    n_kv = ta // ATT_TK
    vt = v.reshape(bsz, n_kv, ATT_TK, A_KV_HEADS, A_HEAD_DIM).transpose(0, 3, 1, 4, 2)
    nq = t_lat // ATT_TQ
    return pl.pallas_call(
        _attn_kernel,
        grid=(bsz, A_KV_HEADS, nq),
        in_specs=[pl.BlockSpec((1, ATT_TQ, gw), lambda b, g, i: (b, i + q_off, g)),
                  pl.BlockSpec((1, ATT_TQ, gw), lambda b, g, i: (b, jnp.minimum(i + 1, nq - 1) + q_off, g)),
                  pl.BlockSpec((1, ta, A_HEAD_DIM), lambda b, g, i: (b, 0, g)),
                  pl.BlockSpec((1, 1, n_kv, A_HEAD_DIM, ATT_TK), lambda b, g, i: (b, g, 0, 0, 0))],
        out_specs=pl.BlockSpec((1, ATT_TQ, gw), lambda b, g, i: (b, i, g)),
        out_shape=jax.ShapeDtypeStruct((bsz, t_lat, A_Q_W), MXU_DTYPE),
        scratch_shapes=[pltpu.VMEM((2, A_HEADS // A_KV_HEADS, ATT_TK, ATT_TQ), F32)],
        compiler_params=_cparams(("arbitrary", "arbitrary", "arbitrary")),
        name="attention",
    )(q, q, k, vt)


def _outproj1_kernel(a_ref, x_ref, mod_ref, w_ref, lng_ref, lnb_ref, rw_ref, rb_ref, x1_ref, h2_ref, e_ref, g_ref):
    y = jnp.dot(a_ref[0], w_ref[...], preferred_element_type=F32)
    _finish_sublayer(x_ref[0], y, mod_ref[0, 0], lng_ref, lnb_ref, rw_ref, rb_ref, x1_ref, h2_ref, e_ref, g_ref)


def outproj1_call(att, xa, modsel, w_out, ln_g, ln_b, router_wt, router_b, ctx_tiles):
    bsz, t_lat, _ = att.shape
    d = xa.shape[-1]
    const = lambda a: pl.BlockSpec(a.shape, lambda b, j: (0, 0))
    out_specs, out_shape = _sublayer_out(bsz, t_lat, d)
    return pl.pallas_call(
        _outproj1_kernel,
        grid=(bsz, t_lat // TM),
        in_specs=[pl.BlockSpec((1, TM, att.shape[-1]), lambda b, j: (b, j, 0)),
                  pl.BlockSpec((1, TM, d), lambda b, j: (b, j + ctx_tiles, 0)),
                  pl.BlockSpec((1, 1, 6, d), lambda b, j: (b, 1, 0, 0)),
                  const(w_out), const(ln_g), const(ln_b), const(router_wt), const(router_b)],
        out_specs=out_specs,
        out_shape=out_shape,
        compiler_params=_cparams(("arbitrary", "arbitrary")),
        name="outproj1",
    )(att, xa, modsel, w_out, ln_g, ln_b, router_wt, router_b)


def _rope_tables(n_ctx, n_lat):
    rows = n_lat // GRID_W
    row = jnp.repeat(jnp.arange(rows), GRID_W).astype(F32)
    col = jnp.tile(jnp.arange(GRID_W), rows).astype(F32)
    n_freq = A_HEAD_DIM // 4
    inv = ROPE_THETA ** (-jnp.arange(n_freq, dtype=F32) / n_freq)
    ang = jnp.concatenate([row[:, None] * inv, col[:, None] * inv], -1)
    cos, sin = jnp.cos(ang), jnp.sin(ang)
    cos2 = jnp.concatenate([cos, cos], -1)
    sin2 = jnp.concatenate([-sin, sin], -1)
    cos2 = jnp.concatenate([jnp.ones((n_ctx, A_HEAD_DIM), F32), cos2], 0)
    sin2 = jnp.concatenate([jnp.zeros((n_ctx, A_HEAD_DIM), F32), sin2], 0)
    return cos2, sin2


def _lane_row(parts, width=GATE_LANES):
    row = jnp.concatenate([p.reshape(-1).astype(F32) for p in parts])
    return jnp.pad(row, (0, width - row.shape[0])).reshape(1, width)


def kernel(x, c, ctx, c_ctx, ada_w, ada_b, ln_g, ln_b, ab_w_in, ab_w_out, ml_ig_b, ml_fg_b, ml_norm_g,
           ssm_conv_w, ssm_conv_b, ssm_dt_b, ssm_a_log, ssm_d, ssm_norm_g, at_w_in, at_w_out, at_q_g, at_k_g,
           router_w, router_b, moe_w_gate, moe_w_up, moe_w_down):
    bsz, n_lat, d = x.shape
    n_ctx = ctx.shape[1]
    assert d == D_MODEL and bsz + 1 <= SUBLANES
    assert n_ctx % TM == 0 and n_lat % TM == 0 and n_lat % GRID_W == 0
    assert (n_ctx + n_lat) % ATT_TK == 0 and n_ctx % ATT_TQ == 0 and n_lat % ATT_TQ == 0
    assert (bsz * (n_ctx + n_lat)) % DISPATCH_ROWS == 0 and (bsz * n_lat) % DISPATCH_ROWS == 0
    ctx_tiles = n_ctx // TM
    ctx_chunks = n_ctx // CHUNK

    crows = jnp.zeros((SUBLANES, d), F32).at[:bsz].set(c).at[bsz].set(c_ctx)
    mods = ada_call(crows, ada_w, ada_b)

    def mod_table(i):
        lat = mods[i, :bsz].reshape(bsz, 1, 6, d)
        cx = jnp.broadcast_to(mods[i, bsz].reshape(1, 1, 6, d), (bsz, 1, 6, d))
        return jnp.concatenate([cx, lat], axis=1)

    router_wt = jnp.pad(router_w, ((0, 0), (0, LANES - N_EXPERTS)))
    router_bc = router_b.reshape(N_EXPERTS, 1)

    modsel = mod_table(0)
    w_in = ab_w_in[0]
    s_q, s_k, s_v, s_o, s_ig, s_fg, s_z, s_xbc = (int(v) for v in
        (0, 512, 1024, 2048, 3072, 3072 + 8, 3072 + 16, 3072 + 16 + 1024))
    s_dt = s_xbc + S_CONV_CH
    w_big = jnp.concatenate([w_in[:, :s_ig], w_in[:, s_z:s_dt]], axis=1).astype(MXU_DTYPE)
    w_small = jnp.concatenate([w_in[:, s_ig:s_z], w_in[:, s_dt:]], axis=1)
    w_small = jnp.pad(w_small, ((0, 0), (0, GATE_LANES - w_small.shape[1])))
    q, k, v, o, z, xbc, gates = inproj0_call(ctx, x, modsel, w_big, w_small, ctx_tiles)
    xbc_act = conv_call(xbc, ssm_conv_w[0], ssm_conv_b[0], ctx_tiles)

    grow = jnp.swapaxes(gates[:, :, :DT_OFF], 1, 2)
    dtrow = jnp.swapaxes(gates[:, :, DT_OFF:DT_OFF + N_DIR * S_HEADS], 1, 2)
    gate_b_row = _lane_row([ml_ig_b[0], ml_fg_b[0], ssm_dt_b[0]])
    gate_b_col = jnp.concatenate([ml_ig_b[0].reshape(-1), ml_fg_b[0].reshape(-1)]).reshape(-1, 1)
    a_neg = -jnp.exp(ssm_a_log[0].astype(F32)).reshape(-1)
    an_row = _lane_row([jnp.zeros((DT_OFF,), F32), a_neg])
    an_col = a_neg.reshape(-1, 1)
    dtb_col = ssm_dt_b[0].reshape(-1, 1)

    dskip = jnp.repeat(ssm_d[0].astype(F32), S_HEAD_DIM).reshape(1, S_WIDTH)
    m_norm_g = ml_norm_g[0].reshape(1, -1)
    s_norm_g = ssm_norm_g[0].reshape(1, -1)
    assert N_DIR == 2
    hm = None
    hs = None
    for dd in range(N_DIR):
        hm = mlstm_call(q, k, v, gates, grow, gate_b_row, gate_b_col, hm, o, m_norm_g, rev=dd == 1, d=dd,
                        ctx_chunks=ctx_chunks)
        hs = ssd_call(xbc_act, gates, dtrow, gate_b_row, dtb_col, an_row, an_col, hs, z, dskip, s_norm_g,
                      rev=dd == 1, d=dd, ctx_chunks=ctx_chunks)

    x1, h2t, e_idx, gates = outproj0_call(hm, hs, ctx, x, modsel, ab_w_out[0].astype(MXU_DTYPE),
                                          ln_g[0, 0].reshape(1, d), ln_b[0, 0].reshape(1, d), router_wt, router_bc,
                                          ctx_tiles)
    modsel1 = mod_table(1)
    cos2, sin2 = _rope_tables(n_ctx, n_lat)
    proj = (modsel1, at_w_in[0].astype(MXU_DTYPE), at_q_g[0].reshape(1, -1), at_k_g[0].reshape(1, -1), cos2, sin2)
    xa, qa, ka, va = moe_block(x1, h2t, e_idx, gates, modsel, ln_g[0, 1].reshape(1, d), ln_b[0, 1].reshape(1, d),
                               moe_w_gate, moe_w_up, moe_w_down, 0, ctx_tiles, proj)
    modsel = modsel1
    att = attn_call(qa, ka, va, n_ctx)
    x1, h2t, e_idx, gates = outproj1_call(att, xa, modsel, at_w_out[0].astype(MXU_DTYPE), ln_g[1, 0].reshape(1, d),
                                          ln_b[1, 0].reshape(1, d), router_wt, router_bc, ctx_tiles)
    return moe_block(x1, h2t, e_idx, gates, modsel, ln_g[1, 1].reshape(1, d), ln_b[1, 1].reshape(1, d),
                     moe_w_gate, moe_w_up, moe_w_down, 1, 0)
```

```python
import functools
import math

import jax
import jax.numpy as jnp
from jax import lax
from jax.experimental import pallas as pl
from jax.experimental.pallas import tpu as pltpu

F32 = jnp.float32
MXU_DTYPE = jnp.bfloat16
HIGHEST = lax.Precision.HIGHEST

D_MODEL = 1024
DEPTH = 2
GRID_W = 64
CHUNK = 128
M_HEADS = 4
M_QK_DIM = D_MODEL // 8
M_V_DIM = D_MODEL // 4
M_WIDTH = M_HEADS * M_V_DIM
S_HEADS = 16
S_HEAD_DIM = D_MODEL // 16
S_GROUPS = 2
S_HEADS_PER_GROUP = S_HEADS // S_GROUPS
S_STATE = 128
S_WIDTH = S_HEADS * S_HEAD_DIM
S_CONV_CH = S_WIDTH + 2 * S_GROUPS * S_STATE
CONV_K = 4
N_DIR = 2
A_HEADS = 8
A_KV_HEADS = 2
A_HEAD_DIM = D_MODEL // A_HEADS
A_Q_W = A_HEADS * A_HEAD_DIM
A_KV_W = A_KV_HEADS * A_HEAD_DIM
ROPE_THETA = 10000.0
N_EXPERTS = 16
N_EXPERT_GROUPS = 4
EXPERTS_PER_GROUP = N_EXPERTS // N_EXPERT_GROUPS
TOP_K = 2
DEEPNORM_ALPHA = (2 * DEPTH) ** 0.25
LN_EPS = 1e-5
RMS_EPS = 1e-6

LANES = 128
SUBLANES = 8
TM = 256
MOE_BLK = 512
DISPATCH_ROWS = 512
ROW_CHUNKS = D_MODEL // LANES
assert ROW_CHUNKS == SUBLANES
ATT_TQ = 256
ATT_TK = 1408
VMEM_LIMIT = 56 * 1024 * 1024
GATE_LANES = 128
IG_OFF, FG_OFF, DT_OFF = 0, N_DIR * M_HEADS, 2 * N_DIR * M_HEADS
ATT_Q_SCALE = A_HEAD_DIM ** -0.5 * math.log2(math.e)


def _cparams(sem):
    return pltpu.CompilerParams(dimension_semantics=sem, vmem_limit_bytes=VMEM_LIMIT)


def _silu(x):
    return x / (1.0 + jnp.exp(-x))


def _sigmoid(x):
    return 1.0 / (1.0 + jnp.exp(-x))


def _softplus(x):
    return jnp.maximum(x, 0.0) + jnp.log(1.0 + jnp.exp(-jnp.abs(x)))


def _log_sigmoid(x):
    return jnp.minimum(x, 0.0) - jnp.log(1.0 + jnp.exp(-jnp.abs(x)))


def _layer_norm_rows(x, g, b):
    mu = jnp.mean(x, axis=-1, keepdims=True)
    xc = x - mu
    var = jnp.mean(xc * xc, axis=-1, keepdims=True)
    return xc * lax.rsqrt(var + LN_EPS) * g + b


def _mm(a, b):
    return jnp.dot(a.astype(MXU_DTYPE), b.astype(MXU_DTYPE), preferred_element_type=F32)


def _mm_nt(a, b):
    return lax.dot_general(a.astype(MXU_DTYPE), b.astype(MXU_DTYPE), (((1,), (1,)), ((), ())),
                           preferred_element_type=F32)


def _mm_tn(a, b):
    return lax.dot_general(a.astype(MXU_DTYPE), b.astype(MXU_DTYPE), (((0,), (0,)), ((), ())),
                           preferred_element_type=F32)


def _ada_kernel(c_ref, w_ref, b_ref, o_ref):
    s = _silu(c_ref[...])
    o_ref[0] = jnp.dot(s, w_ref[0], precision=HIGHEST, preferred_element_type=F32) + b_ref[0]


def ada_call(crows, ada_w, ada_b):
    depth, d, n6 = ada_w.shape
    tn = 1536
    return pl.pallas_call(
        _ada_kernel,
        grid=(depth, n6 // tn),
        in_specs=[pl.BlockSpec((SUBLANES, d), lambda i, j: (0, 0)),
                  pl.BlockSpec((1, d, tn), lambda i, j: (i, 0, j)),
                  pl.BlockSpec((1, 1, tn), lambda i, j: (i, 0, j))],
        out_specs=pl.BlockSpec((1, SUBLANES, tn), lambda i, j: (i, 0, j)),
        out_shape=jax.ShapeDtypeStruct((depth, SUBLANES, n6), F32),
        compiler_params=_cparams(("arbitrary", "arbitrary")),
        name="ada",
    )(crows, ada_w, ada_b.reshape(depth, 1, n6))


def _token_tile(ctx_ref, x_ref, ctx_tiles):
    return jnp.where(pl.program_id(1) < ctx_tiles, ctx_ref[0], x_ref[0])


def _token_specs(d, ctx_tiles):
    return [pl.BlockSpec((1, TM, d), lambda b, j: (b, jnp.minimum(j, ctx_tiles - 1), 0)),
            pl.BlockSpec((1, TM, d), lambda b, j: (b, jnp.maximum(j - ctx_tiles, 0), 0))]


def _inproj0_kernel(ctx_ref, x_ref, mod_ref, wb_ref, ws_ref, q_ref, k_ref, v_ref, o_ref, z_ref, xbc_ref, g_ref, *,
                    ctx_tiles):
    x = _token_tile(ctx_ref, x_ref, ctx_tiles)
    mod = mod_ref[0, 0]
    h = x * (1.0 + mod[1:2]) + mod[0:1]
    hb = h.astype(MXU_DTYPE)
    qk = M_HEADS * M_QK_DIM
    c0 = 0
    q_ref[0] = (jnp.dot(hb, wb_ref[:, c0:c0 + qk], preferred_element_type=F32)
                * (M_QK_DIM ** -0.5)).astype(q_ref.dtype)
    c0 += qk
    k_ref[0] = jnp.dot(hb, wb_ref[:, c0:c0 + qk], preferred_element_type=F32).astype(k_ref.dtype)
    c0 += qk
    v_ref[0] = jnp.dot(hb, wb_ref[:, c0:c0 + M_WIDTH], preferred_element_type=F32).astype(v_ref.dtype)
    c0 += M_WIDTH
    o_ref[0] = jnp.dot(hb, wb_ref[:, c0:c0 + M_WIDTH], preferred_element_type=F32)
    c0 += M_WIDTH
    z_ref[0] = jnp.dot(hb, wb_ref[:, c0:c0 + S_WIDTH], preferred_element_type=F32)
    c0 += S_WIDTH
    xbc_ref[0] = jnp.dot(hb, wb_ref[:, c0:c0 + S_CONV_CH], preferred_element_type=F32)
    h_hi, h_lo = _split_hi_lo(h)
    w_hi, w_lo = _split_hi_lo(ws_ref[...])
    g_ref[0] = (jnp.dot(h_hi, w_hi, preferred_element_type=F32) + jnp.dot(h_lo, w_hi, preferred_element_type=F32)
                + jnp.dot(h_hi, w_lo, preferred_element_type=F32))


def inproj0_call(ctx, x, modsel, w_big, w_small, ctx_tiles):
    bsz, _, d = x.shape
    ta = ctx.shape[1] + x.shape[1]
    nt = ta // TM
    qk = M_HEADS * M_QK_DIM
    widths = (qk, qk, M_WIDTH, M_WIDTH, S_WIDTH, S_CONV_CH, GATE_LANES)
    dtypes = (MXU_DTYPE, MXU_DTYPE, MXU_DTYPE, F32, F32, F32, F32)
    tok = lambda w: pl.BlockSpec((1, TM, w), lambda b, j: (b, j, 0))
    return pl.pallas_call(
        functools.partial(_inproj0_kernel, ctx_tiles=ctx_tiles),
        grid=(bsz, nt),
        in_specs=_token_specs(d, ctx_tiles) + [
                  pl.BlockSpec((1, 1, 6, d), lambda b, j: (b, (j >= ctx_tiles).astype(jnp.int32), 0, 0)),
                  pl.BlockSpec(w_big.shape, lambda b, j: (0, 0)),
                  pl.BlockSpec(w_small.shape, lambda b, j: (0, 0))],
        out_specs=[tok(w) for w in widths],
        out_shape=[jax.ShapeDtypeStruct((bsz, ta, w), dt) for w, dt in zip(widths, dtypes)],
        compiler_params=_cparams(("arbitrary", "arbitrary")),
        name="inproj0",
    )(ctx, x, modsel, w_big, w_small)


def _conv_kernel(cur_ref, prev_ref, next_ref, w_ref, b_ref, o_ref, *, ctx_tiles, n_tiles):
    j = pl.program_id(1)
    has_prev = jnp.logical_and(j != 0, j != ctx_tiles)
    has_next = jnp.logical_and(j != ctx_tiles - 1, j != n_tiles - 1)
    prev = jnp.where(has_prev, prev_ref[0], 0.0)
    nxt = jnp.where(has_next, next_ref[0], 0.0)
    ext = jnp.concatenate([prev, cur_ref[0], nxt], axis=0)
    n = TM + 2 * SUBLANES
    w = w_ref[...]
    lo, hi = SUBLANES, SUBLANES + TM
    acc = ext[lo:hi] * w[2:3]
    acc = acc + pltpu.roll(ext, 2, 0)[lo:hi] * w[0:1]
    acc = acc + pltpu.roll(ext, 1, 0)[lo:hi] * w[1:2]
    acc = acc + pltpu.roll(ext, n - 1, 0)[lo:hi] * w[3:4]
    o_ref[0] = _silu(acc + b_ref[...])


def conv_call(xbc, conv_w, conv_b, ctx_tiles):
    bsz, ta, ch = xbc.shape
    nt = ta // TM
    r = TM // SUBLANES
    last = ta // SUBLANES - 1
    return pl.pallas_call(
        functools.partial(_conv_kernel, ctx_tiles=ctx_tiles, n_tiles=nt),
        grid=(bsz, nt),
        in_specs=[pl.BlockSpec((1, TM, ch), lambda b, j: (b, j, 0)),
                  pl.BlockSpec((1, SUBLANES, ch), lambda b, j: (b, jnp.maximum(j * r - 1, 0), 0)),
                  pl.BlockSpec((1, SUBLANES, ch), lambda b, j: (b, jnp.minimum((j + 1) * r, last), 0)),
                  pl.BlockSpec((CONV_K, ch), lambda b, j: (0, 0)),
                  pl.BlockSpec((1, ch), lambda b, j: (0, 0))],
        out_specs=pl.BlockSpec((1, TM, ch), lambda b, j: (b, j, 0)),
        out_shape=jax.ShapeDtypeStruct((bsz, ta, ch), F32),
        compiler_params=_cparams(("arbitrary", "arbitrary")),
        name="conv",
    )(xbc, xbc, xbc, conv_w, conv_b.reshape(1, ch))


def _chunk_order(i, rev, ctx_chunks, n_chunks):
    if not rev:
        return i
    return jnp.where(i < ctx_chunks, ctx_chunks - 1 - i, n_chunks - 1 - (i - ctx_chunks))


def _scan_masks(rev):
    r = lax.broadcasted_iota(jnp.int32, (CHUNK, CHUNK), 0)
    c = lax.broadcasted_iota(jnp.int32, (CHUNK, CHUNK), 1)
    mask = (c >= r) if rev else (c <= r)
    mask_t = (r >= c) if rev else (r <= c)
    return mask, mask.astype(F32).astype(MXU_DTYPE), mask_t.astype(F32).astype(MXU_DTYPE)


def _split3(x):
    hi = x.astype(MXU_DTYPE)
    r = x - hi.astype(F32)
    mid = r.astype(MXU_DTYPE)
    lo = (r - mid.astype(F32)).astype(MXU_DTYPE)
    return hi, mid, lo


def _prefix_cols(tri, x):
    return sum(jnp.dot(tri, p, preferred_element_type=F32) for p in _split3(x))


def _prefix_rows(x, tri):
    return sum(jnp.dot(p, tri, preferred_element_type=F32) for p in _split3(x))


def _per_batch(body, kinds):
    def kern(*refs, **kw):
        assert len(refs) == len(kinds)
        for bi in range(refs[0].shape[0]):
            sub = [r.at[pl.ds(bi, 1)] if f == 'b' else r.at[bi] if f == 's' else r for r, f in zip(refs, kinds)]
            body(*sub, **kw)
    return kern


def _mlstm_kernel(*refs, rev, d, add_prev):
    if add_prev:
        (q_ref, k_ref, v_ref, gc_ref, gr_ref, brow_ref, bcol_ref, prev_ref, og_ref, mg_ref,
         o_ref, ct_s, n_s, m_s) = refs
    else:
        q_ref, k_ref, v_ref, gc_ref, gr_ref, brow_ref, bcol_ref, o_ref, ct_s, n_s, m_s = refs
        prev_ref = None

    @pl.when(pl.program_id(0) == 0)
    def _():
        ct_s[...] = jnp.zeros_like(ct_s)
        n_s[...] = jnp.zeros_like(n_s)
        m_s[...] = jnp.zeros_like(m_s)

    mask, mask_f, mask_tf = _scan_masks(rev)
    end = 0 if rev else CHUNK - 1
    gcol = gc_ref[0] + brow_ref[...]
    grow = gr_ref[0] + bcol_ref[...]
    nh2 = N_DIR * M_HEADS
    lf_col = _log_sigmoid(gcol)
    lf_row = _log_sigmoid(grow[FG_OFF:FG_OFF + nh2])
    b_col_all = _prefix_cols(mask_f, lf_col)
    b_row_all = _prefix_rows(lf_row, mask_tf)

    for j in range(M_HEADS):
        ci = d * M_HEADS + j
        bcol = b_col_all[:, FG_OFF + ci:FG_OFF + ci + 1]
        brow = b_row_all[ci:ci + 1, :]
        igcol = gcol[:, IG_OFF + ci:IG_OFF + ci + 1]
        igrow = grow[IG_OFF + ci:IG_OFF + ci + 1, :]
        b_end = bcol[end:end + 1, :]
        m_prev = m_s[j][:, 0:1]
        n_prev = n_s[j]
        ct_prev = ct_s[j]
        q = q_ref[0, :, j * M_QK_DIM:(j + 1) * M_QK_DIM]
        k = k_ref[0, :, j * M_QK_DIM:(j + 1) * M_QK_DIM]
        v = v_ref[0, :, j * M_V_DIM:(j + 1) * M_V_DIM]
        qf = q.astype(F32)
        kf = k.astype(F32)
        vf = v.astype(F32)

        dmat = jnp.where(mask, bcol - brow + igrow, -jnp.inf)
        inter = bcol + m_prev
        m_t = jnp.maximum(inter, jnp.max(dmat, axis=1, keepdims=True))
        sc = _mm_nt(q, k) * jnp.exp(dmat - m_t)
        a_in = jnp.exp(inter - m_t)
        num = _mm(sc, v) + a_in * _mm(q, ct_prev)
        den = jnp.sum(sc, axis=1, keepdims=True) + a_in * jnp.sum(qf * n_prev, axis=1, keepdims=True)
        h = num / jnp.maximum(jnp.abs(den), jnp.exp(-m_t))
        sl = slice(j * M_V_DIM, (j + 1) * M_V_DIM)
        if add_prev:
            h = h + prev_ref[0, :, sl]
            mu = jnp.mean(h, axis=-1, keepdims=True)
            hc = h - mu
            var = jnp.mean(hc * hc, axis=-1, keepdims=True)
            h = hc * lax.rsqrt(var + LN_EPS) * mg_ref[:, sl] * _sigmoid(og_ref[0, :, sl])
        o_ref[0, :, sl] = h.astype(o_ref.dtype)

        g_col = b_end - bcol + igcol
        g_row = b_end - brow + igrow
        g_max = jnp.max(g_row, axis=1, keepdims=True)
        w_col = jnp.exp(g_col - g_max)
        d_ct = _mm_tn(k, vf * w_col)
        d_n = jnp.sum(kf * w_col, axis=0, keepdims=True)
        m_new = jnp.maximum(b_end + m_prev, g_max)
        a = jnp.exp(b_end + m_prev - m_new)
        s = jnp.exp(g_max - m_new)
        ct_s[j] = a * ct_prev + s * d_ct
        n_s[j] = a * n_prev + s * d_n
        m_s[j] = jnp.broadcast_to(m_new, (1, LANES))


def mlstm_call(q, k, v, gcol, grow, bias_row, bias_col, prev, o_gate, m_norm_g, *, rev, d, ctx_chunks):
    bsz, ta, _ = q.shape
    nc = ta // CHUNK
    order = lambda i: _chunk_order(i, rev, ctx_chunks, nc)
    tok = lambda w: pl.BlockSpec((bsz, CHUNK, w), lambda i: (0, order(i), 0))
    in_specs = [tok(q.shape[-1]), tok(k.shape[-1]), tok(v.shape[-1]), tok(GATE_LANES),
                pl.BlockSpec((bsz, grow.shape[1], CHUNK), lambda i: (0, 0, order(i))),
                pl.BlockSpec(bias_row.shape, lambda i: (0, 0)),
                pl.BlockSpec(bias_col.shape, lambda i: (0, 0))]
    args = [q, k, v, gcol, grow, bias_row, bias_col]
    kinds = "bbbbbcc"
    last = prev is not None
    if last:
        in_specs += [tok(M_WIDTH), tok(M_WIDTH), pl.BlockSpec(m_norm_g.shape, lambda i: (0, 0))]
        args += [prev, o_gate, m_norm_g]
        kinds += "bbc"
    kinds += "b" + "sss"
    return pl.pallas_call(
        functools.partial(_per_batch(_mlstm_kernel, kinds), rev=rev, d=d, add_prev=last),
        grid=(nc,),
        in_specs=in_specs,
        out_specs=tok(M_WIDTH),
        out_shape=jax.ShapeDtypeStruct((bsz, ta, M_WIDTH), MXU_DTYPE if last else F32),
        scratch_shapes=[pltpu.VMEM((bsz, M_HEADS, M_QK_DIM, M_V_DIM), F32),
                        pltpu.VMEM((bsz, M_HEADS, 1, M_QK_DIM), F32),
                        pltpu.VMEM((bsz, M_HEADS, 1, LANES), F32)],
        compiler_params=_cparams(("arbitrary",)),
        name="mlstm_rev" if rev else "mlstm_fwd",
    )(*args)


def _ssd_kernel(*refs, rev, d, add_prev):
    if add_prev:
        (x_ref, gc_ref, gr_ref, dtb_row_ref, dtb_col_ref, an_row_ref, an_col_ref, prev_ref, z_ref, dsk_ref, sg_ref,
         o_ref, ht_s) = refs
    else:
        x_ref, gc_ref, gr_ref, dtb_row_ref, dtb_col_ref, an_row_ref, an_col_ref, o_ref, ht_s = refs
        prev_ref = None

    @pl.when(pl.program_id(0) == 0)
    def _():
        ht_s[...] = jnp.zeros_like(ht_s)

    mask, mask_f, mask_tf = _scan_masks(rev)
    end = 0 if rev else CHUNK - 1
    lane = lax.broadcasted_iota(jnp.int32, (CHUNK, LANES), 1)
    first_half = lane < S_HEAD_DIM
    dt_col = _softplus(gc_ref[0] + dtb_row_ref[...])
    dt_row = _softplus(gr_ref[0] + dtb_col_ref[...])
    acs_col = _prefix_cols(mask_f, dt_col * an_row_ref[...])
    acs_row = _prefix_rows(dt_row * an_col_ref[...], mask_tf)
    a_end_row = acs_col[end:end + 1, :]
    e_cs = jnp.exp(acs_col)
    e_rem = jnp.exp(a_end_row - acs_col)
    e_end = jnp.exp(a_end_row)
    gw = S_HEADS_PER_GROUP * S_HEAD_DIM
    pairs = S_HEADS_PER_GROUP // 2

    def pick(arr, la):
        return jnp.where(first_half[:arr.shape[0]], arr[:, la:la + 1], arr[:, la + 1:la + 2])

    y_parts = []
    for g in range(S_GROUPS):
        bm = x_ref[0, :, S_WIDTH + g * S_STATE:S_WIDTH + (g + 1) * S_STATE]
        cm = x_ref[0, :, S_WIDTH + (S_GROUPS + g) * S_STATE:S_WIDTH + (S_GROUPS + g + 1) * S_STATE]
        cb = _mm_nt(cm, bm)
        ht_prev = ht_s[g]
        y_inter = _mm(cm, ht_prev)
        xw_parts = []
        decay_parts = []
        for p in range(pairs):
            h0 = g * S_HEADS_PER_GROUP + 2 * p
            la = DT_OFF + d * S_HEADS + h0
            ra = d * S_HEADS + h0
            lhs = []
            for u in range(2):
                seg = acs_col[:, la + u:la + u + 1] - acs_row[ra + u:ra + u + 1, :]
                dec = jnp.exp(jnp.where(mask, seg, -jnp.inf))
                lhs.append((cb * dec).astype(MXU_DTYPE))
            xs = x_ref[0, :, h0 * S_HEAD_DIM:(h0 + 2) * S_HEAD_DIM]
            xsd = xs * pick(dt_col, la)
            rhs = jnp.concatenate([jnp.where(first_half, xsd, 0.0), jnp.where(first_half, 0.0, xsd)],
                                  axis=0).astype(MXU_DTYPE)
            y = jnp.dot(jnp.concatenate(lhs, axis=1), rhs, preferred_element_type=F32)
            y = y + y_inter[:, p * LANES:(p + 1) * LANES] * pick(e_cs, la)
            sl = slice(h0 * S_HEAD_DIM, (h0 + 2) * S_HEAD_DIM)
            if add_prev:
                y = (y + prev_ref[0, :, sl] + dsk_ref[:, sl] * xs) * _silu(z_ref[0, :, sl])
                y_parts.append(y)
            else:
                o_ref[0, :, sl] = y
            xw_parts.append(xsd * pick(e_rem, la))
            decay_parts.append(pick(e_end, la))
        xw = jnp.concatenate(xw_parts, axis=1)
        decay = jnp.concatenate(decay_parts, axis=1)
        ht_s[g] = decay * ht_prev + _mm_tn(bm, xw)

    if add_prev:
        ys = jnp.concatenate(y_parts, axis=1)
        ys = ys * lax.rsqrt(jnp.mean(ys * ys, axis=-1, keepdims=True) + RMS_EPS) * sg_ref[...]
        o_ref[0] = ys.astype(o_ref.dtype)


def ssd_call(xbc_act, gcol, dtrow, dtb_row, dtb_col, an_row, an_col, prev, z, dskip, s_norm_g, *, rev, d,
             ctx_chunks):
    bsz, ta, ch = xbc_act.shape
    nc = ta // CHUNK
    order = lambda i: _chunk_order(i, rev, ctx_chunks, nc)
    tok = lambda w: pl.BlockSpec((bsz, CHUNK, w), lambda i: (0, order(i), 0))
    const = lambda a: pl.BlockSpec(a.shape, lambda i: (0, 0))
    in_specs = [tok(ch), tok(GATE_LANES),
                pl.BlockSpec((bsz, dtrow.shape[1], CHUNK), lambda i: (0, 0, order(i))),
                const(dtb_row), const(dtb_col), const(an_row), const(an_col)]
    args = [xbc_act, gcol, dtrow, dtb_row, dtb_col, an_row, an_col]
    kinds = "bbbcccc"
    last = prev is not None
    if last:
        in_specs += [tok(S_WIDTH), tok(S_WIDTH), const(dskip), const(s_norm_g)]
        args += [prev, z, dskip, s_norm_g]
        kinds += "bbcc"
    kinds += "b" + "s"
    return pl.pallas_call(
        functools.partial(_per_batch(_ssd_kernel, kinds), rev=rev, d=d, add_prev=last),
        grid=(nc,),
        in_specs=in_specs,
        out_specs=tok(S_WIDTH),
        out_shape=jax.ShapeDtypeStruct((bsz, ta, S_WIDTH), MXU_DTYPE if last else F32),
        scratch_shapes=[pltpu.VMEM((bsz, S_GROUPS, S_STATE, S_HEADS_PER_GROUP * S_HEAD_DIM), F32)],
        compiler_params=_cparams(("arbitrary",)),
        name="ssd_rev" if rev else "ssd_fwd",
    )(*args)


def _store_row_tiles(ref, val):
    for s in range(ROW_CHUNKS):
        ref[pl.ds(s, val.shape[0], stride=ROW_CHUNKS), :] = val[:, s * LANES:(s + 1) * LANES]


def _load_row_tiles(ref, rows):
    return jnp.concatenate([ref[pl.ds(s, rows, stride=ROW_CHUNKS), :] for s in range(ROW_CHUNKS)], axis=1)


def _finish_sublayer(x, y, mod, lng_ref, lnb_ref, rw_ref, rb_ref, x1_ref, h2_ref, e_ref, g_ref):
    x1 = _layer_norm_rows(DEEPNORM_ALPHA * x + mod[2:3] * y, lng_ref[...], lnb_ref[...])
    x1_ref[0] = x1
    h2 = x1 * (1.0 + mod[4:5]) + mod[3:4]
    _store_row_tiles(h2_ref, h2)
    e, g = _route(h2, rw_ref, rb_ref)
    e_ref[...] = e
    g_ref[...] = g


def _outproj0_kernel(ym_ref, ys_ref, ctx_ref, x_ref, mod_ref, w_ref, lng_ref, lnb_ref, rw_ref, rb_ref,
                     x1_ref, h2_ref, e_ref, g_ref, *, ctx_tiles):
    y = (jnp.dot(ym_ref[0], w_ref[:M_WIDTH, :], preferred_element_type=F32)
         + jnp.dot(ys_ref[0], w_ref[M_WIDTH:, :], preferred_element_type=F32))
    _finish_sublayer(_token_tile(ctx_ref, x_ref, ctx_tiles), y, mod_ref[0, 0], lng_ref, lnb_ref, rw_ref, rb_ref,
                     x1_ref, h2_ref, e_ref, g_ref)


def _sublayer_out(bsz, tt, d):
    nt = tt // TM
    n = bsz * tt
    specs = [pl.BlockSpec((1, TM, d), lambda b, j: (b, j, 0)),
             pl.BlockSpec((TM * ROW_CHUNKS, LANES), lambda b, j: (b * nt + j, 0)),
             pl.BlockSpec((TOP_K, TM), lambda b, j: (0, b * nt + j)),
             pl.BlockSpec((TOP_K, TM), lambda b, j: (0, b * nt + j))]
    shapes = [jax.ShapeDtypeStruct((bsz, tt, d), F32), jax.ShapeDtypeStruct((n * ROW_CHUNKS, LANES), F32),
              jax.ShapeDtypeStruct((TOP_K, n), jnp.int32), jax.ShapeDtypeStruct((TOP_K, n), F32)]
    return specs, shapes


def outproj0_call(ym, ys, ctx, x, modsel, w_out, ln_g, ln_b, router_wt, router_b, ctx_tiles):
    bsz, ta, _ = ym.shape
    d = x.shape[-1]
    nt = ta // TM
    tok = lambda w: pl.BlockSpec((1, TM, w), lambda b, j: (b, j, 0))
    const = lambda a: pl.BlockSpec(a.shape, lambda b, j: (0, 0))
    out_specs, out_shape = _sublayer_out(bsz, ta, d)
    return pl.pallas_call(
        functools.partial(_outproj0_kernel, ctx_tiles=ctx_tiles),
        grid=(bsz, nt),
        in_specs=[tok(M_WIDTH), tok(S_WIDTH)] + _token_specs(d, ctx_tiles) + [
                  pl.BlockSpec((1, 1, 6, d), lambda b, j: (b, (j >= ctx_tiles).astype(jnp.int32), 0, 0)),
                  const(w_out), const(ln_g), const(ln_b), const(router_wt), const(router_b)],
        out_specs=out_specs,
        out_shape=out_shape,
        compiler_params=_cparams(("arbitrary", "arbitrary")),
        name="outproj0",
    )(ym, ys, ctx, x, modsel, w_out, ln_g, ln_b, router_wt, router_b)


def _top2(vals, probs):
    v1, i1, p1 = vals[0], jnp.zeros_like(vals[0], dtype=jnp.int32), probs[0]
    for i in range(1, len(vals)):
        better = vals[i] > v1
        v1 = jnp.where(better, vals[i], v1)
        i1 = jnp.where(better, i, i1)
        p1 = jnp.where(better, probs[i], p1)
    v2 = jnp.full_like(vals[0], -jnp.inf)
    i2 = jnp.zeros_like(i1)
    p2 = jnp.zeros_like(p1)
    for i in range(len(vals)):
        better = jnp.logical_and(i1 != i, vals[i] > v2)
        v2 = jnp.where(better, vals[i], v2)
        i2 = jnp.where(better, i, i2)
        p2 = jnp.where(better, probs[i], p2)
    return v1, i1, p1, v2, i2, p2


def _split_hi_lo(x):
    hi = x.astype(MXU_DTYPE)
    return hi, (x - hi.astype(F32)).astype(MXU_DTYPE)


def _route(h, w_ref, b_ref):
    h_hi, h_lo = _split_hi_lo(h)
    w_hi, w_lo = _split_hi_lo(w_ref[...])
    lg = (jnp.dot(h_hi, w_hi, preferred_element_type=F32) + jnp.dot(h_lo, w_hi, preferred_element_type=F32)
          + jnp.dot(h_hi, w_lo, preferred_element_type=F32))
    logits = lg.T[:N_EXPERTS]
    mx = jnp.max(logits, axis=0, keepdims=True)
    ex = jnp.exp(logits - mx)
    probs = ex / jnp.sum(ex, axis=0, keepdims=True)
    sel = probs + b_ref[...]
    best = None
    for g in range(N_EXPERT_GROUPS):
        rows = range(g * EXPERTS_PER_GROUP, (g + 1) * EXPERTS_PER_GROUP)
        v1, i1, p1, v2, i2, p2 = _top2([sel[r:r + 1] for r in rows], [probs[r:r + 1] for r in rows])
        cand = (v1 + v2, i1 + g * EXPERTS_PER_GROUP, p1, i2 + g * EXPERTS_PER_GROUP, p2)
        if best is None:
            best = cand
        else:
            better = cand[0] > best[0]
            best = tuple(jnp.where(better, c, o) for c, o in zip(cand, best))
    _, e1, p1, e2, p2 = best
    tot = p1 + p2
    return jnp.concatenate([e1, e2], axis=0), jnp.concatenate([p1 / tot, p2 / tot], axis=0)


def _row_tile(ref, r):
    return ref.at[pl.ds(pl.multiple_of(r * ROW_CHUNKS, ROW_CHUNKS), ROW_CHUNKS)]


def _dispatch_kernel(dest_ref, zblk_ref, h_ref, xs_ref, zero_s, sem, zsem, *, n_tok):
    base = pl.program_id(0) * DISPATCH_ROWS
    blk_rows = MOE_BLK * ROW_CHUNKS

    @pl.when(pl.program_id(0) == 0)
    def _():
        zero_s[...] = jnp.zeros_like(zero_s)

        def zero_copy(t):
            start = pl.multiple_of(zblk_ref[t] * blk_rows, blk_rows)
            return pltpu.make_async_copy(zero_s, xs_ref.at[pl.ds(start, blk_rows)], zsem)

        for t in range(2 * N_EXPERTS):
            pl.when(zblk_ref[t] >= 0)(lambda t=t: zero_copy(t).start())
        for t in range(2 * N_EXPERTS):
            pl.when(zblk_ref[t] >= 0)(lambda t=t: zero_copy(t).wait())

    def slot_copy(r, slot):
        return pltpu.make_async_copy(_row_tile(h_ref, r), _row_tile(xs_ref, slot), sem)

    def issue(r, carry):
        for c in range(TOP_K):
            slot_copy(r, dest_ref[c * n_tok + base + r]).start(priority=c)
        return carry

    def drain(r, carry):
        for c in range(TOP_K):
            slot_copy(r, 0).wait()
        return carry

    lax.fori_loop(0, DISPATCH_ROWS, issue, 0, unroll=8)
    lax.fori_loop(0, DISPATCH_ROWS, drain, 0, unroll=8)


def dispatch_call(dest, zero_blocks, h2t, n_slots):
    n_tok = h2t.shape[0] // ROW_CHUNKS
    return pl.pallas_call(
        functools.partial(_dispatch_kernel, n_tok=n_tok),
        grid_spec=pltpu.PrefetchScalarGridSpec(
            num_scalar_prefetch=2,
            grid=(n_tok // DISPATCH_ROWS,),
            in_specs=[pl.BlockSpec((DISPATCH_ROWS * ROW_CHUNKS, LANES), lambda i, dest, zb: (i, 0))],
            out_specs=pl.BlockSpec(memory_space=pl.ANY),
            scratch_shapes=[pltpu.VMEM((MOE_BLK * ROW_CHUNKS, LANES), h2t.dtype),
                            pltpu.SemaphoreType.DMA(()), pltpu.SemaphoreType.DMA(())]),
        out_shape=jax.ShapeDtypeStruct((n_slots * ROW_CHUNKS, LANES), h2t.dtype),
        compiler_params=_cparams(("arbitrary",)),
        name="dispatch",
    )(dest, zero_blocks, h2t)


def _experts_kernel(be_ref, cnt_ref, x_ref, wg_ref, wu_ref, wd_ref, o_ref, wg_s, wu_s, wd_s):
    i = pl.program_id(0)
    e = be_ref[i]
    e_before = be_ref[jnp.maximum(i - 1, 0)]
    cnt = cnt_ref[i]

    @pl.when(jnp.logical_or(i == 0, e != e_before))
    def _():
        wg_s[...] = wg_ref[0, 0].astype(wg_s.dtype)
        wu_s[...] = wu_ref[0, 0].astype(wu_s.dtype)
        wd_s[...] = wd_ref[0, 0].astype(wd_s.dtype)

    @pl.when(cnt > 0)
    def _():
        xb = _load_row_tiles(x_ref, MOE_BLK).astype(MXU_DTYPE)
        gt = jnp.dot(xb, wg_s[...], preferred_element_type=F32)
        up = jnp.dot(xb, wu_s[...], preferred_element_type=F32)
        y = jnp.dot((_silu(gt) * up).astype(MXU_DTYPE), wd_s[...], preferred_element_type=F32)
        _store_row_tiles(o_ref, y)

    @pl.when(cnt == 0)
    def _():
        o_ref[...] = jnp.zeros_like(o_ref)


def experts_call(blk_exp, blk_cnt, xs, w_gate, w_up, w_down, layer):
    n_slots = xs.shape[0] // ROW_CHUNKS
    d, f = w_gate.shape[-2:]
    wspec = lambda a: pl.BlockSpec((1, 1) + a.shape[2:], lambda i, be, cnt: (layer, be[i], 0, 0))
    blk = pl.BlockSpec((MOE_BLK * ROW_CHUNKS, LANES), lambda i, be, cnt: (i, 0))
    return pl.pallas_call(
        _experts_kernel,
        grid_spec=pltpu.PrefetchScalarGridSpec(
            num_scalar_prefetch=2,
            grid=(n_slots // MOE_BLK,),
            in_specs=[blk, wspec(w_gate), wspec(w_up), wspec(w_down)],
            out_specs=blk,
            scratch_shapes=[pltpu.VMEM((d, f), MXU_DTYPE), pltpu.VMEM((d, f), MXU_DTYPE),
                            pltpu.VMEM((f, d), MXU_DTYPE)]),
        out_shape=jax.ShapeDtypeStruct(xs.shape, F32),
        compiler_params=_cparams(("arbitrary",)),
        name="experts",
    )(blk_exp, blk_cnt, xs, w_gate, w_up, w_down)


def _combined_tile(dest_ref, y_ref, gate_ref, x_ref, mod_ref, lng_ref, lnb_ref, ybuf, sems, *, n_tok, nt, n_steps):
    step = pl.program_id(0) * nt + pl.program_id(1)
    slot = step % 2

    def row_copy(buf, r, c, src_slot):
        return pltpu.make_async_copy(_row_tile(y_ref, src_slot), _row_tile(ybuf.at[buf, c], r), sems.at[buf])

    def issue(tile, buf):
        base = tile * TM

        def body(r, carry):
            for c in range(TOP_K):
                row_copy(buf, r, c, dest_ref[c * n_tok + base + r]).start(priority=c)
            return carry

        lax.fori_loop(0, TM, body, 0, unroll=8)

    def drain(buf):
        def body(r, carry):
            for c in range(TOP_K):
                row_copy(buf, r, c, 0).wait()
            return carry

        lax.fori_loop(0, TM, body, 0, unroll=8)

    pl.when(step == 0)(lambda: issue(0, 0))
    pl.when(step + 1 < n_steps)(lambda: issue(step + 1, 1 - slot))
    drain(slot)
    mod = mod_ref[0, 0]
    gate = gate_ref[...]
    y = (gate[:, 0:1] * _load_row_tiles(ybuf.at[slot, 0], TM)
         + gate[:, 1:2] * _load_row_tiles(ybuf.at[slot, 1], TM))
    return _layer_norm_rows(DEEPNORM_ALPHA * x_ref[0] + mod[5:6] * y, lng_ref[...], lnb_ref[...])


def _combine_kernel(dest_ref, y_ref, gate_ref, x_ref, mod_ref, lng_ref, lnb_ref, o_ref, ybuf, sems, **kw):
    o_ref[0] = _combined_tile(dest_ref, y_ref, gate_ref, x_ref, mod_ref, lng_ref, lnb_ref, ybuf, sems, **kw)


def _combine_inproj1_kernel(dest_ref, y_ref, gate_ref, x_ref, mod_ref, lng_ref, lnb_ref,
                            mod1_ref, w_ref, qg_ref, kg_ref, cos_ref, sin_ref,
                            o_ref, q_ref, k_ref, v_ref, ybuf, sems, **kw):
    x = _combined_tile(dest_ref, y_ref, gate_ref, x_ref, mod_ref, lng_ref, lnb_ref, ybuf, sems, **kw)
    o_ref[0] = x
    _qkv_project(x, mod1_ref[0, 0], w_ref, qg_ref, kg_ref, cos_ref, sin_ref, q_ref, k_ref, v_ref)


def combine_call(dest, y_slots, gate_cols, x1, modsel, ln_g, ln_b, ctx_tiles, proj=None):
    bsz, tt, d = x1.shape
    nt = tt // TM
    tok = lambda w: pl.BlockSpec((1, TM, w), lambda b, j, dest: (b, j, 0))
    const = lambda a: pl.BlockSpec(a.shape, lambda b, j, dest: (0, 0))
    modspec = pl.BlockSpec((1, 1, 6, d), lambda b, j, dest: (b, (j >= ctx_tiles).astype(jnp.int32), 0, 0))
    in_specs = [pl.BlockSpec(memory_space=pl.ANY),
                pl.BlockSpec((TM, TOP_K), lambda b, j, dest: (b * nt + j, 0)),
                tok(d), modspec, const(ln_g), const(ln_b)]
    args = [dest, y_slots, gate_cols, x1, modsel, ln_g, ln_b]
    out_specs = [tok(d)]
    out_shape = [jax.ShapeDtypeStruct((bsz, tt, d), F32)]
    body = _combine_kernel
    if proj is not None:
        modsel1, w_in, q_g, k_g, cos2, sin2 = proj
        rope = pl.BlockSpec((TM, A_HEAD_DIM), lambda b, j, dest: (j, 0))
        in_specs += [modspec, const(w_in), const(q_g), const(k_g), rope, rope]
        args += [modsel1, w_in, q_g, k_g, cos2, sin2]
        widths = (A_Q_W, A_KV_W, A_KV_W)
        out_specs += [tok(w) for w in widths]
        out_shape += [jax.ShapeDtypeStruct((bsz, tt, w), MXU_DTYPE) for w in widths]
        body = _combine_inproj1_kernel
    res = pl.pallas_call(
        functools.partial(body, n_tok=bsz * tt, nt=nt, n_steps=bsz * nt),
        grid_spec=pltpu.PrefetchScalarGridSpec(
            num_scalar_prefetch=1,
            grid=(bsz, nt),
            in_specs=in_specs,
            out_specs=out_specs,
            scratch_shapes=[pltpu.VMEM((2, TOP_K, TM * ROW_CHUNKS, LANES), F32), pltpu.SemaphoreType.DMA((2,))]),
        out_shape=out_shape,
        compiler_params=_cparams(("arbitrary", "arbitrary")),
        name="combine" if proj is None else "combine_inproj1",
    )(*args)
    return res[0] if proj is None else res


def moe_block(x1, h2t, e_idx, gates, modsel, ln_g, ln_b, w_gate, w_up, w_down, layer, ctx_tiles, proj=None):
    bsz, tt, d = x1.shape
    n = bsz * tt

    n_asg = TOP_K * n
    flat_e = e_idx.reshape(n_asg)
    onehot = (flat_e[:, None] == jnp.arange(N_EXPERTS, dtype=jnp.int32)[None, :]).astype(jnp.int32)
    csum = jnp.cumsum(onehot, axis=0)
    rank = jnp.sum(onehot * csum, axis=1) - 1
    counts = csum[-1]
    padded = (counts + MOE_BLK - 1) // MOE_BLK * MOE_BLK
    pend = jnp.cumsum(padded)
    pstart = pend - padded
    dest = (pstart[flat_e] + rank).astype(jnp.int32)
    n_blocks = -(-n_asg // MOE_BLK) + N_EXPERTS
    blk_start = jnp.arange(n_blocks, dtype=jnp.int32) * MOE_BLK
    blk_exp = jnp.sum((pend[None, :] <= blk_start[:, None]).astype(jnp.int32), axis=1)
    blk_exp = jnp.minimum(blk_exp, N_EXPERTS - 1)
    blk_cnt = jnp.clip(counts[blk_exp] - (blk_start - pstart[blk_exp]), 0, MOE_BLK).astype(jnp.int32)

    part = jnp.where(counts % MOE_BLK != 0, pend // MOE_BLK - 1, -1)
    tail = pend[-1] // MOE_BLK + jnp.arange(N_EXPERTS, dtype=jnp.int32)
    tail = jnp.where(tail < n_blocks, tail, -1)
    zero_blocks = jnp.concatenate([part, tail]).astype(jnp.int32)

    xs = dispatch_call(dest, zero_blocks, h2t, n_blocks * MOE_BLK)
    y_slots = experts_call(blk_exp, blk_cnt, xs, w_gate, w_up, w_down, layer)
    return combine_call(dest, y_slots, gates.T, x1, modsel, ln_g, ln_b, ctx_tiles, proj)


def _qkv_project(x, mod, w_ref, qg_ref, kg_ref, cos_ref, sin_ref, q_ref, k_ref, v_ref):
    hb = (x * (1.0 + mod[1:2]) + mod[0:1]).astype(MXU_DTYPE)
    cos = cos_ref[...]
    sin = sin_ref[...]

    def norm_rope(t, g):
        t = t * lax.rsqrt(jnp.mean(t * t, axis=-1, keepdims=True) + RMS_EPS) * g
        return t * cos + pltpu.roll(t, A_HEAD_DIM // 2, 1) * sin

    qkv = jnp.dot(hb, w_ref[...], preferred_element_type=F32)
    for j in range(A_HEADS):
        sl = slice(j * A_HEAD_DIM, (j + 1) * A_HEAD_DIM)
        q_ref[0, :, sl] = (norm_rope(qkv[:, sl], qg_ref[...]) * ATT_Q_SCALE).astype(q_ref.dtype)
    for j in range(A_KV_HEADS):
        sl = slice(j * A_HEAD_DIM, (j + 1) * A_HEAD_DIM)
        t = qkv[:, A_Q_W + j * A_HEAD_DIM:A_Q_W + (j + 1) * A_HEAD_DIM]
        k_ref[0, :, sl] = norm_rope(t, kg_ref[...]).astype(k_ref.dtype)
    v_ref[0] = qkv[:, A_Q_W + A_KV_W:].astype(v_ref.dtype)


def _attn_kernel(q_ref, qn_ref, k_ref, v_ref, o_ref, s_buf):
    rep = A_HEADS // A_KV_HEADS
    n_kv = k_ref.shape[1] // ATT_TK
    assert n_kv % 2 == 0

    def transposed(ref):
        return [ref[0, :, r * A_HEAD_DIM:(r + 1) * A_HEAD_DIM].astype(F32).T.astype(MXU_DTYPE) for r in range(rep)]

    qts = transposed(q_ref)

    def scores(j, slot, qt=qts):
        kb = k_ref[0, pl.ds(pl.multiple_of(j * ATT_TK, ATT_TK), ATT_TK), :]
        for r in range(rep):
            s_buf[slot, r] = jnp.dot(kb, qt[r], preferred_element_type=F32)

    def as_rows(row):
        return jnp.broadcast_to(row, (A_HEAD_DIM, row.shape[1])).T

    def consume(j, slot, stats):
        vb = v_ref[0, pl.ds(pl.multiple_of(j * ATT_TK, ATT_TK), ATT_TK), :]
        ml, acc = stats
        new, ps, alphas = [], [], []
        for r in range(rep):
            m, l = ml[r]
            s = s_buf[slot, r]
            m_new = jnp.maximum(m, jnp.max(s, axis=0, keepdims=True))
            p = jnp.exp2(s - m_new)
            alpha = jnp.exp2(m - m_new)
            new.append((m_new, alpha * l + jnp.sum(p, axis=0, keepdims=True)))
            ps.append(p.astype(vb.dtype))
            alphas.append(alpha)
        pv = lax.dot_general(jnp.concatenate(ps, axis=1), vb, (((0,), (0,)), ((), ())),
                             preferred_element_type=F32)
        return tuple(new), as_rows(jnp.concatenate(alphas, axis=1)) * acc + pv

    def pair(i, stats):
        j = 2 * i
        scores(j + 1, 1)
        stats = consume(j, 0, stats)
        scores(j + 2, 0)
        return consume(j + 1, 1, stats)

    stats = (tuple((jnp.full((1, ATT_TQ), -jnp.inf, F32), jnp.zeros((1, ATT_TQ), F32)) for _ in range(rep)),
             jnp.zeros((rep * ATT_TQ, A_HEAD_DIM), F32))
    pl.when(pl.program_id(2) == 0)(lambda: scores(0, 0))
    stats = lax.fori_loop(0, n_kv // 2 - 1, pair, stats)
    scores(n_kv - 1, 1)
    stats = consume(n_kv - 2, 0, stats)
    scores(0, 0, transposed(qn_ref))
    ml, acc = consume(n_kv - 1, 1, stats)
    out = acc / as_rows(jnp.concatenate([l for _, l in ml], axis=1))
    for r in range(rep):
        o_ref[0, :, r * A_HEAD_DIM:(r + 1) * A_HEAD_DIM] = out[r * ATT_TQ:(r + 1) * ATT_TQ].astype(o_ref.dtype)


def attn_call(q, k, v, n_ctx):
    bsz, ta, _ = q.shape
    t_lat = ta - n_ctx
    gw = (A_HEADS // A_KV_HEADS) * A_HEAD_DIM
    q_off = n_ctx // ATT_TQ
    nq = t_lat // ATT_TQ
    return pl.pallas_call(
        _attn_kernel,
        grid=(bsz, A_KV_HEADS, nq),
        in_specs=[pl.BlockSpec((1, ATT_TQ, gw), lambda b, g, i: (b, i + q_off, g)),
                  pl.BlockSpec((1, ATT_TQ, gw), lambda b, g, i: (b, jnp.minimum(i + 1, nq - 1) + q_off, g)),
                  pl.BlockSpec((1, ta, A_HEAD_DIM), lambda b, g, i: (b, 0, g)),
                  pl.BlockSpec((1, ta, A_HEAD_DIM), lambda b, g, i: (b, 0, g))],
        out_specs=pl.BlockSpec((1, ATT_TQ, gw), lambda b, g, i: (b, i, g)),
        out_shape=jax.ShapeDtypeStruct((bsz, t_lat, A_Q_W), MXU_DTYPE),
        scratch_shapes=[pltpu.VMEM((2, A_HEADS // A_KV_HEADS, ATT_TK, ATT_TQ), F32)],
        compiler_params=_cparams(("arbitrary", "arbitrary", "arbitrary")),
        name="attention",
    )(q, q, k, v)


def _outproj1_kernel(a_ref, x_ref, mod_ref, w_ref, lng_ref, lnb_ref, rw_ref, rb_ref, x1_ref, h2_ref, e_ref, g_ref):
    y = jnp.dot(a_ref[0], w_ref[...], preferred_element_type=F32)
    _finish_sublayer(x_ref[0], y, mod_ref[0, 0], lng_ref, lnb_ref, rw_ref, rb_ref, x1_ref, h2_ref, e_ref, g_ref)


def outproj1_call(att, xa, modsel, w_out, ln_g, ln_b, router_wt, router_b, ctx_tiles):
    bsz, t_lat, _ = att.shape
    d = xa.shape[-1]
    const = lambda a: pl.BlockSpec(a.shape, lambda b, j: (0, 0))
    out_specs, out_shape = _sublayer_out(bsz, t_lat, d)
    return pl.pallas_call(
        _outproj1_kernel,
        grid=(bsz, t_lat // TM),
        in_specs=[pl.BlockSpec((1, TM, att.shape[-1]), lambda b, j: (b, j, 0)),
                  pl.BlockSpec((1, TM, d), lambda b, j: (b, j + ctx_tiles, 0)),
                  pl.BlockSpec((1, 1, 6, d), lambda b, j: (b, 1, 0, 0)),
                  const(w_out), const(ln_g), const(ln_b), const(router_wt), const(router_b)],
        out_specs=out_specs,
        out_shape=out_shape,
        compiler_params=_cparams(("arbitrary", "arbitrary")),
        name="outproj1",
    )(att, xa, modsel, w_out, ln_g, ln_b, router_wt, router_b)


def _rope_tables(n_ctx, n_lat):
    rows = n_lat // GRID_W
    row = jnp.repeat(jnp.arange(rows), GRID_W).astype(F32)
    col = jnp.tile(jnp.arange(GRID_W), rows).astype(F32)
    n_freq = A_HEAD_DIM // 4
    inv = ROPE_THETA ** (-jnp.arange(n_freq, dtype=F32) / n_freq)
    ang = jnp.concatenate([row[:, None] * inv, col[:, None] * inv], -1)
    cos, sin = jnp.cos(ang), jnp.sin(ang)
    cos2 = jnp.concatenate([cos, cos], -1)
    sin2 = jnp.concatenate([-sin, sin], -1)
    cos2 = jnp.concatenate([jnp.ones((n_ctx, A_HEAD_DIM), F32), cos2], 0)
    sin2 = jnp.concatenate([jnp.zeros((n_ctx, A_HEAD_DIM), F32), sin2], 0)
    return cos2, sin2


def _lane_row(parts, width=GATE_LANES):
    row = jnp.concatenate([p.reshape(-1).astype(F32) for p in parts])
    return jnp.pad(row, (0, width - row.shape[0])).reshape(1, width)


def kernel(x, c, ctx, c_ctx, ada_w, ada_b, ln_g, ln_b, ab_w_in, ab_w_out, ml_ig_b, ml_fg_b, ml_norm_g,
           ssm_conv_w, ssm_conv_b, ssm_dt_b, ssm_a_log, ssm_d, ssm_norm_g, at_w_in, at_w_out, at_q_g, at_k_g,
           router_w, router_b, moe_w_gate, moe_w_up, moe_w_down):
    bsz, n_lat, d = x.shape
    n_ctx = ctx.shape[1]
    assert d == D_MODEL and bsz + 1 <= SUBLANES
    assert n_ctx % TM == 0 and n_lat % TM == 0 and n_lat % GRID_W == 0
    assert (n_ctx + n_lat) % ATT_TK == 0 and n_ctx % ATT_TQ == 0 and n_lat % ATT_TQ == 0
    assert (bsz * (n_ctx + n_lat)) % DISPATCH_ROWS == 0 and (bsz * n_lat) % DISPATCH_ROWS == 0
    ctx_tiles = n_ctx // TM
    ctx_chunks = n_ctx // CHUNK

    crows = jnp.zeros((SUBLANES, d), F32).at[:bsz].set(c).at[bsz].set(c_ctx)
    mods = ada_call(crows, ada_w, ada_b)

    def mod_table(i):
        lat = mods[i, :bsz].reshape(bsz, 1, 6, d)
        cx = jnp.broadcast_to(mods[i, bsz].reshape(1, 1, 6, d), (bsz, 1, 6, d))
        return jnp.concatenate([cx, lat], axis=1)

    router_wt = jnp.pad(router_w, ((0, 0), (0, LANES - N_EXPERTS)))
    router_bc = router_b.reshape(N_EXPERTS, 1)

    modsel = mod_table(0)
    w_in = ab_w_in[0]
    s_q, s_k, s_v, s_o, s_ig, s_fg, s_z, s_xbc = (int(v) for v in
        (0, 512, 1024, 2048, 3072, 3072 + 8, 3072 + 16, 3072 + 16 + 1024))
    s_dt = s_xbc + S_CONV_CH
    w_big = jnp.concatenate([w_in[:, :s_ig], w_in[:, s_z:s_dt]], axis=1).astype(MXU_DTYPE)
    w_small = jnp.concatenate([w_in[:, s_ig:s_z], w_in[:, s_dt:]], axis=1)
    w_small = jnp.pad(w_small, ((0, 0), (0, GATE_LANES - w_small.shape[1])))
    q, k, v, o, z, xbc, gates = inproj0_call(ctx, x, modsel, w_big, w_small, ctx_tiles)
    xbc_act = conv_call(xbc, ssm_conv_w[0], ssm_conv_b[0], ctx_tiles)

    grow = jnp.swapaxes(gates[:, :, :DT_OFF], 1, 2)
    dtrow = jnp.swapaxes(gates[:, :, DT_OFF:DT_OFF + N_DIR * S_HEADS], 1, 2)
    gate_b_row = _lane_row([ml_ig_b[0], ml_fg_b[0], ssm_dt_b[0]])
    gate_b_col = jnp.concatenate([ml_ig_b[0].reshape(-1), ml_fg_b[0].reshape(-1)]).reshape(-1, 1)
    a_neg = -jnp.exp(ssm_a_log[0].astype(F32)).reshape(-1)
    an_row = _lane_row([jnp.zeros((DT_OFF,), F32), a_neg])
    an_col = a_neg.reshape(-1, 1)
    dtb_col = ssm_dt_b[0].reshape(-1, 1)

    dskip = jnp.repeat(ssm_d[0].astype(F32), S_HEAD_DIM).reshape(1, S_WIDTH)
    m_norm_g = ml_norm_g[0].reshape(1, -1)
    s_norm_g = ssm_norm_g[0].reshape(1, -1)
    assert N_DIR == 2
    hm = None
    hs = None
    for dd in range(N_DIR):
        hm = mlstm_call(q, k, v, gates, grow, gate_b_row, gate_b_col, hm, o, m_norm_g, rev=dd == 1, d=dd,
                        ctx_chunks=ctx_chunks)
        hs = ssd_call(xbc_act, gates, dtrow, gate_b_row, dtb_col, an_row, an_col, hs, z, dskip, s_norm_g,
                      rev=dd == 1, d=dd, ctx_chunks=ctx_chunks)

    x1, h2t, e_idx, gates = outproj0_call(hm, hs, ctx, x, modsel, ab_w_out[0].astype(MXU_DTYPE),
                                          ln_g[0, 0].reshape(1, d), ln_b[0, 0].reshape(1, d), router_wt, router_bc,
                                          ctx_tiles)
    modsel1 = mod_table(1)
    cos2, sin2 = _rope_tables(n_ctx, n_lat)
    proj = (modsel1, at_w_in[0].astype(MXU_DTYPE), at_q_g[0].reshape(1, -1), at_k_g[0].reshape(1, -1), cos2, sin2)
    xa, qa, ka, va = moe_block(x1, h2t, e_idx, gates, modsel, ln_g[0, 1].reshape(1, d), ln_b[0, 1].reshape(1, d),
                               moe_w_gate, moe_w_up, moe_w_down, 0, ctx_tiles, proj)
    modsel = modsel1
    att = attn_call(qa, ka, va, n_ctx)
    x1, h2t, e_idx, gates = outproj1_call(att, xa, modsel, at_w_out[0].astype(MXU_DTYPE), ln_g[1, 0].reshape(1, d),
                                          ln_b[1, 0].reshape(1, d), router_wt, router_bc, ctx_tiles)
    return moe_block(x1, h2t, e_idx, gates, modsel, ln_g[1, 1].reshape(1, d), ln_b[1, 1].reshape(1, d),
                     moe_w_gate, moe_w_up, moe_w_down, 1, 0)
```

```python
import functools
import math

import jax
import jax.numpy as jnp
from jax import lax
from jax.experimental import pallas as pl
from jax.experimental.pallas import tpu as pltpu

F32 = jnp.float32
MXU_DTYPE = jnp.bfloat16
HIGHEST = lax.Precision.HIGHEST

D_MODEL = 1024
DEPTH = 2
GRID_W = 64
CHUNK = 128
M_HEADS = 4
M_QK_DIM = D_MODEL // 8
M_V_DIM = D_MODEL // 4
M_WIDTH = M_HEADS * M_V_DIM
S_HEADS = 16
S_HEAD_DIM = D_MODEL // 16
S_GROUPS = 2
S_HEADS_PER_GROUP = S_HEADS // S_GROUPS
S_STATE = 128
S_WIDTH = S_HEADS * S_HEAD_DIM
S_CONV_CH = S_WIDTH + 2 * S_GROUPS * S_STATE
CONV_K = 4
N_DIR = 2
A_HEADS = 8
A_KV_HEADS = 2
A_HEAD_DIM = D_MODEL // A_HEADS
A_Q_W = A_HEADS * A_HEAD_DIM
A_KV_W = A_KV_HEADS * A_HEAD_DIM
ROPE_THETA = 10000.0
N_EXPERTS = 16
N_EXPERT_GROUPS = 4
EXPERTS_PER_GROUP = N_EXPERTS // N_EXPERT_GROUPS
TOP_K = 2
DEEPNORM_ALPHA = (2 * DEPTH) ** 0.25
LN_EPS = 1e-5
RMS_EPS = 1e-6

LANES = 128
SUBLANES = 8
TM = 256
MOE_BLK = 512
DISPATCH_ROWS = 512
ROW_CHUNKS = D_MODEL // LANES
assert ROW_CHUNKS == SUBLANES
ATT_TQ = 256
ATT_TK = 4224
VMEM_LIMIT = 56 * 1024 * 1024
GATE_LANES = 128
IG_OFF, FG_OFF, DT_OFF = 0, N_DIR * M_HEADS, 2 * N_DIR * M_HEADS
ATT_Q_SCALE = A_HEAD_DIM ** -0.5 * math.log2(math.e)


def _cparams(sem):
    return pltpu.CompilerParams(dimension_semantics=sem, vmem_limit_bytes=VMEM_LIMIT)


def _silu(x):
    return x / (1.0 + jnp.exp(-x))


def _sigmoid(x):
    return 1.0 / (1.0 + jnp.exp(-x))


def _softplus(x):
    return jnp.maximum(x, 0.0) + jnp.log(1.0 + jnp.exp(-jnp.abs(x)))


def _log_sigmoid(x):
    return jnp.minimum(x, 0.0) - jnp.log(1.0 + jnp.exp(-jnp.abs(x)))


def _layer_norm_rows(x, g, b):
    mu = jnp.mean(x, axis=-1, keepdims=True)
    xc = x - mu
    var = jnp.mean(xc * xc, axis=-1, keepdims=True)
    return xc * lax.rsqrt(var + LN_EPS) * g + b


def _mm(a, b):
    return jnp.dot(a.astype(MXU_DTYPE), b.astype(MXU_DTYPE), preferred_element_type=F32)


def _mm_nt(a, b):
    return lax.dot_general(a.astype(MXU_DTYPE), b.astype(MXU_DTYPE), (((1,), (1,)), ((), ())),
                           preferred_element_type=F32)


def _mm_tn(a, b):
    return lax.dot_general(a.astype(MXU_DTYPE), b.astype(MXU_DTYPE), (((0,), (0,)), ((), ())),
                           preferred_element_type=F32)


def _ada_kernel(c_ref, w_ref, b_ref, o_ref):
    s = _silu(c_ref[...])
    o_ref[0] = jnp.dot(s, w_ref[0], precision=HIGHEST, preferred_element_type=F32) + b_ref[0]


def ada_call(crows, ada_w, ada_b):
    depth, d, n6 = ada_w.shape
    tn = 1536
    return pl.pallas_call(
        _ada_kernel,
        grid=(depth, n6 // tn),
        in_specs=[pl.BlockSpec((SUBLANES, d), lambda i, j: (0, 0)),
                  pl.BlockSpec((1, d, tn), lambda i, j: (i, 0, j)),
                  pl.BlockSpec((1, 1, tn), lambda i, j: (i, 0, j))],
        out_specs=pl.BlockSpec((1, SUBLANES, tn), lambda i, j: (i, 0, j)),
        out_shape=jax.ShapeDtypeStruct((depth, SUBLANES, n6), F32),
        compiler_params=_cparams(("arbitrary", "arbitrary")),
        name="ada",
    )(crows, ada_w, ada_b.reshape(depth, 1, n6))


def _token_tile(ctx_ref, x_ref, ctx_tiles):
    return jnp.where(pl.program_id(1) < ctx_tiles, ctx_ref[0], x_ref[0])


def _token_specs(d, ctx_tiles):
    return [pl.BlockSpec((1, TM, d), lambda b, j: (b, jnp.minimum(j, ctx_tiles - 1), 0)),
            pl.BlockSpec((1, TM, d), lambda b, j: (b, jnp.maximum(j - ctx_tiles, 0), 0))]


def _inproj0_kernel(ctx_ref, x_ref, mod_ref, wb_ref, ws_ref, q_ref, k_ref, v_ref, o_ref, z_ref, xbc_ref, g_ref, *,
                    ctx_tiles):
    x = _token_tile(ctx_ref, x_ref, ctx_tiles)
    mod = mod_ref[0, 0]
    h = x * (1.0 + mod[1:2]) + mod[0:1]
    hb = h.astype(MXU_DTYPE)
    qk = M_HEADS * M_QK_DIM
    c0 = 0
    q_ref[0] = (jnp.dot(hb, wb_ref[:, c0:c0 + qk], preferred_element_type=F32)
                * (M_QK_DIM ** -0.5)).astype(q_ref.dtype)
    c0 += qk
    k_ref[0] = jnp.dot(hb, wb_ref[:, c0:c0 + qk], preferred_element_type=F32).astype(k_ref.dtype)
    c0 += qk
    v_ref[0] = jnp.dot(hb, wb_ref[:, c0:c0 + M_WIDTH], preferred_element_type=F32).astype(v_ref.dtype)
    c0 += M_WIDTH
    o_ref[0] = jnp.dot(hb, wb_ref[:, c0:c0 + M_WIDTH], preferred_element_type=F32)
    c0 += M_WIDTH
    z_ref[0] = jnp.dot(hb, wb_ref[:, c0:c0 + S_WIDTH], preferred_element_type=F32)
    c0 += S_WIDTH
    xbc_ref[0] = jnp.dot(hb, wb_ref[:, c0:c0 + S_CONV_CH], preferred_element_type=F32)
    h_hi, h_lo = _split_hi_lo(h)
    w_hi, w_lo = _split_hi_lo(ws_ref[...])
    g_ref[0] = (jnp.dot(h_hi, w_hi, preferred_element_type=F32) + jnp.dot(h_lo, w_hi, preferred_element_type=F32)
                + jnp.dot(h_hi, w_lo, preferred_element_type=F32))


def inproj0_call(ctx, x, modsel, w_big, w_small, ctx_tiles):
    bsz, _, d = x.shape
    ta = ctx.shape[1] + x.shape[1]
    nt = ta // TM
    qk = M_HEADS * M_QK_DIM
    widths = (qk, qk, M_WIDTH, M_WIDTH, S_WIDTH, S_CONV_CH, GATE_LANES)
    dtypes = (MXU_DTYPE, MXU_DTYPE, MXU_DTYPE, F32, F32, F32, F32)
    tok = lambda w: pl.BlockSpec((1, TM, w), lambda b, j: (b, j, 0))
    return pl.pallas_call(
        functools.partial(_inproj0_kernel, ctx_tiles=ctx_tiles),
        grid=(bsz, nt),
        in_specs=_token_specs(d, ctx_tiles) + [
                  pl.BlockSpec((1, 1, 6, d), lambda b, j: (b, (j >= ctx_tiles).astype(jnp.int32), 0, 0)),
                  pl.BlockSpec(w_big.shape, lambda b, j: (0, 0)),
                  pl.BlockSpec(w_small.shape, lambda b, j: (0, 0))],
        out_specs=[tok(w) for w in widths],
        out_shape=[jax.ShapeDtypeStruct((bsz, ta, w), dt) for w, dt in zip(widths, dtypes)],
        compiler_params=_cparams(("arbitrary", "arbitrary")),
        name="inproj0",
    )(ctx, x, modsel, w_big, w_small)


def _conv_kernel(cur_ref, prev_ref, next_ref, w_ref, b_ref, o_ref, *, ctx_tiles, n_tiles):
    j = pl.program_id(1)
    has_prev = jnp.logical_and(j != 0, j != ctx_tiles)
    has_next = jnp.logical_and(j != ctx_tiles - 1, j != n_tiles - 1)
    prev = jnp.where(has_prev, prev_ref[0], 0.0)
    nxt = jnp.where(has_next, next_ref[0], 0.0)
    ext = jnp.concatenate([prev, cur_ref[0], nxt], axis=0)
    n = TM + 2 * SUBLANES
    w = w_ref[...]
    lo, hi = SUBLANES, SUBLANES + TM
    acc = ext[lo:hi] * w[2:3]
    acc = acc + pltpu.roll(ext, 2, 0)[lo:hi] * w[0:1]
    acc = acc + pltpu.roll(ext, 1, 0)[lo:hi] * w[1:2]
    acc = acc + pltpu.roll(ext, n - 1, 0)[lo:hi] * w[3:4]
    o_ref[0] = _silu(acc + b_ref[...])


def conv_call(xbc, conv_w, conv_b, ctx_tiles):
    bsz, ta, ch = xbc.shape
    nt = ta // TM
    r = TM // SUBLANES
    last = ta // SUBLANES - 1
    return pl.pallas_call(
        functools.partial(_conv_kernel, ctx_tiles=ctx_tiles, n_tiles=nt),
        grid=(bsz, nt),
        in_specs=[pl.BlockSpec((1, TM, ch), lambda b, j: (b, j, 0)),
                  pl.BlockSpec((1, SUBLANES, ch), lambda b, j: (b, jnp.maximum(j * r - 1, 0), 0)),
                  pl.BlockSpec((1, SUBLANES, ch), lambda b, j: (b, jnp.minimum((j + 1) * r, last), 0)),
                  pl.BlockSpec((CONV_K, ch), lambda b, j: (0, 0)),
                  pl.BlockSpec((1, ch), lambda b, j: (0, 0))],
        out_specs=pl.BlockSpec((1, TM, ch), lambda b, j: (b, j, 0)),
        out_shape=jax.ShapeDtypeStruct((bsz, ta, ch), F32),
        compiler_params=_cparams(("arbitrary", "arbitrary")),
        name="conv",
    )(xbc, xbc, xbc, conv_w, conv_b.reshape(1, ch))


def _chunk_order(i, rev, ctx_chunks, n_chunks):
    if not rev:
        return i
    return jnp.where(i < ctx_chunks, ctx_chunks - 1 - i, n_chunks - 1 - (i - ctx_chunks))


def _scan_masks(rev):
    r = lax.broadcasted_iota(jnp.int32, (CHUNK, CHUNK), 0)
    c = lax.broadcasted_iota(jnp.int32, (CHUNK, CHUNK), 1)
    mask = (c >= r) if rev else (c <= r)
    mask_t = (r >= c) if rev else (r <= c)
    return mask, mask.astype(F32).astype(MXU_DTYPE), mask_t.astype(F32).astype(MXU_DTYPE)


def _split3(x):
    hi = x.astype(MXU_DTYPE)
    r = x - hi.astype(F32)
    mid = r.astype(MXU_DTYPE)
    lo = (r - mid.astype(F32)).astype(MXU_DTYPE)
    return hi, mid, lo


def _prefix_cols(tri, x):
    return sum(jnp.dot(tri, p, preferred_element_type=F32) for p in _split3(x))


def _prefix_rows(x, tri):
    return sum(jnp.dot(p, tri, preferred_element_type=F32) for p in _split3(x))


def _per_batch(body, kinds):
    def kern(*refs, **kw):
        assert len(refs) == len(kinds)
        for bi in range(refs[0].shape[0]):
            sub = [r.at[pl.ds(bi, 1)] if f == 'b' else r.at[bi] if f == 's' else r for r, f in zip(refs, kinds)]
            body(*sub, **kw)
    return kern


def _mlstm_kernel(*refs, rev, d, add_prev):
    if add_prev:
        (q_ref, k_ref, v_ref, gc_ref, gr_ref, brow_ref, bcol_ref, prev_ref, og_ref, mg_ref,
         o_ref, ct_s, n_s, m_s) = refs
    else:
        q_ref, k_ref, v_ref, gc_ref, gr_ref, brow_ref, bcol_ref, o_ref, ct_s, n_s, m_s = refs
        prev_ref = None

    @pl.when(pl.program_id(0) == 0)
    def _():
        ct_s[...] = jnp.zeros_like(ct_s)
        n_s[...] = jnp.zeros_like(n_s)
        m_s[...] = jnp.zeros_like(m_s)

    mask, mask_f, mask_tf = _scan_masks(rev)
    end = 0 if rev else CHUNK - 1
    gcol = gc_ref[0] + brow_ref[...]
    grow = gr_ref[0] + bcol_ref[...]
    nh2 = N_DIR * M_HEADS
    lf_col = _log_sigmoid(gcol)
    lf_row = _log_sigmoid(grow[FG_OFF:FG_OFF + nh2])
    b_col_all = _prefix_cols(mask_f, lf_col)
    b_row_all = _prefix_rows(lf_row, mask_tf)

    for j in range(M_HEADS):
        ci = d * M_HEADS + j
        bcol = b_col_all[:, FG_OFF + ci:FG_OFF + ci + 1]
        brow = b_row_all[ci:ci + 1, :]
        igcol = gcol[:, IG_OFF + ci:IG_OFF + ci + 1]
        igrow = grow[IG_OFF + ci:IG_OFF + ci + 1, :]
        b_end = bcol[end:end + 1, :]
        m_prev = m_s[j][:, 0:1]
        n_prev = n_s[j]
        ct_prev = ct_s[j]
        q = q_ref[0, :, j * M_QK_DIM:(j + 1) * M_QK_DIM]
        k = k_ref[0, :, j * M_QK_DIM:(j + 1) * M_QK_DIM]
        v = v_ref[0, :, j * M_V_DIM:(j + 1) * M_V_DIM]
        qf = q.astype(F32)
        kf = k.astype(F32)
        vf = v.astype(F32)

        dmat = jnp.where(mask, bcol - brow + igrow, -jnp.inf)
        inter = bcol + m_prev
        m_t = jnp.maximum(inter, jnp.max(dmat, axis=1, keepdims=True))
        sc = _mm_nt(q, k) * jnp.exp(dmat - m_t)
        a_in = jnp.exp(inter - m_t)
        num = _mm(sc, v) + a_in * _mm(q, ct_prev)
        den = jnp.sum(sc, axis=1, keepdims=True) + a_in * jnp.sum(qf * n_prev, axis=1, keepdims=True)
        h = num / jnp.maximum(jnp.abs(den), jnp.exp(-m_t))
        sl = slice(j * M_V_DIM, (j + 1) * M_V_DIM)
        if add_prev:
            h = h + prev_ref[0, :, sl]
            mu = jnp.mean(h, axis=-1, keepdims=True)
            hc = h - mu
            var = jnp.mean(hc * hc, axis=-1, keepdims=True)
            h = hc * lax.rsqrt(var + LN_EPS) * mg_ref[:, sl] * _sigmoid(og_ref[0, :, sl])
        o_ref[0, :, sl] = h.astype(o_ref.dtype)

        g_col = b_end - bcol + igcol
        g_row = b_end - brow + igrow
        g_max = jnp.max(g_row, axis=1, keepdims=True)
        w_col = jnp.exp(g_col - g_max)
        d_ct = _mm_tn(k, vf * w_col)
        d_n = jnp.sum(kf * w_col, axis=0, keepdims=True)
        m_new = jnp.maximum(b_end + m_prev, g_max)
        a = jnp.exp(b_end + m_prev - m_new)
        s = jnp.exp(g_max - m_new)
        ct_s[j] = a * ct_prev + s * d_ct
        n_s[j] = a * n_prev + s * d_n
        m_s[j] = jnp.broadcast_to(m_new, (1, LANES))


def mlstm_call(q, k, v, gcol, grow, bias_row, bias_col, prev, o_gate, m_norm_g, *, rev, d, ctx_chunks):
    bsz, ta, _ = q.shape
    nc = ta // CHUNK
    order = lambda i: _chunk_order(i, rev, ctx_chunks, nc)
    tok = lambda w: pl.BlockSpec((bsz, CHUNK, w), lambda i: (0, order(i), 0))
    in_specs = [tok(q.shape[-1]), tok(k.shape[-1]), tok(v.shape[-1]), tok(GATE_LANES),
                pl.BlockSpec((bsz, grow.shape[1], CHUNK), lambda i: (0, 0, order(i))),
                pl.BlockSpec(bias_row.shape, lambda i: (0, 0)),
                pl.BlockSpec(bias_col.shape, lambda i: (0, 0))]
    args = [q, k, v, gcol, grow, bias_row, bias_col]
    kinds = "bbbbbcc"
    last = prev is not None
    if last:
        in_specs += [tok(M_WIDTH), tok(M_WIDTH), pl.BlockSpec(m_norm_g.shape, lambda i: (0, 0))]
        args += [prev, o_gate, m_norm_g]
        kinds += "bbc"
    kinds += "b" + "sss"
    return pl.pallas_call(
        functools.partial(_per_batch(_mlstm_kernel, kinds), rev=rev, d=d, add_prev=last),
        grid=(nc,),
        in_specs=in_specs,
        out_specs=tok(M_WIDTH),
        out_shape=jax.ShapeDtypeStruct((bsz, ta, M_WIDTH), MXU_DTYPE if last else F32),
        scratch_shapes=[pltpu.VMEM((bsz, M_HEADS, M_QK_DIM, M_V_DIM), F32),
                        pltpu.VMEM((bsz, M_HEADS, 1, M_QK_DIM), F32),
                        pltpu.VMEM((bsz, M_HEADS, 1, LANES), F32)],
        compiler_params=_cparams(("arbitrary",)),
        name="mlstm_rev" if rev else "mlstm_fwd",
    )(*args)


def _ssd_kernel(*refs, rev, d, add_prev):
    if add_prev:
        (x_ref, gc_ref, gr_ref, dtb_row_ref, dtb_col_ref, an_row_ref, an_col_ref, prev_ref, z_ref, dsk_ref, sg_ref,
         o_ref, ht_s) = refs
    else:
        x_ref, gc_ref, gr_ref, dtb_row_ref, dtb_col_ref, an_row_ref, an_col_ref, o_ref, ht_s = refs
        prev_ref = None

    @pl.when(pl.program_id(0) == 0)
    def _():
        ht_s[...] = jnp.zeros_like(ht_s)

    mask, mask_f, mask_tf = _scan_masks(rev)
    end = 0 if rev else CHUNK - 1
    lane = lax.broadcasted_iota(jnp.int32, (CHUNK, LANES), 1)
    first_half = lane < S_HEAD_DIM
    dt_col = _softplus(gc_ref[0] + dtb_row_ref[...])
    dt_row = _softplus(gr_ref[0] + dtb_col_ref[...])
    acs_col = _prefix_cols(mask_f, dt_col * an_row_ref[...])
    acs_row = _prefix_rows(dt_row * an_col_ref[...], mask_tf)
    a_end_row = acs_col[end:end + 1, :]
    e_cs = jnp.exp(acs_col)
    e_rem = jnp.exp(a_end_row - acs_col)
    e_end = jnp.exp(a_end_row)
    gw = S_HEADS_PER_GROUP * S_HEAD_DIM
    pairs = S_HEADS_PER_GROUP // 2

    def pick(arr, la):
        return jnp.where(first_half[:arr.shape[0]], arr[:, la:la + 1], arr[:, la + 1:la + 2])

    y_parts = []
    for g in range(S_GROUPS):
        bm = x_ref[0, :, S_WIDTH + g * S_STATE:S_WIDTH + (g + 1) * S_STATE]
        cm = x_ref[0, :, S_WIDTH + (S_GROUPS + g) * S_STATE:S_WIDTH + (S_GROUPS + g + 1) * S_STATE]
        cb = _mm_nt(cm, bm)
        ht_prev = ht_s[g]
        y_inter = _mm(cm, ht_prev)
        xw_parts = []
        decay_parts = []
        for p in range(pairs):
            h0 = g * S_HEADS_PER_GROUP + 2 * p
            la = DT_OFF + d * S_HEADS + h0
            ra = d * S_HEADS + h0
            lhs = []
            for u in range(2):
                seg = acs_col[:, la + u:la + u + 1] - acs_row[ra + u:ra + u + 1, :]
                dec = jnp.exp(jnp.where(mask, seg, -jnp.inf))
                lhs.append((cb * dec).astype(MXU_DTYPE))
            xs = x_ref[0, :, h0 * S_HEAD_DIM:(h0 + 2) * S_HEAD_DIM]
            xsd = xs * pick(dt_col, la)
            rhs = jnp.concatenate([jnp.where(first_half, xsd, 0.0), jnp.where(first_half, 0.0, xsd)],
                                  axis=0).astype(MXU_DTYPE)
            y = jnp.dot(jnp.concatenate(lhs, axis=1), rhs, preferred_element_type=F32)
            y = y + y_inter[:, p * LANES:(p + 1) * LANES] * pick(e_cs, la)
            sl = slice(h0 * S_HEAD_DIM, (h0 + 2) * S_HEAD_DIM)
            if add_prev:
                y = (y + prev_ref[0, :, sl] + dsk_ref[:, sl] * xs) * _silu(z_ref[0, :, sl])
                y_parts.append(y)
            else:
                o_ref[0, :, sl] = y
            xw_parts.append(xsd * pick(e_rem, la))
            decay_parts.append(pick(e_end, la))
        xw = jnp.concatenate(xw_parts, axis=1)
        decay = jnp.concatenate(decay_parts, axis=1)
        ht_s[g] = decay * ht_prev + _mm_tn(bm, xw)

    if add_prev:
        ys = jnp.concatenate(y_parts, axis=1)
        ys = ys * lax.rsqrt(jnp.mean(ys * ys, axis=-1, keepdims=True) + RMS_EPS) * sg_ref[...]
        o_ref[0] = ys.astype(o_ref.dtype)


def ssd_call(xbc_act, gcol, dtrow, dtb_row, dtb_col, an_row, an_col, prev, z, dskip, s_norm_g, *, rev, d,
             ctx_chunks):
    bsz, ta, ch = xbc_act.shape
    nc = ta // CHUNK
    order = lambda i: _chunk_order(i, rev, ctx_chunks, nc)
    tok = lambda w: pl.BlockSpec((bsz, CHUNK, w), lambda i: (0, order(i), 0))
    const = lambda a: pl.BlockSpec(a.shape, lambda i: (0, 0))
    in_specs = [tok(ch), tok(GATE_LANES),
                pl.BlockSpec((bsz, dtrow.shape[1], CHUNK), lambda i: (0, 0, order(i))),
                const(dtb_row), const(dtb_col), const(an_row), const(an_col)]
    args = [xbc_act, gcol, dtrow, dtb_row, dtb_col, an_row, an_col]
    kinds = "bbbcccc"
    last = prev is not None
    if last:
        in_specs += [tok(S_WIDTH), tok(S_WIDTH), const(dskip), const(s_norm_g)]
        args += [prev, z, dskip, s_norm_g]
        kinds += "bbcc"
    kinds += "b" + "s"
    return pl.pallas_call(
        functools.partial(_per_batch(_ssd_kernel, kinds), rev=rev, d=d, add_prev=last),
        grid=(nc,),
        in_specs=in_specs,
        out_specs=tok(S_WIDTH),
        out_shape=jax.ShapeDtypeStruct((bsz, ta, S_WIDTH), MXU_DTYPE if last else F32),
        scratch_shapes=[pltpu.VMEM((bsz, S_GROUPS, S_STATE, S_HEADS_PER_GROUP * S_HEAD_DIM), F32)],
        compiler_params=_cparams(("arbitrary",)),
        name="ssd_rev" if rev else "ssd_fwd",
    )(*args)


def _store_row_tiles(ref, val):
    for s in range(ROW_CHUNKS):
        ref[pl.ds(s, val.shape[0], stride=ROW_CHUNKS), :] = val[:, s * LANES:(s + 1) * LANES]


def _load_row_tiles(ref, rows):
    return jnp.concatenate([ref[pl.ds(s, rows, stride=ROW_CHUNKS), :] for s in range(ROW_CHUNKS)], axis=1)


def _finish_sublayer(x, y, mod, lng_ref, lnb_ref, rw_ref, rb_ref, x1_ref, h2_ref, e_ref, g_ref):
    x1 = _layer_norm_rows(DEEPNORM_ALPHA * x + mod[2:3] * y, lng_ref[...], lnb_ref[...])
    x1_ref[0] = x1
    h2 = x1 * (1.0 + mod[4:5]) + mod[3:4]
    _store_row_tiles(h2_ref, h2)
    e, g = _route(h2, rw_ref, rb_ref)
    e_ref[...] = e
    g_ref[...] = g


def _outproj0_kernel(ym_ref, ys_ref, ctx_ref, x_ref, mod_ref, w_ref, lng_ref, lnb_ref, rw_ref, rb_ref,
                     x1_ref, h2_ref, e_ref, g_ref, *, ctx_tiles):
    y = (jnp.dot(ym_ref[0], w_ref[:M_WIDTH, :], preferred_element_type=F32)
         + jnp.dot(ys_ref[0], w_ref[M_WIDTH:, :], preferred_element_type=F32))
    _finish_sublayer(_token_tile(ctx_ref, x_ref, ctx_tiles), y, mod_ref[0, 0], lng_ref, lnb_ref, rw_ref, rb_ref,
                     x1_ref, h2_ref, e_ref, g_ref)


def _sublayer_out(bsz, tt, d):
    nt = tt // TM
    n = bsz * tt
    specs = [pl.BlockSpec((1, TM, d), lambda b, j: (b, j, 0)),
             pl.BlockSpec((TM * ROW_CHUNKS, LANES), lambda b, j: (b * nt + j, 0)),
             pl.BlockSpec((TOP_K, TM), lambda b, j: (0, b * nt + j)),
             pl.BlockSpec((TOP_K, TM), lambda b, j: (0, b * nt + j))]
    shapes = [jax.ShapeDtypeStruct((bsz, tt, d), F32), jax.ShapeDtypeStruct((n * ROW_CHUNKS, LANES), F32),
              jax.ShapeDtypeStruct((TOP_K, n), jnp.int32), jax.ShapeDtypeStruct((TOP_K, n), F32)]
    return specs, shapes


def outproj0_call(ym, ys, ctx, x, modsel, w_out, ln_g, ln_b, router_wt, router_b, ctx_tiles):
    bsz, ta, _ = ym.shape
    d = x.shape[-1]
    nt = ta // TM
    tok = lambda w: pl.BlockSpec((1, TM, w), lambda b, j: (b, j, 0))
    const = lambda a: pl.BlockSpec(a.shape, lambda b, j: (0, 0))
    out_specs, out_shape = _sublayer_out(bsz, ta, d)
    return pl.pallas_call(
        functools.partial(_outproj0_kernel, ctx_tiles=ctx_tiles),
        grid=(bsz, nt),
        in_specs=[tok(M_WIDTH), tok(S_WIDTH)] + _token_specs(d, ctx_tiles) + [
                  pl.BlockSpec((1, 1, 6, d), lambda b, j: (b, (j >= ctx_tiles).astype(jnp.int32), 0, 0)),
                  const(w_out), const(ln_g), const(ln_b), const(router_wt), const(router_b)],
        out_specs=out_specs,
        out_shape=out_shape,
        compiler_params=_cparams(("arbitrary", "arbitrary")),
        name="outproj0",
    )(ym, ys, ctx, x, modsel, w_out, ln_g, ln_b, router_wt, router_b)


def _top2(vals, probs):
    v1, i1, p1 = vals[0], jnp.zeros_like(vals[0], dtype=jnp.int32), probs[0]
    for i in range(1, len(vals)):
        better = vals[i] > v1
        v1 = jnp.where(better, vals[i], v1)
        i1 = jnp.where(better, i, i1)
        p1 = jnp.where(better, probs[i], p1)
    v2 = jnp.full_like(vals[0], -jnp.inf)
    i2 = jnp.zeros_like(i1)
    p2 = jnp.zeros_like(p1)
    for i in range(len(vals)):
        better = jnp.logical_and(i1 != i, vals[i] > v2)
        v2 = jnp.where(better, vals[i], v2)
        i2 = jnp.where(better, i, i2)
        p2 = jnp.where(better, probs[i], p2)
    return v1, i1, p1, v2, i2, p2


def _split_hi_lo(x):
    hi = x.astype(MXU_DTYPE)
    return hi, (x - hi.astype(F32)).astype(MXU_DTYPE)


def _route(h, w_ref, b_ref):
    h_hi, h_lo = _split_hi_lo(h)
    w_hi, w_lo = _split_hi_lo(w_ref[...])
    lg = (jnp.dot(h_hi, w_hi, preferred_element_type=F32) + jnp.dot(h_lo, w_hi, preferred_element_type=F32)
          + jnp.dot(h_hi, w_lo, preferred_element_type=F32))
    logits = lg.T[:N_EXPERTS]
    mx = jnp.max(logits, axis=0, keepdims=True)
    ex = jnp.exp(logits - mx)
    probs = ex / jnp.sum(ex, axis=0, keepdims=True)
    sel = probs + b_ref[...]
    best = None
    for g in range(N_EXPERT_GROUPS):
        rows = range(g * EXPERTS_PER_GROUP, (g + 1) * EXPERTS_PER_GROUP)
        v1, i1, p1, v2, i2, p2 = _top2([sel[r:r + 1] for r in rows], [probs[r:r + 1] for r in rows])
        cand = (v1 + v2, i1 + g * EXPERTS_PER_GROUP, p1, i2 + g * EXPERTS_PER_GROUP, p2)
        if best is None:
            best = cand
        else:
            better = cand[0] > best[0]
            best = tuple(jnp.where(better, c, o) for c, o in zip(cand, best))
    _, e1, p1, e2, p2 = best
    tot = p1 + p2
    return jnp.concatenate([e1, e2], axis=0), jnp.concatenate([p1 / tot, p2 / tot], axis=0)


def _row_tile(ref, r):
    return ref.at[pl.ds(pl.multiple_of(r * ROW_CHUNKS, ROW_CHUNKS), ROW_CHUNKS)]


def _dispatch_kernel(dest_ref, zblk_ref, h_ref, xs_ref, zero_s, sem, zsem, *, n_tok):
    base = pl.program_id(0) * DISPATCH_ROWS
    blk_rows = MOE_BLK * ROW_CHUNKS

    @pl.when(pl.program_id(0) == 0)
    def _():
        zero_s[...] = jnp.zeros_like(zero_s)

        def zero_copy(t):
            start = pl.multiple_of(zblk_ref[t] * blk_rows, blk_rows)
            return pltpu.make_async_copy(zero_s, xs_ref.at[pl.ds(start, blk_rows)], zsem)

        for t in range(2 * N_EXPERTS):
            pl.when(zblk_ref[t] >= 0)(lambda t=t: zero_copy(t).start())
        for t in range(2 * N_EXPERTS):
            pl.when(zblk_ref[t] >= 0)(lambda t=t: zero_copy(t).wait())

    def slot_copy(r, slot):
        return pltpu.make_async_copy(_row_tile(h_ref, r), _row_tile(xs_ref, slot), sem)

    def issue(r, carry):
        for c in range(TOP_K):
            slot_copy(r, dest_ref[c * n_tok + base + r]).start(priority=c)
        return carry

    def drain(r, carry):
        for c in range(TOP_K):
            slot_copy(r, 0).wait()
        return carry

    lax.fori_loop(0, DISPATCH_ROWS, issue, 0, unroll=8)
    lax.fori_loop(0, DISPATCH_ROWS, drain, 0, unroll=8)


def dispatch_call(dest, zero_blocks, h2t, n_slots):
    n_tok = h2t.shape[0] // ROW_CHUNKS
    return pl.pallas_call(
        functools.partial(_dispatch_kernel, n_tok=n_tok),
        grid_spec=pltpu.PrefetchScalarGridSpec(
            num_scalar_prefetch=2,
            grid=(n_tok // DISPATCH_ROWS,),
            in_specs=[pl.BlockSpec((DISPATCH_ROWS * ROW_CHUNKS, LANES), lambda i, dest, zb: (i, 0))],
            out_specs=pl.BlockSpec(memory_space=pl.ANY),
            scratch_shapes=[pltpu.VMEM((MOE_BLK * ROW_CHUNKS, LANES), h2t.dtype),
                            pltpu.SemaphoreType.DMA(()), pltpu.SemaphoreType.DMA(())]),
        out_shape=jax.ShapeDtypeStruct((n_slots * ROW_CHUNKS, LANES), h2t.dtype),
        compiler_params=_cparams(("arbitrary",)),
        name="dispatch",
    )(dest, zero_blocks, h2t)


def _experts_kernel(be_ref, cnt_ref, x_ref, wg_ref, wu_ref, wd_ref, o_ref, wg_s, wu_s, wd_s):
    i = pl.program_id(0)
    e = be_ref[i]
    e_before = be_ref[jnp.maximum(i - 1, 0)]
    cnt = cnt_ref[i]

    @pl.when(jnp.logical_or(i == 0, e != e_before))
    def _():
        wg_s[...] = wg_ref[0, 0].astype(wg_s.dtype)
        wu_s[...] = wu_ref[0, 0].astype(wu_s.dtype)
        wd_s[...] = wd_ref[0, 0].astype(wd_s.dtype)

    @pl.when(cnt > 0)
    def _():
        xb = _load_row_tiles(x_ref, MOE_BLK).astype(MXU_DTYPE)
        gt = jnp.dot(xb, wg_s[...], preferred_element_type=F32)
        up = jnp.dot(xb, wu_s[...], preferred_element_type=F32)
        y = jnp.dot((_silu(gt) * up).astype(MXU_DTYPE), wd_s[...], preferred_element_type=F32)
        _store_row_tiles(o_ref, y)

    @pl.when(cnt == 0)
    def _():
        o_ref[...] = jnp.zeros_like(o_ref)


def experts_call(blk_exp, blk_cnt, xs, w_gate, w_up, w_down, layer):
    n_slots = xs.shape[0] // ROW_CHUNKS
    d, f = w_gate.shape[-2:]
    wspec = lambda a: pl.BlockSpec((1, 1) + a.shape[2:], lambda i, be, cnt: (layer, be[i], 0, 0))
    blk = pl.BlockSpec((MOE_BLK * ROW_CHUNKS, LANES), lambda i, be, cnt: (i, 0))
    return pl.pallas_call(
        _experts_kernel,
        grid_spec=pltpu.PrefetchScalarGridSpec(
            num_scalar_prefetch=2,
            grid=(n_slots // MOE_BLK,),
            in_specs=[blk, wspec(w_gate), wspec(w_up), wspec(w_down)],
            out_specs=blk,
            scratch_shapes=[pltpu.VMEM((d, f), MXU_DTYPE), pltpu.VMEM((d, f), MXU_DTYPE),
                            pltpu.VMEM((f, d), MXU_DTYPE)]),
        out_shape=jax.ShapeDtypeStruct(xs.shape, F32),
        compiler_params=_cparams(("arbitrary",)),
        name="experts",
    )(blk_exp, blk_cnt, xs, w_gate, w_up, w_down)


def _combined_tile(dest_ref, y_ref, gate_ref, x_ref, mod_ref, lng_ref, lnb_ref, ybuf, sems, *, n_tok, nt, n_steps):
    step = pl.program_id(0) * nt + pl.program_id(1)
    slot = step % 2

    def row_copy(buf, r, c, src_slot):
        return pltpu.make_async_copy(_row_tile(y_ref, src_slot), _row_tile(ybuf.at[buf, c], r), sems.at[buf])

    def issue(tile, buf):
        base = tile * TM

        def body(r, carry):
            for c in range(TOP_K):
                row_copy(buf, r, c, dest_ref[c * n_tok + base + r]).start(priority=c)
            return carry

        lax.fori_loop(0, TM, body, 0, unroll=8)

    def drain(buf):
        def body(r, carry):
            for c in range(TOP_K):
                row_copy(buf, r, c, 0).wait()
            return carry

        lax.fori_loop(0, TM, body, 0, unroll=8)

    pl.when(step == 0)(lambda: issue(0, 0))
    pl.when(step + 1 < n_steps)(lambda: issue(step + 1, 1 - slot))
    drain(slot)
    mod = mod_ref[0, 0]
    gate = gate_ref[...]
    y = (gate[:, 0:1] * _load_row_tiles(ybuf.at[slot, 0], TM)
         + gate[:, 1:2] * _load_row_tiles(ybuf.at[slot, 1], TM))
    return _layer_norm_rows(DEEPNORM_ALPHA * x_ref[0] + mod[5:6] * y, lng_ref[...], lnb_ref[...])


def _combine_kernel(dest_ref, y_ref, gate_ref, x_ref, mod_ref, lng_ref, lnb_ref, o_ref, ybuf, sems, **kw):
    o_ref[0] = _combined_tile(dest_ref, y_ref, gate_ref, x_ref, mod_ref, lng_ref, lnb_ref, ybuf, sems, **kw)


def _combine_inproj1_kernel(dest_ref, y_ref, gate_ref, x_ref, mod_ref, lng_ref, lnb_ref,
                            mod1_ref, w_ref, qg_ref, kg_ref, cos_ref, sin_ref,
                            o_ref, q_ref, k_ref, v_ref, ybuf, sems, **kw):
    x = _combined_tile(dest_ref, y_ref, gate_ref, x_ref, mod_ref, lng_ref, lnb_ref, ybuf, sems, **kw)
    o_ref[0] = x
    _qkv_project(x, mod1_ref[0, 0], w_ref, qg_ref, kg_ref, cos_ref, sin_ref, q_ref, k_ref, v_ref)


def combine_call(dest, y_slots, gate_cols, x1, modsel, ln_g, ln_b, ctx_tiles, proj=None):
    bsz, tt, d = x1.shape
    nt = tt // TM
    tok = lambda w: pl.BlockSpec((1, TM, w), lambda b, j, dest: (b, j, 0))
    const = lambda a: pl.BlockSpec(a.shape, lambda b, j, dest: (0, 0))
    modspec = pl.BlockSpec((1, 1, 6, d), lambda b, j, dest: (b, (j >= ctx_tiles).astype(jnp.int32), 0, 0))
    in_specs = [pl.BlockSpec(memory_space=pl.ANY),
                pl.BlockSpec((TM, TOP_K), lambda b, j, dest: (b * nt + j, 0)),
                tok(d), modspec, const(ln_g), const(ln_b)]
    args = [dest, y_slots, gate_cols, x1, modsel, ln_g, ln_b]
    out_specs = [tok(d)]
    out_shape = [jax.ShapeDtypeStruct((bsz, tt, d), F32)]
    body = _combine_kernel
    if proj is not None:
        modsel1, w_in, q_g, k_g, cos2, sin2 = proj
        rope = pl.BlockSpec((TM, A_HEAD_DIM), lambda b, j, dest: (j, 0))
        in_specs += [modspec, const(w_in), const(q_g), const(k_g), rope, rope]
        args += [modsel1, w_in, q_g, k_g, cos2, sin2]
        widths = (A_Q_W, A_KV_W, A_KV_W)
        out_specs += [tok(w) for w in widths]
        out_shape += [jax.ShapeDtypeStruct((bsz, tt, w), MXU_DTYPE) for w in widths]
        body = _combine_inproj1_kernel
    res = pl.pallas_call(
        functools.partial(body, n_tok=bsz * tt, nt=nt, n_steps=bsz * nt),
        grid_spec=pltpu.PrefetchScalarGridSpec(
            num_scalar_prefetch=1,
            grid=(bsz, nt),
            in_specs=in_specs,
            out_specs=out_specs,
            scratch_shapes=[pltpu.VMEM((2, TOP_K, TM * ROW_CHUNKS, LANES), F32), pltpu.SemaphoreType.DMA((2,))]),
        out_shape=out_shape,
        compiler_params=_cparams(("arbitrary", "arbitrary")),
        name="combine" if proj is None else "combine_inproj1",
    )(*args)
    return res[0] if proj is None else res


def moe_block(x1, h2t, e_idx, gates, modsel, ln_g, ln_b, w_gate, w_up, w_down, layer, ctx_tiles, proj=None):
    bsz, tt, d = x1.shape
    n = bsz * tt

    n_asg = TOP_K * n
    flat_e = e_idx.reshape(n_asg)
    onehot = (flat_e[:, None] == jnp.arange(N_EXPERTS, dtype=jnp.int32)[None, :]).astype(jnp.int32)
    csum = jnp.cumsum(onehot, axis=0)
    rank = jnp.sum(onehot * csum, axis=1) - 1
    counts = csum[-1]
    padded = (counts + MOE_BLK - 1) // MOE_BLK * MOE_BLK
    pend = jnp.cumsum(padded)
    pstart = pend - padded
    dest = (pstart[flat_e] + rank).astype(jnp.int32)
    n_blocks = -(-n_asg // MOE_BLK) + N_EXPERTS
    blk_start = jnp.arange(n_blocks, dtype=jnp.int32) * MOE_BLK
    blk_exp = jnp.sum((pend[None, :] <= blk_start[:, None]).astype(jnp.int32), axis=1)
    blk_exp = jnp.minimum(blk_exp, N_EXPERTS - 1)
    blk_cnt = jnp.clip(counts[blk_exp] - (blk_start - pstart[blk_exp]), 0, MOE_BLK).astype(jnp.int32)

    part = jnp.where(counts % MOE_BLK != 0, pend // MOE_BLK - 1, -1)
    tail = pend[-1] // MOE_BLK + jnp.arange(N_EXPERTS, dtype=jnp.int32)
    tail = jnp.where(tail < n_blocks, tail, -1)
    zero_blocks = jnp.concatenate([part, tail]).astype(jnp.int32)

    xs = dispatch_call(dest, zero_blocks, h2t, n_blocks * MOE_BLK)
    y_slots = experts_call(blk_exp, blk_cnt, xs, w_gate, w_up, w_down, layer)
    return combine_call(dest, y_slots, gates.T, x1, modsel, ln_g, ln_b, ctx_tiles, proj)


def _qkv_project(x, mod, w_ref, qg_ref, kg_ref, cos_ref, sin_ref, q_ref, k_ref, v_ref):
    hb = (x * (1.0 + mod[1:2]) + mod[0:1]).astype(MXU_DTYPE)
    cos = cos_ref[...]
    sin = sin_ref[...]

    def norm_rope(t, g):
        t = t * lax.rsqrt(jnp.mean(t * t, axis=-1, keepdims=True) + RMS_EPS) * g
        return t * cos + pltpu.roll(t, A_HEAD_DIM // 2, 1) * sin

    qkv = jnp.dot(hb, w_ref[...], preferred_element_type=F32)
    for j in range(A_HEADS):
        sl = slice(j * A_HEAD_DIM, (j + 1) * A_HEAD_DIM)
        q_ref[0, :, sl] = (norm_rope(qkv[:, sl], qg_ref[...]) * ATT_Q_SCALE).astype(q_ref.dtype)
    for j in range(A_KV_HEADS):
        sl = slice(j * A_HEAD_DIM, (j + 1) * A_HEAD_DIM)
        t = qkv[:, A_Q_W + j * A_HEAD_DIM:A_Q_W + (j + 1) * A_HEAD_DIM]
        k_ref[0, :, sl] = norm_rope(t, kg_ref[...]).astype(k_ref.dtype)
    v_ref[0] = qkv[:, A_Q_W + A_KV_W:].astype(v_ref.dtype)


def _attn_kernel(q_ref, qn_ref, k_ref, vt_ref, o_ref, s_buf):
    rep = A_HEADS // A_KV_HEADS
    n_kv = vt_ref.shape[2]
    assert n_kv % 2 == 0

    def transposed(ref):
        return [ref[0, :, r * A_HEAD_DIM:(r + 1) * A_HEAD_DIM].astype(F32).T.astype(MXU_DTYPE) for r in range(rep)]

    qts = transposed(q_ref)

    def scores(j, slot, qt=qts):
        kb = k_ref[0, pl.ds(pl.multiple_of(j * ATT_TK, ATT_TK), ATT_TK), :]
        for r in range(rep):
            s_buf[slot, r] = jnp.dot(kb, qt[r], preferred_element_type=F32)

    def consume(j, slot, stats):
        vt = vt_ref[0, 0, j]
        new = []
        for r in range(rep):
            m, l, acc = stats[r]
            s = s_buf[slot, r]
            m_new = jnp.maximum(m, jnp.max(s, axis=0, keepdims=True))
            p = jnp.exp2(s - m_new)
            alpha = jnp.exp2(m - m_new)
            l = alpha * l + jnp.sum(p, axis=0, keepdims=True)
            acc = alpha * acc + jnp.dot(vt, p.astype(vt.dtype), preferred_element_type=F32)
            new.append((m_new, l, acc))
        return tuple(new)

    def pair(i, stats):
        j = 2 * i
        scores(j + 1, 1)
        stats = consume(j, 0, stats)
        scores(j + 2, 0)
        return consume(j + 1, 1, stats)

    stats = tuple((jnp.full((1, ATT_TQ), -jnp.inf, F32), jnp.zeros((1, ATT_TQ), F32),
                   jnp.zeros((A_HEAD_DIM, ATT_TQ), F32)) for _ in range(rep))
    pl.when(pl.program_id(2) == 0)(lambda: scores(0, 0))
    stats = lax.fori_loop(0, n_kv // 2 - 1, pair, stats)
    scores(n_kv - 1, 1)
    stats = consume(n_kv - 2, 0, stats)
    scores(0, 0, transposed(qn_ref))
    final = consume(n_kv - 1, 1, stats)
    for r in range(rep):
        _, l, acc = final[r]
        o_ref[0, :, r * A_HEAD_DIM:(r + 1) * A_HEAD_DIM] = (acc / l).T.astype(o_ref.dtype)


def attn_call(q, k, v, n_ctx):
    bsz, ta, _ = q.shape
    t_lat = ta - n_ctx
    gw = (A_HEADS // A_KV_HEADS) * A_HEAD_DIM
    q_off = n_ctx // ATT_TQ
    n_kv = ta // ATT_TK
    vt = v.reshape(bsz, n_kv, ATT_TK, A_KV_HEADS, A_HEAD_DIM).transpose(0, 3, 1, 4, 2)
    nq = t_lat // ATT_TQ
    return pl.pallas_call(
        _attn_kernel,
        grid=(bsz, A_KV_HEADS, nq),
        in_specs=[pl.BlockSpec((1, ATT_TQ, gw), lambda b, g, i: (b, i + q_off, g)),
                  pl.BlockSpec((1, ATT_TQ, gw), lambda b, g, i: (b, jnp.minimum(i + 1, nq - 1) + q_off, g)),
                  pl.BlockSpec((1, ta, A_HEAD_DIM), lambda b, g, i: (b, 0, g)),
                  pl.BlockSpec((1, 1, n_kv, A_HEAD_DIM, ATT_TK), lambda b, g, i: (b, g, 0, 0, 0))],
        out_specs=pl.BlockSpec((1, ATT_TQ, gw), lambda b, g, i: (b, i, g)),
        out_shape=jax.ShapeDtypeStruct((bsz, t_lat, A_Q_W), MXU_DTYPE),
        scratch_shapes=[pltpu.VMEM((2, A_HEADS // A_KV_HEADS, ATT_TK, ATT_TQ), F32)],
        compiler_params=_cparams(("arbitrary", "arbitrary", "arbitrary")),
        name="attention",
    )(q, q, k, vt)


def _outproj1_kernel(a_ref, x_ref, mod_ref, w_ref, lng_ref, lnb_ref, rw_ref, rb_ref, x1_ref, h2_ref, e_ref, g_ref):
    y = jnp.dot(a_ref[0], w_ref[...], preferred_element_type=F32)
    _finish_sublayer(x_ref[0], y, mod_ref[0, 0], lng_ref, lnb_ref, rw_ref, rb_ref, x1_ref, h2_ref, e_ref, g_ref)


def outproj1_call(att, xa, modsel, w_out, ln_g, ln_b, router_wt, router_b, ctx_tiles):
    bsz, t_lat, _ = att.shape
    d = xa.shape[-1]
    const = lambda a: pl.BlockSpec(a.shape, lambda b, j: (0, 0))
    out_specs, out_shape = _sublayer_out(bsz, t_lat, d)
    return pl.pallas_call(
        _outproj1_kernel,
        grid=(bsz, t_lat // TM),
        in_specs=[pl.BlockSpec((1, TM, att.shape[-1]), lambda b, j: (b, j, 0)),
                  pl.BlockSpec((1, TM, d), lambda b, j: (b, j + ctx_tiles, 0)),
                  pl.BlockSpec((1, 1, 6, d), lambda b, j: (b, 1, 0, 0)),
                  const(w_out), const(ln_g), const(ln_b), const(router_wt), const(router_b)],
        out_specs=out_specs,
        out_shape=out_shape,
        compiler_params=_cparams(("arbitrary", "arbitrary")),
        name="outproj1",
    )(att, xa, modsel, w_out, ln_g, ln_b, router_wt, router_b)


def _rope_tables(n_ctx, n_lat):
    rows = n_lat // GRID_W
    row = jnp.repeat(jnp.arange(rows), GRID_W).astype(F32)
    col = jnp.tile(jnp.arange(GRID_W), rows).astype(F32)
    n_freq = A_HEAD_DIM // 4
    inv = ROPE_THETA ** (-jnp.arange(n_freq, dtype=F32) / n_freq)
    ang = jnp.concatenate([row[:, None] * inv, col[:, None] * inv], -1)
    cos, sin = jnp.cos(ang), jnp.sin(ang)
    cos2 = jnp.concatenate([cos, cos], -1)
    sin2 = jnp.concatenate([-sin, sin], -1)
    cos2 = jnp.concatenate([jnp.ones((n_ctx, A_HEAD_DIM), F32), cos2], 0)
    sin2 = jnp.concatenate([jnp.zeros((n_ctx, A_HEAD_DIM), F32), sin2], 0)
    return cos2, sin2


def _lane_row(parts, width=GATE_LANES):
    row = jnp.concatenate([p.reshape(-1).astype(F32) for p in parts])
    return jnp.pad(row, (0, width - row.shape[0])).reshape(1, width)


def kernel(x, c, ctx, c_ctx, ada_w, ada_b, ln_g, ln_b, ab_w_in, ab_w_out, ml_ig_b, ml_fg_b, ml_norm_g,
           ssm_conv_w, ssm_conv_b, ssm_dt_b, ssm_a_log, ssm_d, ssm_norm_g, at_w_in, at_w_out, at_q_g, at_k_g,
           router_w, router_b, moe_w_gate, moe_w_up, moe_w_down):
    bsz, n_lat, d = x.shape
    n_ctx = ctx.shape[1]
    assert d == D_MODEL and bsz + 1 <= SUBLANES
    assert n_ctx % TM == 0 and n_lat % TM == 0 and n_lat % GRID_W == 0
    assert (n_ctx + n_lat) % ATT_TK == 0 and n_ctx % ATT_TQ == 0 and n_lat % ATT_TQ == 0
    assert (bsz * (n_ctx + n_lat)) % DISPATCH_ROWS == 0 and (bsz * n_lat) % DISPATCH_ROWS == 0
    ctx_tiles = n_ctx // TM
    ctx_chunks = n_ctx // CHUNK

    crows = jnp.zeros((SUBLANES, d), F32).at[:bsz].set(c).at[bsz].set(c_ctx)
    mods = ada_call(crows, ada_w, ada_b)

    def mod_table(i):
        lat = mods[i, :bsz].reshape(bsz, 1, 6, d)
        cx = jnp.broadcast_to(mods[i, bsz].reshape(1, 1, 6, d), (bsz, 1, 6, d))
        return jnp.concatenate([cx, lat], axis=1)

    router_wt = jnp.pad(router_w, ((0, 0), (0, LANES - N_EXPERTS)))
    router_bc = router_b.reshape(N_EXPERTS, 1)

    modsel = mod_table(0)
    w_in = ab_w_in[0]
    s_q, s_k, s_v, s_o, s_ig, s_fg, s_z, s_xbc = (int(v) for v in
        (0, 512, 1024, 2048, 3072, 3072 + 8, 3072 + 16, 3072 + 16 + 1024))
    s_dt = s_xbc + S_CONV_CH
    w_big = jnp.concatenate([w_in[:, :s_ig], w_in[:, s_z:s_dt]], axis=1).astype(MXU_DTYPE)
    w_small = jnp.concatenate([w_in[:, s_ig:s_z], w_in[:, s_dt:]], axis=1)
    w_small = jnp.pad(w_small, ((0, 0), (0, GATE_LANES - w_small.shape[1])))
    q, k, v, o, z, xbc, gates = inproj0_call(ctx, x, modsel, w_big, w_small, ctx_tiles)
    xbc_act = conv_call(xbc, ssm_conv_w[0], ssm_conv_b[0], ctx_tiles)

    grow = jnp.swapaxes(gates[:, :, :DT_OFF], 1, 2)
    dtrow = jnp.swapaxes(gates[:, :, DT_OFF:DT_OFF + N_DIR * S_HEADS], 1, 2)
    gate_b_row = _lane_row([ml_ig_b[0], ml_fg_b[0], ssm_dt_b[0]])
    gate_b_col = jnp.concatenate([ml_ig_b[0].reshape(-1), ml_fg_b[0].reshape(-1)]).reshape(-1, 1)
    a_neg = -jnp.exp(ssm_a_log[0].astype(F32)).reshape(-1)
    an_row = _lane_row([jnp.zeros((DT_OFF,), F32), a_neg])
    an_col = a_neg.reshape(-1, 1)
    dtb_col = ssm_dt_b[0].reshape(-1, 1)

    dskip = jnp.repeat(ssm_d[0].astype(F32), S_HEAD_DIM).reshape(1, S_WIDTH)
    m_norm_g = ml_norm_g[0].reshape(1, -1)
    s_norm_g = ssm_norm_g[0].reshape(1, -1)
    assert N_DIR == 2
    hm = None
    hs = None
    for dd in range(N_DIR):
        hm = mlstm_call(q, k, v, gates, grow, gate_b_row, gate_b_col, hm, o, m_norm_g, rev=dd == 1, d=dd,
                        ctx_chunks=ctx_chunks)
        hs = ssd_call(xbc_act, gates, dtrow, gate_b_row, dtb_col, an_row, an_col, hs, z, dskip, s_norm_g,
                      rev=dd == 1, d=dd, ctx_chunks=ctx_chunks)

    x1, h2t, e_idx, gates = outproj0_call(hm, hs, ctx, x, modsel, ab_w_out[0].astype(MXU_DTYPE),
                                          ln_g[0, 0].reshape(1, d), ln_b[0, 0].reshape(1, d), router_wt, router_bc,
                                          ctx_tiles)
    modsel1 = mod_table(1)
    cos2, sin2 = _rope_tables(n_ctx, n_lat)
    proj = (modsel1, at_w_in[0].astype(MXU_DTYPE), at_q_g[0].reshape(1, -1), at_k_g[0].reshape(1, -1), cos2, sin2)
    xa, qa, ka, va = moe_block(x1, h2t, e_idx, gates, modsel, ln_g[0, 1].reshape(1, d), ln_b[0, 1].reshape(1, d),
                               moe_w_gate, moe_w_up, moe_w_down, 0, ctx_tiles, proj)
    modsel = modsel1
    att = attn_call(qa, ka, va, n_ctx)
    x1, h2t, e_idx, gates = outproj1_call(att, xa, modsel, at_w_out[0].astype(MXU_DTYPE), ln_g[1, 0].reshape(1, d),
                                          ln_b[1, 0].reshape(1, d), router_wt, router_bc, ctx_tiles)
    return moe_block(x1, h2t, e_idx, gates, modsel, ln_g[1, 1].reshape(1, d), ln_b[1, 1].reshape(1, d),
                     moe_w_gate, moe_w_up, moe_w_down, 1, 0)
```

```python
import functools
import math

import jax
import jax.numpy as jnp
from jax import lax
from jax.experimental import pallas as pl
from jax.experimental.pallas import tpu as pltpu

F32 = jnp.float32
MXU_DTYPE = jnp.bfloat16
HIGHEST = lax.Precision.HIGHEST

D_MODEL = 1024
DEPTH = 2
GRID_W = 64
CHUNK = 128
M_HEADS = 4
M_QK_DIM = D_MODEL // 8
M_V_DIM = D_MODEL // 4
M_WIDTH = M_HEADS * M_V_DIM
S_HEADS = 16
S_HEAD_DIM = D_MODEL // 16
S_GROUPS = 2
S_HEADS_PER_GROUP = S_HEADS // S_GROUPS
S_STATE = 128
S_WIDTH = S_HEADS * S_HEAD_DIM
S_CONV_CH = S_WIDTH + 2 * S_GROUPS * S_STATE
CONV_K = 4
N_DIR = 2
A_HEADS = 8
A_KV_HEADS = 2
A_HEAD_DIM = D_MODEL // A_HEADS
A_Q_W = A_HEADS * A_HEAD_DIM
A_KV_W = A_KV_HEADS * A_HEAD_DIM
ROPE_THETA = 10000.0
N_EXPERTS = 16
N_EXPERT_GROUPS = 4
EXPERTS_PER_GROUP = N_EXPERTS // N_EXPERT_GROUPS
TOP_K = 2
DEEPNORM_ALPHA = (2 * DEPTH) ** 0.25
LN_EPS = 1e-5
RMS_EPS = 1e-6

LANES = 128
SUBLANES = 8
TM = 256
MOE_BLK = 512
DISPATCH_ROWS = 512
COMBINE_ISSUE_PARTS = 4
ROW_CHUNKS = D_MODEL // LANES
assert ROW_CHUNKS == SUBLANES
ATT_TQ = 256
ATT_TK = 4224
VMEM_LIMIT = 56 * 1024 * 1024
GATE_LANES = 128
IG_OFF, FG_OFF, DT_OFF = 0, N_DIR * M_HEADS, 2 * N_DIR * M_HEADS
ATT_Q_SCALE = A_HEAD_DIM ** -0.5 * math.log2(math.e)


def _cparams(sem):
    return pltpu.CompilerParams(dimension_semantics=sem, vmem_limit_bytes=VMEM_LIMIT)


def _silu(x):
    return x / (1.0 + jnp.exp(-x))


def _sigmoid(x):
    return 1.0 / (1.0 + jnp.exp(-x))


def _softplus(x):
    return jnp.maximum(x, 0.0) + jnp.log(1.0 + jnp.exp(-jnp.abs(x)))


def _log_sigmoid(x):
    return jnp.minimum(x, 0.0) - jnp.log(1.0 + jnp.exp(-jnp.abs(x)))


def _layer_norm_rows(x, g, b):
    mu = jnp.mean(x, axis=-1, keepdims=True)
    xc = x - mu
    var = jnp.mean(xc * xc, axis=-1, keepdims=True)
    return xc * lax.rsqrt(var + LN_EPS) * g + b


def _mm(a, b):
    return jnp.dot(a.astype(MXU_DTYPE), b.astype(MXU_DTYPE), preferred_element_type=F32)


def _mm_nt(a, b):
    return lax.dot_general(a.astype(MXU_DTYPE), b.astype(MXU_DTYPE), (((1,), (1,)), ((), ())),
                           preferred_element_type=F32)


def _mm_tn(a, b):
    return lax.dot_general(a.astype(MXU_DTYPE), b.astype(MXU_DTYPE), (((0,), (0,)), ((), ())),
                           preferred_element_type=F32)


def _ada_kernel(c_ref, w_ref, b_ref, o_ref):
    s = _silu(c_ref[...])
    o_ref[0] = jnp.dot(s, w_ref[0], precision=HIGHEST, preferred_element_type=F32) + b_ref[0]


def ada_call(crows, ada_w, ada_b):
    depth, d, n6 = ada_w.shape
    tn = 1536
    return pl.pallas_call(
        _ada_kernel,
        grid=(depth, n6 // tn),
        in_specs=[pl.BlockSpec((SUBLANES, d), lambda i, j: (0, 0)),
                  pl.BlockSpec((1, d, tn), lambda i, j: (i, 0, j)),
                  pl.BlockSpec((1, 1, tn), lambda i, j: (i, 0, j))],
        out_specs=pl.BlockSpec((1, SUBLANES, tn), lambda i, j: (i, 0, j)),
        out_shape=jax.ShapeDtypeStruct((depth, SUBLANES, n6), F32),
        compiler_params=_cparams(("arbitrary", "arbitrary")),
        name="ada",
    )(crows, ada_w, ada_b.reshape(depth, 1, n6))


def _token_tile(ctx_ref, x_ref, ctx_tiles):
    return jnp.where(pl.program_id(1) < ctx_tiles, ctx_ref[0], x_ref[0])


def _token_specs(d, ctx_tiles):
    return [pl.BlockSpec((1, TM, d), lambda b, j: (b, jnp.minimum(j, ctx_tiles - 1), 0)),
            pl.BlockSpec((1, TM, d), lambda b, j: (b, jnp.maximum(j - ctx_tiles, 0), 0))]


def _inproj0_kernel(ctx_ref, x_ref, mod_ref, wb_ref, ws_ref, q_ref, k_ref, v_ref, o_ref, z_ref, xbc_ref, g_ref, *,
                    ctx_tiles):
    x = _token_tile(ctx_ref, x_ref, ctx_tiles)
    mod = mod_ref[0, 0]
    h = x * (1.0 + mod[1:2]) + mod[0:1]
    hb = h.astype(MXU_DTYPE)
    qk = M_HEADS * M_QK_DIM
    c0 = 0
    q_ref[0] = (jnp.dot(hb, wb_ref[:, c0:c0 + qk], preferred_element_type=F32)
                * (M_QK_DIM ** -0.5)).astype(q_ref.dtype)
    c0 += qk
    k_ref[0] = jnp.dot(hb, wb_ref[:, c0:c0 + qk], preferred_element_type=F32).astype(k_ref.dtype)
    c0 += qk
    v_ref[0] = jnp.dot(hb, wb_ref[:, c0:c0 + M_WIDTH], preferred_element_type=F32).astype(v_ref.dtype)
    c0 += M_WIDTH
    o_ref[0] = jnp.dot(hb, wb_ref[:, c0:c0 + M_WIDTH], preferred_element_type=F32)
    c0 += M_WIDTH
    z_ref[0] = jnp.dot(hb, wb_ref[:, c0:c0 + S_WIDTH], preferred_element_type=F32)
    c0 += S_WIDTH
    xbc_ref[0] = jnp.dot(hb, wb_ref[:, c0:c0 + S_CONV_CH], preferred_element_type=F32)
    h_hi, h_lo = _split_hi_lo(h)
    w_hi, w_lo = _split_hi_lo(ws_ref[...])
    g_ref[0] = (jnp.dot(h_hi, w_hi, preferred_element_type=F32) + jnp.dot(h_lo, w_hi, preferred_element_type=F32)
                + jnp.dot(h_hi, w_lo, preferred_element_type=F32))


def inproj0_call(ctx, x, modsel, w_big, w_small, ctx_tiles):
    bsz, _, d = x.shape
    ta = ctx.shape[1] + x.shape[1]
    nt = ta // TM
    qk = M_HEADS * M_QK_DIM
    widths = (qk, qk, M_WIDTH, M_WIDTH, S_WIDTH, S_CONV_CH, GATE_LANES)
    dtypes = (MXU_DTYPE, MXU_DTYPE, MXU_DTYPE, F32, F32, F32, F32)
    tok = lambda w: pl.BlockSpec((1, TM, w), lambda b, j: (b, j, 0))
    return pl.pallas_call(
        functools.partial(_inproj0_kernel, ctx_tiles=ctx_tiles),
        grid=(bsz, nt),
        in_specs=_token_specs(d, ctx_tiles) + [
                  pl.BlockSpec((1, 1, 6, d), lambda b, j: (b, (j >= ctx_tiles).astype(jnp.int32), 0, 0)),
                  pl.BlockSpec(w_big.shape, lambda b, j: (0, 0)),
                  pl.BlockSpec(w_small.shape, lambda b, j: (0, 0))],
        out_specs=[tok(w) for w in widths],
        out_shape=[jax.ShapeDtypeStruct((bsz, ta, w), dt) for w, dt in zip(widths, dtypes)],
        compiler_params=_cparams(("arbitrary", "arbitrary")),
        name="inproj0",
    )(ctx, x, modsel, w_big, w_small)


def _conv_kernel(cur_ref, prev_ref, next_ref, w_ref, b_ref, o_ref, *, ctx_tiles, n_tiles):
    j = pl.program_id(1)
    has_prev = jnp.logical_and(j != 0, j != ctx_tiles)
    has_next = jnp.logical_and(j != ctx_tiles - 1, j != n_tiles - 1)
    prev = jnp.where(has_prev, prev_ref[0], 0.0)
    nxt = jnp.where(has_next, next_ref[0], 0.0)
    ext = jnp.concatenate([prev, cur_ref[0], nxt], axis=0)
    n = TM + 2 * SUBLANES
    w = w_ref[...]
    lo, hi = SUBLANES, SUBLANES + TM
    acc = ext[lo:hi] * w[2:3]
    acc = acc + pltpu.roll(ext, 2, 0)[lo:hi] * w[0:1]
    acc = acc + pltpu.roll(ext, 1, 0)[lo:hi] * w[1:2]
    acc = acc + pltpu.roll(ext, n - 1, 0)[lo:hi] * w[3:4]
    o_ref[0] = _silu(acc + b_ref[...])


def conv_call(xbc, conv_w, conv_b, ctx_tiles):
    bsz, ta, ch = xbc.shape
    nt = ta // TM
    r = TM // SUBLANES
    last = ta // SUBLANES - 1
    return pl.pallas_call(
        functools.partial(_conv_kernel, ctx_tiles=ctx_tiles, n_tiles=nt),
        grid=(bsz, nt),
        in_specs=[pl.BlockSpec((1, TM, ch), lambda b, j: (b, j, 0)),
                  pl.BlockSpec((1, SUBLANES, ch), lambda b, j: (b, jnp.maximum(j * r - 1, 0), 0)),
                  pl.BlockSpec((1, SUBLANES, ch), lambda b, j: (b, jnp.minimum((j + 1) * r, last), 0)),
                  pl.BlockSpec((CONV_K, ch), lambda b, j: (0, 0)),
                  pl.BlockSpec((1, ch), lambda b, j: (0, 0))],
        out_specs=pl.BlockSpec((1, TM, ch), lambda b, j: (b, j, 0)),
        out_shape=jax.ShapeDtypeStruct((bsz, ta, ch), F32),
        compiler_params=_cparams(("arbitrary", "arbitrary")),
        name="conv",
    )(xbc, xbc, xbc, conv_w, conv_b.reshape(1, ch))


def _chunk_order(i, rev, ctx_chunks, n_chunks):
    if not rev:
        return i
    return jnp.where(i < ctx_chunks, ctx_chunks - 1 - i, n_chunks - 1 - (i - ctx_chunks))


def _scan_masks(rev):
    r = lax.broadcasted_iota(jnp.int32, (CHUNK, CHUNK), 0)
    c = lax.broadcasted_iota(jnp.int32, (CHUNK, CHUNK), 1)
    mask = (c >= r) if rev else (c <= r)
    mask_t = (r >= c) if rev else (r <= c)
    return mask, mask.astype(F32).astype(MXU_DTYPE), mask_t.astype(F32).astype(MXU_DTYPE)


def _split3(x):
    hi = x.astype(MXU_DTYPE)
    r = x - hi.astype(F32)
    mid = r.astype(MXU_DTYPE)
    lo = (r - mid.astype(F32)).astype(MXU_DTYPE)
    return hi, mid, lo


def _prefix_cols(tri, x):
    return sum(jnp.dot(tri, p, preferred_element_type=F32) for p in _split3(x))


def _prefix_rows(x, tri):
    return sum(jnp.dot(p, tri, preferred_element_type=F32) for p in _split3(x))


def _per_batch(body, kinds):
    def kern(*refs, **kw):
        assert len(refs) == len(kinds)
        for bi in range(refs[0].shape[0]):
            sub = [r.at[pl.ds(bi, 1)] if f == 'b' else r.at[bi] if f == 's' else r for r, f in zip(refs, kinds)]
            body(*sub, **kw)
    return kern


def _mlstm_kernel(*refs, rev, d, add_prev):
    if add_prev:
        (q_ref, k_ref, v_ref, gc_ref, gr_ref, brow_ref, bcol_ref, prev_ref, og_ref, mg_ref,
         o_ref, ct_s, n_s, m_s) = refs
    else:
        q_ref, k_ref, v_ref, gc_ref, gr_ref, brow_ref, bcol_ref, o_ref, ct_s, n_s, m_s = refs
        prev_ref = None

    @pl.when(pl.program_id(0) == 0)
    def _():
        ct_s[...] = jnp.zeros_like(ct_s)
        n_s[...] = jnp.zeros_like(n_s)
        m_s[...] = jnp.zeros_like(m_s)

    mask, mask_f, mask_tf = _scan_masks(rev)
    end = 0 if rev else CHUNK - 1
    gcol = gc_ref[0] + brow_ref[...]
    grow = gr_ref[0] + bcol_ref[...]
    nh2 = N_DIR * M_HEADS
    lf_col = _log_sigmoid(gcol)
    lf_row = _log_sigmoid(grow[FG_OFF:FG_OFF + nh2])
    b_col_all = _prefix_cols(mask_f, lf_col)
    b_row_all = _prefix_rows(lf_row, mask_tf)

    for j in range(M_HEADS):
        ci = d * M_HEADS + j
        bcol = b_col_all[:, FG_OFF + ci:FG_OFF + ci + 1]
        brow = b_row_all[ci:ci + 1, :]
        igcol = gcol[:, IG_OFF + ci:IG_OFF + ci + 1]
        igrow = grow[IG_OFF + ci:IG_OFF + ci + 1, :]
        b_end = bcol[end:end + 1, :]
        m_prev = m_s[j][:, 0:1]
        n_prev = n_s[j]
        ct_prev = ct_s[j]
        q = q_ref[0, :, j * M_QK_DIM:(j + 1) * M_QK_DIM]
        k = k_ref[0, :, j * M_QK_DIM:(j + 1) * M_QK_DIM]
        v = v_ref[0, :, j * M_V_DIM:(j + 1) * M_V_DIM]
        qf = q.astype(F32)
        kf = k.astype(F32)
        vf = v.astype(F32)

        dmat = jnp.where(mask, bcol - brow + igrow, -jnp.inf)
        inter = bcol + m_prev
        m_t = jnp.maximum(inter, jnp.max(dmat, axis=1, keepdims=True))
        sc = _mm_nt(q, k) * jnp.exp(dmat - m_t)
        a_in = jnp.exp(inter - m_t)
        num = _mm(sc, v) + a_in * _mm(q, ct_prev)
        den = jnp.sum(sc, axis=1, keepdims=True) + a_in * jnp.sum(qf * n_prev, axis=1, keepdims=True)
        h = num / jnp.maximum(jnp.abs(den), jnp.exp(-m_t))
        sl = slice(j * M_V_DIM, (j + 1) * M_V_DIM)
        if add_prev:
            h = h + prev_ref[0, :, sl]
            mu = jnp.mean(h, axis=-1, keepdims=True)
            hc = h - mu
            var = jnp.mean(hc * hc, axis=-1, keepdims=True)
            h = hc * lax.rsqrt(var + LN_EPS) * mg_ref[:, sl] * _sigmoid(og_ref[0, :, sl])
        o_ref[0, :, sl] = h.astype(o_ref.dtype)

        g_col = b_end - bcol + igcol
        g_row = b_end - brow + igrow
        g_max = jnp.max(g_row, axis=1, keepdims=True)
        w_col = jnp.exp(g_col - g_max)
        d_ct = _mm_tn(k, vf * w_col)
        d_n = jnp.sum(kf * w_col, axis=0, keepdims=True)
        m_new = jnp.maximum(b_end + m_prev, g_max)
        a = jnp.exp(b_end + m_prev - m_new)
        s = jnp.exp(g_max - m_new)
        ct_s[j] = a * ct_prev + s * d_ct
        n_s[j] = a * n_prev + s * d_n
        m_s[j] = jnp.broadcast_to(m_new, (1, LANES))


def mlstm_call(q, k, v, gcol, grow, bias_row, bias_col, prev, o_gate, m_norm_g, *, rev, d, ctx_chunks):
    bsz, ta, _ = q.shape
    nc = ta // CHUNK
    order = lambda i: _chunk_order(i, rev, ctx_chunks, nc)
    tok = lambda w: pl.BlockSpec((bsz, CHUNK, w), lambda i: (0, order(i), 0))
    in_specs = [tok(q.shape[-1]), tok(k.shape[-1]), tok(v.shape[-1]), tok(GATE_LANES),
                pl.BlockSpec((bsz, grow.shape[1], CHUNK), lambda i: (0, 0, order(i))),
                pl.BlockSpec(bias_row.shape, lambda i: (0, 0)),
                pl.BlockSpec(bias_col.shape, lambda i: (0, 0))]
    args = [q, k, v, gcol, grow, bias_row, bias_col]
    kinds = "bbbbbcc"
    last = prev is not None
    if last:
        in_specs += [tok(M_WIDTH), tok(M_WIDTH), pl.BlockSpec(m_norm_g.shape, lambda i: (0, 0))]
        args += [prev, o_gate, m_norm_g]
        kinds += "bbc"
    kinds += "b" + "sss"
    return pl.pallas_call(
        functools.partial(_per_batch(_mlstm_kernel, kinds), rev=rev, d=d, add_prev=last),
        grid=(nc,),
        in_specs=in_specs,
        out_specs=tok(M_WIDTH),
        out_shape=jax.ShapeDtypeStruct((bsz, ta, M_WIDTH), MXU_DTYPE if last else F32),
        scratch_shapes=[pltpu.VMEM((bsz, M_HEADS, M_QK_DIM, M_V_DIM), F32),
                        pltpu.VMEM((bsz, M_HEADS, 1, M_QK_DIM), F32),
                        pltpu.VMEM((bsz, M_HEADS, 1, LANES), F32)],
        compiler_params=_cparams(("arbitrary",)),
        name="mlstm_rev" if rev else "mlstm_fwd",
    )(*args)


def _ssd_kernel(*refs, rev, d, add_prev):
    if add_prev:
        (x_ref, gc_ref, gr_ref, dtb_row_ref, dtb_col_ref, an_row_ref, an_col_ref, prev_ref, z_ref, dsk_ref, sg_ref,
         o_ref, ht_s) = refs
    else:
        x_ref, gc_ref, gr_ref, dtb_row_ref, dtb_col_ref, an_row_ref, an_col_ref, o_ref, ht_s = refs
        prev_ref = None

    @pl.when(pl.program_id(0) == 0)
    def _():
        ht_s[...] = jnp.zeros_like(ht_s)

    mask, mask_f, mask_tf = _scan_masks(rev)
    end = 0 if rev else CHUNK - 1
    lane = lax.broadcasted_iota(jnp.int32, (CHUNK, LANES), 1)
    first_half = lane < S_HEAD_DIM
    dt_col = _softplus(gc_ref[0] + dtb_row_ref[...])
    dt_row = _softplus(gr_ref[0] + dtb_col_ref[...])
    acs_col = _prefix_cols(mask_f, dt_col * an_row_ref[...])
    acs_row = _prefix_rows(dt_row * an_col_ref[...], mask_tf)
    a_end_row = acs_col[end:end + 1, :]
    e_cs = jnp.exp(acs_col)
    e_rem = jnp.exp(a_end_row - acs_col)
    e_end = jnp.exp(a_end_row)
    gw = S_HEADS_PER_GROUP * S_HEAD_DIM
    pairs = S_HEADS_PER_GROUP // 2

    def pick(arr, la):
        return jnp.where(first_half[:arr.shape[0]], arr[:, la:la + 1], arr[:, la + 1:la + 2])

    y_parts = []
    for g in range(S_GROUPS):
        bm = x_ref[0, :, S_WIDTH + g * S_STATE:S_WIDTH + (g + 1) * S_STATE]
        cm = x_ref[0, :, S_WIDTH + (S_GROUPS + g) * S_STATE:S_WIDTH + (S_GROUPS + g + 1) * S_STATE]
        cb = _mm_nt(cm, bm)
        ht_prev = ht_s[g]
        y_inter = _mm(cm, ht_prev)
        xw_parts = []
        decay_parts = []
        for p in range(pairs):
            h0 = g * S_HEADS_PER_GROUP + 2 * p
            la = DT_OFF + d * S_HEADS + h0
            ra = d * S_HEADS + h0
            lhs = []
            for u in range(2):
                seg = acs_col[:, la + u:la + u + 1] - acs_row[ra + u:ra + u + 1, :]
                dec = jnp.exp(jnp.where(mask, seg, -jnp.inf))
                lhs.append((cb * dec).astype(MXU_DTYPE))
            xs = x_ref[0, :, h0 * S_HEAD_DIM:(h0 + 2) * S_HEAD_DIM]
            xsd = xs * pick(dt_col, la)
            rhs = jnp.concatenate([jnp.where(first_half, xsd, 0.0), jnp.where(first_half, 0.0, xsd)],
                                  axis=0).astype(MXU_DTYPE)
            y = jnp.dot(jnp.concatenate(lhs, axis=1), rhs, preferred_element_type=F32)
            y = y + y_inter[:, p * LANES:(p + 1) * LANES] * pick(e_cs, la)
            sl = slice(h0 * S_HEAD_DIM, (h0 + 2) * S_HEAD_DIM)
            if add_prev:
                y = (y + prev_ref[0, :, sl] + dsk_ref[:, sl] * xs) * _silu(z_ref[0, :, sl])
                y_parts.append(y)
            else:
                o_ref[0, :, sl] = y
            xw_parts.append(xsd * pick(e_rem, la))
            decay_parts.append(pick(e_end, la))
        xw = jnp.concatenate(xw_parts, axis=1)
        decay = jnp.concatenate(decay_parts, axis=1)
        ht_s[g] = decay * ht_prev + _mm_tn(bm, xw)

    if add_prev:
        ys = jnp.concatenate(y_parts, axis=1)
        ys = ys * lax.rsqrt(jnp.mean(ys * ys, axis=-1, keepdims=True) + RMS_EPS) * sg_ref[...]
        o_ref[0] = ys.astype(o_ref.dtype)


def ssd_call(xbc_act, gcol, dtrow, dtb_row, dtb_col, an_row, an_col, prev, z, dskip, s_norm_g, *, rev, d,
             ctx_chunks):
    bsz, ta, ch = xbc_act.shape
    nc = ta // CHUNK
    order = lambda i: _chunk_order(i, rev, ctx_chunks, nc)
    tok = lambda w: pl.BlockSpec((bsz, CHUNK, w), lambda i: (0, order(i), 0))
    const = lambda a: pl.BlockSpec(a.shape, lambda i: (0, 0))
    in_specs = [tok(ch), tok(GATE_LANES),
                pl.BlockSpec((bsz, dtrow.shape[1], CHUNK), lambda i: (0, 0, order(i))),
                const(dtb_row), const(dtb_col), const(an_row), const(an_col)]
    args = [xbc_act, gcol, dtrow, dtb_row, dtb_col, an_row, an_col]
    kinds = "bbbcccc"
    last = prev is not None
    if last:
        in_specs += [tok(S_WIDTH), tok(S_WIDTH), const(dskip), const(s_norm_g)]
        args += [prev, z, dskip, s_norm_g]
        kinds += "bbcc"
    kinds += "b" + "s"
    return pl.pallas_call(
        functools.partial(_per_batch(_ssd_kernel, kinds), rev=rev, d=d, add_prev=last),
        grid=(nc,),
        in_specs=in_specs,
        out_specs=tok(S_WIDTH),
        out_shape=jax.ShapeDtypeStruct((bsz, ta, S_WIDTH), MXU_DTYPE if last else F32),
        scratch_shapes=[pltpu.VMEM((bsz, S_GROUPS, S_STATE, S_HEADS_PER_GROUP * S_HEAD_DIM), F32)],
        compiler_params=_cparams(("arbitrary",)),
        name="ssd_rev" if rev else "ssd_fwd",
    )(*args)


def _store_row_tiles(ref, val):
    for s in range(ROW_CHUNKS):
        ref[pl.ds(s, val.shape[0], stride=ROW_CHUNKS), :] = val[:, s * LANES:(s + 1) * LANES]


def _load_row_tiles(ref, rows):
    return jnp.concatenate([ref[pl.ds(s, rows, stride=ROW_CHUNKS), :] for s in range(ROW_CHUNKS)], axis=1)


def _finish_sublayer(x, y, mod, lng_ref, lnb_ref, rw_ref, rb_ref, x1_ref, h2_ref, e_ref, g_ref):
    x1 = _layer_norm_rows(DEEPNORM_ALPHA * x + mod[2:3] * y, lng_ref[...], lnb_ref[...])
    x1_ref[0] = x1
    h2 = x1 * (1.0 + mod[4:5]) + mod[3:4]
    _store_row_tiles(h2_ref, h2)
    e, g = _route(h2, rw_ref, rb_ref)
    e_ref[...] = e
    g_ref[...] = g


def _outproj0_kernel(ym_ref, ys_ref, ctx_ref, x_ref, mod_ref, w_ref, lng_ref, lnb_ref, rw_ref, rb_ref,
                     x1_ref, h2_ref, e_ref, g_ref, *, ctx_tiles):
    y = (jnp.dot(ym_ref[0], w_ref[:M_WIDTH, :], preferred_element_type=F32)
         + jnp.dot(ys_ref[0], w_ref[M_WIDTH:, :], preferred_element_type=F32))
    _finish_sublayer(_token_tile(ctx_ref, x_ref, ctx_tiles), y, mod_ref[0, 0], lng_ref, lnb_ref, rw_ref, rb_ref,
                     x1_ref, h2_ref, e_ref, g_ref)


def _sublayer_out(bsz, tt, d):
    nt = tt // TM
    n = bsz * tt
    specs = [pl.BlockSpec((1, TM, d), lambda b, j: (b, j, 0)),
             pl.BlockSpec((TM * ROW_CHUNKS, LANES), lambda b, j: (b * nt + j, 0)),
             pl.BlockSpec((TOP_K, TM), lambda b, j: (0, b * nt + j)),
             pl.BlockSpec((TOP_K, TM), lambda b, j: (0, b * nt + j))]
    shapes = [jax.ShapeDtypeStruct((bsz, tt, d), F32), jax.ShapeDtypeStruct((n * ROW_CHUNKS, LANES), F32),
              jax.ShapeDtypeStruct((TOP_K, n), jnp.int32), jax.ShapeDtypeStruct((TOP_K, n), F32)]
    return specs, shapes


def outproj0_call(ym, ys, ctx, x, modsel, w_out, ln_g, ln_b, router_wt, router_b, ctx_tiles):
    bsz, ta, _ = ym.shape
    d = x.shape[-1]
    nt = ta // TM
    tok = lambda w: pl.BlockSpec((1, TM, w), lambda b, j: (b, j, 0))
    const = lambda a: pl.BlockSpec(a.shape, lambda b, j: (0, 0))
    out_specs, out_shape = _sublayer_out(bsz, ta, d)
    return pl.pallas_call(
        functools.partial(_outproj0_kernel, ctx_tiles=ctx_tiles),
        grid=(bsz, nt),
        in_specs=[tok(M_WIDTH), tok(S_WIDTH)] + _token_specs(d, ctx_tiles) + [
                  pl.BlockSpec((1, 1, 6, d), lambda b, j: (b, (j >= ctx_tiles).astype(jnp.int32), 0, 0)),
                  const(w_out), const(ln_g), const(ln_b), const(router_wt), const(router_b)],
        out_specs=out_specs,
        out_shape=out_shape,
        compiler_params=_cparams(("arbitrary", "arbitrary")),
        name="outproj0",
    )(ym, ys, ctx, x, modsel, w_out, ln_g, ln_b, router_wt, router_b)


def _top2(vals, probs):
    v1, i1, p1 = vals[0], jnp.zeros_like(vals[0], dtype=jnp.int32), probs[0]
    for i in range(1, len(vals)):
        better = vals[i] > v1
        v1 = jnp.where(better, vals[i], v1)
        i1 = jnp.where(better, i, i1)
        p1 = jnp.where(better, probs[i], p1)
    v2 = jnp.full_like(vals[0], -jnp.inf)
    i2 = jnp.zeros_like(i1)
    p2 = jnp.zeros_like(p1)
    for i in range(len(vals)):
        better = jnp.logical_and(i1 != i, vals[i] > v2)
        v2 = jnp.where(better, vals[i], v2)
        i2 = jnp.where(better, i, i2)
        p2 = jnp.where(better, probs[i], p2)
    return v1, i1, p1, v2, i2, p2


def _split_hi_lo(x):
    hi = x.astype(MXU_DTYPE)
    return hi, (x - hi.astype(F32)).astype(MXU_DTYPE)


def _route(h, w_ref, b_ref):
    h_hi, h_lo = _split_hi_lo(h)
    w_hi, w_lo = _split_hi_lo(w_ref[...])
    lg = (jnp.dot(h_hi, w_hi, preferred_element_type=F32) + jnp.dot(h_lo, w_hi, preferred_element_type=F32)
          + jnp.dot(h_hi, w_lo, preferred_element_type=F32))
    logits = lg.T[:N_EXPERTS]
    mx = jnp.max(logits, axis=0, keepdims=True)
    ex = jnp.exp(logits - mx)
    probs = ex / jnp.sum(ex, axis=0, keepdims=True)
    sel = probs + b_ref[...]
    best = None
    for g in range(N_EXPERT_GROUPS):
        rows = range(g * EXPERTS_PER_GROUP, (g + 1) * EXPERTS_PER_GROUP)
        v1, i1, p1, v2, i2, p2 = _top2([sel[r:r + 1] for r in rows], [probs[r:r + 1] for r in rows])
        cand = (v1 + v2, i1 + g * EXPERTS_PER_GROUP, p1, i2 + g * EXPERTS_PER_GROUP, p2)
        if best is None:
            best = cand
        else:
            better = cand[0] > best[0]
            best = tuple(jnp.where(better, c, o) for c, o in zip(cand, best))
    _, e1, p1, e2, p2 = best
    tot = p1 + p2
    return jnp.concatenate([e1, e2], axis=0), jnp.concatenate([p1 / tot, p2 / tot], axis=0)


def _row_tile(ref, r):
    return ref.at[pl.ds(pl.multiple_of(r * ROW_CHUNKS, ROW_CHUNKS), ROW_CHUNKS)]


def _dispatch_kernel(dest_ref, zblk_ref, h_ref, xs_ref, zero_s, sem, zsem, *, n_tok):
    base = pl.program_id(0) * DISPATCH_ROWS
    blk_rows = MOE_BLK * ROW_CHUNKS

    @pl.when(pl.program_id(0) == 0)
    def _():
        zero_s[...] = jnp.zeros_like(zero_s)

        def zero_copy(t):
            start = pl.multiple_of(zblk_ref[t] * blk_rows, blk_rows)
            return pltpu.make_async_copy(zero_s, xs_ref.at[pl.ds(start, blk_rows)], zsem)

        for t in range(2 * N_EXPERTS):
            pl.when(zblk_ref[t] >= 0)(lambda t=t: zero_copy(t).start())
        for t in range(2 * N_EXPERTS):
            pl.when(zblk_ref[t] >= 0)(lambda t=t: zero_copy(t).wait())

    def slot_copy(r, slot):
        return pltpu.make_async_copy(_row_tile(h_ref, r), _row_tile(xs_ref, slot), sem)

    def issue(r, carry):
        for c in range(TOP_K):
            slot_copy(r, dest_ref[c * n_tok + base + r]).start(priority=c)
        return carry

    def drain(r, carry):
        for c in range(TOP_K):
            slot_copy(r, 0).wait()
        return carry

    lax.fori_loop(0, DISPATCH_ROWS, issue, 0, unroll=8)
    lax.fori_loop(0, DISPATCH_ROWS, drain, 0, unroll=8)


def dispatch_call(dest, zero_blocks, h2t, n_slots):
    n_tok = h2t.shape[0] // ROW_CHUNKS
    return pl.pallas_call(
        functools.partial(_dispatch_kernel, n_tok=n_tok),
        grid_spec=pltpu.PrefetchScalarGridSpec(
            num_scalar_prefetch=2,
            grid=(n_tok // DISPATCH_ROWS,),
            in_specs=[pl.BlockSpec((DISPATCH_ROWS * ROW_CHUNKS, LANES), lambda i, dest, zb: (i, 0))],
            out_specs=pl.BlockSpec(memory_space=pl.ANY),
            scratch_shapes=[pltpu.VMEM((MOE_BLK * ROW_CHUNKS, LANES), h2t.dtype),
                            pltpu.SemaphoreType.DMA(()), pltpu.SemaphoreType.DMA(())]),
        out_shape=jax.ShapeDtypeStruct((n_slots * ROW_CHUNKS, LANES), h2t.dtype),
        compiler_params=_cparams(("arbitrary",)),
        name="dispatch",
    )(dest, zero_blocks, h2t)


def _experts_kernel(be_ref, cnt_ref, x_ref, wg_ref, wu_ref, wd_ref, o_ref, wg_s, wu_s, wd_s):
    i = pl.program_id(0)
    e = be_ref[i]
    e_before = be_ref[jnp.maximum(i - 1, 0)]
    cnt = cnt_ref[i]

    @pl.when(jnp.logical_or(i == 0, e != e_before))
    def _():
        wg_s[...] = wg_ref[0, 0].astype(wg_s.dtype)
        wu_s[...] = wu_ref[0, 0].astype(wu_s.dtype)
        wd_s[...] = wd_ref[0, 0].astype(wd_s.dtype)

    @pl.when(cnt > 0)
    def _():
        xb = _load_row_tiles(x_ref, MOE_BLK).astype(MXU_DTYPE)
        gt = jnp.dot(xb, wg_s[...], preferred_element_type=F32)
        up = jnp.dot(xb, wu_s[...], preferred_element_type=F32)
        y = jnp.dot((_silu(gt) * up).astype(MXU_DTYPE), wd_s[...], preferred_element_type=F32)
        _store_row_tiles(o_ref, y)

    @pl.when(cnt == 0)
    def _():
        o_ref[...] = jnp.zeros_like(o_ref)


def experts_call(blk_exp, blk_cnt, xs, w_gate, w_up, w_down, layer):
    n_slots = xs.shape[0] // ROW_CHUNKS
    d, f = w_gate.shape[-2:]
    wspec = lambda a: pl.BlockSpec((1, 1) + a.shape[2:], lambda i, be, cnt: (layer, be[i], 0, 0))
    blk = pl.BlockSpec((MOE_BLK * ROW_CHUNKS, LANES), lambda i, be, cnt: (i, 0))
    return pl.pallas_call(
        _experts_kernel,
        grid_spec=pltpu.PrefetchScalarGridSpec(
            num_scalar_prefetch=2,
            grid=(n_slots // MOE_BLK,),
            in_specs=[blk, wspec(w_gate), wspec(w_up), wspec(w_down)],
            out_specs=blk,
            scratch_shapes=[pltpu.VMEM((d, f), MXU_DTYPE), pltpu.VMEM((d, f), MXU_DTYPE),
                            pltpu.VMEM((f, d), MXU_DTYPE)]),
        out_shape=jax.ShapeDtypeStruct(xs.shape, F32),
        compiler_params=_cparams(("arbitrary",)),
        name="experts",
    )(blk_exp, blk_cnt, xs, w_gate, w_up, w_down)


def _combined_tile(dest_ref, y_ref, gate_ref, x_ref, mod_ref, lng_ref, lnb_ref, ybuf, sems, *, n_tok, nt, n_steps,
                   parts=1):
    step = pl.program_id(0) * nt + pl.program_id(1)
    slot = step % 2

    def row_copy(buf, r, c, src_slot):
        return pltpu.make_async_copy(_row_tile(y_ref, src_slot), _row_tile(ybuf.at[buf, c], r), sems.at[buf])

    def issue(tile, buf, lo=0, hi=TM):
        base = tile * TM

        def body(r, carry):
            for c in range(TOP_K):
                row_copy(buf, r, c, dest_ref[c * n_tok + base + r]).start(priority=c)
            return carry

        lax.fori_loop(lo, hi, body, 0, unroll=8)

    def issue_part(k):
        rows = TM // parts
        pl.when(step + 1 < n_steps)(lambda: issue(step + 1, 1 - slot, k * rows, (k + 1) * rows))

    def drain(buf):
        def body(r, carry):
            for c in range(TOP_K):
                row_copy(buf, r, c, 0).wait()
            return carry

        lax.fori_loop(0, TM, body, 0, unroll=8)

    pl.when(step == 0)(lambda: issue(0, 0))
    if parts == 1:
        pl.when(step + 1 < n_steps)(lambda: issue(step + 1, 1 - slot))
    drain(slot)
    mod = mod_ref[0, 0]
    gate = gate_ref[...]
    y = (gate[:, 0:1] * _load_row_tiles(ybuf.at[slot, 0], TM)
         + gate[:, 1:2] * _load_row_tiles(ybuf.at[slot, 1], TM))
    out = _layer_norm_rows(DEEPNORM_ALPHA * x_ref[0] + mod[5:6] * y, lng_ref[...], lnb_ref[...])
    return out if parts == 1 else (out, issue_part)


def _combine_kernel(dest_ref, y_ref, gate_ref, x_ref, mod_ref, lng_ref, lnb_ref, o_ref, ybuf, sems, **kw):
    o_ref[0] = _combined_tile(dest_ref, y_ref, gate_ref, x_ref, mod_ref, lng_ref, lnb_ref, ybuf, sems, **kw)


def _combine_inproj1_kernel(dest_ref, y_ref, gate_ref, x_ref, mod_ref, lng_ref, lnb_ref,
                            mod1_ref, w_ref, qg_ref, kg_ref, cos_ref, sin_ref,
                            o_ref, q_ref, k_ref, v_ref, ybuf, sems, **kw):
    x, issue_part = _combined_tile(dest_ref, y_ref, gate_ref, x_ref, mod_ref, lng_ref, lnb_ref, ybuf, sems,
                                   parts=COMBINE_ISSUE_PARTS, **kw)
    o_ref[0] = x
    _qkv_project(x, mod1_ref[0, 0], w_ref, qg_ref, kg_ref, cos_ref, sin_ref, q_ref, k_ref, v_ref, issue_part)


def combine_call(dest, y_slots, gate_cols, x1, modsel, ln_g, ln_b, ctx_tiles, proj=None):
    bsz, tt, d = x1.shape
    nt = tt // TM
    tok = lambda w: pl.BlockSpec((1, TM, w), lambda b, j, dest: (b, j, 0))
    const = lambda a: pl.BlockSpec(a.shape, lambda b, j, dest: (0, 0))
    modspec = pl.BlockSpec((1, 1, 6, d), lambda b, j, dest: (b, (j >= ctx_tiles).astype(jnp.int32), 0, 0))
    in_specs = [pl.BlockSpec(memory_space=pl.ANY),
                pl.BlockSpec((TM, TOP_K), lambda b, j, dest: (b * nt + j, 0)),
                tok(d), modspec, const(ln_g), const(ln_b)]
    args = [dest, y_slots, gate_cols, x1, modsel, ln_g, ln_b]
    out_specs = [tok(d)]
    out_shape = [jax.ShapeDtypeStruct((bsz, tt, d), F32)]
    body = _combine_kernel
    if proj is not None:
        modsel1, w_in, q_g, k_g, cos2, sin2 = proj
        rope = pl.BlockSpec((TM, A_HEAD_DIM), lambda b, j, dest: (j, 0))
        in_specs += [modspec, const(w_in), const(q_g), const(k_g), rope, rope]
        args += [modsel1, w_in, q_g, k_g, cos2, sin2]
        widths = (A_Q_W, A_KV_W, A_KV_W)
        out_specs += [tok(w) for w in widths]
        out_shape += [jax.ShapeDtypeStruct((bsz, tt, w), MXU_DTYPE) for w in widths]
        body = _combine_inproj1_kernel
    res = pl.pallas_call(
        functools.partial(body, n_tok=bsz * tt, nt=nt, n_steps=bsz * nt),
        grid_spec=pltpu.PrefetchScalarGridSpec(
            num_scalar_prefetch=1,
            grid=(bsz, nt),
            in_specs=in_specs,
            out_specs=out_specs,
            scratch_shapes=[pltpu.VMEM((2, TOP_K, TM * ROW_CHUNKS, LANES), F32), pltpu.SemaphoreType.DMA((2,))]),
        out_shape=out_shape,
        compiler_params=_cparams(("arbitrary", "arbitrary")),
        name="combine" if proj is None else "combine_inproj1",
    )(*args)
    return res[0] if proj is None else res


def moe_block(x1, h2t, e_idx, gates, modsel, ln_g, ln_b, w_gate, w_up, w_down, layer, ctx_tiles, proj=None):
    bsz, tt, d = x1.shape
    n = bsz * tt

    n_asg = TOP_K * n
    flat_e = e_idx.reshape(n_asg)
    onehot = (flat_e[:, None] == jnp.arange(N_EXPERTS, dtype=jnp.int32)[None, :]).astype(jnp.int32)
    csum = jnp.cumsum(onehot, axis=0)
    rank = jnp.sum(onehot * csum, axis=1) - 1
    counts = csum[-1]
    padded = (counts + MOE_BLK - 1) // MOE_BLK * MOE_BLK
    pend = jnp.cumsum(padded)
    pstart = pend - padded
    dest = (pstart[flat_e] + rank).astype(jnp.int32)
    n_blocks = -(-n_asg // MOE_BLK) + N_EXPERTS
    blk_start = jnp.arange(n_blocks, dtype=jnp.int32) * MOE_BLK
    blk_exp = jnp.sum((pend[None, :] <= blk_start[:, None]).astype(jnp.int32), axis=1)
    blk_exp = jnp.minimum(blk_exp, N_EXPERTS - 1)
    blk_cnt = jnp.clip(counts[blk_exp] - (blk_start - pstart[blk_exp]), 0, MOE_BLK).astype(jnp.int32)

    part = jnp.where(counts % MOE_BLK != 0, pend // MOE_BLK - 1, -1)
    tail = pend[-1] // MOE_BLK + jnp.arange(N_EXPERTS, dtype=jnp.int32)
    tail = jnp.where(tail < n_blocks, tail, -1)
    zero_blocks = jnp.concatenate([part, tail]).astype(jnp.int32)

    xs = dispatch_call(dest, zero_blocks, h2t, n_blocks * MOE_BLK)
    y_slots = experts_call(blk_exp, blk_cnt, xs, w_gate, w_up, w_down, layer)
    return combine_call(dest, y_slots, gates.T, x1, modsel, ln_g, ln_b, ctx_tiles, proj)


def _qkv_project(x, mod, w_ref, qg_ref, kg_ref, cos_ref, sin_ref, q_ref, k_ref, v_ref, issue_part):
    issue_part(0)
    hb = (x * (1.0 + mod[1:2]) + mod[0:1]).astype(MXU_DTYPE)
    cos = cos_ref[...]
    sin = sin_ref[...]

    def norm_rope(t, g):
        t = t * lax.rsqrt(jnp.mean(t * t, axis=-1, keepdims=True) + RMS_EPS) * g
        return t * cos + pltpu.roll(t, A_HEAD_DIM // 2, 1) * sin

    qkv = jnp.dot(hb, w_ref[...], preferred_element_type=F32)
    issue_part(1)
    for j in range(A_HEADS):
        if j == A_HEADS // 2:
            issue_part(2)
        sl = slice(j * A_HEAD_DIM, (j + 1) * A_HEAD_DIM)
        q_ref[0, :, sl] = (norm_rope(qkv[:, sl], qg_ref[...]) * ATT_Q_SCALE).astype(q_ref.dtype)
    issue_part(3)
    for j in range(A_KV_HEADS):
        sl = slice(j * A_HEAD_DIM, (j + 1) * A_HEAD_DIM)
        t = qkv[:, A_Q_W + j * A_HEAD_DIM:A_Q_W + (j + 1) * A_HEAD_DIM]
        k_ref[0, :, sl] = norm_rope(t, kg_ref[...]).astype(k_ref.dtype)
    v_ref[0] = qkv[:, A_Q_W + A_KV_W:].astype(v_ref.dtype)


def _attn_kernel(q_ref, qn_ref, k_ref, vt_ref, o_ref, s_buf):
    rep = A_HEADS // A_KV_HEADS
    n_kv = vt_ref.shape[2]
    assert n_kv % 2 == 0

    def transposed(ref):
        return [ref[0, :, r * A_HEAD_DIM:(r + 1) * A_HEAD_DIM].astype(F32).T.astype(MXU_DTYPE) for r in range(rep)]

    qts = transposed(q_ref)

    def scores(j, slot, qt=qts):
        kb = k_ref[0, pl.ds(pl.multiple_of(j * ATT_TK, ATT_TK), ATT_TK), :]
        for r in range(rep):
            s_buf[slot, r] = jnp.dot(kb, qt[r], preferred_element_type=F32)

    def consume(j, slot, stats):
        vt = vt_ref[0, 0, j]
        new = []
        for r in range(rep):
            m, l, acc = stats[r]
            s = s_buf[slot, r]
            m_new = jnp.maximum(m, jnp.max(s, axis=0, keepdims=True))
            p = jnp.exp2(s - m_new)
            alpha = jnp.exp2(m - m_new)
            l = alpha * l + jnp.sum(p, axis=0, keepdims=True)
            acc = alpha * acc + jnp.dot(vt, p.astype(vt.dtype), preferred_element_type=F32)
            new.append((m_new, l, acc))
        return tuple(new)

    def pair(i, stats):
        j = 2 * i
        scores(j + 1, 1)
        stats = consume(j, 0, stats)
        scores(j + 2, 0)
        return consume(j + 1, 1, stats)

    stats = tuple((jnp.full((1, ATT_TQ), -jnp.inf, F32), jnp.zeros((1, ATT_TQ), F32),
                   jnp.zeros((A_HEAD_DIM, ATT_TQ), F32)) for _ in range(rep))
    pl.when(pl.program_id(2) == 0)(lambda: scores(0, 0))
    stats = lax.fori_loop(0, n_kv // 2 - 1, pair, stats)
    scores(n_kv - 1, 1)
    stats = consume(n_kv - 2, 0, stats)
    scores(0, 0, transposed(qn_ref))
    final = consume(n_kv - 1, 1, stats)
    for r in range(rep):
        _, l, acc = final[r]
        o_ref[0, :, r * A_HEAD_DIM:(r + 1) * A_HEAD_DIM] = (acc / l).T.astype(o_ref.dtype)


def attn_call(q, k, v, n_ctx):
    bsz, ta, _ = q.shape
    t_lat = ta - n_ctx
    gw = (A_HEADS // A_KV_HEADS) * A_HEAD_DIM
    q_off = n_ctx // ATT_TQ
    n_kv = ta // ATT_TK
    vt = v.reshape(bsz, n_kv, ATT_TK, A_KV_HEADS, A_HEAD_DIM).transpose(0, 3, 1, 4, 2)
    nq = t_lat // ATT_TQ
    return pl.pallas_call(
        _attn_kernel,
        grid=(bsz, A_KV_HEADS, nq),
        in_specs=[pl.BlockSpec((1, ATT_TQ, gw), lambda b, g, i: (b, i + q_off, g)),
                  pl.BlockSpec((1, ATT_TQ, gw), lambda b, g, i: (b, jnp.minimum(i + 1, nq - 1) + q_off, g)),
                  pl.BlockSpec((1, ta, A_HEAD_DIM), lambda b, g, i: (b, 0, g)),
                  pl.BlockSpec((1, 1, n_kv, A_HEAD_DIM, ATT_TK), lambda b, g, i: (b, g, 0, 0, 0))],
        out_specs=pl.BlockSpec((1, ATT_TQ, gw), lambda b, g, i: (b, i, g)),
        out_shape=jax.ShapeDtypeStruct((bsz, t_lat, A_Q_W), MXU_DTYPE),
        scratch_shapes=[pltpu.VMEM((2, A_HEADS // A_KV_HEADS, ATT_TK, ATT_TQ), F32)],
        compiler_params=_cparams(("arbitrary", "arbitrary", "arbitrary")),
        name="attention",
    )(q, q, k, vt)


def _outproj1_kernel(a_ref, x_ref, mod_ref, w_ref, lng_ref, lnb_ref, rw_ref, rb_ref, x1_ref, h2_ref, e_ref, g_ref):
    y = jnp.dot(a_ref[0], w_ref[...], preferred_element_type=F32)
    _finish_sublayer(x_ref[0], y, mod_ref[0, 0], lng_ref, lnb_ref, rw_ref, rb_ref, x1_ref, h2_ref, e_ref, g_ref)


def outproj1_call(att, xa, modsel, w_out, ln_g, ln_b, router_wt, router_b, ctx_tiles):
    bsz, t_lat, _ = att.shape
    d = xa.shape[-1]
    const = lambda a: pl.BlockSpec(a.shape, lambda b, j: (0, 0))
    out_specs, out_shape = _sublayer_out(bsz, t_lat, d)
    return pl.pallas_call(
        _outproj1_kernel,
        grid=(bsz, t_lat // TM),
        in_specs=[pl.BlockSpec((1, TM, att.shape[-1]), lambda b, j: (b, j, 0)),
                  pl.BlockSpec((1, TM, d), lambda b, j: (b, j + ctx_tiles, 0)),
                  pl.BlockSpec((1, 1, 6, d), lambda b, j: (b, 1, 0, 0)),
                  const(w_out), const(ln_g), const(ln_b), const(router_wt), const(router_b)],
        out_specs=out_specs,
        out_shape=out_shape,
        compiler_params=_cparams(("arbitrary", "arbitrary")),
        name="outproj1",
    )(att, xa, modsel, w_out, ln_g, ln_b, router_wt, router_b)


def _rope_tables(n_ctx, n_lat):
    rows = n_lat // GRID_W
    row = jnp.repeat(jnp.arange(rows), GRID_W).astype(F32)
    col = jnp.tile(jnp.arange(GRID_W), rows).astype(F32)
    n_freq = A_HEAD_DIM // 4
    inv = ROPE_THETA ** (-jnp.arange(n_freq, dtype=F32) / n_freq)
    ang = jnp.concatenate([row[:, None] * inv, col[:, None] * inv], -1)
    cos, sin = jnp.cos(ang), jnp.sin(ang)
    cos2 = jnp.concatenate([cos, cos], -1)
    sin2 = jnp.concatenate([-sin, sin], -1)
    cos2 = jnp.concatenate([jnp.ones((n_ctx, A_HEAD_DIM), F32), cos2], 0)
    sin2 = jnp.concatenate([jnp.zeros((n_ctx, A_HEAD_DIM), F32), sin2], 0)
    return cos2, sin2


def _lane_row(parts, width=GATE_LANES):
    row = jnp.concatenate([p.reshape(-1).astype(F32) for p in parts])
    return jnp.pad(row, (0, width - row.shape[0])).reshape(1, width)


def kernel(x, c, ctx, c_ctx, ada_w, ada_b, ln_g, ln_b, ab_w_in, ab_w_out, ml_ig_b, ml_fg_b, ml_norm_g,
           ssm_conv_w, ssm_conv_b, ssm_dt_b, ssm_a_log, ssm_d, ssm_norm_g, at_w_in, at_w_out, at_q_g, at_k_g,
           router_w, router_b, moe_w_gate, moe_w_up, moe_w_down):
    bsz, n_lat, d = x.shape
    n_ctx = ctx.shape[1]
    assert d == D_MODEL and bsz + 1 <= SUBLANES
    assert n_ctx % TM == 0 and n_lat % TM == 0 and n_lat % GRID_W == 0
    assert (n_ctx + n_lat) % ATT_TK == 0 and n_ctx % ATT_TQ == 0 and n_lat % ATT_TQ == 0
    assert (bsz * (n_ctx + n_lat)) % DISPATCH_ROWS == 0 and (bsz * n_lat) % DISPATCH_ROWS == 0
    ctx_tiles = n_ctx // TM
    ctx_chunks = n_ctx // CHUNK

    crows = jnp.zeros((SUBLANES, d), F32).at[:bsz].set(c).at[bsz].set(c_ctx)
    mods = ada_call(crows, ada_w, ada_b)

    def mod_table(i):
        lat = mods[i, :bsz].reshape(bsz, 1, 6, d)
        cx = jnp.broadcast_to(mods[i, bsz].reshape(1, 1, 6, d), (bsz, 1, 6, d))
        return jnp.concatenate([cx, lat], axis=1)

    router_wt = jnp.pad(router_w, ((0, 0), (0, LANES - N_EXPERTS)))
    router_bc = router_b.reshape(N_EXPERTS, 1)

    modsel = mod_table(0)
    w_in = ab_w_in[0]
    s_q, s_k, s_v, s_o, s_ig, s_fg, s_z, s_xbc = (int(v) for v in
        (0, 512, 1024, 2048, 3072, 3072 + 8, 3072 + 16, 3072 + 16 + 1024))
    s_dt = s_xbc + S_CONV_CH
    w_big = jnp.concatenate([w_in[:, :s_ig], w_in[:, s_z:s_dt]], axis=1).astype(MXU_DTYPE)
    w_small = jnp.concatenate([w_in[:, s_ig:s_z], w_in[:, s_dt:]], axis=1)
    w_small = jnp.pad(w_small, ((0, 0), (0, GATE_LANES - w_small.shape[1])))
    q, k, v, o, z, xbc, gates = inproj0_call(ctx, x, modsel, w_big, w_small, ctx_tiles)
    xbc_act = conv_call(xbc, ssm_conv_w[0], ssm_conv_b[0], ctx_tiles)

    grow = jnp.swapaxes(gates[:, :, :DT_OFF], 1, 2)
    dtrow = jnp.swapaxes(gates[:, :, DT_OFF:DT_OFF + N_DIR * S_HEADS], 1, 2)
    gate_b_row = _lane_row([ml_ig_b[0], ml_fg_b[0], ssm_dt_b[0]])
    gate_b_col = jnp.concatenate([ml_ig_b[0].reshape(-1), ml_fg_b[0].reshape(-1)]).reshape(-1, 1)
    a_neg = -jnp.exp(ssm_a_log[0].astype(F32)).reshape(-1)
    an_row = _lane_row([jnp.zeros((DT_OFF,), F32), a_neg])
    an_col = a_neg.reshape(-1, 1)
    dtb_col = ssm_dt_b[0].reshape(-1, 1)

    dskip = jnp.repeat(ssm_d[0].astype(F32), S_HEAD_DIM).reshape(1, S_WIDTH)
    m_norm_g = ml_norm_g[0].reshape(1, -1)
    s_norm_g = ssm_norm_g[0].reshape(1, -1)
    assert N_DIR == 2
    hm = None
    hs = None
    for dd in range(N_DIR):
        hm = mlstm_call(q, k, v, gates, grow, gate_b_row, gate_b_col, hm, o, m_norm_g, rev=dd == 1, d=dd,
                        ctx_chunks=ctx_chunks)
        hs = ssd_call(xbc_act, gates, dtrow, gate_b_row, dtb_col, an_row, an_col, hs, z, dskip, s_norm_g,
                      rev=dd == 1, d=dd, ctx_chunks=ctx_chunks)

    x1, h2t, e_idx, gates = outproj0_call(hm, hs, ctx, x, modsel, ab_w_out[0].astype(MXU_DTYPE),
                                          ln_g[0, 0].reshape(1, d), ln_b[0, 0].reshape(1, d), router_wt, router_bc,
                                          ctx_tiles)
    modsel1 = mod_table(1)
    cos2, sin2 = _rope_tables(n_ctx, n_lat)
    proj = (modsel1, at_w_in[0].astype(MXU_DTYPE), at_q_g[0].reshape(1, -1), at_k_g[0].reshape(1, -1), cos2, sin2)
    xa, qa, ka, va = moe_block(x1, h2t, e_idx, gates, modsel, ln_g[0, 1].reshape(1, d), ln_b[0, 1].reshape(1, d),
                               moe_w_gate, moe_w_up, moe_w_down, 0, ctx_tiles, proj)
    modsel = modsel1
    att = attn_call(qa, ka, va, n_ctx)
    x1, h2t, e_idx, gates = outproj1_call(att, xa, modsel, at_w_out[0].astype(MXU_DTYPE), ln_g[1, 0].reshape(1, d),
                                          ln_b[1, 0].reshape(1, d), router_wt, router_bc, ctx_tiles)
    return moe_block(x1, h2t, e_idx, gates, modsel, ln_g[1, 1].reshape(1, d), ln_b[1, 1].reshape(1, d),
                     moe_w_gate, moe_w_up, moe_w_down, 1, 0)
```
